```python
import jax, jax.numpy as jnp
from jax import lax
import numpy as np

D_MODEL = 1024
BATCH = 8
SEQ = 4096
DEPTH = 4

HEAD_DIM = 64
N_HEADS_TOTAL = D_MODEL // HEAD_DIM
N_MEM_HEADS = 4
N_MIX_HEADS = N_HEADS_TOTAL - N_MEM_HEADS
D_MIX = N_MIX_HEADS * HEAD_DIM
D_MEMQ = N_MEM_HEADS * HEAD_DIM
MEM_TOKENS = 256
D_FF = 2816
CONV_WIDTH = 3
ROPE_THETA = 10000.0
Q_BLOCK = 128
DILATED_BRANCHES = ((128, 1), (512, 4), (2048, 16))
N_MIXERS = 2
N_A_LAYERS = (DEPTH + 1) // 2
N_B_LAYERS = DEPTH // 2
FOX_IN = 3 * D_MIX + N_MIX_HEADS + D_MEMQ
DIL_IN = 3 * D_MIX + D_MEMQ
NORM_EPS = 1e-6
NEG = -1e30

kernel_name = 'hybrid_fox_dilated_memory_convffn'


def rmsnorm(x, g):
    xf = x.astype(jnp.float32)
    y = xf * lax.rsqrt(jnp.mean(xf * xf, axis=-1, keepdims=True) + NORM_EPS)
    return (y * g.astype(jnp.float32)).astype(x.dtype)


def split_heads(t, n_heads):
    b, s, _ = t.shape
    return t.reshape(b, s, n_heads, HEAD_DIM).transpose(0, 2, 1, 3)


def merge_heads(t):
    b, h, s, d = t.shape
    return t.transpose(0, 2, 1, 3).reshape(b, s, h * d)


def rope_tables(seq):
    inv = 1.0 / (ROPE_THETA ** (jnp.arange(0, HEAD_DIM, 2, dtype=jnp.float32) / HEAD_DIM))
    ang = jnp.arange(seq, dtype=jnp.float32)[:, None] * inv[None, :]
    return jnp.cos(ang), jnp.sin(ang)


def apply_rope(x, cos, sin):
    x1, x2 = jnp.split(x.astype(jnp.float32), 2, axis=-1)
    y = jnp.concatenate([x1 * cos - x2 * sin, x2 * cos + x1 * sin], axis=-1)
    return y.astype(x.dtype)


def fox_attention(q, k, v, log_f):
    b, h, s, dh = q.shape
    c = lax.cumsum(log_f, axis=2)
    nb = s // Q_BLOCK
    scale = dh ** -0.5
    qb = q.reshape(b, h, nb, Q_BLOCK, dh).transpose(2, 0, 1, 3, 4)
    cb = c.reshape(b, h, nb, Q_BLOCK).transpose(2, 0, 1, 3)
    starts = jnp.arange(nb, dtype=jnp.int32) * Q_BLOCK
    kpos = jnp.arange(s, dtype=jnp.int32)

    def one_block(args):
        qi, ci, st = args
        sc = jnp.einsum('bhqd,bhkd->bhqk', qi, k, preferred_element_type=jnp.float32) * scale
        sc = sc + ci[..., :, None] - c[..., None, :]
        qpos = st + jnp.arange(Q_BLOCK, dtype=jnp.int32)
        sc = jnp.where(kpos[None, :] <= qpos[:, None], sc, NEG)
        p = jax.nn.softmax(sc, axis=-1)
        return jnp.einsum('bhqk,bhkd->bhqd', p.astype(v.dtype), v)

    out = lax.map(one_block, (qb, cb, starts))
    return out.transpose(1, 2, 0, 3, 4).reshape(b, h, s, dh)


def dilated_branch(q, k, v, window, dilation):
    b, h, s, dh = q.shape
    L = window // dilation
    chunk = dilation * L
    sp = -(-s // chunk) * chunk
    n = sp // dilation
    nb = n // L
    scale = dh ** -0.5

    def strided(t):
        t = jnp.pad(t, ((0, 0), (0, 0), (0, sp - s), (0, 0)))
        t = t.reshape(b, h, n, dilation, dh).transpose(0, 1, 3, 2, 4)
        return t.reshape(b, h, dilation, nb, L, dh)

    def with_prev(t):
        prev = jnp.pad(t, ((0, 0), (0, 0), (0, 0), (1, 0), (0, 0), (0, 0)))[:, :, :, :-1]
        return jnp.concatenate([prev, t], axis=4)

    qs = strided(q)
    kb = with_prev(strided(k))
    vb = with_prev(strided(v))
    sc = jnp.einsum('bhrnqd,bhrnkd->bhrnqk', qs, kb, preferred_element_type=jnp.float32) * scale
    a = jnp.arange(L, dtype=jnp.int32)
    cidx = jnp.arange(2 * L, dtype=jnp.int32)
    blk = jnp.arange(nb, dtype=jnp.int32)
    dist = a[:, None] + L - cidx[None, :]
    valid = (dist >= 0) & (dist <= L)
    valid = valid[None] & ((blk[:, None, None] > 0) | (cidx[None, None, :] >= L))
    sc = jnp.where(valid, sc, NEG)
    m = jnp.max(sc, axis=-1, keepdims=True)
    e = jnp.exp(sc - m)
    den = jnp.sum(e, axis=-1, keepdims=True)
    o = jnp.einsum('bhrnqk,bhrnkd->bhrnqd', (e / den).astype(v.dtype), vb)
    lse = (m + jnp.log(den))[..., 0]
    o = o.reshape(b, h, dilation, n, dh).transpose(0, 1, 3, 2, 4).reshape(b, h, sp, dh)[:, :, :s]
    lse = lse.reshape(b, h, dilation, n).transpose(0, 1, 3, 2).reshape(b, h, sp)[:, :, :s]
    return o, lse


def dilated_attention(q, k, v):
    outs, lses = [], []
    for window, dilation in DILATED_BRANCHES:
        o, lse = dilated_branch(q, k, v, window, dilation)
        outs.append(o)
        lses.append(lse)
    wts = jax.nn.softmax(jnp.stack(lses, axis=0), axis=0)
    return jnp.einsum('gbhs,gbhsd->bhsd', wts.astype(v.dtype), jnp.stack(outs, axis=0))


def memory_attention(qm, mem_n, w_mem_kv):
    km, vm = jnp.split(mem_n @ w_mem_kv, 2, axis=-1)
    km = split_heads(km, N_MEM_HEADS)
    vm = split_heads(vm, N_MEM_HEADS)
    sc = jnp.einsum('bhqd,bhkd->bhqk', qm, km, preferred_element_type=jnp.float32) * (HEAD_DIM ** -0.5)
    p = jax.nn.softmax(sc, axis=-1)
    return jnp.einsum('bhqk,bhkd->bhqd', p.astype(vm.dtype), vm)


def conv_ffn(xn, w_up, conv_w, conv_b, w_down):
    u = xn @ w_up
    s = u.shape[1]
    up = jnp.pad(u, ((0, 0), (CONV_WIDTH - 1, 0), (0, 0)))
    c = conv_b
    for j in range(CONV_WIDTH):
        c = c + conv_w[j] * up[:, j:j + s]
    val, gate = jnp.split(c, 2, axis=-1)
    return (jax.nn.silu(gate) * val) @ w_down


def _fwd_setup_inputs(seed: int = 0) -> dict:
    key = jax.random.key(seed)
    ks = jax.random.split(key, 16)
    f32 = jnp.float32
    nrm = lambda k, shape, scale: jax.random.normal(k, shape, f32) * scale
    forget_bias_init = 3.0
    return {
        'x': nrm(ks[0], (BATCH, SEQ, D_MODEL), 1.0),
        'mem': nrm(ks[1], (BATCH, MEM_TOKENS, D_MODEL), 1.0),
        'norm_mix': 1.0 + nrm(ks[2], (DEPTH, D_MODEL), 0.1),
        'norm_mem': 1.0 + nrm(ks[3], (DEPTH, D_MODEL), 0.1),
        'norm_ffn': 1.0 + nrm(ks[4], (DEPTH, D_MODEL), 0.1),
        'w_in_fox': nrm(ks[5], (N_A_LAYERS, D_MODEL, FOX_IN), D_MODEL ** -0.5),
        'b_forget': forget_bias_init + nrm(ks[6], (N_A_LAYERS, N_MIX_HEADS), 0.5),
        'w_in_dil': nrm(ks[7], (N_B_LAYERS, D_MODEL, DIL_IN), D_MODEL ** -0.5),
        'w_mem_kv': nrm(ks[8], (DEPTH, D_MODEL, 2 * D_MEMQ), D_MODEL ** -0.5),
        'w_out': nrm(ks[9], (DEPTH, D_MODEL, D_MODEL), D_MODEL ** -0.5),
        'w_up': nrm(ks[10], (DEPTH, D_MODEL, 2 * D_FF), D_MODEL ** -0.5),
        'conv_w': nrm(ks[11], (DEPTH, CONV_WIDTH, 2 * D_FF), CONV_WIDTH ** -0.5),
        'conv_b': nrm(ks[12], (DEPTH, 2 * D_FF), 0.02),
        'w_down': nrm(ks[13], (DEPTH, D_FF, D_MODEL), D_FF ** -0.5),
        'norm_final': 1.0 + nrm(ks[14], (D_MODEL,), 0.1),
    }


def _fwd_reference(x, mem, norm_mix, norm_mem, norm_ffn, w_in_fox, b_forget, w_in_dil,
              w_mem_kv, w_out, w_up, conv_w, conv_b, w_down, norm_final):
    s = x.shape[1]
    cos, sin = rope_tables(s)
    h = x
    for layer in range(DEPTH):
        kind = layer % N_MIXERS
        slot = layer // N_MIXERS
        xn = rmsnorm(h, norm_mix[layer])
        mn = rmsnorm(mem, norm_mem[layer])
        if kind == 0:
            proj = xn @ w_in_fox[slot]
            q, k, v, f_logit, qm = jnp.split(
                proj, [D_MIX, 2 * D_MIX, 3 * D_MIX, 3 * D_MIX + N_MIX_HEADS], axis=-1)
            log_f = jax.nn.log_sigmoid(
                (f_logit + b_forget[slot]).astype(jnp.float32)).transpose(0, 2, 1)
            mix = fox_attention(split_heads(q, N_MIX_HEADS), split_heads(k, N_MIX_HEADS),
                                split_heads(v, N_MIX_HEADS), log_f)
        else:
            proj = xn @ w_in_dil[slot]
            q, k, v, qm = jnp.split(proj, [D_MIX, 2 * D_MIX, 3 * D_MIX], axis=-1)
            qh = apply_rope(split_heads(q, N_MIX_HEADS), cos, sin)
            kh = apply_rope(split_heads(k, N_MIX_HEADS), cos, sin)
            mix = dilated_attention(qh, kh, split_heads(v, N_MIX_HEADS))
        mem_out = memory_attention(split_heads(qm, N_MEM_HEADS), mn, w_mem_kv[layer])
        heads = jnp.concatenate([mix, mem_out], axis=1)
        h = h + merge_heads(heads) @ w_out[layer]
        h = h + conv_ffn(rmsnorm(h, norm_ffn[layer]), w_up[layer], conv_w[layer],
                         conv_b[layer], w_down[layer])
    return rmsnorm(h, norm_final)


import jax as _jax
import jax.numpy as _jnp

TWIN_FORMAT = 'train_step'
FWD_PARAMS = ['x', 'mem', 'norm_mix', 'norm_mem', 'norm_ffn', 'w_in_fox', 'b_forget', 'w_in_dil', 'w_mem_kv', 'w_out', 'w_up', 'conv_w', 'conv_b', 'w_down', 'norm_final']
TWIN_WEIGHTS = ['norm_mix', 'norm_mem', 'norm_ffn', 'w_in_fox', 'b_forget', 'w_in_dil', 'w_mem_kv', 'w_out', 'w_up', 'conv_w', 'conv_b', 'w_down', 'norm_final']
TWIN_DIFF_INPUT = 'x'
TWIN_INPUTS = ['x', 'mem', 'norm_mix', 'norm_mem', 'norm_ffn', 'w_in_fox', 'b_forget', 'w_in_dil', 'w_mem_kv', 'w_out', 'w_up', 'conv_w', 'conv_b', 'w_down', 'norm_final', 'loss_target', 'm_norm_mix', 'm_norm_mem', 'm_norm_ffn', 'm_w_in_fox', 'm_b_forget', 'm_w_in_dil', 'm_w_mem_kv', 'm_w_out', 'm_w_up', 'm_conv_w', 'm_conv_b', 'm_w_down', 'm_norm_final', 'v_norm_mix', 'v_norm_mem', 'v_norm_ffn', 'v_w_in_fox', 'v_b_forget', 'v_w_in_dil', 'v_w_mem_kv', 'v_w_out', 'v_w_up', 'v_conv_w', 'v_conv_b', 'v_w_down', 'v_norm_final']
TWIN_OUTPUTS = ['loss', 'grad_x', 'grad_norm_mix', 'grad_norm_mem', 'grad_norm_ffn', 'grad_w_in_fox', 'grad_b_forget', 'grad_w_in_dil', 'grad_w_mem_kv', 'grad_w_out', 'grad_w_up', 'grad_conv_w', 'grad_conv_b', 'grad_w_down', 'grad_norm_final', 'delta_norm_mix', 'delta_norm_mem', 'delta_norm_ffn', 'delta_w_in_fox', 'delta_b_forget', 'delta_w_in_dil', 'delta_w_mem_kv', 'delta_w_out', 'delta_w_up', 'delta_conv_w', 'delta_conv_b', 'delta_w_down', 'delta_norm_final', 'new_m_norm_mix', 'new_m_norm_mem', 'new_m_norm_ffn', 'new_m_w_in_fox', 'new_m_b_forget', 'new_m_w_in_dil', 'new_m_w_mem_kv', 'new_m_w_out', 'new_m_w_up', 'new_m_conv_w', 'new_m_conv_b', 'new_m_w_down', 'new_m_norm_final', 'new_v_norm_mix', 'new_v_norm_mem', 'new_v_norm_ffn', 'new_v_w_in_fox', 'new_v_b_forget', 'new_v_w_in_dil', 'new_v_w_mem_kv', 'new_v_w_out', 'new_v_w_up', 'new_v_conv_w', 'new_v_conv_b', 'new_v_w_down', 'new_v_norm_final']
TWIN_LEAF_KINDS = {'loss': 'loss', 'grad_x': 'grad_x', 'grad_norm_mix': 'grad_w', 'grad_norm_mem': 'grad_w', 'grad_norm_ffn': 'grad_w', 'grad_w_in_fox': 'grad_w', 'grad_b_forget': 'grad_w', 'grad_w_in_dil': 'grad_w', 'grad_w_mem_kv': 'grad_w', 'grad_w_out': 'grad_w', 'grad_w_up': 'grad_w', 'grad_conv_w': 'grad_w', 'grad_conv_b': 'grad_w', 'grad_w_down': 'grad_w', 'grad_norm_final': 'grad_w', 'delta_norm_mix': 'delta_w', 'delta_norm_mem': 'delta_w', 'delta_norm_ffn': 'delta_w', 'delta_w_in_fox': 'delta_w', 'delta_b_forget': 'delta_w', 'delta_w_in_dil': 'delta_w', 'delta_w_mem_kv': 'delta_w', 'delta_w_out': 'delta_w', 'delta_w_up': 'delta_w', 'delta_conv_w': 'delta_w', 'delta_conv_b': 'delta_w', 'delta_w_down': 'delta_w', 'delta_norm_final': 'delta_w', 'new_m_norm_mix': 'new_m', 'new_m_norm_mem': 'new_m', 'new_m_norm_ffn': 'new_m', 'new_m_w_in_fox': 'new_m', 'new_m_b_forget': 'new_m', 'new_m_w_in_dil': 'new_m', 'new_m_w_mem_kv': 'new_m', 'new_m_w_out': 'new_m', 'new_m_w_up': 'new_m', 'new_m_conv_w': 'new_m', 'new_m_conv_b': 'new_m', 'new_m_w_down': 'new_m', 'new_m_norm_final': 'new_m', 'new_v_norm_mix': 'new_v', 'new_v_norm_mem': 'new_v', 'new_v_norm_ffn': 'new_v', 'new_v_w_in_fox': 'new_v', 'new_v_b_forget': 'new_v', 'new_v_w_in_dil': 'new_v', 'new_v_w_mem_kv': 'new_v', 'new_v_w_out': 'new_v', 'new_v_w_up': 'new_v', 'new_v_conv_w': 'new_v', 'new_v_conv_b': 'new_v', 'new_v_w_down': 'new_v', 'new_v_norm_final': 'new_v'}


def _forward(args):
    return _fwd_reference(*[args[k] for k in FWD_PARAMS])


def _output_shape():
    def fwd():
        inp = _fwd_setup_inputs(0)
        return _fwd_reference(*[inp[k] for k in FWD_PARAMS])
    out = _jax.eval_shape(fwd)
    return out.shape, out.dtype

N_MICROBATCH = 1
ADAM_LR = 0.001
ADAM_B1 = 0.9
ADAM_B2 = 0.999
ADAM_EPS = 1e-08
ADAM_WD = 0.01
ADAM_STEP = 10
PER_EXAMPLE_BATCH_AXIS = {'x': 0, 'mem': 0, 'loss_target': 0}
SHARED_INPUTS = []
_WEIGHT_DTYPES = {'norm_mix': _jnp.float32, 'norm_mem': _jnp.float32, 'norm_ffn': _jnp.float32, 'w_in_fox': _jnp.float32, 'b_forget': _jnp.float32, 'w_in_dil': _jnp.float32, 'w_mem_kv': _jnp.float32, 'w_out': _jnp.float32, 'w_up': _jnp.float32, 'conv_w': _jnp.float32, 'conv_b': _jnp.float32, 'w_down': _jnp.float32, 'norm_final': _jnp.float32}
MOMENT_SCALE = {'norm_mix': 7.704439e-02, 'norm_mem': 1.583833e-02, 'norm_ffn': 1.363254e-01, 'w_in_fox': 5.920782e-02, 'b_forget': 2.969205e-01, 'w_in_dil': 3.480404e-02, 'w_mem_kv': 2.208548e-02, 'w_out': 5.428580e-02, 'w_up': 5.849831e-02, 'conv_w': 5.954074e-02, 'conv_b': 6.176829e-02, 'w_down': 9.777555e-02, 'norm_final': 3.219778e+01}


def _to_microbatches(a, axis):
    t = _jnp.moveaxis(a, axis, 0)
    t = t.reshape((N_MICROBATCH, t.shape[0] // N_MICROBATCH) + t.shape[1:])
    return _jnp.moveaxis(t, 1, axis + 1)


def setup_inputs(seed: int = 0) -> dict:
    inp = _fwd_setup_inputs(seed)
    key = _jax.random.fold_in(_jax.random.key(seed), 7919)
    shape, _ = _output_shape()
    out = dict(inp)
    out["loss_target"] = _jax.random.normal(_jax.random.fold_in(key, 0), shape, _jnp.float32)
    for i, name in enumerate(TWIN_WEIGHTS):
        w = inp[name].astype(_jnp.float32)
        if MOMENT_SCALE is None:
            s = _jnp.sqrt(_jnp.mean(_jnp.square(w)) + 1e-30)
        else:
            s = MOMENT_SCALE[name]
        km, kv = _jax.random.split(_jax.random.fold_in(key, i + 1))
        out[name] = w
        out["m_" + name] = s * _jax.random.normal(km, w.shape, _jnp.float32)
        out["v_" + name] = (s * s) * _jax.random.uniform(kv, w.shape, _jnp.float32, 0.5, 1.5)
    if N_MICROBATCH > 1:
        for name, axis in PER_EXAMPLE_BATCH_AXIS.items():
            out[name] = _to_microbatches(out[name], axis)
    return {'x': out['x'], 'mem': out['mem'], 'norm_mix': out['norm_mix'], 'norm_mem': out['norm_mem'], 'norm_ffn': out['norm_ffn'], 'w_in_fox': out['w_in_fox'], 'b_forget': out['b_forget'], 'w_in_dil': out['w_in_dil'], 'w_mem_kv': out['w_mem_kv'], 'w_out': out['w_out'], 'w_up': out['w_up'], 'conv_w': out['conv_w'], 'conv_b': out['conv_b'], 'w_down': out['w_down'], 'norm_final': out['norm_final'], 'loss_target': out['loss_target'], 'm_norm_mix': out['m_norm_mix'], 'm_norm_mem': out['m_norm_mem'], 'm_norm_ffn': out['m_norm_ffn'], 'm_w_in_fox': out['m_w_in_fox'], 'm_b_forget': out['m_b_forget'], 'm_w_in_dil': out['m_w_in_dil'], 'm_w_mem_kv': out['m_w_mem_kv'], 'm_w_out': out['m_w_out'], 'm_w_up': out['m_w_up'], 'm_conv_w': out['m_conv_w'], 'm_conv_b': out['m_conv_b'], 'm_w_down': out['m_w_down'], 'm_norm_final': out['m_norm_final'], 'v_norm_mix': out['v_norm_mix'], 'v_norm_mem': out['v_norm_mem'], 'v_norm_ffn': out['v_norm_ffn'], 'v_w_in_fox': out['v_w_in_fox'], 'v_b_forget': out['v_b_forget'], 'v_w_in_dil': out['v_w_in_dil'], 'v_w_mem_kv': out['v_w_mem_kv'], 'v_w_out': out['v_w_out'], 'v_w_up': out['v_w_up'], 'v_conv_w': out['v_conv_w'], 'v_conv_b': out['v_conv_b'], 'v_w_down': out['v_w_down'], 'v_norm_final': out['v_norm_final']}


def _loss(weights, diff, rest, loss_target):
    with _jax.named_scope("forward"):
        args = {**rest, TWIN_DIFF_INPUT: diff, **{k: w.astype(_WEIGHT_DTYPES[k]) for k, w in weights.items()}}
        y = _forward(args)
    with _jax.named_scope("loss_head"):
        err = _jnp.square(y.astype(_jnp.float32) - loss_target)
        return 0.5 * _jnp.sum(_jnp.mean(err, axis=-1)) if err.ndim else 0.5 * err


def _adamw(w, g, m, v):
    m = ADAM_B1 * m + (1.0 - ADAM_B1) * g
    v = ADAM_B2 * v + (1.0 - ADAM_B2) * _jnp.square(g)
    m_hat = m / (1.0 - ADAM_B1 ** ADAM_STEP)
    v_hat = v / (1.0 - ADAM_B2 ** ADAM_STEP)
    delta = -ADAM_LR * (m_hat / (_jnp.sqrt(v_hat) + ADAM_EPS) + ADAM_WD * w)
    return delta, m, v


def reference(x, mem, norm_mix, norm_mem, norm_ffn, w_in_fox, b_forget, w_in_dil, w_mem_kv, w_out, w_up, conv_w, conv_b, w_down, norm_final, loss_target, m_norm_mix, m_norm_mem, m_norm_ffn, m_w_in_fox, m_b_forget, m_w_in_dil, m_w_mem_kv, m_w_out, m_w_up, m_conv_w, m_conv_b, m_w_down, m_norm_final, v_norm_mix, v_norm_mem, v_norm_ffn, v_w_in_fox, v_b_forget, v_w_in_dil, v_w_mem_kv, v_w_out, v_w_up, v_conv_w, v_conv_b, v_w_down, v_norm_final):
    given = dict(x=x, mem=mem, norm_mix=norm_mix, norm_mem=norm_mem, norm_ffn=norm_ffn, w_in_fox=w_in_fox, b_forget=b_forget, w_in_dil=w_in_dil, w_mem_kv=w_mem_kv, w_out=w_out, w_up=w_up, conv_w=conv_w, conv_b=conv_b, w_down=w_down, norm_final=norm_final, loss_target=loss_target, m_norm_mix=m_norm_mix, m_norm_mem=m_norm_mem, m_norm_ffn=m_norm_ffn, m_w_in_fox=m_w_in_fox, m_b_forget=m_b_forget, m_w_in_dil=m_w_in_dil, m_w_mem_kv=m_w_mem_kv, m_w_out=m_w_out, m_w_up=m_w_up, m_conv_w=m_conv_w, m_conv_b=m_conv_b, m_w_down=m_w_down, m_norm_final=m_norm_final, v_norm_mix=v_norm_mix, v_norm_mem=v_norm_mem, v_norm_ffn=v_norm_ffn, v_w_in_fox=v_w_in_fox, v_b_forget=v_b_forget, v_w_in_dil=v_w_in_dil, v_w_mem_kv=v_w_mem_kv, v_w_out=v_w_out, v_w_up=v_w_up, v_conv_w=v_conv_w, v_conv_b=v_conv_b, v_w_down=v_w_down, v_norm_final=v_norm_final)
    weights = {n: given[n] for n in TWIN_WEIGHTS}
    shared = {n: given[n] for n in SHARED_INPUTS}
    per_example = {n: given[n] for n in ['x', 'mem']}
    grad_fn = _jax.value_and_grad(_loss, argnums=(0, 1))

    def one_microbatch(ex, loss_target):
        ex = dict(ex)
        diff = ex.pop(TWIN_DIFF_INPUT)
        return grad_fn(weights, diff, {**shared, **ex}, loss_target)

    if N_MICROBATCH == 1:
        loss, (grad_w, grad_x) = one_microbatch(per_example, given["loss_target"])
    else:
        def body(carry, xs):
            loss_sum, grad_sum = carry
            l_k, (gw_k, gx_k) = one_microbatch(xs[0], xs[1])
            with _jax.named_scope("update"):
                return (loss_sum + l_k, _jax.tree.map(_jnp.add, grad_sum, gw_k)), gx_k

        init = (_jnp.zeros((), _jnp.float32), _jax.tree.map(_jnp.zeros_like, weights))
        (loss, grad_w), grad_x = _jax.lax.scan(body, init, (per_example, given["loss_target"]))
    with _jax.named_scope("update"):
        delta_w, new_m, new_v = {}, {}, {}
        for n in TWIN_WEIGHTS:
            delta_w[n], new_m[n], new_v[n] = _adamw(weights[n], grad_w[n], given["m_" + n], given["v_" + n])
    return (loss, grad_x, *[grad_w[n] for n in TWIN_WEIGHTS], *[delta_w[n] for n in TWIN_WEIGHTS],
            *[new_m[n] for n in TWIN_WEIGHTS], *[new_v[n] for n in TWIN_WEIGHTS])
```

```python
import functools
import math

import numpy as np
import jax
import jax.numpy as jnp
from jax import lax
from jax.experimental import pallas as pl
from jax.experimental.pallas import tpu as pltpu

F32 = jnp.float32
BF16 = jnp.bfloat16

D_MODEL = 1024
DEPTH = 4
HEAD_DIM = 64
N_MIX = 12
N_MEM = 4
D_MIX = N_MIX * HEAD_DIM
D_MEMQ = N_MEM * HEAD_DIM
D_FF = 2816
FOX_IN = 3 * D_MIX + N_MIX + D_MEMQ
DIL_IN = 3 * D_MIX + D_MEMQ
HP = 128
SCALE = HEAD_DIM ** -0.5
NEG = -1e30
NORM_EPS = 1e-6
DIL_MAX = 2048
LOG2 = math.log(2.0)
LOG3 = math.log(3.0)
ROPE_THETA = 10000.0
N_CHIPS = 4
FLAT_W = 1024
CONV_TF = 128
FLASH_BLK = 512
VMEM_LIMIT = 48 * 1024 * 1024

ADAM_LR = 0.001
ADAM_B1 = 0.9
ADAM_B2 = 0.999
ADAM_EPS = 1e-08
ADAM_WD = 0.01
ADAM_STEP = 10

MESH = pl.DeviceIdType.MESH


def _cp(*sem):
    return pltpu.CompilerParams(dimension_semantics=tuple(sem), vmem_limit_bytes=VMEM_LIMIT)


def _tile(n, cap, mult=128):
    if n <= cap:
        return n
    t = (cap // mult) * mult
    while t >= mult:
        if n % t == 0:
            return t
        t -= mult
    raise ValueError(f"no tile for {n} under {cap}")


_DIMS = {"nn": (((1,), (0,)), ((), ())), "nt": (((1,), (1,)), ((), ())), "tn": (((0,), (0,)), ((), ()))}


def _matmul(a, b, mode, out_dtype, name, residual=None, tm_cap=512, tn_cap=1024, tk_cap=1408):
    if mode == "nn":
        (m, k), n = a.shape, b.shape[1]
    elif mode == "nt":
        (m, k), n = a.shape, b.shape[0]
    else:
        (k, m), n = a.shape, b.shape[1]
    tm, tn, tk = _tile(m, tm_cap), _tile(n, tn_cap), _tile(k, tk_cap)
    nk = k // tk
    dims = _DIMS[mode]
    has_res = residual is not None

    def body(*refs):
        if has_res:
            a_ref, b_ref, r_ref, o_ref = refs[:4]
        else:
            a_ref, b_ref, o_ref = refs[:3]
        part = lax.dot_general(a_ref[...].astype(BF16), b_ref[...].astype(BF16), dims, preferred_element_type=F32)
        if nk == 1:
            if has_res:
                part = part + r_ref[...]
            o_ref[...] = part.astype(o_ref.dtype)
            return
        acc_ref = refs[-1]
        kk = pl.program_id(2)

        @pl.when(kk == 0)
        def _():
            acc_ref[...] = part

        @pl.when(kk > 0)
        def _():
            acc_ref[...] += part

        @pl.when(kk == nk - 1)
        def _():
            tot = acc_ref[...]
            if has_res:
                tot = tot + r_ref[...]
            o_ref[...] = tot.astype(o_ref.dtype)

    if mode == "nn":
        a_spec = pl.BlockSpec((tm, tk), lambda i, j, kk: (i, kk))
        b_spec = pl.BlockSpec((tk, tn), lambda i, j, kk: (kk, j))
    elif mode == "nt":
        a_spec = pl.BlockSpec((tm, tk), lambda i, j, kk: (i, kk))
        b_spec = pl.BlockSpec((tn, tk), lambda i, j, kk: (j, kk))
    else:
        a_spec = pl.BlockSpec((tk, tm), lambda i, j, kk: (kk, i))
        b_spec = pl.BlockSpec((tk, tn), lambda i, j, kk: (kk, j))
    in_specs = [a_spec, b_spec]
    args = [a, b]
    if has_res:
        in_specs.append(pl.BlockSpec((tm, tn), lambda i, j, kk: (i, j)))
        args.append(residual)
    return pl.pallas_call(
        body,
        out_shape=jax.ShapeDtypeStruct((m, n), out_dtype),
        grid=(m // tm, n // tn, nk),
        in_specs=in_specs,
        out_specs=pl.BlockSpec((tm, tn), lambda i, j, kk: (i, j)),
        scratch_shapes=[pltpu.VMEM((tm, tn), F32)] if nk > 1 else [],
        compiler_params=_cp("parallel", "parallel", "arbitrary"),
        name=name,
    )(*args)


def _rms_fwd(h, g, name):
    r, d = h.shape
    tr = _tile(r, 512, 8)

    def body(h_ref, g_ref, o_ref):
        x = h_ref[...]
        rstd = lax.rsqrt(jnp.mean(x * x, axis=-1, keepdims=True) + NORM_EPS)
        o_ref[...] = ((x * rstd) * g_ref[...]).astype(o_ref.dtype)

    return pl.pallas_call(
        body,
        out_shape=jax.ShapeDtypeStruct((r, d), BF16),
        grid=(r // tr,),
        in_specs=[pl.BlockSpec((tr, d), lambda i: (i, 0)), pl.BlockSpec((1, d), lambda i: (0, 0))],
        out_specs=pl.BlockSpec((tr, d), lambda i: (i, 0)),
        compiler_params=_cp("parallel"),
        name=name,
    )(h, g)


def _rms_bwd(h, g, dy, dres, name):
    r, d = h.shape
    tr = _tile(r, 512, 8)
    need_dh = dres is not None

    def body(*refs):
        if need_dh:
            h_ref, g_ref, dy_ref, dres_ref, dh_ref, dg_ref = refs
        else:
            h_ref, g_ref, dy_ref, dg_ref = refs
        i = pl.program_id(0)
        x = h_ref[...]
        rstd = lax.rsqrt(jnp.mean(x * x, axis=-1, keepdims=True) + NORM_EPS)
        nrm = x * rstd
        dyv = dy_ref[...].astype(F32)
        part = jnp.sum(dyv * nrm, axis=0, keepdims=True)

        @pl.when(i == 0)
        def _():
            dg_ref[...] = part

        @pl.when(i > 0)
        def _():
            dg_ref[...] += part

        if need_dh:
            gy = dyv * g_ref[...]
            dx = rstd * (gy - nrm * jnp.mean(gy * nrm, axis=-1, keepdims=True))
            dh_ref[...] = dres_ref[...] + dx

    row = pl.BlockSpec((tr, d), lambda i: (i, 0))
    vec = pl.BlockSpec((1, d), lambda i: (0, 0))
    if need_dh:
        return pl.pallas_call(
            body,
            out_shape=(jax.ShapeDtypeStruct((r, d), F32), jax.ShapeDtypeStruct((1, d), F32)),
            grid=(r // tr,),
            in_specs=[row, vec, row, row],
            out_specs=(row, vec),
            compiler_params=_cp("arbitrary"),
            name=name,
        )(h, g, dy, dres)
    return None, pl.pallas_call(
        body,
        out_shape=jax.ShapeDtypeStruct((1, d), F32),
        grid=(r // tr,),
        in_specs=[row, vec, row],
        out_specs=vec,
        compiler_params=_cp("arbitrary"),
        name=name,
    )(h, g, dy)


def _loss_head(h, g, target, name):
    r, d = h.shape
    tr = _tile(r, 512, 8)

    def body(h_ref, g_ref, t_ref, loss_ref, dh_ref, dg_ref):
        i = pl.program_id(0)
        x = h_ref[...]
        gv = g_ref[...]
        rstd = lax.rsqrt(jnp.mean(x * x, axis=-1, keepdims=True) + NORM_EPS)
        nrm = x * rstd
        err = nrm * gv - t_ref[...]
        lpart = 0.5 * jnp.sum(jnp.mean(err * err, axis=-1, keepdims=True), axis=0, keepdims=True)
        dyv = err * (1.0 / d)
        gpart = jnp.sum(dyv * nrm, axis=0, keepdims=True)

        @pl.when(i == 0)
        def _():
            loss_ref[...] = jnp.broadcast_to(lpart, loss_ref.shape)
            dg_ref[...] = gpart

        @pl.when(i > 0)
        def _():
            loss_ref[...] += jnp.broadcast_to(lpart, loss_ref.shape)
            dg_ref[...] += gpart

        gy = dyv * gv
        dh_ref[...] = rstd * (gy - nrm * jnp.mean(gy * nrm, axis=-1, keepdims=True))

    row = pl.BlockSpec((tr, d), lambda i: (i, 0))
    vec = pl.BlockSpec((1, d), lambda i: (0, 0))
    lsp = pl.BlockSpec((1, 128), lambda i: (0, 0))
    return pl.pallas_call(
        body,
        out_shape=(jax.ShapeDtypeStruct((1, 128), F32), jax.ShapeDtypeStruct((r, d), F32), jax.ShapeDtypeStruct((1, d), F32)),
        grid=(r // tr,),
        in_specs=[row, vec, row],
        out_specs=(lsp, row, vec),
        compiler_params=_cp("arbitrary"),
        name=name,
    )(h, g, target)


def _lane(shape):
    return lax.broadcasted_iota(jnp.int32, shape, 1)


def _head_tile(pair_tile, odd):
    return pltpu.roll(pair_tile, 64, 1) if odd else pair_tile


def _pair_tile(even_tile, odd_tile):
    lane = _lane(even_tile.shape)
    return jnp.where(lane < 64, even_tile, pltpu.roll(odd_tile, 64, 1))


def _pad_heads(xs, scales, out_widths, name):
    r = xs[0][0].shape[0]
    tr = _tile(r, 256, 8)
    n_in = len(xs)

    def body(*refs):
        for idx in range(n_in):
            x_ref, o_ref = refs[idx], refs[n_in + idx]
            nh = xs[idx][2]
            for p in range(nh // 2):
                t = x_ref[:, p * 128:(p + 1) * 128].astype(F32) * scales[idx]
                lane = _lane(t.shape)
                o_ref[:, (2 * p) * HP:(2 * p + 1) * HP] = jnp.where(lane < 64, t, 0.0).astype(o_ref.dtype)
                o_ref[:, (2 * p + 1) * HP:(2 * p + 2) * HP] = jnp.where(lane < 64, pltpu.roll(t, 64, 1), 0.0).astype(o_ref.dtype)

    in_specs, args, out_specs, out_shape = [], [], [], []
    for (arr, c0, nh), w in zip(xs, out_widths):
        wcols = nh * 64
        assert c0 % wcols == 0 or c0 == 0
        blk = c0 // wcols if wcols else 0
        in_specs.append(pl.BlockSpec((tr, wcols), functools.partial(lambda i, b: (i, b), b=blk)))
        args.append(arr)
        out_specs.append(pl.BlockSpec((tr, w), lambda i: (i, 0)))
        out_shape.append(jax.ShapeDtypeStruct((r, w), BF16))
    return pl.pallas_call(
        body,
        out_shape=tuple(out_shape),
        grid=(r // tr,),
        in_specs=in_specs,
        out_specs=tuple(out_specs),
        compiler_params=_cp("parallel"),
        name=name,
    )(*args)


def _unpad_heads(xs, out_dtype, name):
    r = xs[0].shape[0]
    tr = _tile(r, 256, 8)
    nhs = [x.shape[1] // HP for x in xs]
    total = sum(nhs) * 64

    def body(*refs):
        o_ref = refs[-1]
        col = 0
        for x_ref, nh in zip(refs[:-1], nhs):
            for p in range(nh // 2):
                ev = x_ref[:, (2 * p) * HP:(2 * p + 1) * HP].astype(F32)
                od = x_ref[:, (2 * p + 1) * HP:(2 * p + 2) * HP].astype(F32)
                o_ref[:, col:col + 128] = _pair_tile(ev, od).astype(o_ref.dtype)
                col += 128

    return pl.pallas_call(
        body,
        out_shape=jax.ShapeDtypeStruct((r, total), out_dtype),
        grid=(r // tr,),
        in_specs=[pl.BlockSpec((tr, x.shape[1]), lambda i: (i, 0)) for x in xs],
        out_specs=pl.BlockSpec((tr, total), lambda i: (i, 0)),
        compiler_params=_cp("parallel"),
        name=name,
    )(*xs)


def _bf16_split3(c):
    hi = c.astype(BF16).astype(F32)
    r1 = c - hi
    mid = r1.astype(BF16).astype(F32)
    lo = (r1 - mid).astype(BF16).astype(F32)
    return hi, mid, lo


def _log_sigmoid(z):
    return jnp.minimum(z, 0.0) - jnp.log(1.0 + jnp.exp(-jnp.abs(z)))


def _prep_fox_fwd(proj, bfg, name):
    s = proj.shape[0]
    ts = _tile(s, 256, 8)
    fcol = 3 * D_MIX + D_MEMQ

    def body(p_ref, b_ref, q_ref, k_ref, v_ref, qm_ref, carry_ref):
        i = pl.program_id(0)

        @pl.when(i == 0)
        def _():
            carry_ref[...] = jnp.zeros_like(carry_ref)

        lane = _lane((ts, 128))
        z = p_ref[:, fcol:fcol + 128] + b_ref[...]
        logf = jnp.where(lane < N_MIX, _log_sigmoid(z), 0.0)
        rr = lax.broadcasted_iota(jnp.int32, (ts, ts), 0)
        cc = lax.broadcasted_iota(jnp.int32, (ts, ts), 1)
        tri = jnp.where(cc <= rr, 1.0, 0.0).astype(F32)
        c = jnp.dot(tri, logf, preferred_element_type=F32, precision=lax.Precision.HIGHEST) + carry_ref[0:1, :]
        carry_ref[...] = jnp.broadcast_to(c[ts - 1:ts, :], carry_ref.shape)
        for hh in range(N_MIX):
            p, odd = hh // 2, hh % 2
            ch = jnp.sum(jnp.where(lane == hh, c, 0.0), axis=-1, keepdims=True)
            hi, mid, lo = _bf16_split3(ch)
            qt = _head_tile(p_ref[:, p * 128:(p + 1) * 128], odd) * SCALE
            kt = _head_tile(p_ref[:, D_MIX + p * 128:D_MIX + (p + 1) * 128], odd)
            vt = _head_tile(p_ref[:, 2 * D_MIX + p * 128:2 * D_MIX + (p + 1) * 128], odd)
            qa = jnp.where(lane == 64, hi, jnp.where(lane == 65, mid, jnp.where(lane == 66, lo, jnp.where(lane < 70, 1.0, 0.0))))
            ka = jnp.where(lane < 67, 1.0, jnp.where(lane == 67, -hi, jnp.where(lane == 68, -mid, jnp.where(lane == 69, -lo, 0.0))))
            q_ref[:, hh * HP:(hh + 1) * HP] = jnp.where(lane < 64, qt, qa).astype(BF16)
            k_ref[:, hh * HP:(hh + 1) * HP] = jnp.where(lane < 64, kt, ka).astype(BF16)
            v_ref[:, hh * HP:(hh + 1) * HP] = jnp.where(lane < 64, vt, 0.0).astype(BF16)
        for hh in range(N_MEM):
            p, odd = hh // 2, hh % 2
            t = _head_tile(p_ref[:, 3 * D_MIX + p * 128:3 * D_MIX + (p + 1) * 128], odd) * SCALE
            qm_ref[:, hh * HP:(hh + 1) * HP] = jnp.where(lane < 64, t, 0.0).astype(BF16)

    wmix, wmem = N_MIX * HP, N_MEM * HP
    return pl.pallas_call(
        body,
        out_shape=(jax.ShapeDtypeStruct((s, wmix), BF16),) * 3 + (jax.ShapeDtypeStruct((s, wmem), BF16),),
        grid=(s // ts,),
        in_specs=[pl.BlockSpec((ts, proj.shape[1]), lambda i: (i, 0)), pl.BlockSpec((1, 128), lambda i: (0, 0))],
        out_specs=(pl.BlockSpec((ts, wmix), lambda i: (i, 0)),) * 3 + (pl.BlockSpec((ts, wmem), lambda i: (i, 0)),),
        scratch_shapes=[pltpu.VMEM((8, 128), F32)],
        compiler_params=_cp("arbitrary"),
        name=name,
    )(proj, bfg)


def _prep_fox_bwd(dq, dk, dv, dqm, proj, bfg, name):
    s = proj.shape[0]
    ts = _tile(s, 256, 8)
    nb = s // ts
    fcol = 3 * D_MIX + D_MEMQ

    def body(dq_ref, dk_ref, dv_ref, dqm_ref, p_ref, b_ref, o_ref, db_ref, carry_ref):
        i = pl.program_id(0)

        @pl.when(i == 0)
        def _():
            carry_ref[...] = jnp.zeros_like(carry_ref)
            db_ref[...] = jnp.zeros_like(db_ref)

        lane = _lane((ts, 128))
        dc = jnp.zeros((ts, 128), F32)
        for p in range(N_MIX // 2):
            tq, tk, tv = [], [], []
            for odd in (0, 1):
                hh = 2 * p + odd
                dqt = dq_ref[:, hh * HP:(hh + 1) * HP]
                dkt = dk_ref[:, hh * HP:(hh + 1) * HP]
                col = jnp.sum(jnp.where(lane == 64, dqt, 0.0) - jnp.where(lane == 67, dkt, 0.0), axis=-1, keepdims=True)
                dc = dc + jnp.where(lane == hh, col, 0.0)
                tq.append(dqt)
                tk.append(dkt)
                tv.append(dv_ref[:, hh * HP:(hh + 1) * HP])
            o_ref[:, p * 128:(p + 1) * 128] = (_pair_tile(tq[0], tq[1]) * SCALE).astype(o_ref.dtype)
            o_ref[:, D_MIX + p * 128:D_MIX + (p + 1) * 128] = _pair_tile(tk[0], tk[1]).astype(o_ref.dtype)
            o_ref[:, 2 * D_MIX + p * 128:2 * D_MIX + (p + 1) * 128] = _pair_tile(tv[0], tv[1]).astype(o_ref.dtype)
        for p in range(N_MEM // 2):
            ev = dqm_ref[:, (2 * p) * HP:(2 * p + 1) * HP]
            od = dqm_ref[:, (2 * p + 1) * HP:(2 * p + 2) * HP]
            o_ref[:, 3 * D_MIX + p * 128:3 * D_MIX + (p + 1) * 128] = (_pair_tile(ev, od) * SCALE).astype(o_ref.dtype)
        rr = lax.broadcasted_iota(jnp.int32, (ts, ts), 0)
        cc = lax.broadcasted_iota(jnp.int32, (ts, ts), 1)
        triu = jnp.where(cc >= rr, 1.0, 0.0).astype(F32)
        dlogf = jnp.dot(triu, dc, preferred_element_type=F32, precision=lax.Precision.HIGHEST) + carry_ref[0:1, :]
        carry_ref[...] = jnp.broadcast_to(dlogf[0:1, :], carry_ref.shape)
        z = p_ref[:, fcol:fcol + 128] + b_ref[...]
        dz = jnp.where(lane < N_MIX, dlogf / (1.0 + jnp.exp(z)), 0.0)
        o_ref[:, fcol:fcol + 128] = dz.astype(o_ref.dtype)
        db_ref[...] += jnp.sum(dz, axis=0, keepdims=True)

    wmix, wmem = N_MIX * HP, N_MEM * HP
    rev = lambda i: (nb - 1 - i, 0)
    return pl.pallas_call(
        body,
        out_shape=(jax.ShapeDtypeStruct(proj.shape, BF16), jax.ShapeDtypeStruct((1, 128), F32)),
        grid=(nb,),
        in_specs=[pl.BlockSpec((ts, wmix), rev)] * 3 + [pl.BlockSpec((ts, wmem), rev), pl.BlockSpec((ts, proj.shape[1]), rev),
                                                         pl.BlockSpec((1, 128), lambda i: (0, 0))],
        out_specs=(pl.BlockSpec((ts, proj.shape[1]), rev), pl.BlockSpec((1, 128), lambda i: (0, 0))),
        scratch_shapes=[pltpu.VMEM((8, 128), F32)],
        compiler_params=_cp("arbitrary"),
        name=name,
    )(dq, dk, dv, dqm, proj, bfg)


def _rope_partner(x):
    lane = _lane(x.shape)
    return jnp.where((lane % 64) < 32, pltpu.roll(x, 96, 1), pltpu.roll(x, 32, 1))


def _prep_dil_fwd(proj, cos2, sin2, name):
    s = proj.shape[0]
    ts = _tile(s, 256, 8)

    def body(p_ref, c_ref, s_ref, q_ref, k_ref, v_ref, qm_ref):
        lane = _lane((ts, 128))
        cosv, sinv = c_ref[...], s_ref[...]
        for p in range(N_MIX // 2):
            xq = p_ref[:, p * 128:(p + 1) * 128]
            xk = p_ref[:, D_MIX + p * 128:D_MIX + (p + 1) * 128]
            xv = p_ref[:, 2 * D_MIX + p * 128:2 * D_MIX + (p + 1) * 128]
            yq = (xq * cosv + _rope_partner(xq) * sinv) * SCALE
            yk = xk * cosv + _rope_partner(xk) * sinv
            for odd in (0, 1):
                hh = 2 * p + odd
                q_ref[:, hh * HP:(hh + 1) * HP] = jnp.where(lane < 64, _head_tile(yq, odd), 0.0).astype(BF16)
                k_ref[:, hh * HP:(hh + 1) * HP] = jnp.where(lane < 64, _head_tile(yk, odd), 0.0).astype(BF16)
                v_ref[:, hh * HP:(hh + 1) * HP] = jnp.where(lane < 64, _head_tile(xv, odd), 0.0).astype(BF16)
        for p in range(N_MEM // 2):
            t = p_ref[:, 3 * D_MIX + p * 128:3 * D_MIX + (p + 1) * 128] * SCALE
            for odd in (0, 1):
                hh = 2 * p + odd
                qm_ref[:, hh * HP:(hh + 1) * HP] = jnp.where(lane < 64, _head_tile(t, odd), 0.0).astype(BF16)

    wmix, wmem = N_MIX * HP, N_MEM * HP
    return pl.pallas_call(
        body,
        out_shape=(jax.ShapeDtypeStruct((s, wmix), BF16),) * 3 + (jax.ShapeDtypeStruct((s, wmem), BF16),),
        grid=(s // ts,),
        in_specs=[pl.BlockSpec((ts, proj.shape[1]), lambda i: (i, 0)), pl.BlockSpec((ts, 128), lambda i: (i, 0)),
                  pl.BlockSpec((ts, 128), lambda i: (i, 0))],
        out_specs=(pl.BlockSpec((ts, wmix), lambda i: (i, 0)),) * 3 + (pl.BlockSpec((ts, wmem), lambda i: (i, 0)),),
        compiler_params=_cp("parallel"),
        name=name,
    )(proj, cos2, sin2)


def _prep_dil_bwd(dq, dk, dv, dqm, cos2, sin2, name):
    s = dq.shape[0]
    ts = _tile(s, 256, 8)

    def body(dq_ref, dk_ref, dv_ref, dqm_ref, c_ref, s_ref, o_ref):
        cosv, sinv = c_ref[...], s_ref[...]
        for p in range(N_MIX // 2):
            e, o = 2 * p, 2 * p + 1
            dyq = _pair_tile(dq_ref[:, e * HP:(e + 1) * HP], dq_ref[:, o * HP:(o + 1) * HP]) * SCALE
            dyk = _pair_tile(dk_ref[:, e * HP:(e + 1) * HP], dk_ref[:, o * HP:(o + 1) * HP])
            dxv = _pair_tile(dv_ref[:, e * HP:(e + 1) * HP], dv_ref[:, o * HP:(o + 1) * HP])
            o_ref[:, p * 128:(p + 1) * 128] = (dyq * cosv - _rope_partner(dyq) * sinv).astype(o_ref.dtype)
            o_ref[:, D_MIX + p * 128:D_MIX + (p + 1) * 128] = (dyk * cosv - _rope_partner(dyk) * sinv).astype(o_ref.dtype)
            o_ref[:, 2 * D_MIX + p * 128:2 * D_MIX + (p + 1) * 128] = dxv.astype(o_ref.dtype)
        for p in range(N_MEM // 2):
            e, o = 2 * p, 2 * p + 1
            t = _pair_tile(dqm_ref[:, e * HP:(e + 1) * HP], dqm_ref[:, o * HP:(o + 1) * HP]) * SCALE
            o_ref[:, 3 * D_MIX + p * 128:3 * D_MIX + (p + 1) * 128] = t.astype(o_ref.dtype)

    wmix, wmem = N_MIX * HP, N_MEM * HP
    row = lambda w: pl.BlockSpec((ts, w), lambda i: (i, 0))
    return pl.pallas_call(
        body,
        out_shape=jax.ShapeDtypeStruct((s, DIL_IN), BF16),
        grid=(s // ts,),
        in_specs=[row(wmix)] * 3 + [row(wmem), row(128), row(128)],
        out_specs=row(DIL_IN),
        compiler_params=_cp("parallel"),
        name=name,
    )(dq, dk, dv, dqm, cos2, sin2)


def _mask_scores(sc, mode, row0, col0):
    if mode == "full":
        return sc
    tq, tk = sc.shape
    dist = (row0 + lax.broadcasted_iota(jnp.int32, (tq, tk), 0)) - (col0 + lax.broadcasted_iota(jnp.int32, (tq, tk), 1))
    if mode == "causal":
        return jnp.where(dist >= 0, sc, NEG)
    in1 = dist <= 128
    in2 = (dist <= 512) & ((dist & 3) == 0)
    in3 = (dist <= DIL_MAX) & ((dist & 15) == 0)
    cnt = in1.astype(jnp.int32) + in2.astype(jnp.int32) + in3.astype(jnp.int32)
    bias = jnp.where(cnt == 3, LOG3, jnp.where(cnt == 2, LOG2, 0.0))
    return jnp.where((dist >= 0) & (cnt > 0), sc + bias, NEG)


def _band(mode, blk):
    return DIL_MAX // blk if mode == "dilated" else None


def _flash_fwd(q, k, v, mode, name):
    sq, w = q.shape
    sk = k.shape[0]
    nh = w // HP
    tq = _tile(sq, FLASH_BLK, 8)
    tk = sk if mode == "full" else tq
    nq, nk = sq // tq, sk // tk
    band = _band(mode, tk)

    def kidx(h, qi, kj):
        if mode == "full":
            return (kj, h)
        lo = jnp.maximum(qi - band, 0) if mode == "dilated" else 0
        return (jnp.clip(kj, lo, qi), h)

    def body(q_ref, k_ref, v_ref, o_ref, lse_ref, m_ref, l_ref, acc_ref):
        qi, kj = pl.program_id(1), pl.program_id(2)

        @pl.when(kj == 0)
        def _():
            m_ref[...] = jnp.full_like(m_ref, -jnp.inf)
            l_ref[...] = jnp.zeros_like(l_ref)
            acc_ref[...] = jnp.zeros_like(acc_ref)

        def step():
            sc = lax.dot_general(q_ref[...], k_ref[...], _DIMS["nt"], preferred_element_type=F32)
            sc = _mask_scores(sc, mode, qi * tq, kj * tk)
            m_prev = m_ref[...]
            m_new = jnp.maximum(m_prev, jnp.max(sc, axis=-1, keepdims=True))
            alpha = jnp.exp(m_prev - m_new)
            p = jnp.exp(sc - m_new)
            l_ref[...] = alpha * l_ref[...] + jnp.sum(p, axis=-1, keepdims=True)
            acc_ref[...] = alpha * acc_ref[...] + jnp.dot(p.astype(BF16), v_ref[...], preferred_element_type=F32)
            m_ref[...] = m_new

        if mode == "full":
            step()
        elif mode == "causal":
            pl.when(kj <= qi)(step)
        else:
            pl.when((kj <= qi) & (kj >= qi - band))(step)

        @pl.when(kj == nk - 1)
        def _():
            l = l_ref[...]
            o_ref[...] = (acc_ref[...] / l).astype(o_ref.dtype)
            lse_ref[...] = jnp.broadcast_to(m_ref[...] + jnp.log(l), lse_ref.shape)

    return pl.pallas_call(
        body,
        out_shape=(jax.ShapeDtypeStruct((sq, w), BF16), jax.ShapeDtypeStruct((sq, w), F32)),
        grid=(nh, nq, nk),
        in_specs=[pl.BlockSpec((tq, HP), lambda h, qi, kj: (qi, h)), pl.BlockSpec((tk, HP), kidx), pl.BlockSpec((tk, HP), kidx)],
        out_specs=(pl.BlockSpec((tq, HP), lambda h, qi, kj: (qi, h)), pl.BlockSpec((tq, HP), lambda h, qi, kj: (qi, h))),
        scratch_shapes=[pltpu.VMEM((tq, 1), F32), pltpu.VMEM((tq, 1), F32), pltpu.VMEM((tq, HP), F32)],
        compiler_params=_cp("parallel", "parallel", "arbitrary"),
        name=name,
    )(q, k, v)


def _flash_bwd(q, k, v, o, do, lse, mode, name):
    sq, w = q.shape
    sk = k.shape[0]
    nh = w // HP
    tq = _tile(sq, FLASH_BLK, 8)
    tk = sk if mode == "full" else tq
    nq, nk = sq // tq, sk // tk
    band = _band(mode, tk)

    def qidx(h, kj, qi):
        if mode == "full":
            return (qi, h)
        hi = jnp.minimum(kj + band, nq - 1) if mode == "dilated" else nq - 1
        return (jnp.clip(qi, kj, hi), h)

    def body(q_ref, k_ref, v_ref, o_ref, do_ref, lse_ref, dq_ref, dk_ref, dv_ref):
        kj, qi = pl.program_id(1), pl.program_id(2)

        @pl.when((kj == 0) & (qi == 0))
        def _():
            dq_ref[...] = jnp.zeros_like(dq_ref)

        @pl.when(qi == 0)
        def _():
            dk_ref[...] = jnp.zeros_like(dk_ref)
            dv_ref[...] = jnp.zeros_like(dv_ref)

        def step():
            qv, kv, dov = q_ref[...], k_ref[...], do_ref[...]
            sc = lax.dot_general(qv, kv, _DIMS["nt"], preferred_element_type=F32)
            sc = _mask_scores(sc, mode, qi * tq, kj * tk)
            p = jnp.exp(sc - lse_ref[:, 0:1])
            delta = jnp.sum(dov.astype(F32) * o_ref[...].astype(F32), axis=-1, keepdims=True)
            dp = lax.dot_general(dov, v_ref[...], _DIMS["nt"], preferred_element_type=F32)
            ds = (p * (dp - delta)).astype(BF16)
            dv_ref[...] += lax.dot_general(p.astype(BF16), dov, _DIMS["tn"], preferred_element_type=F32)
            dk_ref[...] += lax.dot_general(ds, qv, _DIMS["tn"], preferred_element_type=F32)
            rows = pl.ds(pl.multiple_of(qi * tq, tq), tq)
            dq_ref[rows, :] += jnp.dot(ds, kv, preferred_element_type=F32)

        if mode == "full":
            step()
        elif mode == "causal":
            pl.when(qi >= kj)(step)
        else:
            pl.when((qi >= kj) & (qi <= kj + band))(step)

    qspec = pl.BlockSpec((tq, HP), qidx)
    kspec = pl.BlockSpec((tk, HP), lambda h, kj, qi: (kj, h))
    return pl.pallas_call(
        body,
        out_shape=(jax.ShapeDtypeStruct((sq, w), F32), jax.ShapeDtypeStruct((sk, w), F32), jax.ShapeDtypeStruct((sk, w), F32)),
        grid=(nh, nk, nq),
        in_specs=[qspec, kspec, kspec, qspec, qspec, qspec],
        out_specs=(pl.BlockSpec((sq, HP), lambda h, kj, qi: (0, h)), kspec, kspec),
        compiler_params=_cp("parallel", "arbitrary", "arbitrary"),
        name=name,
    )(q, k, v, o, do, lse)


def _conv_rc(s):
    return _tile(s, 256, 8)


def _shift_down(x, prev8, nrows):
    rows = lax.broadcasted_iota(jnp.int32, x.shape, 0)
    out = pltpu.roll(x, nrows, 0)
    for i in range(nrows):
        out = jnp.where(rows == i, prev8[8 - nrows + i:8 - nrows + i + 1, :], out)
    return out


def _shift_up(x, next8, nrows):
    n = x.shape[0]
    rows = lax.broadcasted_iota(jnp.int32, x.shape, 0)
    out = pltpu.roll(x, n - nrows, 0)
    for i in range(nrows):
        out = jnp.where(rows == n - nrows + i, next8[i:i + 1, :], out)
    return out


def _conv_taps(u_ref, r, rc):
    r0 = pl.multiple_of(r * rc, rc)
    x = u_ref[pl.ds(r0, rc), :]
    p0 = pl.multiple_of(jnp.maximum(r0 - 8, 0), 8)
    prev8 = jnp.where(r > 0, u_ref[pl.ds(p0, 8), :], 0.0)
    return r0, x, _shift_down(x, prev8, 1), _shift_down(x, prev8, 2)


def _conv_fwd(u, cw, cb, name):
    s, f2 = u.shape
    tf = CONV_TF
    nf = f2 // (2 * tf)
    rc = _conv_rc(s)

    def body(u_ref, w_ref, b_ref, a_ref):
        w0, w1, w2, b = w_ref[0:1, :], w_ref[1:2, :], w_ref[2:3, :], b_ref[...]

        def chunk(r, carry):
            r0, x, x1, x2 = _conv_taps(u_ref, r, rc)
            c = b + w0 * x2 + w1 * x1 + w2 * x
            val, gate = c[:, :tf], c[:, tf:]
            a_ref[pl.ds(r0, rc), :] = (gate * jax.nn.sigmoid(gate) * val).astype(a_ref.dtype)
            return carry

        lax.fori_loop(0, s // rc, chunk, 0)

    return pl.pallas_call(
        body,
        out_shape=jax.ShapeDtypeStruct((s, f2 // 2), BF16),
        grid=(nf,),
        in_specs=[pl.BlockSpec((s, 2 * tf), lambda j: (0, j)), pl.BlockSpec((3, 2 * tf), lambda j: (0, j)),
                  pl.BlockSpec((1, 2 * tf), lambda j: (0, j))],
        out_specs=pl.BlockSpec((s, tf), lambda j: (0, j)),
        compiler_params=_cp("parallel"),
        name=name,
    )(u, cw, cb)


def _conv_bwd(u, da, cw, cb, name):
    s, f2 = u.shape
    tf = CONV_TF
    nf = f2 // (2 * tf)
    rc = _conv_rc(s)
    nchunk = s // rc

    def body(u_ref, da_ref, w_ref, b_ref, du_ref, dw_ref, db_ref, next_ref):
        w0, w1, w2, b = w_ref[0:1, :], w_ref[1:2, :], w_ref[2:3, :], b_ref[...]
        next_ref[...] = jnp.zeros_like(next_ref)

        def chunk(it, carry):
            g0, g1, g2, gb = carry
            r = nchunk - 1 - it
            r0, x, x1, x2 = _conv_taps(u_ref, r, rc)
            c = b + w0 * x2 + w1 * x1 + w2 * x
            val, gate = c[:, :tf], c[:, tf:]
            sg = jax.nn.sigmoid(gate)
            dav = da_ref[pl.ds(r0, rc), :]
            dc = jnp.concatenate([dav * (gate * sg), dav * val * (sg * (1.0 + gate * (1.0 - sg)))], axis=1)
            nxt = next_ref[...]
            du = w2 * dc + w1 * _shift_up(dc, nxt, 1) + w0 * _shift_up(dc, nxt, 2)
            du_ref[pl.ds(r0, rc), :] = du.astype(du_ref.dtype)
            next_ref[...] = dc[0:8, :]
            return (g0 + jnp.sum(dc * x2, axis=0, keepdims=True), g1 + jnp.sum(dc * x1, axis=0, keepdims=True),
                    g2 + jnp.sum(dc * x, axis=0, keepdims=True), gb + jnp.sum(dc, axis=0, keepdims=True))

        zero = jnp.zeros((1, 2 * tf), F32)
        g0, g1, g2, gb = lax.fori_loop(0, nchunk, chunk, (zero, zero, zero, zero))
        dw_ref[0:1, :] = g0
        dw_ref[1:2, :] = g1
        dw_ref[2:3, :] = g2
        db_ref[...] = gb

    return pl.pallas_call(
        body,
        out_shape=(jax.ShapeDtypeStruct((s, f2), BF16), jax.ShapeDtypeStruct((3, f2), F32), jax.ShapeDtypeStruct((1, f2), F32)),
        grid=(nf,),
        in_specs=[pl.BlockSpec((s, 2 * tf), lambda j: (0, j)), pl.BlockSpec((s, tf), lambda j: (0, j)),
                  pl.BlockSpec((3, 2 * tf), lambda j: (0, j)), pl.BlockSpec((1, 2 * tf), lambda j: (0, j))],
        out_specs=(pl.BlockSpec((s, 2 * tf), lambda j: (0, j)), pl.BlockSpec((3, 2 * tf), lambda j: (0, j)),
                   pl.BlockSpec((1, 2 * tf), lambda j: (0, j))),
        scratch_shapes=[pltpu.VMEM((8, 2 * tf), F32)],
        compiler_params=_cp("parallel"),
        name=name,
    )(u, da, cw, cb)


def _adamw(w, g, m, v, name):
    r, c = w.shape
    tr = _tile(r, 256, 8) if r % 8 == 0 else r
    c1 = 1.0 - ADAM_B1 ** ADAM_STEP
    c2 = 1.0 - ADAM_B2 ** ADAM_STEP

    def body(w_ref, g_ref, m_ref, v_ref, d_ref, mo_ref, vo_ref):
        gv = g_ref[...]
        mn = ADAM_B1 * m_ref[...] + (1.0 - ADAM_B1) * gv
        vn = ADAM_B2 * v_ref[...] + (1.0 - ADAM_B2) * (gv * gv)
        d_ref[...] = -ADAM_LR * ((mn / c1) / (jnp.sqrt(vn / c2) + ADAM_EPS) + ADAM_WD * w_ref[...])
        mo_ref[...] = mn
        vo_ref[...] = vn

    blk = pl.BlockSpec((tr, c), lambda i: (i, 0))
    shp = jax.ShapeDtypeStruct((r, c), F32)
    return pl.pallas_call(
        body, out_shape=(shp, shp, shp), grid=(r // tr,), in_specs=[blk] * 4, out_specs=(blk,) * 3,
        compiler_params=_cp("parallel"), name=name,
    )(w, g, m, v)


def _sum_rows(parts, out_dtype, name):
    n, r, c = parts.shape
    tr = _tile(r, 256, 8)

    def body(p_ref, o_ref):
        tot = p_ref[0].astype(F32)
        for i in range(1, n):
            tot = tot + p_ref[i].astype(F32)
        o_ref[...] = tot.astype(o_ref.dtype)

    return pl.pallas_call(
        body, out_shape=jax.ShapeDtypeStruct((r, c), out_dtype), grid=(r // tr,),
        in_specs=[pl.BlockSpec((n, tr, c), lambda i: (0, i, 0))], out_specs=pl.BlockSpec((tr, c), lambda i: (i, 0)),
        compiler_params=_cp("parallel"), name=name,
    )(parts)


def _add_pairs(a, b, name):
    n, r, c = a.shape
    tr = _tile(r, 256, 8)

    def body(a_ref, b_ref, o_ref):
        o_ref[...] = (a_ref[...].astype(F32) + b_ref[...].astype(F32)).astype(o_ref.dtype)

    blk = pl.BlockSpec((1, tr, c), lambda j, i: (j, i, 0))
    return pl.pallas_call(
        body, out_shape=jax.ShapeDtypeStruct((n, r, c), BF16), grid=(n, r // tr), in_specs=[blk, blk], out_specs=blk,
        compiler_params=_cp("parallel", "parallel"), name=name,
    )(a, b)


_ANY = pl.BlockSpec(memory_space=pl.ANY)


def _place():
    return lax.axis_index("x"), lax.axis_index("y"), lax.axis_index("c")


def _other_chips(x, y):
    return [(1 - x, y), (x, 1 - y), (1 - x, 1 - y)]


def _gather_shards(shard, name):
    _, rh, w = shard.shape

    def body(in_ref, out_ref, send_sems, recv_sems, local_sem):
        x, y, c = _place()
        me = 2 * x + y
        sib = (x, y, 1 - c)
        chips = _other_chips(x, y)
        mine = pltpu.make_async_copy(in_ref, out_ref.at[me], local_sem)
        mine.start()

        def copy(kk, src, dst, to):
            return pltpu.make_async_remote_copy(src_ref=src, dst_ref=dst, send_sem=send_sems.at[kk], recv_sem=recv_sems.at[kk],
                                                device_id=to, device_id_type=MESH)

        first = [copy(j, in_ref.at[c], out_ref.at[me, c], (cx, cy, c)) for j, (cx, cy) in enumerate(chips)]
        for cp in first:
            cp.start()
        passed = []
        for j, (cx, cy) in enumerate(chips):
            slot = out_ref.at[2 * cx + cy, c]
            copy(j, slot, slot, (x, y, c)).wait_recv()
            fwd = copy(3 + j, slot, slot, sib)
            fwd.start()
            passed.append(fwd)
        for j, (cx, cy) in enumerate(chips):
            slot = out_ref.at[2 * cx + cy, 1 - c]
            copy(3 + j, slot, slot, (x, y, c)).wait_recv()
        for cp in first + passed:
            cp.wait_send()
        mine.wait()

    return pl.pallas_call(
        body,
        out_shape=jax.ShapeDtypeStruct((N_CHIPS, 2, rh, w), shard.dtype),
        in_specs=[_ANY], out_specs=_ANY,
        scratch_shapes=[pltpu.SemaphoreType.DMA((6,)), pltpu.SemaphoreType.DMA((6,)), pltpu.SemaphoreType.DMA],
        name=name,
    )(shard)


def _swap_halves(g, name):
    n, _, rh, w = g.shape

    def body(in_ref, out_ref, send_sems, recv_sems):
        x, y, c = _place()
        copies = [pltpu.make_async_remote_copy(src_ref=in_ref.at[j, 1 - c], dst_ref=out_ref.at[j], send_sem=send_sems.at[j],
                                               recv_sem=recv_sems.at[j], device_id=(x, y, 1 - c), device_id_type=MESH)
                  for j in range(n)]
        for cp in copies:
            cp.start()
        for cp in copies:
            cp.wait()

    return pl.pallas_call(
        body, out_shape=jax.ShapeDtypeStruct((n, rh, w), g.dtype), in_specs=[_ANY], out_specs=_ANY,
        scratch_shapes=[pltpu.SemaphoreType.DMA((n,)), pltpu.SemaphoreType.DMA((n,))], name=name,
    )(g)


def _scatter_to_chips(p, name):
    n, rh, w = p.shape

    def body(in_ref, out_ref, send_sems, recv_sems, local_sem):
        x, y, c = _place()
        me = 2 * x + y
        mine = pltpu.make_async_copy(in_ref.at[me], out_ref.at[me], local_sem)
        mine.start()
        sends = []
        for j, (cx, cy) in enumerate(_other_chips(x, y)):
            cp = pltpu.make_async_remote_copy(src_ref=in_ref.at[2 * cx + cy], dst_ref=out_ref.at[me], send_sem=send_sems.at[j],
                                              recv_sem=recv_sems.at[j], device_id=(cx, cy, c), device_id_type=MESH)
            cp.start()
            sends.append(cp)
        for j, (cx, cy) in enumerate(_other_chips(x, y)):
            slot = out_ref.at[2 * cx + cy]
            pltpu.make_async_remote_copy(src_ref=slot, dst_ref=slot, send_sem=send_sems.at[j], recv_sem=recv_sems.at[j],
                                         device_id=(x, y, c), device_id_type=MESH).wait_recv()
        for cp in sends:
            cp.wait_send()
        mine.wait()

    return pl.pallas_call(
        body, out_shape=jax.ShapeDtypeStruct((n, rh, w), p.dtype), in_specs=[_ANY], out_specs=_ANY,
        scratch_shapes=[pltpu.SemaphoreType.DMA((3,)), pltpu.SemaphoreType.DMA((3,)), pltpu.SemaphoreType.DMA], name=name,
    )(p)


def _join_halves(r, name):
    rh, w = r.shape

    def body(in_ref, out_ref, send_sem, recv_sem, local_sem):
        x, y, c = _place()
        mine = pltpu.make_async_copy(in_ref, out_ref.at[c], local_sem)
        mine.start()
        cp = pltpu.make_async_remote_copy(src_ref=in_ref, dst_ref=out_ref.at[c], send_sem=send_sem, recv_sem=recv_sem,
                                          device_id=(x, y, 1 - c), device_id_type=MESH)
        cp.start()
        other = out_ref.at[1 - c]
        pltpu.make_async_remote_copy(src_ref=other, dst_ref=other, send_sem=send_sem, recv_sem=recv_sem,
                                     device_id=(x, y, c), device_id_type=MESH).wait_recv()
        cp.wait_send()
        mine.wait()

    return pl.pallas_call(
        body, out_shape=jax.ShapeDtypeStruct((2, rh, w), r.dtype), in_specs=[_ANY], out_specs=_ANY,
        scratch_shapes=[pltpu.SemaphoreType.DMA, pltpu.SemaphoreType.DMA, pltpu.SemaphoreType.DMA], name=name,
    )(r)


def _gather_all(small, name):
    r, w = small.shape
    flips = [(dx, dy, dc) for dx in (0, 1) for dy in (0, 1) for dc in (0, 1) if (dx, dy, dc) != (0, 0, 0)]

    def body(in_ref, out_ref, send_sems, recv_sems, local_sem):
        x, y, c = _place()
        me = 4 * x + 2 * y + c
        mine = pltpu.make_async_copy(in_ref, out_ref.at[me], local_sem)
        mine.start()
        sends = []
        for j, (dx, dy, dc) in enumerate(flips):
            to = (x ^ dx, y ^ dy, c ^ dc)
            cp = pltpu.make_async_remote_copy(src_ref=in_ref, dst_ref=out_ref.at[me], send_sem=send_sems.at[j],
                                              recv_sem=recv_sems.at[j], device_id=to, device_id_type=MESH)
            cp.start()
            sends.append(cp)
        for j, (dx, dy, dc) in enumerate(flips):
            slot = out_ref.at[4 * (x ^ dx) + 2 * (y ^ dy) + (c ^ dc)]
            pltpu.make_async_remote_copy(src_ref=slot, dst_ref=slot, send_sem=send_sems.at[j], recv_sem=recv_sems.at[j],
                                         device_id=(x, y, c), device_id_type=MESH).wait_recv()
        for cp in sends:
            cp.wait_send()
        mine.wait()

    return pl.pallas_call(
        body, out_shape=jax.ShapeDtypeStruct((8, r, w), small.dtype), in_specs=[_ANY], out_specs=_ANY,
        scratch_shapes=[pltpu.SemaphoreType.DMA((7,)), pltpu.SemaphoreType.DMA((7,)), pltpu.SemaphoreType.DMA], name=name,
    )(small)


_BIG = ("w_in_fox", "w_in_dil", "w_mem_kv", "w_out", "w_up", "w_down")
_BIG_SHARD_SHAPES = {
    "w_in_fox": (2, D_MODEL, FOX_IN // N_CHIPS), "w_in_dil": (2, D_MODEL, DIL_IN // N_CHIPS),
    "w_mem_kv": (DEPTH, D_MODEL // N_CHIPS, 2 * D_MEMQ), "w_out": (DEPTH, D_MODEL // N_CHIPS, D_MODEL),
    "w_up": (DEPTH, D_MODEL, 2 * D_FF // N_CHIPS), "w_down": (DEPTH, D_FF // N_CHIPS, D_MODEL),
}
_CONVW_SHARD = (DEPTH, 3, 2 * D_FF // N_CHIPS)
_COL_SHARDED = ("w_in_fox", "w_in_dil", "w_up")
_HALF_ROWS_MULT = 256


def _flat_rows():
    n = sum(int(np.prod(s)) for s in _BIG_SHARD_SHAPES.values())
    rows = -(-n // FLAT_W)
    half = -(-rows // (2 * _HALF_ROWS_MULT)) * _HALF_ROWS_MULT
    return n, half


def _pack_flat(parts, dtype):
    n, half = _flat_rows()
    flat = jnp.concatenate([p.reshape(-1).astype(dtype) for p in parts])
    flat = jnp.pad(flat, (0, 2 * half * FLAT_W - flat.shape[0]))
    return flat.reshape(2, half, FLAT_W)


def _unpack_flat(flat):
    lead = flat.shape[:-1]
    out, off = {}, 0
    for name in _BIG:
        shp = _BIG_SHARD_SHAPES[name]
        n = int(np.prod(shp))
        out[name] = flat[..., off:off + n].reshape(lead + shp)
        off += n
    return out


_SMALL = (("norm_mix", (DEPTH, D_MODEL)), ("norm_mem", (DEPTH, D_MODEL)), ("norm_ffn", (DEPTH, D_MODEL)),
          ("conv_b", (DEPTH, 2 * D_FF)), ("norm_final", (D_MODEL,)), ("b_forget", (2, N_MIX)), ("conv_w", (DEPTH, 3, 2 * D_FF)))


def _pack_small(vals, spec):
    flat = jnp.concatenate([vals[n].reshape(-1).astype(F32) for n, _ in spec])
    rows = -(-flat.shape[0] // (8 * 128)) * 8
    return jnp.pad(flat, (0, rows * 128 - flat.shape[0])).reshape(rows, 128)


def _unpack_small(buf, spec):
    flat, out, off = buf.reshape(-1), {}, 0
    for n, shp in spec:
        k = int(np.prod(shp))
        out[n] = flat[off:off + k].reshape(shp)
        off += k
    return out


def _interleave_cols(a):
    lead = a.shape[:-1]
    nf = D_FF // CONV_TF
    return a.reshape(lead + (2, nf, CONV_TF)).swapaxes(-3, -2).reshape(lead + (2 * D_FF,))


def _deinterleave_cols(a):
    lead = a.shape[:-1]
    nf = D_FF // CONV_TF
    return a.reshape(lead + (nf, 2, CONV_TF)).swapaxes(-3, -2).reshape(lead + (2 * D_FF,))


def _fox_cols_to_kernel(w):
    qkv, f, qm = w[:, :3 * D_MIX], w[:, 3 * D_MIX:3 * D_MIX + N_MIX], w[:, 3 * D_MIX + N_MIX:]
    return jnp.concatenate([qkv, qm, f, jnp.zeros((w.shape[0], 128 - N_MIX), w.dtype)], axis=1)


def _fox_cols_from_kernel(w):
    qkv, qm, f = w[:, :3 * D_MIX], w[:, 3 * D_MIX:3 * D_MIX + D_MEMQ], w[:, 3 * D_MIX + D_MEMQ:3 * D_MIX + D_MEMQ + N_MIX]
    return jnp.concatenate([qkv, f, qm], axis=1)


def _rope_pair_tables(s):
    inv = 1.0 / (ROPE_THETA ** (jnp.arange(0, HEAD_DIM, 2, dtype=F32) / HEAD_DIM))
    ang = jnp.arange(s, dtype=F32)[:, None] * inv[None, :]
    cos, sin = jnp.cos(ang), jnp.sin(ang)
    return jnp.concatenate([cos, cos, cos, cos], axis=1), jnp.concatenate([-sin, sin, -sin, sin], axis=1)


def _local_step(x, mem, target, small, wts):
    s = x.shape[0]
    cos2, sin2 = _rope_pair_tables(s)
    saved = []
    h = x
    for l in range(DEPTH):
        fox = l % 2 == 0
        slot = l // 2
        tag = f"L{l}"
        g_mix, g_mem, g_ffn = (small[n][l:l + 1] for n in ("norm_mix", "norm_mem", "norm_ffn"))
        xn = _rms_fwd(h, g_mix, f"rms_mix_{tag}")
        proj = _matmul(xn, wts["w_in"][l], "nn", F32, f"mm_in_{tag}")
        if fox:
            bfg = jnp.pad(small["b_forget"][slot:slot + 1], ((0, 0), (0, 128 - N_MIX)))
            qp, kp, vp, qmp = _prep_fox_fwd(proj, bfg, f"prep_fox_{tag}")
        else:
            bfg = None
            qp, kp, vp, qmp = _prep_dil_fwd(proj, cos2, sin2, f"prep_dil_{tag}")
        mn = _rms_fwd(mem, g_mem, f"rms_mem_{tag}")
        kvm = _matmul(mn, wts["w_mem_kv"][l], "nn", F32, f"mm_memkv_{tag}")
        kmp, vmp = _pad_heads([(kvm, 0, N_MEM), (kvm, D_MEMQ, N_MEM)], (1.0, 1.0), (N_MEM * HP,) * 2, f"pad_memkv_{tag}")
        mode = "causal" if fox else "dilated"
        o_mix, lse_mix = _flash_fwd(qp, kp, vp, mode, f"flash_{mode}_fwd_{tag}")
        o_mem, lse_mem = _flash_fwd(qmp, kmp, vmp, "full", f"flash_mem_fwd_{tag}")
        heads = _unpad_heads([o_mix, o_mem], BF16, f"unpad_heads_{tag}")
        h_mid = _matmul(heads, wts["w_out"][l], "nn", F32, f"mm_out_{tag}", residual=h)
        xn2 = _rms_fwd(h_mid, g_ffn, f"rms_ffn_{tag}")
        u = _matmul(xn2, wts["w_up"][l], "nn", F32, f"mm_up_{tag}")
        act = _conv_fwd(u, wts["conv_w"][l], wts["conv_b"][l], f"conv_fwd_{tag}")
        h_out = _matmul(act, wts["w_down"][l], "nn", F32, f"mm_down_{tag}", residual=h_mid)
        saved.append(dict(h=h, xn=xn, proj=proj, bfg=bfg, qp=qp, kp=kp, vp=vp, qmp=qmp, mn=mn, kmp=kmp, vmp=vmp, o_mix=o_mix,
                          lse_mix=lse_mix, o_mem=o_mem, lse_mem=lse_mem, heads=heads, h_mid=h_mid, xn2=xn2, u=u, act=act))
        h = h_out

    loss_blk, dh, dg_final = _loss_head(h, small["norm_final"].reshape(1, D_MODEL), target, "loss_head")

    grads = {k: [None] * DEPTH for k in ("w_in", "w_mem_kv", "w_out", "w_up", "w_down", "conv_w", "conv_b", "norm_mix", "norm_mem",
                                         "norm_ffn")}
    grads["b_forget"] = [None, None]
    for l in reversed(range(DEPTH)):
        fox = l % 2 == 0
        tag = f"L{l}"
        sv = saved[l]
        g_mix, g_mem, g_ffn = (small[n][l:l + 1] for n in ("norm_mix", "norm_mem", "norm_ffn"))
        da = _matmul(dh, wts["w_down"][l], "nt", F32, f"mm_da_{tag}")
        grads["w_down"][l] = _matmul(sv["act"], dh, "tn", BF16, f"mm_dwdown_{tag}")
        du, dcw, dcb = _conv_bwd(sv["u"], da, wts["conv_w"][l], wts["conv_b"][l], f"conv_bwd_{tag}")
        grads["conv_w"][l], grads["conv_b"][l] = dcw, dcb
        dxn2 = _matmul(du, wts["w_up"][l], "nt", F32, f"mm_dxn2_{tag}")
        grads["w_up"][l] = _matmul(sv["xn2"], du, "tn", BF16, f"mm_dwup_{tag}")
        dh_mid, grads["norm_ffn"][l] = _rms_bwd(sv["h_mid"], g_ffn, dxn2, dh, f"rms_ffn_bwd_{tag}")
        dheads = _matmul(dh_mid, wts["w_out"][l], "nt", F32, f"mm_dheads_{tag}")
        grads["w_out"][l] = _matmul(sv["heads"], dh_mid, "tn", BF16, f"mm_dwout_{tag}")
        do_mix, do_mem = _pad_heads([(dheads, 0, N_MIX), (dheads, D_MIX, N_MEM)], (1.0, 1.0), (N_MIX * HP, N_MEM * HP),
                                    f"pad_dheads_{tag}")
        mode = "causal" if fox else "dilated"
        dqp, dkp, dvp = _flash_bwd(sv["qp"], sv["kp"], sv["vp"], sv["o_mix"], do_mix, sv["lse_mix"], mode, f"flash_{mode}_bwd_{tag}")
        dqmp, dkmp, dvmp = _flash_bwd(sv["qmp"], sv["kmp"], sv["vmp"], sv["o_mem"], do_mem, sv["lse_mem"], "full",
                                      f"flash_mem_bwd_{tag}")
        if fox:
            dproj, dbf = _prep_fox_bwd(dqp, dkp, dvp, dqmp, sv["proj"], sv["bfg"], f"prep_fox_bwd_{tag}")
            grads["b_forget"][l // 2] = dbf[0, :N_MIX]
        else:
            dproj = _prep_dil_bwd(dqp, dkp, dvp, dqmp, cos2, sin2, f"prep_dil_bwd_{tag}")
        dkvm = _unpad_heads([dkmp, dvmp], BF16, f"unpad_dkvm_{tag}")
        grads["w_mem_kv"][l] = _matmul(sv["mn"], dkvm, "tn", BF16, f"mm_dwmemkv_{tag}")
        dmn = _matmul(dkvm, wts["w_mem_kv"][l], "nt", F32, f"mm_dmn_{tag}")
        _, grads["norm_mem"][l] = _rms_bwd(mem, g_mem, dmn, None, f"rms_mem_bwd_{tag}")
        dxn = _matmul(dproj, wts["w_in"][l], "nt", F32, f"mm_dxn_{tag}")
        grads["w_in"][l] = _matmul(sv["xn"], dproj, "tn", BF16, f"mm_dwin_{tag}")
        dh, grads["norm_mix"][l] = _rms_bwd(sv["h"], g_mix, dxn, dh_mid, f"rms_mix_bwd_{tag}")
    grads["norm_final"] = dg_final
    return loss_blk, dh, grads


def _full_from_chips(per_chip, name):
    axis = 2 if name in _COL_SHARDED else 1
    n_layers = per_chip.shape[1]
    return [jnp.concatenate([per_chip[j, l] for j in range(N_CHIPS)], axis=axis - 1) for l in range(n_layers)]


def _shards_from_full(layers, name):
    axis = 1 if name in _COL_SHARDED else 0
    return jnp.stack([jnp.stack(jnp.split(g, N_CHIPS, axis=axis)) for g in layers], axis=1)


def kernel(x, mem, norm_mix, norm_mem, norm_ffn, w_in_fox, b_forget, w_in_dil, w_mem_kv, w_out, w_up, conv_w, conv_b, w_down, norm_final, loss_target, m_norm_mix, m_norm_mem, m_norm_ffn, m_w_in_fox, m_b_forget, m_w_in_dil, m_w_mem_kv, m_w_out, m_w_up, m_conv_w, m_conv_b, m_w_down, m_norm_final, v_norm_mix, v_norm_mem, v_norm_ffn, v_w_in_fox, v_b_forget, v_w_in_dil, v_w_mem_kv, v_w_out, v_w_up, v_conv_w, v_conv_b, v_w_down, v_norm_final):
    w_sh = dict(w_in_fox=w_in_fox, w_in_dil=w_in_dil, w_mem_kv=w_mem_kv, w_out=w_out, w_up=w_up, w_down=w_down, conv_w=conv_w)
    m_sh = dict(w_in_fox=m_w_in_fox, w_in_dil=m_w_in_dil, w_mem_kv=m_w_mem_kv, w_out=m_w_out, w_up=m_w_up, w_down=m_w_down, conv_w=m_conv_w)
    v_sh = dict(w_in_fox=v_w_in_fox, w_in_dil=v_w_in_dil, w_mem_kv=v_w_mem_kv, w_out=v_w_out, w_up=v_w_up, w_down=v_w_down, conv_w=v_conv_w)
    small = dict(norm_mix=norm_mix, norm_mem=norm_mem, norm_ffn=norm_ffn, conv_b=conv_b, norm_final=norm_final, b_forget=b_forget)
    m_small = dict(norm_mix=m_norm_mix, norm_mem=m_norm_mem, norm_ffn=m_norm_ffn, conv_b=m_conv_b, norm_final=m_norm_final, b_forget=m_b_forget)
    v_small = dict(norm_mix=v_norm_mix, norm_mem=v_norm_mem, norm_ffn=v_norm_ffn, conv_b=v_conv_b, norm_final=v_norm_final, b_forget=v_b_forget)
    chip = 2 * lax.axis_index("x") + lax.axis_index("y")
    core = lax.axis_index("c")

    gathered = _gather_shards(_pack_flat([w_sh[n] for n in _BIG], BF16), "gather_weights")
    per_chip = _unpack_flat(gathered.reshape(N_CHIPS, -1))
    conv_spec = (("conv_w", _CONVW_SHARD),)
    conv_all = _gather_all(_pack_small(dict(conv_w=conv_w), conv_spec), "gather_conv_w")
    conv_w_full = jnp.concatenate([_unpack_small(conv_all[2 * j], conv_spec)["conv_w"] for j in range(N_CHIPS)], axis=-1)
    fox_full = [_fox_cols_to_kernel(w) for w in _full_from_chips(per_chip["w_in_fox"], "w_in_fox")]
    dil_full = _full_from_chips(per_chip["w_in_dil"], "w_in_dil")
    wts = dict(
        w_in=[fox_full[0], dil_full[0], fox_full[1], dil_full[1]],
        w_mem_kv=_full_from_chips(per_chip["w_mem_kv"], "w_mem_kv"),
        w_out=_full_from_chips(per_chip["w_out"], "w_out"),
        w_up=[_interleave_cols(w) for w in _full_from_chips(per_chip["w_up"], "w_up")],
        w_down=_full_from_chips(per_chip["w_down"], "w_down"),
        conv_w=[_interleave_cols(conv_w_full[l]) for l in range(DEPTH)],
        conv_b=[_interleave_cols(conv_b[l:l + 1]) for l in range(DEPTH)],
    )

    loss_blk, dx, grads = _local_step(x[0], mem[0], loss_target[0], small, wts)
    loss = lax.psum(loss_blk[0, 0], ("x", "y", "c"))

    g_layers = dict(
        w_in_fox=[_fox_cols_from_kernel(grads["w_in"][0]), _fox_cols_from_kernel(grads["w_in"][2])],
        w_in_dil=[grads["w_in"][1], grads["w_in"][3]],
        w_mem_kv=grads["w_mem_kv"], w_out=grads["w_out"],
        w_up=[_deinterleave_cols(g) for g in grads["w_up"]], w_down=grads["w_down"],
    )
    g_sh = {n: _shards_from_full(g_layers[n], n) for n in _BIG}
    g_flat = jnp.stack([_pack_flat([g_sh[n][j] for n in _BIG], BF16) for j in range(N_CHIPS)])
    from_sibling = _swap_halves(g_flat, "grad_swap_halves")
    mine = lax.dynamic_index_in_dim(g_flat, core, axis=1, keepdims=False)
    pair_sum = _add_pairs(mine, from_sibling, "grad_add_pairs")
    from_chips = _scatter_to_chips(pair_sum, "grad_scatter_chips")
    half = _sum_rows(from_chips, F32, "grad_sum_chips")
    reduced = _join_halves(half, "grad_join_halves")
    g_big = _unpack_flat(reduced.reshape(-1))

    g_small_local = dict(
        norm_mix=jnp.concatenate(grads["norm_mix"]), norm_mem=jnp.concatenate(grads["norm_mem"]),
        norm_ffn=jnp.concatenate(grads["norm_ffn"]),
        conv_b=jnp.concatenate([_deinterleave_cols(g) for g in grads["conv_b"]]),
        norm_final=grads["norm_final"], b_forget=jnp.stack(grads["b_forget"]),
        conv_w=jnp.stack([_deinterleave_cols(g) for g in grads["conv_w"]]),
    )
    small_all = _gather_all(_pack_small(g_small_local, _SMALL), "small_gather_all")
    g_small = _unpack_small(_sum_rows(small_all, F32, "small_sum"), _SMALL)
    ncol = 2 * D_FF // N_CHIPS
    g_big["conv_w"] = lax.dynamic_slice_in_dim(g_small["conv_w"], chip * ncol, ncol, axis=2)

    out_g, out_d, out_m, out_v = {}, {}, {}, {}
    for n in _BIG + ("conv_w",):
        shp = w_sh[n].shape
        two_d = (-1, shp[-1])
        d, mo, vo = _adamw(w_sh[n].reshape(two_d), g_big[n].reshape(two_d), m_sh[n].reshape(two_d), v_sh[n].reshape(two_d), f"adamw_{n}")
        out_g[n], out_d[n], out_m[n], out_v[n] = g_big[n].reshape(shp), d.reshape(shp), mo.reshape(shp), vo.reshape(shp)
    spec = _SMALL[:-1]
    d, mo, vo = _adamw(_pack_small(small, spec), _pack_small(g_small, spec), _pack_small(m_small, spec), _pack_small(v_small, spec),
                       "adamw_small")
    d, mo, vo = _unpack_small(d, spec), _unpack_small(mo, spec), _unpack_small(vo, spec)
    for n, shp in spec:
        out_g[n], out_d[n], out_m[n], out_v[n] = g_small[n].reshape(shp), d[n], mo[n], vo[n]

    order = ("norm_mix", "norm_mem", "norm_ffn", "w_in_fox", "b_forget", "w_in_dil", "w_mem_kv", "w_out", "w_up", "conv_w", "conv_b",
             "w_down", "norm_final")
    return (loss, dx[None], *[out_g[n] for n in order], *[out_d[n] for n in order], *[out_m[n] for n in order],
            *[out_v[n] for n in order])
```

```python
import functools
import math

import numpy as np
import jax
import jax.numpy as jnp
from jax import lax
from jax.experimental import pallas as pl
from jax.experimental.pallas import tpu as pltpu

F32 = jnp.float32
BF16 = jnp.bfloat16

D_MODEL = 1024
DEPTH = 4
HEAD_DIM = 64
N_MIX = 12
N_MEM = 4
D_MIX = N_MIX * HEAD_DIM
D_MEMQ = N_MEM * HEAD_DIM
D_FF = 2816
FOX_IN = 3 * D_MIX + N_MIX + D_MEMQ
DIL_IN = 3 * D_MIX + D_MEMQ
HP = 128
SCALE = HEAD_DIM ** -0.5
NEG = -1e30
NORM_EPS = 1e-6
DIL_MAX = 2048
LOG2E = 1.0 / math.log(2.0)
LN2 = math.log(2.0)
QSCALE = SCALE * LOG2E
HEADS_PER_STEP = 2
ROPE_THETA = 10000.0
N_CHIPS = 4
FLAT_W = 1024
CONV_TF = 128
FLASH_BLK = 512
VMEM_LIMIT = 48 * 1024 * 1024

ADAM_LR = 0.001
ADAM_B1 = 0.9
ADAM_B2 = 0.999
ADAM_EPS = 1e-08
ADAM_WD = 0.01
ADAM_STEP = 10

MESH = pl.DeviceIdType.MESH


def _cp(*sem):
    return pltpu.CompilerParams(dimension_semantics=tuple(sem), vmem_limit_bytes=VMEM_LIMIT)


def _tile(n, cap, mult=128):
    if n <= cap:
        return n
    t = (cap // mult) * mult
    while t >= mult:
        if n % t == 0:
            return t
        t -= mult
    raise ValueError(f"no tile for {n} under {cap}")


_DIMS = {"nn": (((1,), (0,)), ((), ())), "nt": (((1,), (1,)), ((), ())), "tn": (((0,), (0,)), ((), ()))}


def _matmul(a, b, mode, out_dtype, name, residual=None):
    if mode == "nn":
        (m, k), n = a.shape, b.shape[1]
    elif mode == "nt":
        (m, k), n = a.shape, b.shape[0]
    else:
        (k, m), n = a.shape, b.shape[1]
    tn = 1408 if n % 1408 == 0 else _tile(n, 1024)
    tm = 1408 if (m % 1408 == 0 and tn <= 1024) else _tile(m, 1024 if tn <= 1024 else 512)
    tk = _tile(k, 1408)
    nk = k // tk
    dims = _DIMS[mode]
    has_res = residual is not None

    def body(*refs):
        if has_res:
            a_ref, b_ref, r_ref, o_ref = refs[:4]
        else:
            a_ref, b_ref, o_ref = refs[:3]
        part = lax.dot_general(a_ref[...].astype(BF16), b_ref[...].astype(BF16), dims, preferred_element_type=F32)
        if nk == 1:
            if has_res:
                part = part + r_ref[...]
            o_ref[...] = part.astype(o_ref.dtype)
            return
        acc_ref = refs[-1]
        kk = pl.program_id(2)

        @pl.when(kk == 0)
        def _():
            acc_ref[...] = part

        @pl.when(kk > 0)
        def _():
            acc_ref[...] += part

        @pl.when(kk == nk - 1)
        def _():
            tot = acc_ref[...]
            if has_res:
                tot = tot + r_ref[...]
            o_ref[...] = tot.astype(o_ref.dtype)

    if mode == "nn":
        a_spec = pl.BlockSpec((tm, tk), lambda i, j, kk: (i, kk))
        b_spec = pl.BlockSpec((tk, tn), lambda i, j, kk: (kk, j))
    elif mode == "nt":
        a_spec = pl.BlockSpec((tm, tk), lambda i, j, kk: (i, kk))
        b_spec = pl.BlockSpec((tn, tk), lambda i, j, kk: (j, kk))
    else:
        a_spec = pl.BlockSpec((tk, tm), lambda i, j, kk: (kk, i))
        b_spec = pl.BlockSpec((tk, tn), lambda i, j, kk: (kk, j))
    in_specs = [a_spec, b_spec]
    args = [a, b]
    if has_res:
        in_specs.append(pl.BlockSpec((tm, tn), lambda i, j, kk: (i, j)))
        args.append(residual)
    return pl.pallas_call(
        body,
        out_shape=jax.ShapeDtypeStruct((m, n), out_dtype),
        grid=(m // tm, n // tn, nk),
        in_specs=in_specs,
        out_specs=pl.BlockSpec((tm, tn), lambda i, j, kk: (i, j)),
        scratch_shapes=[pltpu.VMEM((tm, tn), F32)] if nk > 1 else [],
        compiler_params=_cp("parallel", "parallel", "arbitrary"),
        name=name,
    )(*args)


def _rms_fwd(h, g, name):
    r, d = h.shape
    tr = _tile(r, 512, 8)

    def body(h_ref, g_ref, o_ref):
        x = h_ref[...]
        rstd = lax.rsqrt(jnp.mean(x * x, axis=-1, keepdims=True) + NORM_EPS)
        o_ref[...] = ((x * rstd) * g_ref[...]).astype(o_ref.dtype)

    return pl.pallas_call(
        body,
        out_shape=jax.ShapeDtypeStruct((r, d), BF16),
        grid=(r // tr,),
        in_specs=[pl.BlockSpec((tr, d), lambda i: (i, 0)), pl.BlockSpec((1, d), lambda i: (0, 0))],
        out_specs=pl.BlockSpec((tr, d), lambda i: (i, 0)),
        compiler_params=_cp("parallel"),
        name=name,
    )(h, g)


def _rms_bwd(h, g, dy, dres, name):
    r, d = h.shape
    tr = _tile(r, 512, 8)
    need_dh = dres is not None

    def body(*refs):
        if need_dh:
            h_ref, g_ref, dy_ref, dres_ref, dh_ref, dg_ref = refs
        else:
            h_ref, g_ref, dy_ref, dg_ref = refs
        i = pl.program_id(0)
        x = h_ref[...]
        rstd = lax.rsqrt(jnp.mean(x * x, axis=-1, keepdims=True) + NORM_EPS)
        nrm = x * rstd
        dyv = dy_ref[...].astype(F32)
        part = jnp.sum(dyv * nrm, axis=0, keepdims=True)

        @pl.when(i == 0)
        def _():
            dg_ref[...] = part

        @pl.when(i > 0)
        def _():
            dg_ref[...] += part

        if need_dh:
            gy = dyv * g_ref[...]
            dx = rstd * (gy - nrm * jnp.mean(gy * nrm, axis=-1, keepdims=True))
            dh_ref[...] = dres_ref[...] + dx

    row = pl.BlockSpec((tr, d), lambda i: (i, 0))
    vec = pl.BlockSpec((1, d), lambda i: (0, 0))
    if need_dh:
        return pl.pallas_call(
            body,
            out_shape=(jax.ShapeDtypeStruct((r, d), F32), jax.ShapeDtypeStruct((1, d), F32)),
            grid=(r // tr,),
            in_specs=[row, vec, row, row],
            out_specs=(row, vec),
            compiler_params=_cp("arbitrary"),
            name=name,
        )(h, g, dy, dres)
    return None, pl.pallas_call(
        body,
        out_shape=jax.ShapeDtypeStruct((1, d), F32),
        grid=(r // tr,),
        in_specs=[row, vec, row],
        out_specs=vec,
        compiler_params=_cp("arbitrary"),
        name=name,
    )(h, g, dy)


def _loss_head(h, g, target, name):
    r, d = h.shape
    tr = _tile(r, 512, 8)

    def body(h_ref, g_ref, t_ref, loss_ref, dh_ref, dg_ref):
        i = pl.program_id(0)
        x = h_ref[...]
        gv = g_ref[...]
        rstd = lax.rsqrt(jnp.mean(x * x, axis=-1, keepdims=True) + NORM_EPS)
        nrm = x * rstd
        err = nrm * gv - t_ref[...]
        lpart = 0.5 * jnp.sum(jnp.mean(err * err, axis=-1, keepdims=True), axis=0, keepdims=True)
        dyv = err * (1.0 / d)
        gpart = jnp.sum(dyv * nrm, axis=0, keepdims=True)

        @pl.when(i == 0)
        def _():
            loss_ref[...] = jnp.broadcast_to(lpart, loss_ref.shape)
            dg_ref[...] = gpart

        @pl.when(i > 0)
        def _():
            loss_ref[...] += jnp.broadcast_to(lpart, loss_ref.shape)
            dg_ref[...] += gpart

        gy = dyv * gv
        dh_ref[...] = rstd * (gy - nrm * jnp.mean(gy * nrm, axis=-1, keepdims=True))

    row = pl.BlockSpec((tr, d), lambda i: (i, 0))
    vec = pl.BlockSpec((1, d), lambda i: (0, 0))
    lsp = pl.BlockSpec((1, 128), lambda i: (0, 0))
    return pl.pallas_call(
        body,
        out_shape=(jax.ShapeDtypeStruct((1, 128), F32), jax.ShapeDtypeStruct((r, d), F32), jax.ShapeDtypeStruct((1, d), F32)),
        grid=(r // tr,),
        in_specs=[row, vec, row],
        out_specs=(lsp, row, vec),
        compiler_params=_cp("arbitrary"),
        name=name,
    )(h, g, target)


def _lane(shape):
    return lax.broadcasted_iota(jnp.int32, shape, 1)


def _head_tile(pair_tile, odd):
    return pltpu.roll(pair_tile, 64, 1) if odd else pair_tile


def _pair_tile(even_tile, odd_tile):
    lane = _lane(even_tile.shape)
    return jnp.where(lane < 64, even_tile, pltpu.roll(odd_tile, 64, 1))


def _pad_heads(xs, scales, out_widths, name):
    r = xs[0][0].shape[0]
    tr = _tile(r, 256, 8)
    n_in = len(xs)

    def body(*refs):
        for idx in range(n_in):
            x_ref, o_ref = refs[idx], refs[n_in + idx]
            nh = xs[idx][2]
            for p in range(nh // 2):
                t = x_ref[:, p * 128:(p + 1) * 128].astype(F32) * scales[idx]
                lane = _lane(t.shape)
                o_ref[:, (2 * p) * HP:(2 * p + 1) * HP] = jnp.where(lane < 64, t, 0.0).astype(o_ref.dtype)
                o_ref[:, (2 * p + 1) * HP:(2 * p + 2) * HP] = jnp.where(lane < 64, pltpu.roll(t, 64, 1), 0.0).astype(o_ref.dtype)

    in_specs, args, out_specs, out_shape = [], [], [], []
    for (arr, c0, nh), w in zip(xs, out_widths):
        wcols = nh * 64
        assert c0 % wcols == 0 or c0 == 0
        blk = c0 // wcols if wcols else 0
        in_specs.append(pl.BlockSpec((tr, wcols), functools.partial(lambda i, b: (i, b), b=blk)))
        args.append(arr)
        out_specs.append(pl.BlockSpec((tr, w), lambda i: (i, 0)))
        out_shape.append(jax.ShapeDtypeStruct((r, w), BF16))
    return pl.pallas_call(
        body,
        out_shape=tuple(out_shape),
        grid=(r // tr,),
        in_specs=in_specs,
        out_specs=tuple(out_specs),
        compiler_params=_cp("parallel"),
        name=name,
    )(*args)


def _unpad_heads(xs, scales, out_dtype, name):
    r = xs[0].shape[0]
    tr = _tile(r, 256, 8)
    nhs = [x.shape[1] // HP for x in xs]
    total = sum(nhs) * 64

    def body(*refs):
        o_ref = refs[-1]
        col = 0
        for x_ref, nh, sc in zip(refs[:-1], nhs, scales):
            for p in range(nh // 2):
                ev = x_ref[:, (2 * p) * HP:(2 * p + 1) * HP].astype(F32)
                od = x_ref[:, (2 * p + 1) * HP:(2 * p + 2) * HP].astype(F32)
                o_ref[:, col:col + 128] = (_pair_tile(ev, od) * sc).astype(o_ref.dtype)
                col += 128

    return pl.pallas_call(
        body,
        out_shape=jax.ShapeDtypeStruct((r, total), out_dtype),
        grid=(r // tr,),
        in_specs=[pl.BlockSpec((tr, x.shape[1]), lambda i: (i, 0)) for x in xs],
        out_specs=pl.BlockSpec((tr, total), lambda i: (i, 0)),
        compiler_params=_cp("parallel"),
        name=name,
    )(*xs)


def _bf16_split3(c):
    hi = c.astype(BF16).astype(F32)
    r1 = c - hi
    mid = r1.astype(BF16).astype(F32)
    lo = (r1 - mid).astype(BF16).astype(F32)
    return hi, mid, lo


def _log_sigmoid(z):
    return jnp.minimum(z, 0.0) - jnp.log(1.0 + jnp.exp(-jnp.abs(z)))


def _prep_fox_fwd(proj, bfg, name):
    s = proj.shape[0]
    ts = _tile(s, 256, 8)
    fcol = 3 * D_MIX + D_MEMQ

    def body(p_ref, b_ref, q_ref, k_ref, v_ref, qm_ref, carry_ref):
        i = pl.program_id(0)

        @pl.when(i == 0)
        def _():
            carry_ref[...] = jnp.zeros_like(carry_ref)

        lane = _lane((ts, 128))
        z = p_ref[:, fcol:fcol + 128] + b_ref[...]
        logf = jnp.where(lane < N_MIX, _log_sigmoid(z), 0.0)
        rr = lax.broadcasted_iota(jnp.int32, (ts, ts), 0)
        cc = lax.broadcasted_iota(jnp.int32, (ts, ts), 1)
        tri = jnp.where(cc <= rr, 1.0, 0.0).astype(F32)
        c = jnp.dot(tri, logf, preferred_element_type=F32, precision=lax.Precision.HIGHEST) + carry_ref[0:1, :]
        carry_ref[...] = jnp.broadcast_to(c[ts - 1:ts, :], carry_ref.shape)
        for hh in range(N_MIX):
            p, odd = hh // 2, hh % 2
            ch = jnp.sum(jnp.where(lane == hh, c, 0.0), axis=-1, keepdims=True) * LOG2E
            hi, mid, lo = _bf16_split3(ch)
            qt = _head_tile(p_ref[:, p * 128:(p + 1) * 128], odd) * QSCALE
            kt = _head_tile(p_ref[:, D_MIX + p * 128:D_MIX + (p + 1) * 128], odd)
            vt = _head_tile(p_ref[:, 2 * D_MIX + p * 128:2 * D_MIX + (p + 1) * 128], odd)
            qa = jnp.where(lane == 64, hi, jnp.where(lane == 65, mid, jnp.where(lane == 66, lo, jnp.where(lane < 70, 1.0, 0.0))))
            ka = jnp.where(lane < 67, 1.0, jnp.where(lane == 67, -hi, jnp.where(lane == 68, -mid, jnp.where(lane == 69, -lo, 0.0))))
            q_ref[:, hh * HP:(hh + 1) * HP] = jnp.where(lane < 64, qt, qa).astype(BF16)
            k_ref[:, hh * HP:(hh + 1) * HP] = jnp.where(lane < 64, kt, ka).astype(BF16)
            v_ref[:, hh * HP:(hh + 1) * HP] = jnp.where(lane < 64, vt, 0.0).astype(BF16)
        for hh in range(N_MEM):
            p, odd = hh // 2, hh % 2
            t = _head_tile(p_ref[:, 3 * D_MIX + p * 128:3 * D_MIX + (p + 1) * 128], odd) * QSCALE
            qm_ref[:, hh * HP:(hh + 1) * HP] = jnp.where(lane < 64, t, 0.0).astype(BF16)

    wmix, wmem = N_MIX * HP, N_MEM * HP
    return pl.pallas_call(
        body,
        out_shape=(jax.ShapeDtypeStruct((s, wmix), BF16),) * 3 + (jax.ShapeDtypeStruct((s, wmem), BF16),),
        grid=(s // ts,),
        in_specs=[pl.BlockSpec((ts, proj.shape[1]), lambda i: (i, 0)), pl.BlockSpec((1, 128), lambda i: (0, 0))],
        out_specs=(pl.BlockSpec((ts, wmix), lambda i: (i, 0)),) * 3 + (pl.BlockSpec((ts, wmem), lambda i: (i, 0)),),
        scratch_shapes=[pltpu.VMEM((8, 128), F32)],
        compiler_params=_cp("arbitrary"),
        name=name,
    )(proj, bfg)


def _prep_fox_bwd(dq, dk, dv, dqm, proj, bfg, name):
    s = proj.shape[0]
    ts = _tile(s, 256, 8)
    nb = s // ts
    fcol = 3 * D_MIX + D_MEMQ

    def body(dq_ref, dk_ref, dv_ref, dqm_ref, p_ref, b_ref, o_ref, db_ref, carry_ref):
        i = pl.program_id(0)

        @pl.when(i == 0)
        def _():
            carry_ref[...] = jnp.zeros_like(carry_ref)
            db_ref[...] = jnp.zeros_like(db_ref)

        lane = _lane((ts, 128))
        dc = jnp.zeros((ts, 128), F32)
        for p in range(N_MIX // 2):
            tq, tk, tv = [], [], []
            for odd in (0, 1):
                hh = 2 * p + odd
                dqt = dq_ref[:, hh * HP:(hh + 1) * HP]
                dkt = dk_ref[:, hh * HP:(hh + 1) * HP]
                col = jnp.sum(jnp.where(lane == 64, dqt, 0.0) - jnp.where(lane == 67, dkt, 0.0), axis=-1, keepdims=True)
                dc = dc + jnp.where(lane == hh, col, 0.0)
                tq.append(dqt)
                tk.append(dkt)
                tv.append(dv_ref[:, hh * HP:(hh + 1) * HP])
            o_ref[:, p * 128:(p + 1) * 128] = (_pair_tile(tq[0], tq[1]) * SCALE).astype(o_ref.dtype)
            o_ref[:, D_MIX + p * 128:D_MIX + (p + 1) * 128] = (_pair_tile(tk[0], tk[1]) * LN2).astype(o_ref.dtype)
            o_ref[:, 2 * D_MIX + p * 128:2 * D_MIX + (p + 1) * 128] = _pair_tile(tv[0], tv[1]).astype(o_ref.dtype)
        for p in range(N_MEM // 2):
            ev = dqm_ref[:, (2 * p) * HP:(2 * p + 1) * HP]
            od = dqm_ref[:, (2 * p + 1) * HP:(2 * p + 2) * HP]
            o_ref[:, 3 * D_MIX + p * 128:3 * D_MIX + (p + 1) * 128] = (_pair_tile(ev, od) * SCALE).astype(o_ref.dtype)
        rr = lax.broadcasted_iota(jnp.int32, (ts, ts), 0)
        cc = lax.broadcasted_iota(jnp.int32, (ts, ts), 1)
        triu = jnp.where(cc >= rr, 1.0, 0.0).astype(F32)
        dlogf = jnp.dot(triu, dc, preferred_element_type=F32, precision=lax.Precision.HIGHEST) + carry_ref[0:1, :]
        carry_ref[...] = jnp.broadcast_to(dlogf[0:1, :], carry_ref.shape)
        z = p_ref[:, fcol:fcol + 128] + b_ref[...]
        dz = jnp.where(lane < N_MIX, dlogf / (1.0 + jnp.exp(z)), 0.0)
        o_ref[:, fcol:fcol + 128] = dz.astype(o_ref.dtype)
        db_ref[...] += jnp.sum(dz, axis=0, keepdims=True)

    wmix, wmem = N_MIX * HP, N_MEM * HP
    rev = lambda i: (nb - 1 - i, 0)
    return pl.pallas_call(
        body,
        out_shape=(jax.ShapeDtypeStruct(proj.shape, BF16), jax.ShapeDtypeStruct((1, 128), F32)),
        grid=(nb,),
        in_specs=[pl.BlockSpec((ts, wmix), rev)] * 3 + [pl.BlockSpec((ts, wmem), rev), pl.BlockSpec((ts, proj.shape[1]), rev),
                                                         pl.BlockSpec((1, 128), lambda i: (0, 0))],
        out_specs=(pl.BlockSpec((ts, proj.shape[1]), rev), pl.BlockSpec((1, 128), lambda i: (0, 0))),
        scratch_shapes=[pltpu.VMEM((8, 128), F32)],
        compiler_params=_cp("arbitrary"),
        name=name,
    )(dq, dk, dv, dqm, proj, bfg)


def _rope_partner(x):
    lane = _lane(x.shape)
    return jnp.where((lane % 64) < 32, pltpu.roll(x, 96, 1), pltpu.roll(x, 32, 1))


def _prep_dil_fwd(proj, cos2, sin2, name):
    s = proj.shape[0]
    ts = _tile(s, 256, 8)

    def body(p_ref, c_ref, s_ref, q_ref, k_ref, v_ref, qm_ref):
        lane = _lane((ts, 128))
        cosv, sinv = c_ref[...], s_ref[...]
        for p in range(N_MIX // 2):
            xq = p_ref[:, p * 128:(p + 1) * 128]
            xk = p_ref[:, D_MIX + p * 128:D_MIX + (p + 1) * 128]
            xv = p_ref[:, 2 * D_MIX + p * 128:2 * D_MIX + (p + 1) * 128]
            yq = (xq * cosv + _rope_partner(xq) * sinv) * QSCALE
            yk = xk * cosv + _rope_partner(xk) * sinv
            for odd in (0, 1):
                hh = 2 * p + odd
                q_ref[:, hh * HP:(hh + 1) * HP] = jnp.where(lane < 64, _head_tile(yq, odd), 0.0).astype(BF16)
                k_ref[:, hh * HP:(hh + 1) * HP] = jnp.where(lane < 64, _head_tile(yk, odd), 0.0).astype(BF16)
                v_ref[:, hh * HP:(hh + 1) * HP] = jnp.where(lane < 64, _head_tile(xv, odd), 0.0).astype(BF16)
        for p in range(N_MEM // 2):
            t = p_ref[:, 3 * D_MIX + p * 128:3 * D_MIX + (p + 1) * 128] * QSCALE
            for odd in (0, 1):
                hh = 2 * p + odd
                qm_ref[:, hh * HP:(hh + 1) * HP] = jnp.where(lane < 64, _head_tile(t, odd), 0.0).astype(BF16)

    wmix, wmem = N_MIX * HP, N_MEM * HP
    return pl.pallas_call(
        body,
        out_shape=(jax.ShapeDtypeStruct((s, wmix), BF16),) * 3 + (jax.ShapeDtypeStruct((s, wmem), BF16),),
        grid=(s // ts,),
        in_specs=[pl.BlockSpec((ts, proj.shape[1]), lambda i: (i, 0)), pl.BlockSpec((ts, 128), lambda i: (i, 0)),
                  pl.BlockSpec((ts, 128), lambda i: (i, 0))],
        out_specs=(pl.BlockSpec((ts, wmix), lambda i: (i, 0)),) * 3 + (pl.BlockSpec((ts, wmem), lambda i: (i, 0)),),
        compiler_params=_cp("parallel"),
        name=name,
    )(proj, cos2, sin2)


def _prep_dil_bwd(dq, dk, dv, dqm, cos2, sin2, name):
    s = dq.shape[0]
    ts = _tile(s, 256, 8)

    def body(dq_ref, dk_ref, dv_ref, dqm_ref, c_ref, s_ref, o_ref):
        cosv, sinv = c_ref[...], s_ref[...]
        for p in range(N_MIX // 2):
            e, o = 2 * p, 2 * p + 1
            dyq = _pair_tile(dq_ref[:, e * HP:(e + 1) * HP], dq_ref[:, o * HP:(o + 1) * HP]) * SCALE
            dyk = _pair_tile(dk_ref[:, e * HP:(e + 1) * HP], dk_ref[:, o * HP:(o + 1) * HP]) * LN2
            dxv = _pair_tile(dv_ref[:, e * HP:(e + 1) * HP], dv_ref[:, o * HP:(o + 1) * HP])
            o_ref[:, p * 128:(p + 1) * 128] = (dyq * cosv - _rope_partner(dyq) * sinv).astype(o_ref.dtype)
            o_ref[:, D_MIX + p * 128:D_MIX + (p + 1) * 128] = (dyk * cosv - _rope_partner(dyk) * sinv).astype(o_ref.dtype)
            o_ref[:, 2 * D_MIX + p * 128:2 * D_MIX + (p + 1) * 128] = dxv.astype(o_ref.dtype)
        for p in range(N_MEM // 2):
            e, o = 2 * p, 2 * p + 1
            t = _pair_tile(dqm_ref[:, e * HP:(e + 1) * HP], dqm_ref[:, o * HP:(o + 1) * HP]) * SCALE
            o_ref[:, 3 * D_MIX + p * 128:3 * D_MIX + (p + 1) * 128] = t.astype(o_ref.dtype)

    wmix, wmem = N_MIX * HP, N_MEM * HP
    row = lambda w: pl.BlockSpec((ts, w), lambda i: (i, 0))
    return pl.pallas_call(
        body,
        out_shape=jax.ShapeDtypeStruct((s, DIL_IN), BF16),
        grid=(s // ts,),
        in_specs=[row(wmix)] * 3 + [row(wmem), row(128), row(128)],
        out_specs=row(DIL_IN),
        compiler_params=_cp("parallel"),
        name=name,
    )(dq, dk, dv, dqm, cos2, sin2)


def _mask_bias(mode, blk):
    n = 1 if mode == "causal" else DIL_MAX // blk + 1
    idx = jnp.arange(blk, dtype=jnp.int32)
    dist = jnp.arange(n, dtype=jnp.int32)[:, None, None] * blk + idx[None, :, None] - idx[None, None, :]
    if mode == "causal":
        return jnp.where(dist >= 0, 0.0, NEG).astype(F32)
    cnt = ((dist <= 128).astype(jnp.int32) + ((dist <= 512) & (dist % 4 == 0)).astype(jnp.int32)
           + ((dist <= DIL_MAX) & (dist % 16 == 0)).astype(jnp.int32))
    bias = jnp.where(cnt == 3, math.log2(3.0), jnp.where(cnt == 2, 1.0, 0.0))
    return jnp.where((dist >= 0) & (cnt > 0), bias, NEG).astype(F32)


def _flash_dims(q, k, mode):
    sq, w = q.shape
    sk = k.shape[0]
    tq = _tile(sq, FLASH_BLK, 8)
    tk = sk if mode == "full" else tq
    band = DIL_MAX // tk if mode == "dilated" else None
    return sq, sk, w, tq, tk, sq // tq, sk // tk, band


def _flash_fwd(q, k, v, mode, name):
    sq, sk, w, tq, tk, nq, nk, band = _flash_dims(q, k, mode)
    hb = HEADS_PER_STEP
    wb = hb * HP
    nch = tk // 128
    has_bias = mode != "full"

    def kidx(hp, qi, kj):
        if mode == "full":
            return (kj, hp)
        lo = jnp.maximum(qi - band, 0) if mode == "dilated" else 0
        return (jnp.clip(kj, lo, qi), hp)

    def body(*refs):
        if has_bias:
            q_ref, k_ref, v_ref, b_ref, o_ref, lse_ref, m_ref, l_ref, acc_ref = refs
        else:
            q_ref, k_ref, v_ref, o_ref, lse_ref, m_ref, l_ref, acc_ref = refs
        qi, kj = pl.program_id(1), pl.program_id(2)

        @pl.when(kj == 0)
        def _():
            m_ref[...] = jnp.full_like(m_ref, -jnp.inf)
            l_ref[...] = jnp.zeros_like(l_ref)
            acc_ref[...] = jnp.zeros_like(acc_ref)

        def step(bias_tile):
            for h in range(hb):
                cols = slice(h * HP, (h + 1) * HP)
                sc = lax.dot_general(q_ref[:, cols], k_ref[:, cols], _DIMS["nt"], preferred_element_type=F32)
                if bias_tile is not None:
                    sc = sc + bias_tile()
                m_prev = m_ref[h]
                m_new = jnp.maximum(m_prev, jnp.max(sc, axis=-1, keepdims=True))
                alpha = jnp.exp2(m_prev - m_new)
                psum, chunks = None, []
                for c in range(nch):
                    pc = jnp.exp2(sc[:, c * 128:(c + 1) * 128] - m_new)
                    psum = pc if psum is None else psum + pc
                    chunks.append(pc.astype(BF16))
                p = chunks[0] if nch == 1 else jnp.concatenate(chunks, axis=1)
                l_ref[h] = alpha * l_ref[h] + psum
                acc_ref[h] = alpha * acc_ref[h] + jnp.dot(p, v_ref[:, cols], preferred_element_type=F32)
                m_ref[h] = m_new

        if mode == "full":
            step(None)
        elif mode == "causal":
            pl.when(kj == qi)(lambda: step(lambda: b_ref[0]))
            pl.when(kj < qi)(lambda: step(None))
        else:
            pl.when((kj <= qi) & (kj >= qi - band))(lambda: step(lambda: b_ref[qi - kj]))

        @pl.when(kj == nk - 1)
        def _():
            for h in range(hb):
                cols = slice(h * HP, (h + 1) * HP)
                l = jnp.sum(l_ref[h], axis=-1, keepdims=True)
                o_ref[:, cols] = (acc_ref[h] / l).astype(o_ref.dtype)
                lse_ref[:, cols] = m_ref[h] + jnp.log2(l)

    qspec = pl.BlockSpec((tq, wb), lambda hp, qi, kj: (qi, hp))
    in_specs = [qspec, pl.BlockSpec((tk, wb), kidx), pl.BlockSpec((tk, wb), kidx)]
    args = [q, k, v]
    if has_bias:
        bias = _mask_bias(mode, tq)
        in_specs.append(pl.BlockSpec(bias.shape, lambda hp, qi, kj: (0, 0, 0)))
        args.append(bias)
    return pl.pallas_call(
        body,
        out_shape=(jax.ShapeDtypeStruct((sq, w), BF16), jax.ShapeDtypeStruct((sq, w), F32)),
        grid=(w // wb, nq, nk),
        in_specs=in_specs,
        out_specs=(qspec, qspec),
        scratch_shapes=[pltpu.VMEM((hb, tq, HP), F32), pltpu.VMEM((hb, tq, HP), F32), pltpu.VMEM((hb, tq, HP), F32)],
        compiler_params=_cp("parallel", "parallel", "arbitrary"),
        name=name,
    )(*args)


def _flash_bwd(q, k, v, o, do, lse, mode, name):
    sq, sk, w, tq, tk, nq, nk, band = _flash_dims(q, k, mode)
    hb = HEADS_PER_STEP
    wb = hb * HP
    nch = tk // 128
    has_bias = mode != "full"

    def qidx(hp, kj, qi):
        if mode == "full":
            return (qi, hp)
        hi = jnp.minimum(kj + band, nq - 1) if mode == "dilated" else nq - 1
        return (jnp.clip(qi, kj, hi), hp)

    def body(*refs):
        if has_bias:
            q_ref, k_ref, v_ref, o_ref, do_ref, lse_ref, b_ref, dq_ref, dk_ref, dv_ref, delta_ref = refs
        else:
            q_ref, k_ref, v_ref, o_ref, do_ref, lse_ref, dq_ref, dk_ref, dv_ref, delta_ref = refs
        kj, qi = pl.program_id(1), pl.program_id(2)
        rows = pl.ds(pl.multiple_of(qi * tq, tq), tq)

        @pl.when((kj == 0) & (qi == 0))
        def _():
            dq_ref[...] = jnp.zeros_like(dq_ref)

        @pl.when(qi == 0)
        def _():
            dk_ref[...] = jnp.zeros_like(dk_ref)
            dv_ref[...] = jnp.zeros_like(dv_ref)

        first = kj == (jnp.maximum(qi - band, 0) if mode == "dilated" else 0)

        @pl.when(first)
        def _():
            for h in range(hb):
                cols = slice(h * HP, (h + 1) * HP)
                dl = jnp.sum(do_ref[:, cols].astype(F32) * o_ref[:, cols].astype(F32), axis=-1, keepdims=True)
                delta_ref[h, rows, :] = jnp.broadcast_to(dl, (tq, HP))

        def step(bias_tile):
            for h in range(hb):
                cols = slice(h * HP, (h + 1) * HP)
                qv, kv, dov = q_ref[:, cols], k_ref[:, cols], do_ref[:, cols]
                sc = lax.dot_general(qv, kv, _DIMS["nt"], preferred_element_type=F32)
                if bias_tile is not None:
                    sc = sc + bias_tile()
                dp = lax.dot_general(dov, v_ref[:, cols], _DIMS["nt"], preferred_element_type=F32)
                lse_b = lse_ref[:, cols]
                dlt = delta_ref[h, rows, :]
                pch, dsch = [], []
                for c in range(nch):
                    lanes = slice(c * 128, (c + 1) * 128)
                    pc = jnp.exp2(sc[:, lanes] - lse_b)
                    pch.append(pc.astype(BF16))
                    dsch.append((pc * (dp[:, lanes] - dlt)).astype(BF16))
                p = pch[0] if nch == 1 else jnp.concatenate(pch, axis=1)
                ds = dsch[0] if nch == 1 else jnp.concatenate(dsch, axis=1)
                dv_ref[:, cols] += lax.dot_general(p, dov, _DIMS["tn"], preferred_element_type=F32)
                dk_ref[:, cols] += lax.dot_general(ds, qv, _DIMS["tn"], preferred_element_type=F32)
                dq_ref[rows, cols] += jnp.dot(ds, kv, preferred_element_type=F32)

        if mode == "full":
            step(None)
        elif mode == "causal":
            pl.when(qi == kj)(lambda: step(lambda: b_ref[0]))
            pl.when(qi > kj)(lambda: step(None))
        else:
            pl.when((qi >= kj) & (qi <= kj + band))(lambda: step(lambda: b_ref[qi - kj]))

    qspec = pl.BlockSpec((tq, wb), qidx)
    kspec = pl.BlockSpec((tk, wb), lambda hp, kj, qi: (kj, hp))
    in_specs = [qspec, kspec, kspec, qspec, qspec, qspec]
    args = [q, k, v, o, do, lse]
    if has_bias:
        bias = _mask_bias(mode, tq)
        in_specs.append(pl.BlockSpec(bias.shape, lambda hp, kj, qi: (0, 0, 0)))
        args.append(bias)
    return pl.pallas_call(
        body,
        out_shape=(jax.ShapeDtypeStruct((sq, w), F32), jax.ShapeDtypeStruct((sk, w), F32), jax.ShapeDtypeStruct((sk, w), F32)),
        grid=(w // wb, nk, nq),
        in_specs=in_specs,
        out_specs=(pl.BlockSpec((sq, wb), lambda hp, kj, qi: (0, hp)), kspec, kspec),
        scratch_shapes=[pltpu.VMEM((hb, sq, HP), F32)],
        compiler_params=_cp("parallel", "arbitrary", "arbitrary"),
        name=name,
    )(*args)


def _conv_rc(s):
    return _tile(s, 256, 8)


def _shift_down(x, prev8, nrows):
    rows = lax.broadcasted_iota(jnp.int32, x.shape, 0)
    out = pltpu.roll(x, nrows, 0)
    for i in range(nrows):
        out = jnp.where(rows == i, prev8[8 - nrows + i:8 - nrows + i + 1, :], out)
    return out


def _shift_up(x, next8, nrows):
    n = x.shape[0]
    rows = lax.broadcasted_iota(jnp.int32, x.shape, 0)
    out = pltpu.roll(x, n - nrows, 0)
    for i in range(nrows):
        out = jnp.where(rows == n - nrows + i, next8[i:i + 1, :], out)
    return out


def _conv_taps(u_ref, r, rc):
    r0 = pl.multiple_of(r * rc, rc)
    x = u_ref[pl.ds(r0, rc), :]
    p0 = pl.multiple_of(jnp.maximum(r0 - 8, 0), 8)
    prev8 = jnp.where(r > 0, u_ref[pl.ds(p0, 8), :], 0.0)
    return r0, x, _shift_down(x, prev8, 1), _shift_down(x, prev8, 2)


def _conv_fwd(u, cw, cb, name):
    s, f2 = u.shape
    tf = CONV_TF
    nf = f2 // (2 * tf)
    rc = _conv_rc(s)

    def body(u_ref, w_ref, b_ref, a_ref):
        w0, w1, w2, b = w_ref[0:1, :], w_ref[1:2, :], w_ref[2:3, :], b_ref[...]

        def chunk(r, carry):
            r0, x, x1, x2 = _conv_taps(u_ref, r, rc)
            c = b + w0 * x2 + w1 * x1 + w2 * x
            val, gate = c[:, :tf], c[:, tf:]
            a_ref[pl.ds(r0, rc), :] = (gate * jax.nn.sigmoid(gate) * val).astype(a_ref.dtype)
            return carry

        lax.fori_loop(0, s // rc, chunk, 0)

    return pl.pallas_call(
        body,
        out_shape=jax.ShapeDtypeStruct((s, f2 // 2), BF16),
        grid=(nf,),
        in_specs=[pl.BlockSpec((s, 2 * tf), lambda j: (0, j)), pl.BlockSpec((3, 2 * tf), lambda j: (0, j)),
                  pl.BlockSpec((1, 2 * tf), lambda j: (0, j))],
        out_specs=pl.BlockSpec((s, tf), lambda j: (0, j)),
        compiler_params=_cp("parallel"),
        name=name,
    )(u, cw, cb)


def _conv_bwd(u, da, cw, cb, name):
    s, f2 = u.shape
    tf = CONV_TF
    nf = f2 // (2 * tf)
    rc = _conv_rc(s)
    nchunk = s // rc

    def body(u_ref, da_ref, w_ref, b_ref, du_ref, dw_ref, db_ref, next_ref):
        w0, w1, w2, b = w_ref[0:1, :], w_ref[1:2, :], w_ref[2:3, :], b_ref[...]
        next_ref[...] = jnp.zeros_like(next_ref)

        def chunk(it, carry):
            g0, g1, g2, gb = carry
            r = nchunk - 1 - it
            r0, x, x1, x2 = _conv_taps(u_ref, r, rc)
            c = b + w0 * x2 + w1 * x1 + w2 * x
            val, gate = c[:, :tf], c[:, tf:]
            sg = jax.nn.sigmoid(gate)
            dav = da_ref[pl.ds(r0, rc), :]
            dc = jnp.concatenate([dav * (gate * sg), dav * val * (sg * (1.0 + gate * (1.0 - sg)))], axis=1)
            nxt = next_ref[...]
            du = w2 * dc + w1 * _shift_up(dc, nxt, 1) + w0 * _shift_up(dc, nxt, 2)
            du_ref[pl.ds(r0, rc), :] = du.astype(du_ref.dtype)
            next_ref[...] = dc[0:8, :]
            return (g0 + jnp.sum(dc * x2, axis=0, keepdims=True), g1 + jnp.sum(dc * x1, axis=0, keepdims=True),
                    g2 + jnp.sum(dc * x, axis=0, keepdims=True), gb + jnp.sum(dc, axis=0, keepdims=True))

        zero = jnp.zeros((1, 2 * tf), F32)
        g0, g1, g2, gb = lax.fori_loop(0, nchunk, chunk, (zero, zero, zero, zero))
        dw_ref[0:1, :] = g0
        dw_ref[1:2, :] = g1
        dw_ref[2:3, :] = g2
        db_ref[...] = gb

    return pl.pallas_call(
        body,
        out_shape=(jax.ShapeDtypeStruct((s, f2), BF16), jax.ShapeDtypeStruct((3, f2), F32), jax.ShapeDtypeStruct((1, f2), F32)),
        grid=(nf,),
        in_specs=[pl.BlockSpec((s, 2 * tf), lambda j: (0, j)), pl.BlockSpec((s, tf), lambda j: (0, j)),
                  pl.BlockSpec((3, 2 * tf), lambda j: (0, j)), pl.BlockSpec((1, 2 * tf), lambda j: (0, j))],
        out_specs=(pl.BlockSpec((s, 2 * tf), lambda j: (0, j)), pl.BlockSpec((3, 2 * tf), lambda j: (0, j)),
                   pl.BlockSpec((1, 2 * tf), lambda j: (0, j))),
        scratch_shapes=[pltpu.VMEM((8, 2 * tf), F32)],
        compiler_params=_cp("parallel"),
        name=name,
    )(u, da, cw, cb)


def _adamw(w, g, m, v, name):
    r, c = w.shape
    tr = _tile(r, 256, 8) if r % 8 == 0 else r
    c1 = 1.0 - ADAM_B1 ** ADAM_STEP
    c2 = 1.0 - ADAM_B2 ** ADAM_STEP

    def body(w_ref, g_ref, m_ref, v_ref, d_ref, mo_ref, vo_ref):
        gv = g_ref[...]
        mn = ADAM_B1 * m_ref[...] + (1.0 - ADAM_B1) * gv
        vn = ADAM_B2 * v_ref[...] + (1.0 - ADAM_B2) * (gv * gv)
        d_ref[...] = -ADAM_LR * ((mn / c1) / (jnp.sqrt(vn / c2) + ADAM_EPS) + ADAM_WD * w_ref[...])
        mo_ref[...] = mn
        vo_ref[...] = vn

    blk = pl.BlockSpec((tr, c), lambda i: (i, 0))
    shp = jax.ShapeDtypeStruct((r, c), F32)
    return pl.pallas_call(
        body, out_shape=(shp, shp, shp), grid=(r // tr,), in_specs=[blk] * 4, out_specs=(blk,) * 3,
        compiler_params=_cp("parallel"), name=name,
    )(w, g, m, v)


def _sum_rows(parts, out_dtype, name):
    n, r, c = parts.shape
    tr = _tile(r, 256, 8)

    def body(p_ref, o_ref):
        tot = p_ref[0].astype(F32)
        for i in range(1, n):
            tot = tot + p_ref[i].astype(F32)
        o_ref[...] = tot.astype(o_ref.dtype)

    return pl.pallas_call(
        body, out_shape=jax.ShapeDtypeStruct((r, c), out_dtype), grid=(r // tr,),
        in_specs=[pl.BlockSpec((n, tr, c), lambda i: (0, i, 0))], out_specs=pl.BlockSpec((tr, c), lambda i: (i, 0)),
        compiler_params=_cp("parallel"), name=name,
    )(parts)


def _add_pairs(a, b, name):
    n, r, c = a.shape
    tr = _tile(r, 256, 8)

    def body(a_ref, b_ref, o_ref):
        o_ref[...] = (a_ref[...].astype(F32) + b_ref[...].astype(F32)).astype(o_ref.dtype)

    blk = pl.BlockSpec((1, tr, c), lambda j, i: (j, i, 0))
    return pl.pallas_call(
        body, out_shape=jax.ShapeDtypeStruct((n, r, c), BF16), grid=(n, r // tr), in_specs=[blk, blk], out_specs=blk,
        compiler_params=_cp("parallel", "parallel"), name=name,
    )(a, b)


_ANY = pl.BlockSpec(memory_space=pl.ANY)


def _place():
    return lax.axis_index("x"), lax.axis_index("y"), lax.axis_index("c")


def _other_chips(x, y):
    return [(1 - x, y), (x, 1 - y), (1 - x, 1 - y)]


def _gather_shards(shard, name):
    _, rh, w = shard.shape

    def body(in_ref, out_ref, send_sems, recv_sems, local_sem):
        x, y, c = _place()
        me = 2 * x + y
        sib = (x, y, 1 - c)
        chips = _other_chips(x, y)
        mine = pltpu.make_async_copy(in_ref, out_ref.at[me], local_sem)
        mine.start()

        def copy(kk, src, dst, to):
            return pltpu.make_async_remote_copy(src_ref=src, dst_ref=dst, send_sem=send_sems.at[kk], recv_sem=recv_sems.at[kk],
                                                device_id=to, device_id_type=MESH)

        first = [copy(j, in_ref.at[c], out_ref.at[me, c], (cx, cy, c)) for j, (cx, cy) in enumerate(chips)]
        for cp in first:
            cp.start()
        passed = []
        for j, (cx, cy) in enumerate(chips):
            slot = out_ref.at[2 * cx + cy, c]
            copy(j, slot, slot, (x, y, c)).wait_recv()
            fwd = copy(3 + j, slot, slot, sib)
            fwd.start()
            passed.append(fwd)
        for j, (cx, cy) in enumerate(chips):
            slot = out_ref.at[2 * cx + cy, 1 - c]
            copy(3 + j, slot, slot, (x, y, c)).wait_recv()
        for cp in first + passed:
            cp.wait_send()
        mine.wait()

    return pl.pallas_call(
        body,
        out_shape=jax.ShapeDtypeStruct((N_CHIPS, 2, rh, w), shard.dtype),
        in_specs=[_ANY], out_specs=_ANY,
        scratch_shapes=[pltpu.SemaphoreType.DMA((6,)), pltpu.SemaphoreType.DMA((6,)), pltpu.SemaphoreType.DMA],
        name=name,
    )(shard)


def _swap_halves(g, name):
    n, _, rh, w = g.shape

    def body(in_ref, out_ref, send_sems, recv_sems):
        x, y, c = _place()
        copies = [pltpu.make_async_remote_copy(src_ref=in_ref.at[j, 1 - c], dst_ref=out_ref.at[j], send_sem=send_sems.at[j],
                                               recv_sem=recv_sems.at[j], device_id=(x, y, 1 - c), device_id_type=MESH)
                  for j in range(n)]
        for cp in copies:
            cp.start()
        for cp in copies:
            cp.wait()

    return pl.pallas_call(
        body, out_shape=jax.ShapeDtypeStruct((n, rh, w), g.dtype), in_specs=[_ANY], out_specs=_ANY,
        scratch_shapes=[pltpu.SemaphoreType.DMA((n,)), pltpu.SemaphoreType.DMA((n,))], name=name,
    )(g)


def _scatter_to_chips(p, name):
    n, rh, w = p.shape

    def body(in_ref, out_ref, send_sems, recv_sems, local_sem):
        x, y, c = _place()
        me = 2 * x + y
        mine = pltpu.make_async_copy(in_ref.at[me], out_ref.at[me], local_sem)
        mine.start()
        sends = []
        for j, (cx, cy) in enumerate(_other_chips(x, y)):
            cp = pltpu.make_async_remote_copy(src_ref=in_ref.at[2 * cx + cy], dst_ref=out_ref.at[me], send_sem=send_sems.at[j],
                                              recv_sem=recv_sems.at[j], device_id=(cx, cy, c), device_id_type=MESH)
            cp.start()
            sends.append(cp)
        for j, (cx, cy) in enumerate(_other_chips(x, y)):
            slot = out_ref.at[2 * cx + cy]
            pltpu.make_async_remote_copy(src_ref=slot, dst_ref=slot, send_sem=send_sems.at[j], recv_sem=recv_sems.at[j],
                                         device_id=(x, y, c), device_id_type=MESH).wait_recv()
        for cp in sends:
            cp.wait_send()
        mine.wait()

    return pl.pallas_call(
        body, out_shape=jax.ShapeDtypeStruct((n, rh, w), p.dtype), in_specs=[_ANY], out_specs=_ANY,
        scratch_shapes=[pltpu.SemaphoreType.DMA((3,)), pltpu.SemaphoreType.DMA((3,)), pltpu.SemaphoreType.DMA], name=name,
    )(p)


def _join_halves(r, name):
    rh, w = r.shape

    def body(in_ref, out_ref, send_sem, recv_sem, local_sem):
        x, y, c = _place()
        mine = pltpu.make_async_copy(in_ref, out_ref.at[c], local_sem)
        mine.start()
        cp = pltpu.make_async_remote_copy(src_ref=in_ref, dst_ref=out_ref.at[c], send_sem=send_sem, recv_sem=recv_sem,
                                          device_id=(x, y, 1 - c), device_id_type=MESH)
        cp.start()
        other = out_ref.at[1 - c]
        pltpu.make_async_remote_copy(src_ref=other, dst_ref=other, send_sem=send_sem, recv_sem=recv_sem,
                                     device_id=(x, y, c), device_id_type=MESH).wait_recv()
        cp.wait_send()
        mine.wait()

    return pl.pallas_call(
        body, out_shape=jax.ShapeDtypeStruct((2, rh, w), r.dtype), in_specs=[_ANY], out_specs=_ANY,
        scratch_shapes=[pltpu.SemaphoreType.DMA, pltpu.SemaphoreType.DMA, pltpu.SemaphoreType.DMA], name=name,
    )(r)


def _gather_all(small, name):
    r, w = small.shape
    flips = [(dx, dy, dc) for dx in (0, 1) for dy in (0, 1) for dc in (0, 1) if (dx, dy, dc) != (0, 0, 0)]

    def body(in_ref, out_ref, send_sems, recv_sems, local_sem):
        x, y, c = _place()
        me = 4 * x + 2 * y + c
        mine = pltpu.make_async_copy(in_ref, out_ref.at[me], local_sem)
        mine.start()
        sends = []
        for j, (dx, dy, dc) in enumerate(flips):
            to = (x ^ dx, y ^ dy, c ^ dc)
            cp = pltpu.make_async_remote_copy(src_ref=in_ref, dst_ref=out_ref.at[me], send_sem=send_sems.at[j],
                                              recv_sem=recv_sems.at[j], device_id=to, device_id_type=MESH)
            cp.start()
            sends.append(cp)
        for j, (dx, dy, dc) in enumerate(flips):
            slot = out_ref.at[4 * (x ^ dx) + 2 * (y ^ dy) + (c ^ dc)]
            pltpu.make_async_remote_copy(src_ref=slot, dst_ref=slot, send_sem=send_sems.at[j], recv_sem=recv_sems.at[j],
                                         device_id=(x, y, c), device_id_type=MESH).wait_recv()
        for cp in sends:
            cp.wait_send()
        mine.wait()

    return pl.pallas_call(
        body, out_shape=jax.ShapeDtypeStruct((8, r, w), small.dtype), in_specs=[_ANY], out_specs=_ANY,
        scratch_shapes=[pltpu.SemaphoreType.DMA((7,)), pltpu.SemaphoreType.DMA((7,)), pltpu.SemaphoreType.DMA], name=name,
    )(small)


_BIG = ("w_in_fox", "w_in_dil", "w_mem_kv", "w_out", "w_up", "w_down")
_BIG_SHARD_SHAPES = {
    "w_in_fox": (2, D_MODEL, FOX_IN // N_CHIPS), "w_in_dil": (2, D_MODEL, DIL_IN // N_CHIPS),
    "w_mem_kv": (DEPTH, D_MODEL // N_CHIPS, 2 * D_MEMQ), "w_out": (DEPTH, D_MODEL // N_CHIPS, D_MODEL),
    "w_up": (DEPTH, D_MODEL, 2 * D_FF // N_CHIPS), "w_down": (DEPTH, D_FF // N_CHIPS, D_MODEL),
}
_CONVW_SHARD = (DEPTH, 3, 2 * D_FF // N_CHIPS)
_COL_SHARDED = ("w_in_fox", "w_in_dil", "w_up")
_HALF_ROWS_MULT = 256


def _flat_rows():
    n = sum(int(np.prod(s)) for s in _BIG_SHARD_SHAPES.values())
    rows = -(-n // FLAT_W)
    half = -(-rows // (2 * _HALF_ROWS_MULT)) * _HALF_ROWS_MULT
    return n, half


def _pack_flat(parts, dtype):
    n, half = _flat_rows()
    flat = jnp.concatenate([p.reshape(-1).astype(dtype) for p in parts])
    flat = jnp.pad(flat, (0, 2 * half * FLAT_W - flat.shape[0]))
    return flat.reshape(2, half, FLAT_W)


def _unpack_flat(flat):
    lead = flat.shape[:-1]
    out, off = {}, 0
    for name in _BIG:
        shp = _BIG_SHARD_SHAPES[name]
        n = int(np.prod(shp))
        out[name] = flat[..., off:off + n].reshape(lead + shp)
        off += n
    return out


_SMALL = (("norm_mix", (DEPTH, D_MODEL)), ("norm_mem", (DEPTH, D_MODEL)), ("norm_ffn", (DEPTH, D_MODEL)),
          ("conv_b", (DEPTH, 2 * D_FF)), ("norm_final", (D_MODEL,)), ("b_forget", (2, N_MIX)), ("conv_w", (DEPTH, 3, 2 * D_FF)))


def _pack_small(vals, spec):
    flat = jnp.concatenate([vals[n].reshape(-1).astype(F32) for n, _ in spec])
    rows = -(-flat.shape[0] // (8 * 128)) * 8
    return jnp.pad(flat, (0, rows * 128 - flat.shape[0])).reshape(rows, 128)


def _unpack_small(buf, spec):
    flat, out, off = buf.reshape(-1), {}, 0
    for n, shp in spec:
        k = int(np.prod(shp))
        out[n] = flat[off:off + k].reshape(shp)
        off += k
    return out


def _interleave_cols(a):
    lead = a.shape[:-1]
    nf = D_FF // CONV_TF
    return a.reshape(lead + (2, nf, CONV_TF)).swapaxes(-3, -2).reshape(lead + (2 * D_FF,))


def _deinterleave_cols(a):
    lead = a.shape[:-1]
    nf = D_FF // CONV_TF
    return a.reshape(lead + (nf, 2, CONV_TF)).swapaxes(-3, -2).reshape(lead + (2 * D_FF,))


def _fox_cols_to_kernel(w):
    qkv, f, qm = w[:, :3 * D_MIX], w[:, 3 * D_MIX:3 * D_MIX + N_MIX], w[:, 3 * D_MIX + N_MIX:]
    return jnp.concatenate([qkv, qm, f, jnp.zeros((w.shape[0], 128 - N_MIX), w.dtype)], axis=1)


def _fox_cols_from_kernel(w):
    qkv, qm, f = w[:, :3 * D_MIX], w[:, 3 * D_MIX:3 * D_MIX + D_MEMQ], w[:, 3 * D_MIX + D_MEMQ:3 * D_MIX + D_MEMQ + N_MIX]
    return jnp.concatenate([qkv, f, qm], axis=1)


def _rope_pair_tables(s):
    inv = 1.0 / (ROPE_THETA ** (jnp.arange(0, HEAD_DIM, 2, dtype=F32) / HEAD_DIM))
    ang = jnp.arange(s, dtype=F32)[:, None] * inv[None, :]
    cos, sin = jnp.cos(ang), jnp.sin(ang)
    return jnp.concatenate([cos, cos, cos, cos], axis=1), jnp.concatenate([-sin, sin, -sin, sin], axis=1)


def _local_step(x, mem, target, small, wts):
    s = x.shape[0]
    cos2, sin2 = _rope_pair_tables(s)
    saved = []
    h = x
    for l in range(DEPTH):
        fox = l % 2 == 0
        slot = l // 2
        tag = f"L{l}"
        g_mix, g_mem, g_ffn = (small[n][l:l + 1] for n in ("norm_mix", "norm_mem", "norm_ffn"))
        xn = _rms_fwd(h, g_mix, f"rms_mix_{tag}")
        proj = _matmul(xn, wts["w_in"][l], "nn", F32, f"mm_in_{tag}")
        if fox:
            bfg = jnp.pad(small["b_forget"][slot:slot + 1], ((0, 0), (0, 128 - N_MIX)))
            qp, kp, vp, qmp = _prep_fox_fwd(proj, bfg, f"prep_fox_{tag}")
        else:
            bfg = None
            qp, kp, vp, qmp = _prep_dil_fwd(proj, cos2, sin2, f"prep_dil_{tag}")
        mn = _rms_fwd(mem, g_mem, f"rms_mem_{tag}")
        kvm = _matmul(mn, wts["w_mem_kv"][l], "nn", F32, f"mm_memkv_{tag}")
        kmp, vmp = _pad_heads([(kvm, 0, N_MEM), (kvm, D_MEMQ, N_MEM)], (1.0, 1.0), (N_MEM * HP,) * 2, f"pad_memkv_{tag}")
        mode = "causal" if fox else "dilated"
        o_mix, lse_mix = _flash_fwd(qp, kp, vp, mode, f"flash_{mode}_fwd_{tag}")
        o_mem, lse_mem = _flash_fwd(qmp, kmp, vmp, "full", f"flash_mem_fwd_{tag}")
        heads = _unpad_heads([o_mix, o_mem], (1.0, 1.0), BF16, f"unpad_heads_{tag}")
        h_mid = _matmul(heads, wts["w_out"][l], "nn", F32, f"mm_out_{tag}", residual=h)
        xn2 = _rms_fwd(h_mid, g_ffn, f"rms_ffn_{tag}")
        u = _matmul(xn2, wts["w_up"][l], "nn", F32, f"mm_up_{tag}")
        act = _conv_fwd(u, wts["conv_w"][l], wts["conv_b"][l], f"conv_fwd_{tag}")
        h_out = _matmul(act, wts["w_down"][l], "nn", F32, f"mm_down_{tag}", residual=h_mid)
        saved.append(dict(h=h, xn=xn, proj=proj, bfg=bfg, qp=qp, kp=kp, vp=vp, qmp=qmp, mn=mn, kmp=kmp, vmp=vmp, o_mix=o_mix,
                          lse_mix=lse_mix, o_mem=o_mem, lse_mem=lse_mem, heads=heads, h_mid=h_mid, xn2=xn2, u=u, act=act))
        h = h_out

    loss_blk, dh, dg_final = _loss_head(h, small["norm_final"].reshape(1, D_MODEL), target, "loss_head")

    grads = {k: [None] * DEPTH for k in ("w_in", "w_mem_kv", "w_out", "w_up", "w_down", "conv_w", "conv_b", "norm_mix", "norm_mem",
                                         "norm_ffn")}
    grads["b_forget"] = [None, None]
    for l in reversed(range(DEPTH)):
        fox = l % 2 == 0
        tag = f"L{l}"
        sv = saved[l]
        g_mix, g_mem, g_ffn = (small[n][l:l + 1] for n in ("norm_mix", "norm_mem", "norm_ffn"))
        da = _matmul(dh, wts["w_down"][l], "nt", F32, f"mm_da_{tag}")
        grads["w_down"][l] = _matmul(sv["act"], dh, "tn", BF16, f"mm_dwdown_{tag}")
        du, dcw, dcb = _conv_bwd(sv["u"], da, wts["conv_w"][l], wts["conv_b"][l], f"conv_bwd_{tag}")
        grads["conv_w"][l], grads["conv_b"][l] = dcw, dcb
        dxn2 = _matmul(du, wts["w_up"][l], "nt", F32, f"mm_dxn2_{tag}")
        grads["w_up"][l] = _matmul(sv["xn2"], du, "tn", BF16, f"mm_dwup_{tag}")
        dh_mid, grads["norm_ffn"][l] = _rms_bwd(sv["h_mid"], g_ffn, dxn2, dh, f"rms_ffn_bwd_{tag}")
        dheads = _matmul(dh_mid, wts["w_out"][l], "nt", F32, f"mm_dheads_{tag}")
        grads["w_out"][l] = _matmul(sv["heads"], dh_mid, "tn", BF16, f"mm_dwout_{tag}")
        do_mix, do_mem = _pad_heads([(dheads, 0, N_MIX), (dheads, D_MIX, N_MEM)], (1.0, 1.0), (N_MIX * HP, N_MEM * HP),
                                    f"pad_dheads_{tag}")
        mode = "causal" if fox else "dilated"
        dqp, dkp, dvp = _flash_bwd(sv["qp"], sv["kp"], sv["vp"], sv["o_mix"], do_mix, sv["lse_mix"], mode, f"flash_{mode}_bwd_{tag}")
        dqmp, dkmp, dvmp = _flash_bwd(sv["qmp"], sv["kmp"], sv["vmp"], sv["o_mem"], do_mem, sv["lse_mem"], "full",
                                      f"flash_mem_bwd_{tag}")
        if fox:
            dproj, dbf = _prep_fox_bwd(dqp, dkp, dvp, dqmp, sv["proj"], sv["bfg"], f"prep_fox_bwd_{tag}")
            grads["b_forget"][l // 2] = dbf[0, :N_MIX]
        else:
            dproj = _prep_dil_bwd(dqp, dkp, dvp, dqmp, cos2, sin2, f"prep_dil_bwd_{tag}")
        dkvm = _unpad_heads([dkmp, dvmp], (LN2, 1.0), BF16, f"unpad_dkvm_{tag}")
        grads["w_mem_kv"][l] = _matmul(sv["mn"], dkvm, "tn", BF16, f"mm_dwmemkv_{tag}")
        dmn = _matmul(dkvm, wts["w_mem_kv"][l], "nt", F32, f"mm_dmn_{tag}")
        _, grads["norm_mem"][l] = _rms_bwd(mem, g_mem, dmn, None, f"rms_mem_bwd_{tag}")
        dxn = _matmul(dproj, wts["w_in"][l], "nt", F32, f"mm_dxn_{tag}")
        grads["w_in"][l] = _matmul(sv["xn"], dproj, "tn", BF16, f"mm_dwin_{tag}")
        dh, grads["norm_mix"][l] = _rms_bwd(sv["h"], g_mix, dxn, dh_mid, f"rms_mix_bwd_{tag}")
    grads["norm_final"] = dg_final
    return loss_blk, dh, grads


def _full_from_chips(per_chip, name):
    axis = 2 if name in _COL_SHARDED else 1
    n_layers = per_chip.shape[1]
    return [jnp.concatenate([per_chip[j, l] for j in range(N_CHIPS)], axis=axis - 1) for l in range(n_layers)]


def _shards_from_full(layers, name):
    axis = 1 if name in _COL_SHARDED else 0
    return jnp.stack([jnp.stack(jnp.split(g, N_CHIPS, axis=axis)) for g in layers], axis=1)


def kernel(x, mem, norm_mix, norm_mem, norm_ffn, w_in_fox, b_forget, w_in_dil, w_mem_kv, w_out, w_up, conv_w, conv_b, w_down, norm_final, loss_target, m_norm_mix, m_norm_mem, m_norm_ffn, m_w_in_fox, m_b_forget, m_w_in_dil, m_w_mem_kv, m_w_out, m_w_up, m_conv_w, m_conv_b, m_w_down, m_norm_final, v_norm_mix, v_norm_mem, v_norm_ffn, v_w_in_fox, v_b_forget, v_w_in_dil, v_w_mem_kv, v_w_out, v_w_up, v_conv_w, v_conv_b, v_w_down, v_norm_final):
    w_sh = dict(w_in_fox=w_in_fox, w_in_dil=w_in_dil, w_mem_kv=w_mem_kv, w_out=w_out, w_up=w_up, w_down=w_down, conv_w=conv_w)
    m_sh = dict(w_in_fox=m_w_in_fox, w_in_dil=m_w_in_dil, w_mem_kv=m_w_mem_kv, w_out=m_w_out, w_up=m_w_up, w_down=m_w_down, conv_w=m_conv_w)
    v_sh = dict(w_in_fox=v_w_in_fox, w_in_dil=v_w_in_dil, w_mem_kv=v_w_mem_kv, w_out=v_w_out, w_up=v_w_up, w_down=v_w_down, conv_w=v_conv_w)
    small = dict(norm_mix=norm_mix, norm_mem=norm_mem, norm_ffn=norm_ffn, conv_b=conv_b, norm_final=norm_final, b_forget=b_forget)
    m_small = dict(norm_mix=m_norm_mix, norm_mem=m_norm_mem, norm_ffn=m_norm_ffn, conv_b=m_conv_b, norm_final=m_norm_final, b_forget=m_b_forget)
    v_small = dict(norm_mix=v_norm_mix, norm_mem=v_norm_mem, norm_ffn=v_norm_ffn, conv_b=v_conv_b, norm_final=v_norm_final, b_forget=v_b_forget)
    chip = 2 * lax.axis_index("x") + lax.axis_index("y")
    core = lax.axis_index("c")

    gathered = _gather_shards(_pack_flat([w_sh[n] for n in _BIG], BF16), "gather_weights")
    per_chip = _unpack_flat(gathered.reshape(N_CHIPS, -1))
    conv_spec = (("conv_w", _CONVW_SHARD),)
    conv_all = _gather_all(_pack_small(dict(conv_w=conv_w), conv_spec), "gather_conv_w")
    conv_w_full = jnp.concatenate([_unpack_small(conv_all[2 * j], conv_spec)["conv_w"] for j in range(N_CHIPS)], axis=-1)
    fox_full = [_fox_cols_to_kernel(w) for w in _full_from_chips(per_chip["w_in_fox"], "w_in_fox")]
    dil_full = _full_from_chips(per_chip["w_in_dil"], "w_in_dil")
    wts = dict(
        w_in=[fox_full[0], dil_full[0], fox_full[1], dil_full[1]],
        w_mem_kv=_full_from_chips(per_chip["w_mem_kv"], "w_mem_kv"),
        w_out=_full_from_chips(per_chip["w_out"], "w_out"),
        w_up=[_interleave_cols(w) for w in _full_from_chips(per_chip["w_up"], "w_up")],
        w_down=_full_from_chips(per_chip["w_down"], "w_down"),
        conv_w=[_interleave_cols(conv_w_full[l]) for l in range(DEPTH)],
        conv_b=[_interleave_cols(conv_b[l:l + 1]) for l in range(DEPTH)],
    )

    loss_blk, dx, grads = _local_step(x[0], mem[0], loss_target[0], small, wts)
    loss = lax.psum(loss_blk[0, 0], ("x", "y", "c"))

    g_layers = dict(
        w_in_fox=[_fox_cols_from_kernel(grads["w_in"][0]), _fox_cols_from_kernel(grads["w_in"][2])],
        w_in_dil=[grads["w_in"][1], grads["w_in"][3]],
        w_mem_kv=grads["w_mem_kv"], w_out=grads["w_out"],
        w_up=[_deinterleave_cols(g) for g in grads["w_up"]], w_down=grads["w_down"],
    )
    g_sh = {n: _shards_from_full(g_layers[n], n) for n in _BIG}
    g_flat = jnp.stack([_pack_flat([g_sh[n][j] for n in _BIG], BF16) for j in range(N_CHIPS)])
    from_sibling = _swap_halves(g_flat, "grad_swap_halves")
    mine = lax.dynamic_index_in_dim(g_flat, core, axis=1, keepdims=False)
    pair_sum = _add_pairs(mine, from_sibling, "grad_add_pairs")
    from_chips = _scatter_to_chips(pair_sum, "grad_scatter_chips")
    half = _sum_rows(from_chips, F32, "grad_sum_chips")
    reduced = _join_halves(half, "grad_join_halves")
    g_big = _unpack_flat(reduced.reshape(-1))

    g_small_local = dict(
        norm_mix=jnp.concatenate(grads["norm_mix"]), norm_mem=jnp.concatenate(grads["norm_mem"]),
        norm_ffn=jnp.concatenate(grads["norm_ffn"]),
        conv_b=jnp.concatenate([_deinterleave_cols(g) for g in grads["conv_b"]]),
        norm_final=grads["norm_final"], b_forget=jnp.stack(grads["b_forget"]),
        conv_w=jnp.stack([_deinterleave_cols(g) for g in grads["conv_w"]]),
    )
    small_all = _gather_all(_pack_small(g_small_local, _SMALL), "small_gather_all")
    g_small = _unpack_small(_sum_rows(small_all, F32, "small_sum"), _SMALL)
    ncol = 2 * D_FF // N_CHIPS
    g_big["conv_w"] = lax.dynamic_slice_in_dim(g_small["conv_w"], chip * ncol, ncol, axis=2)

    out_g, out_d, out_m, out_v = {}, {}, {}, {}
    for n in _BIG + ("conv_w",):
        shp = w_sh[n].shape
        two_d = (-1, shp[-1])
        d, mo, vo = _adamw(w_sh[n].reshape(two_d), g_big[n].reshape(two_d), m_sh[n].reshape(two_d), v_sh[n].reshape(two_d), f"adamw_{n}")
        out_g[n], out_d[n], out_m[n], out_v[n] = g_big[n].reshape(shp), d.reshape(shp), mo.reshape(shp), vo.reshape(shp)
    spec = _SMALL[:-1]
    d, mo, vo = _adamw(_pack_small(small, spec), _pack_small(g_small, spec), _pack_small(m_small, spec), _pack_small(v_small, spec),
                       "adamw_small")
    d, mo, vo = _unpack_small(d, spec), _unpack_small(mo, spec), _unpack_small(vo, spec)
    for n, shp in spec:
        out_g[n], out_d[n], out_m[n], out_v[n] = g_small[n].reshape(shp), d[n], mo[n], vo[n]

    order = ("norm_mix", "norm_mem", "norm_ffn", "w_in_fox", "b_forget", "w_in_dil", "w_mem_kv", "w_out", "w_up", "conv_w", "conv_b",
             "w_down", "norm_final")
    return (loss, dx[None], *[out_g[n] for n in order], *[out_d[n] for n in order], *[out_m[n] for n in order],
            *[out_v[n] for n in order])
```

```python
import functools
import math

import numpy as np
import jax
import jax.numpy as jnp
from jax import lax
from jax.experimental import pallas as pl
from jax.experimental.pallas import tpu as pltpu

F32 = jnp.float32
BF16 = jnp.bfloat16

D_MODEL = 1024
DEPTH = 4
HEAD_DIM = 64
N_MIX = 12
N_MEM = 4
D_MIX = N_MIX * HEAD_DIM
D_MEMQ = N_MEM * HEAD_DIM
D_FF = 2816
FOX_IN = 3 * D_MIX + N_MIX + D_MEMQ
DIL_IN = 3 * D_MIX + D_MEMQ
HP = 128
SCALE = HEAD_DIM ** -0.5
NEG = -1e30
NORM_EPS = 1e-6
DIL_MAX = 2048
LOG2E = 1.0 / math.log(2.0)
LN2 = math.log(2.0)
QSCALE = SCALE * LOG2E
HEADS_PER_STEP = 2
ROPE_THETA = 10000.0
N_CHIPS = 4
FLAT_W = 1024
CONV_TF = 128
FLASH_BLK = 512
VMEM_LIMIT = 48 * 1024 * 1024

ADAM_LR = 0.001
ADAM_B1 = 0.9
ADAM_B2 = 0.999
ADAM_EPS = 1e-08
ADAM_WD = 0.01
ADAM_STEP = 10

MESH = pl.DeviceIdType.MESH


def _cp(*sem):
    return pltpu.CompilerParams(dimension_semantics=tuple(sem), vmem_limit_bytes=VMEM_LIMIT)


def _tile(n, cap, mult=128):
    if n <= cap:
        return n
    t = (cap // mult) * mult
    while t >= mult:
        if n % t == 0:
            return t
        t -= mult
    raise ValueError(f"no tile for {n} under {cap}")


_DIMS = {"nn": (((1,), (0,)), ((), ())), "nt": (((1,), (1,)), ((), ())), "tn": (((0,), (0,)), ((), ()))}


def _matmul(a, b, mode, out_dtype, name, residual=None):
    if mode == "nn":
        (m, k), n = a.shape, b.shape[1]
    elif mode == "nt":
        (m, k), n = a.shape, b.shape[0]
    else:
        (k, m), n = a.shape, b.shape[1]
    tn = 1408 if n % 1408 == 0 else _tile(n, 1024)
    tm = 1408 if (m % 1408 == 0 and tn <= 1024) else _tile(m, 1024 if tn <= 1024 else 512)
    tk = _tile(k, 1408)
    nk = k // tk
    dims = _DIMS[mode]
    has_res = residual is not None

    def body(*refs):
        if has_res:
            a_ref, b_ref, r_ref, o_ref = refs[:4]
        else:
            a_ref, b_ref, o_ref = refs[:3]
        part = lax.dot_general(a_ref[...].astype(BF16), b_ref[...].astype(BF16), dims, preferred_element_type=F32)
        if nk == 1:
            if has_res:
                part = part + r_ref[...]
            o_ref[...] = part.astype(o_ref.dtype)
            return
        acc_ref = refs[-1]
        kk = pl.program_id(2)

        @pl.when(kk == 0)
        def _():
            acc_ref[...] = part

        @pl.when(kk > 0)
        def _():
            acc_ref[...] += part

        @pl.when(kk == nk - 1)
        def _():
            tot = acc_ref[...]
            if has_res:
                tot = tot + r_ref[...]
            o_ref[...] = tot.astype(o_ref.dtype)

    if mode == "nn":
        a_spec = pl.BlockSpec((tm, tk), lambda i, j, kk: (i, kk))
        b_spec = pl.BlockSpec((tk, tn), lambda i, j, kk: (kk, j))
    elif mode == "nt":
        a_spec = pl.BlockSpec((tm, tk), lambda i, j, kk: (i, kk))
        b_spec = pl.BlockSpec((tn, tk), lambda i, j, kk: (j, kk))
    else:
        a_spec = pl.BlockSpec((tk, tm), lambda i, j, kk: (kk, i))
        b_spec = pl.BlockSpec((tk, tn), lambda i, j, kk: (kk, j))
    in_specs = [a_spec, b_spec]
    args = [a, b]
    if has_res:
        in_specs.append(pl.BlockSpec((tm, tn), lambda i, j, kk: (i, j)))
        args.append(residual)
    return pl.pallas_call(
        body,
        out_shape=jax.ShapeDtypeStruct((m, n), out_dtype),
        grid=(m // tm, n // tn, nk),
        in_specs=in_specs,
        out_specs=pl.BlockSpec((tm, tn), lambda i, j, kk: (i, j)),
        scratch_shapes=[pltpu.VMEM((tm, tn), F32)] if nk > 1 else [],
        compiler_params=_cp("parallel", "parallel", "arbitrary"),
        name=name,
    )(*args)


def _rms_fwd(h, g, name):
    r, d = h.shape
    tr = _tile(r, 512, 8)

    def body(h_ref, g_ref, o_ref):
        x = h_ref[...]
        rstd = lax.rsqrt(jnp.mean(x * x, axis=-1, keepdims=True) + NORM_EPS)
        o_ref[...] = ((x * rstd) * g_ref[...]).astype(o_ref.dtype)

    return pl.pallas_call(
        body,
        out_shape=jax.ShapeDtypeStruct((r, d), BF16),
        grid=(r // tr,),
        in_specs=[pl.BlockSpec((tr, d), lambda i: (i, 0)), pl.BlockSpec((1, d), lambda i: (0, 0))],
        out_specs=pl.BlockSpec((tr, d), lambda i: (i, 0)),
        compiler_params=_cp("parallel"),
        name=name,
    )(h, g)


def _rms_bwd(h, g, dy, dres, name):
    r, d = h.shape
    tr = _tile(r, 512, 8)
    need_dh = dres is not None

    def body(*refs):
        if need_dh:
            h_ref, g_ref, dy_ref, dres_ref, dh_ref, dg_ref = refs
        else:
            h_ref, g_ref, dy_ref, dg_ref = refs
        i = pl.program_id(0)
        x = h_ref[...]
        rstd = lax.rsqrt(jnp.mean(x * x, axis=-1, keepdims=True) + NORM_EPS)
        nrm = x * rstd
        dyv = dy_ref[...].astype(F32)
        part = jnp.sum(dyv * nrm, axis=0, keepdims=True)

        @pl.when(i == 0)
        def _():
            dg_ref[...] = part

        @pl.when(i > 0)
        def _():
            dg_ref[...] += part

        if need_dh:
            gy = dyv * g_ref[...]
            dx = rstd * (gy - nrm * jnp.mean(gy * nrm, axis=-1, keepdims=True))
            dh_ref[...] = dres_ref[...] + dx

    row = pl.BlockSpec((tr, d), lambda i: (i, 0))
    vec = pl.BlockSpec((1, d), lambda i: (0, 0))
    if need_dh:
        return pl.pallas_call(
            body,
            out_shape=(jax.ShapeDtypeStruct((r, d), F32), jax.ShapeDtypeStruct((1, d), F32)),
            grid=(r // tr,),
            in_specs=[row, vec, row, row],
            out_specs=(row, vec),
            compiler_params=_cp("arbitrary"),
            name=name,
        )(h, g, dy, dres)
    return None, pl.pallas_call(
        body,
        out_shape=jax.ShapeDtypeStruct((1, d), F32),
        grid=(r // tr,),
        in_specs=[row, vec, row],
        out_specs=vec,
        compiler_params=_cp("arbitrary"),
        name=name,
    )(h, g, dy)


def _loss_head(h, g, target, name):
    r, d = h.shape
    tr = _tile(r, 512, 8)

    def body(h_ref, g_ref, t_ref, loss_ref, dh_ref, dg_ref):
        i = pl.program_id(0)
        x = h_ref[...]
        gv = g_ref[...]
        rstd = lax.rsqrt(jnp.mean(x * x, axis=-1, keepdims=True) + NORM_EPS)
        nrm = x * rstd
        err = nrm * gv - t_ref[...]
        lpart = 0.5 * jnp.sum(jnp.mean(err * err, axis=-1, keepdims=True), axis=0, keepdims=True)
        dyv = err * (1.0 / d)
        gpart = jnp.sum(dyv * nrm, axis=0, keepdims=True)

        @pl.when(i == 0)
        def _():
            loss_ref[...] = jnp.broadcast_to(lpart, loss_ref.shape)
            dg_ref[...] = gpart

        @pl.when(i > 0)
        def _():
            loss_ref[...] += jnp.broadcast_to(lpart, loss_ref.shape)
            dg_ref[...] += gpart

        gy = dyv * gv
        dh_ref[...] = rstd * (gy - nrm * jnp.mean(gy * nrm, axis=-1, keepdims=True))

    row = pl.BlockSpec((tr, d), lambda i: (i, 0))
    vec = pl.BlockSpec((1, d), lambda i: (0, 0))
    lsp = pl.BlockSpec((1, 128), lambda i: (0, 0))
    return pl.pallas_call(
        body,
        out_shape=(jax.ShapeDtypeStruct((1, 128), F32), jax.ShapeDtypeStruct((r, d), F32), jax.ShapeDtypeStruct((1, d), F32)),
        grid=(r // tr,),
        in_specs=[row, vec, row],
        out_specs=(lsp, row, vec),
        compiler_params=_cp("arbitrary"),
        name=name,
    )(h, g, target)


def _lane(shape):
    return lax.broadcasted_iota(jnp.int32, shape, 1)


def _head_tile(pair_tile, odd):
    return pltpu.roll(pair_tile, 64, 1) if odd else pair_tile


def _pair_tile(even_tile, odd_tile):
    lane = _lane(even_tile.shape)
    return jnp.where(lane < 64, even_tile, pltpu.roll(odd_tile, 64, 1))


def _pad_heads(xs, scales, out_widths, name):
    r = xs[0][0].shape[0]
    tr = _tile(r, 256, 8)
    n_in = len(xs)

    def body(*refs):
        for idx in range(n_in):
            x_ref, o_ref = refs[idx], refs[n_in + idx]
            nh = xs[idx][2]
            for p in range(nh // 2):
                t = x_ref[:, p * 128:(p + 1) * 128].astype(F32) * scales[idx]
                lane = _lane(t.shape)
                o_ref[:, (2 * p) * HP:(2 * p + 1) * HP] = jnp.where(lane < 64, t, 0.0).astype(o_ref.dtype)
                o_ref[:, (2 * p + 1) * HP:(2 * p + 2) * HP] = jnp.where(lane < 64, pltpu.roll(t, 64, 1), 0.0).astype(o_ref.dtype)

    in_specs, args, out_specs, out_shape = [], [], [], []
    for (arr, c0, nh), w in zip(xs, out_widths):
        wcols = nh * 64
        assert c0 % wcols == 0 or c0 == 0
        blk = c0 // wcols if wcols else 0
        in_specs.append(pl.BlockSpec((tr, wcols), functools.partial(lambda i, b: (i, b), b=blk)))
        args.append(arr)
        out_specs.append(pl.BlockSpec((tr, w), lambda i: (i, 0)))
        out_shape.append(jax.ShapeDtypeStruct((r, w), BF16))
    return pl.pallas_call(
        body,
        out_shape=tuple(out_shape),
        grid=(r // tr,),
        in_specs=in_specs,
        out_specs=tuple(out_specs),
        compiler_params=_cp("parallel"),
        name=name,
    )(*args)


def _unpad_heads(xs, scales, out_dtype, name):
    r = xs[0].shape[0]
    tr = _tile(r, 256, 8)
    nhs = [x.shape[1] // HP for x in xs]
    total = sum(nhs) * 64

    def body(*refs):
        o_ref = refs[-1]
        col = 0
        for x_ref, nh, sc in zip(refs[:-1], nhs, scales):
            for p in range(nh // 2):
                ev = x_ref[:, (2 * p) * HP:(2 * p + 1) * HP].astype(F32)
                od = x_ref[:, (2 * p + 1) * HP:(2 * p + 2) * HP].astype(F32)
                o_ref[:, col:col + 128] = (_pair_tile(ev, od) * sc).astype(o_ref.dtype)
                col += 128

    return pl.pallas_call(
        body,
        out_shape=jax.ShapeDtypeStruct((r, total), out_dtype),
        grid=(r // tr,),
        in_specs=[pl.BlockSpec((tr, x.shape[1]), lambda i: (i, 0)) for x in xs],
        out_specs=pl.BlockSpec((tr, total), lambda i: (i, 0)),
        compiler_params=_cp("parallel"),
        name=name,
    )(*xs)


def _bf16_split3(c):
    hi = c.astype(BF16).astype(F32)
    r1 = c - hi
    mid = r1.astype(BF16).astype(F32)
    lo = (r1 - mid).astype(BF16).astype(F32)
    return hi, mid, lo


def _log_sigmoid(z):
    return jnp.minimum(z, 0.0) - jnp.log(1.0 + jnp.exp(-jnp.abs(z)))


def _prep_fox_fwd(proj, bfg, name):
    s = proj.shape[0]
    ts = _tile(s, 256, 8)
    fcol = 3 * D_MIX + D_MEMQ

    def body(p_ref, b_ref, q_ref, k_ref, v_ref, qm_ref, carry_ref):
        i = pl.program_id(0)

        @pl.when(i == 0)
        def _():
            carry_ref[...] = jnp.zeros_like(carry_ref)

        lane = _lane((ts, 128))
        z = p_ref[:, fcol:fcol + 128] + b_ref[...]
        logf = jnp.where(lane < N_MIX, _log_sigmoid(z), 0.0)
        rr = lax.broadcasted_iota(jnp.int32, (ts, ts), 0)
        cc = lax.broadcasted_iota(jnp.int32, (ts, ts), 1)
        tri = jnp.where(cc <= rr, 1.0, 0.0).astype(F32)
        c = jnp.dot(tri, logf, preferred_element_type=F32, precision=lax.Precision.HIGHEST) + carry_ref[0:1, :]
        carry_ref[...] = jnp.broadcast_to(c[ts - 1:ts, :], carry_ref.shape)
        for hh in range(N_MIX):
            p, odd = hh // 2, hh % 2
            ch = jnp.sum(jnp.where(lane == hh, c, 0.0), axis=-1, keepdims=True) * LOG2E
            hi, mid, lo = _bf16_split3(ch)
            qt = _head_tile(p_ref[:, p * 128:(p + 1) * 128], odd) * QSCALE
            kt = _head_tile(p_ref[:, D_MIX + p * 128:D_MIX + (p + 1) * 128], odd)
            vt = _head_tile(p_ref[:, 2 * D_MIX + p * 128:2 * D_MIX + (p + 1) * 128], odd)
            qa = jnp.where(lane == 64, hi, jnp.where(lane == 65, mid, jnp.where(lane == 66, lo, jnp.where(lane < 70, 1.0, 0.0))))
            ka = jnp.where(lane < 67, 1.0, jnp.where(lane == 67, -hi, jnp.where(lane == 68, -mid, jnp.where(lane == 69, -lo, 0.0))))
            q_ref[:, hh * HP:(hh + 1) * HP] = jnp.where(lane < 64, qt, qa).astype(BF16)
            k_ref[:, hh * HP:(hh + 1) * HP] = jnp.where(lane < 64, kt, ka).astype(BF16)
            v_ref[:, hh * HP:(hh + 1) * HP] = jnp.where(lane < 64, vt, 0.0).astype(BF16)
        for hh in range(N_MEM):
            p, odd = hh // 2, hh % 2
            t = _head_tile(p_ref[:, 3 * D_MIX + p * 128:3 * D_MIX + (p + 1) * 128], odd) * QSCALE
            qm_ref[:, hh * HP:(hh + 1) * HP] = jnp.where(lane < 64, t, 0.0).astype(BF16)

    wmix, wmem = N_MIX * HP, N_MEM * HP
    return pl.pallas_call(
        body,
        out_shape=(jax.ShapeDtypeStruct((s, wmix), BF16),) * 3 + (jax.ShapeDtypeStruct((s, wmem), BF16),),
        grid=(s // ts,),
        in_specs=[pl.BlockSpec((ts, proj.shape[1]), lambda i: (i, 0)), pl.BlockSpec((1, 128), lambda i: (0, 0))],
        out_specs=(pl.BlockSpec((ts, wmix), lambda i: (i, 0)),) * 3 + (pl.BlockSpec((ts, wmem), lambda i: (i, 0)),),
        scratch_shapes=[pltpu.VMEM((8, 128), F32)],
        compiler_params=_cp("arbitrary"),
        name=name,
    )(proj, bfg)


def _prep_fox_bwd(dq, dk, dv, dqm, proj, bfg, name):
    s = proj.shape[0]
    ts = _tile(s, 256, 8)
    nb = s // ts
    fcol = 3 * D_MIX + D_MEMQ

    def body(dq_ref, dk_ref, dv_ref, dqm_ref, p_ref, b_ref, o_ref, db_ref, carry_ref):
        i = pl.program_id(0)

        @pl.when(i == 0)
        def _():
            carry_ref[...] = jnp.zeros_like(carry_ref)
            db_ref[...] = jnp.zeros_like(db_ref)

        lane = _lane((ts, 128))
        dc = jnp.zeros((ts, 128), F32)
        for p in range(N_MIX // 2):
            tq, tk, tv = [], [], []
            for odd in (0, 1):
                hh = 2 * p + odd
                dqt = dq_ref[:, hh * HP:(hh + 1) * HP]
                dkt = dk_ref[:, hh * HP:(hh + 1) * HP]
                col = jnp.sum(jnp.where(lane == 64, dqt, 0.0) - jnp.where(lane == 67, dkt, 0.0), axis=-1, keepdims=True)
                dc = dc + jnp.where(lane == hh, col, 0.0)
                tq.append(dqt)
                tk.append(dkt)
                tv.append(dv_ref[:, hh * HP:(hh + 1) * HP])
            o_ref[:, p * 128:(p + 1) * 128] = (_pair_tile(tq[0], tq[1]) * SCALE).astype(o_ref.dtype)
            o_ref[:, D_MIX + p * 128:D_MIX + (p + 1) * 128] = (_pair_tile(tk[0], tk[1]) * LN2).astype(o_ref.dtype)
            o_ref[:, 2 * D_MIX + p * 128:2 * D_MIX + (p + 1) * 128] = _pair_tile(tv[0], tv[1]).astype(o_ref.dtype)
        for p in range(N_MEM // 2):
            ev = dqm_ref[:, (2 * p) * HP:(2 * p + 1) * HP]
            od = dqm_ref[:, (2 * p + 1) * HP:(2 * p + 2) * HP]
            o_ref[:, 3 * D_MIX + p * 128:3 * D_MIX + (p + 1) * 128] = (_pair_tile(ev, od) * SCALE).astype(o_ref.dtype)
        rr = lax.broadcasted_iota(jnp.int32, (ts, ts), 0)
        cc = lax.broadcasted_iota(jnp.int32, (ts, ts), 1)
        triu = jnp.where(cc >= rr, 1.0, 0.0).astype(F32)
        dlogf = jnp.dot(triu, dc, preferred_element_type=F32, precision=lax.Precision.HIGHEST) + carry_ref[0:1, :]
        carry_ref[...] = jnp.broadcast_to(dlogf[0:1, :], carry_ref.shape)
        z = p_ref[:, fcol:fcol + 128] + b_ref[...]
        dz = jnp.where(lane < N_MIX, dlogf / (1.0 + jnp.exp(z)), 0.0)
        o_ref[:, fcol:fcol + 128] = dz.astype(o_ref.dtype)
        db_ref[...] += jnp.sum(dz, axis=0, keepdims=True)

    wmix, wmem = N_MIX * HP, N_MEM * HP
    rev = lambda i: (nb - 1 - i, 0)
    return pl.pallas_call(
        body,
        out_shape=(jax.ShapeDtypeStruct(proj.shape, BF16), jax.ShapeDtypeStruct((1, 128), F32)),
        grid=(nb,),
        in_specs=[pl.BlockSpec((ts, wmix), rev)] * 3 + [pl.BlockSpec((ts, wmem), rev), pl.BlockSpec((ts, proj.shape[1]), rev),
                                                         pl.BlockSpec((1, 128), lambda i: (0, 0))],
        out_specs=(pl.BlockSpec((ts, proj.shape[1]), rev), pl.BlockSpec((1, 128), lambda i: (0, 0))),
        scratch_shapes=[pltpu.VMEM((8, 128), F32)],
        compiler_params=_cp("arbitrary"),
        name=name,
    )(dq, dk, dv, dqm, proj, bfg)


def _rope_partner(x):
    lane = _lane(x.shape)
    return jnp.where((lane % 64) < 32, pltpu.roll(x, 96, 1), pltpu.roll(x, 32, 1))


def _prep_dil_fwd(proj, cos2, sin2, name):
    s = proj.shape[0]
    ts = _tile(s, 256, 8)

    def body(p_ref, c_ref, s_ref, q_ref, k_ref, v_ref, qm_ref):
        lane = _lane((ts, 128))
        cosv, sinv = c_ref[...], s_ref[...]
        for p in range(N_MIX // 2):
            xq = p_ref[:, p * 128:(p + 1) * 128]
            xk = p_ref[:, D_MIX + p * 128:D_MIX + (p + 1) * 128]
            xv = p_ref[:, 2 * D_MIX + p * 128:2 * D_MIX + (p + 1) * 128]
            yq = (xq * cosv + _rope_partner(xq) * sinv) * QSCALE
            yk = xk * cosv + _rope_partner(xk) * sinv
            for odd in (0, 1):
                hh = 2 * p + odd
                q_ref[:, hh * HP:(hh + 1) * HP] = jnp.where(lane < 64, _head_tile(yq, odd), 0.0).astype(BF16)
                k_ref[:, hh * HP:(hh + 1) * HP] = jnp.where(lane < 64, _head_tile(yk, odd), 0.0).astype(BF16)
                v_ref[:, hh * HP:(hh + 1) * HP] = jnp.where(lane < 64, _head_tile(xv, odd), 0.0).astype(BF16)
        for p in range(N_MEM // 2):
            t = p_ref[:, 3 * D_MIX + p * 128:3 * D_MIX + (p + 1) * 128] * QSCALE
            for odd in (0, 1):
                hh = 2 * p + odd
                qm_ref[:, hh * HP:(hh + 1) * HP] = jnp.where(lane < 64, _head_tile(t, odd), 0.0).astype(BF16)

    wmix, wmem = N_MIX * HP, N_MEM * HP
    return pl.pallas_call(
        body,
        out_shape=(jax.ShapeDtypeStruct((s, wmix), BF16),) * 3 + (jax.ShapeDtypeStruct((s, wmem), BF16),),
        grid=(s // ts,),
        in_specs=[pl.BlockSpec((ts, proj.shape[1]), lambda i: (i, 0)), pl.BlockSpec((ts, 128), lambda i: (i, 0)),
                  pl.BlockSpec((ts, 128), lambda i: (i, 0))],
        out_specs=(pl.BlockSpec((ts, wmix), lambda i: (i, 0)),) * 3 + (pl.BlockSpec((ts, wmem), lambda i: (i, 0)),),
        compiler_params=_cp("parallel"),
        name=name,
    )(proj, cos2, sin2)


def _prep_dil_bwd(dq, dk, dv, dqm, cos2, sin2, name):
    s = dq.shape[0]
    ts = _tile(s, 256, 8)

    def body(dq_ref, dk_ref, dv_ref, dqm_ref, c_ref, s_ref, o_ref):
        cosv, sinv = c_ref[...], s_ref[...]
        for p in range(N_MIX // 2):
            e, o = 2 * p, 2 * p + 1
            dyq = _pair_tile(dq_ref[:, e * HP:(e + 1) * HP], dq_ref[:, o * HP:(o + 1) * HP]) * SCALE
            dyk = _pair_tile(dk_ref[:, e * HP:(e + 1) * HP], dk_ref[:, o * HP:(o + 1) * HP]) * LN2
            dxv = _pair_tile(dv_ref[:, e * HP:(e + 1) * HP], dv_ref[:, o * HP:(o + 1) * HP])
            o_ref[:, p * 128:(p + 1) * 128] = (dyq * cosv - _rope_partner(dyq) * sinv).astype(o_ref.dtype)
            o_ref[:, D_MIX + p * 128:D_MIX + (p + 1) * 128] = (dyk * cosv - _rope_partner(dyk) * sinv).astype(o_ref.dtype)
            o_ref[:, 2 * D_MIX + p * 128:2 * D_MIX + (p + 1) * 128] = dxv.astype(o_ref.dtype)
        for p in range(N_MEM // 2):
            e, o = 2 * p, 2 * p + 1
            t = _pair_tile(dqm_ref[:, e * HP:(e + 1) * HP], dqm_ref[:, o * HP:(o + 1) * HP]) * SCALE
            o_ref[:, 3 * D_MIX + p * 128:3 * D_MIX + (p + 1) * 128] = t.astype(o_ref.dtype)

    wmix, wmem = N_MIX * HP, N_MEM * HP
    row = lambda w: pl.BlockSpec((ts, w), lambda i: (i, 0))
    return pl.pallas_call(
        body,
        out_shape=jax.ShapeDtypeStruct((s, DIL_IN), BF16),
        grid=(s // ts,),
        in_specs=[row(wmix)] * 3 + [row(wmem), row(128), row(128)],
        out_specs=row(DIL_IN),
        compiler_params=_cp("parallel"),
        name=name,
    )(dq, dk, dv, dqm, cos2, sin2)


def _mask_bias(mode, blk):
    n = 1 if mode == "causal" else DIL_MAX // blk + 1
    idx = jnp.arange(blk, dtype=jnp.int32)
    dist = jnp.arange(n, dtype=jnp.int32)[:, None, None] * blk + idx[None, :, None] - idx[None, None, :]
    if mode == "causal":
        return jnp.where(dist >= 0, 0.0, NEG).astype(F32)
    cnt = ((dist <= 128).astype(jnp.int32) + ((dist <= 512) & (dist % 4 == 0)).astype(jnp.int32)
           + ((dist <= DIL_MAX) & (dist % 16 == 0)).astype(jnp.int32))
    bias = jnp.where(cnt == 3, math.log2(3.0), jnp.where(cnt == 2, 1.0, 0.0))
    return jnp.where((dist >= 0) & (cnt > 0), bias, NEG).astype(F32)


def _flash_dims(q, k, mode):
    sq, w = q.shape
    sk = k.shape[0]
    tq = _tile(sq, FLASH_BLK, 8)
    tk = sk if mode == "full" else tq
    band = DIL_MAX // tk if mode == "dilated" else None
    return sq, sk, w, tq, tk, sq // tq, sk // tk, band


def _flash_fwd(q, k, v, mode, name):
    sq, sk, w, tq, tk, nq, nk, band = _flash_dims(q, k, mode)
    hb = HEADS_PER_STEP
    wb = hb * HP
    nch = tk // 128
    has_bias = mode != "full"

    def kidx(hp, qi, kj):
        if mode == "full":
            return (kj, hp)
        lo = jnp.maximum(qi - band, 0) if mode == "dilated" else 0
        return (jnp.clip(kj, lo, qi), hp)

    def body(*refs):
        if has_bias:
            q_ref, k_ref, v_ref, b_ref, o_ref, lse_ref, m_ref, l_ref, acc_ref = refs
        else:
            q_ref, k_ref, v_ref, o_ref, lse_ref, m_ref, l_ref, acc_ref = refs
        qi, kj = pl.program_id(1), pl.program_id(2)

        @pl.when(kj == 0)
        def _():
            m_ref[...] = jnp.full_like(m_ref, -jnp.inf)
            l_ref[...] = jnp.zeros_like(l_ref)
            acc_ref[...] = jnp.zeros_like(acc_ref)

        def step(bias_tile):
            for h in range(hb):
                cols = slice(h * HP, (h + 1) * HP)
                sc = lax.dot_general(q_ref[:, cols], k_ref[:, cols], _DIMS["nt"], preferred_element_type=F32)
                if bias_tile is not None:
                    sc = sc + bias_tile()
                m_prev = m_ref[h]
                m_new = jnp.maximum(m_prev, jnp.max(sc, axis=-1, keepdims=True))
                alpha = jnp.exp2(m_prev - m_new)
                psum, chunks = None, []
                for c in range(nch):
                    pc = jnp.exp2(sc[:, c * 128:(c + 1) * 128] - m_new)
                    psum = pc if psum is None else psum + pc
                    chunks.append(pc.astype(BF16))
                p = chunks[0] if nch == 1 else jnp.concatenate(chunks, axis=1)
                l_ref[h] = alpha * l_ref[h] + psum
                acc_ref[h] = alpha * acc_ref[h] + jnp.dot(p, v_ref[:, cols], preferred_element_type=F32)
                m_ref[h] = m_new

        if mode == "full":
            step(None)
        elif mode == "causal":
            pl.when(kj == qi)(lambda: step(lambda: b_ref[0]))
            pl.when(kj < qi)(lambda: step(None))
        else:
            pl.when((kj <= qi) & (kj >= qi - band))(lambda: step(lambda: b_ref[qi - kj]))

        @pl.when(kj == nk - 1)
        def _():
            for h in range(hb):
                cols = slice(h * HP, (h + 1) * HP)
                l = jnp.sum(l_ref[h], axis=-1, keepdims=True)
                o_ref[:, cols] = (acc_ref[h] / l).astype(o_ref.dtype)
                lse_ref[:, cols] = m_ref[h] + jnp.log2(l)

    qspec = pl.BlockSpec((tq, wb), lambda hp, qi, kj: (qi, hp))
    in_specs = [qspec, pl.BlockSpec((tk, wb), kidx), pl.BlockSpec((tk, wb), kidx)]
    args = [q, k, v]
    if has_bias:
        bias = _mask_bias(mode, tq)
        in_specs.append(pl.BlockSpec(bias.shape, lambda hp, qi, kj: (0, 0, 0)))
        args.append(bias)
    return pl.pallas_call(
        body,
        out_shape=(jax.ShapeDtypeStruct((sq, w), BF16), jax.ShapeDtypeStruct((sq, w), F32)),
        grid=(w // wb, nq, nk),
        in_specs=in_specs,
        out_specs=(qspec, qspec),
        scratch_shapes=[pltpu.VMEM((hb, tq, HP), F32), pltpu.VMEM((hb, tq, HP), F32), pltpu.VMEM((hb, tq, HP), F32)],
        compiler_params=_cp("parallel", "parallel", "arbitrary"),
        name=name,
    )(*args)


def _flash_bwd(q, k, v, o, do, lse, mode, name):
    sq, sk, w, tq, tk, nq, nk, band = _flash_dims(q, k, mode)
    hb = HEADS_PER_STEP
    wb = hb * HP
    nch = tk // 128
    has_bias = mode != "full"

    def qidx(hp, kj, qi):
        if mode == "full":
            return (qi, hp)
        hi = jnp.minimum(kj + band, nq - 1) if mode == "dilated" else nq - 1
        return (jnp.clip(qi, kj, hi), hp)

    def body(*refs):
        if has_bias:
            q_ref, k_ref, v_ref, o_ref, do_ref, lse_ref, b_ref, dq_ref, dk_ref, dv_ref, delta_ref = refs
        else:
            q_ref, k_ref, v_ref, o_ref, do_ref, lse_ref, dq_ref, dk_ref, dv_ref, delta_ref = refs
        kj, qi = pl.program_id(1), pl.program_id(2)
        rows = pl.ds(pl.multiple_of(qi * tq, tq), tq)

        @pl.when((kj == 0) & (qi == 0))
        def _():
            dq_ref[...] = jnp.zeros_like(dq_ref)

        @pl.when(qi == 0)
        def _():
            dk_ref[...] = jnp.zeros_like(dk_ref)
            dv_ref[...] = jnp.zeros_like(dv_ref)

        first = kj == (jnp.maximum(qi - band, 0) if mode == "dilated" else 0)

        @pl.when(first)
        def _():
            for h in range(hb):
                cols = slice(h * HP, (h + 1) * HP)
                dl = jnp.sum(do_ref[:, cols].astype(F32) * o_ref[:, cols].astype(F32), axis=-1, keepdims=True)
                delta_ref[h, rows, :] = jnp.broadcast_to(dl, (tq, HP))

        def step(bias_tile):
            for h in range(hb):
                cols = slice(h * HP, (h + 1) * HP)
                qv, kv, dov = q_ref[:, cols], k_ref[:, cols], do_ref[:, cols]
                sc = lax.dot_general(qv, kv, _DIMS["nt"], preferred_element_type=F32)
                if bias_tile is not None:
                    sc = sc + bias_tile()
                dp = lax.dot_general(dov, v_ref[:, cols], _DIMS["nt"], preferred_element_type=F32)
                lse_b = lse_ref[:, cols]
                dlt = delta_ref[h, rows, :]
                pch, dsch = [], []
                for c in range(nch):
                    lanes = slice(c * 128, (c + 1) * 128)
                    pc = jnp.exp2(sc[:, lanes] - lse_b)
                    pch.append(pc.astype(BF16))
                    dsch.append((pc * (dp[:, lanes] - dlt)).astype(BF16))
                p = pch[0] if nch == 1 else jnp.concatenate(pch, axis=1)
                ds = dsch[0] if nch == 1 else jnp.concatenate(dsch, axis=1)
                dv_ref[:, cols] += lax.dot_general(p, dov, _DIMS["tn"], preferred_element_type=F32)
                dk_ref[:, cols] += lax.dot_general(ds, qv, _DIMS["tn"], preferred_element_type=F32)
                dq_ref[rows, cols] += jnp.dot(ds, kv, preferred_element_type=F32)

        if mode == "full":
            step(None)
        elif mode == "causal":
            pl.when(qi == kj)(lambda: step(lambda: b_ref[0]))
            pl.when(qi > kj)(lambda: step(None))
        else:
            pl.when((qi >= kj) & (qi <= kj + band))(lambda: step(lambda: b_ref[qi - kj]))

    qspec = pl.BlockSpec((tq, wb), qidx)
    kspec = pl.BlockSpec((tk, wb), lambda hp, kj, qi: (kj, hp))
    in_specs = [qspec, kspec, kspec, qspec, qspec, qspec]
    args = [q, k, v, o, do, lse]
    if has_bias:
        bias = _mask_bias(mode, tq)
        in_specs.append(pl.BlockSpec(bias.shape, lambda hp, kj, qi: (0, 0, 0)))
        args.append(bias)
    return pl.pallas_call(
        body,
        out_shape=(jax.ShapeDtypeStruct((sq, w), F32), jax.ShapeDtypeStruct((sk, w), F32), jax.ShapeDtypeStruct((sk, w), F32)),
        grid=(w // wb, nk, nq),
        in_specs=in_specs,
        out_specs=(pl.BlockSpec((sq, wb), lambda hp, kj, qi: (0, hp)), kspec, kspec),
        scratch_shapes=[pltpu.VMEM((hb, sq, HP), F32)],
        compiler_params=_cp("parallel", "arbitrary", "arbitrary"),
        name=name,
    )(*args)


def _conv_rc(s):
    return _tile(s, 256, 8)


def _shift_down(x, prev8, nrows):
    rows = lax.broadcasted_iota(jnp.int32, x.shape, 0)
    out = pltpu.roll(x, nrows, 0)
    for i in range(nrows):
        out = jnp.where(rows == i, prev8[8 - nrows + i:8 - nrows + i + 1, :], out)
    return out


def _shift_up(x, next8, nrows):
    n = x.shape[0]
    rows = lax.broadcasted_iota(jnp.int32, x.shape, 0)
    out = pltpu.roll(x, n - nrows, 0)
    for i in range(nrows):
        out = jnp.where(rows == n - nrows + i, next8[i:i + 1, :], out)
    return out


def _conv_taps(u_ref, r, rc):
    r0 = pl.multiple_of(r * rc, rc)
    x = u_ref[pl.ds(r0, rc), :]
    p0 = pl.multiple_of(jnp.maximum(r0 - 8, 0), 8)
    prev8 = jnp.where(r > 0, u_ref[pl.ds(p0, 8), :], 0.0)
    return r0, x, _shift_down(x, prev8, 1), _shift_down(x, prev8, 2)


def _conv_fwd(u, cw, cb, name):
    s, f2 = u.shape
    tf = CONV_TF
    nf = f2 // (2 * tf)
    rc = _conv_rc(s)

    def body(u_ref, w_ref, b_ref, a_ref):
        w0, w1, w2, b = w_ref[0:1, :], w_ref[1:2, :], w_ref[2:3, :], b_ref[...]

        def chunk(r, carry):
            r0, x, x1, x2 = _conv_taps(u_ref, r, rc)
            c = b + w0 * x2 + w1 * x1 + w2 * x
            val, gate = c[:, :tf], c[:, tf:]
            a_ref[pl.ds(r0, rc), :] = (gate * jax.nn.sigmoid(gate) * val).astype(a_ref.dtype)
            return carry

        lax.fori_loop(0, s // rc, chunk, 0)

    return pl.pallas_call(
        body,
        out_shape=jax.ShapeDtypeStruct((s, f2 // 2), BF16),
        grid=(nf,),
        in_specs=[pl.BlockSpec((s, 2 * tf), lambda j: (0, j)), pl.BlockSpec((3, 2 * tf), lambda j: (0, j)),
                  pl.BlockSpec((1, 2 * tf), lambda j: (0, j))],
        out_specs=pl.BlockSpec((s, tf), lambda j: (0, j)),
        compiler_params=_cp("parallel"),
        name=name,
    )(u, cw, cb)


def _conv_bwd(u, da, cw, cb, name):
    s, f2 = u.shape
    tf = CONV_TF
    nf = f2 // (2 * tf)
    rc = _conv_rc(s)
    nchunk = s // rc

    def body(u_ref, da_ref, w_ref, b_ref, du_ref, dw_ref, db_ref, next_ref):
        w0, w1, w2, b = w_ref[0:1, :], w_ref[1:2, :], w_ref[2:3, :], b_ref[...]
        next_ref[...] = jnp.zeros_like(next_ref)

        def chunk(it, carry):
            g0, g1, g2, gb = carry
            r = nchunk - 1 - it
            r0, x, x1, x2 = _conv_taps(u_ref, r, rc)
            c = b + w0 * x2 + w1 * x1 + w2 * x
            val, gate = c[:, :tf], c[:, tf:]
            sg = jax.nn.sigmoid(gate)
            dav = da_ref[pl.ds(r0, rc), :]
            dc = jnp.concatenate([dav * (gate * sg), dav * val * (sg * (1.0 + gate * (1.0 - sg)))], axis=1)
            nxt = next_ref[...]
            du = w2 * dc + w1 * _shift_up(dc, nxt, 1) + w0 * _shift_up(dc, nxt, 2)
            du_ref[pl.ds(r0, rc), :] = du.astype(du_ref.dtype)
            next_ref[...] = dc[0:8, :]
            return (g0 + jnp.sum(dc * x2, axis=0, keepdims=True), g1 + jnp.sum(dc * x1, axis=0, keepdims=True),
                    g2 + jnp.sum(dc * x, axis=0, keepdims=True), gb + jnp.sum(dc, axis=0, keepdims=True))

        zero = jnp.zeros((1, 2 * tf), F32)
        g0, g1, g2, gb = lax.fori_loop(0, nchunk, chunk, (zero, zero, zero, zero))
        dw_ref[0:1, :] = g0
        dw_ref[1:2, :] = g1
        dw_ref[2:3, :] = g2
        db_ref[...] = gb

    return pl.pallas_call(
        body,
        out_shape=(jax.ShapeDtypeStruct((s, f2), BF16), jax.ShapeDtypeStruct((3, f2), F32), jax.ShapeDtypeStruct((1, f2), F32)),
        grid=(nf,),
        in_specs=[pl.BlockSpec((s, 2 * tf), lambda j: (0, j)), pl.BlockSpec((s, tf), lambda j: (0, j)),
                  pl.BlockSpec((3, 2 * tf), lambda j: (0, j)), pl.BlockSpec((1, 2 * tf), lambda j: (0, j))],
        out_specs=(pl.BlockSpec((s, 2 * tf), lambda j: (0, j)), pl.BlockSpec((3, 2 * tf), lambda j: (0, j)),
                   pl.BlockSpec((1, 2 * tf), lambda j: (0, j))),
        scratch_shapes=[pltpu.VMEM((8, 2 * tf), F32)],
        compiler_params=_cp("parallel"),
        name=name,
    )(u, da, cw, cb)


def _adamw(w, g, m, v, name):
    r, c = w.shape
    tr = _tile(r, 256, 8) if r % 8 == 0 else r
    c1 = 1.0 - ADAM_B1 ** ADAM_STEP
    c2 = 1.0 - ADAM_B2 ** ADAM_STEP

    def body(w_ref, g_ref, m_ref, v_ref, d_ref, mo_ref, vo_ref):
        gv = g_ref[...]
        mn = ADAM_B1 * m_ref[...] + (1.0 - ADAM_B1) * gv
        vn = ADAM_B2 * v_ref[...] + (1.0 - ADAM_B2) * (gv * gv)
        d_ref[...] = -ADAM_LR * ((mn / c1) / (jnp.sqrt(vn / c2) + ADAM_EPS) + ADAM_WD * w_ref[...])
        mo_ref[...] = mn
        vo_ref[...] = vn

    blk = pl.BlockSpec((tr, c), lambda i: (i, 0))
    shp = jax.ShapeDtypeStruct((r, c), F32)
    return pl.pallas_call(
        body, out_shape=(shp, shp, shp), grid=(r // tr,), in_specs=[blk] * 4, out_specs=(blk,) * 3,
        compiler_params=_cp("parallel"), name=name,
    )(w, g, m, v)


def _sum_rows(parts, out_dtype, name):
    n, r, c = parts.shape
    tr = _tile(r, 256, 8)

    def body(p_ref, o_ref):
        tot = p_ref[0].astype(F32)
        for i in range(1, n):
            tot = tot + p_ref[i].astype(F32)
        o_ref[...] = tot.astype(o_ref.dtype)

    return pl.pallas_call(
        body, out_shape=jax.ShapeDtypeStruct((r, c), out_dtype), grid=(r // tr,),
        in_specs=[pl.BlockSpec((n, tr, c), lambda i: (0, i, 0))], out_specs=pl.BlockSpec((tr, c), lambda i: (i, 0)),
        compiler_params=_cp("parallel"), name=name,
    )(parts)


def _add_pairs(a, b, name):
    n, r, c = a.shape
    tr = _tile(r, 256, 8)

    def body(a_ref, b_ref, o_ref):
        o_ref[...] = (a_ref[...].astype(F32) + b_ref[...].astype(F32)).astype(o_ref.dtype)

    blk = pl.BlockSpec((1, tr, c), lambda j, i: (j, i, 0))
    return pl.pallas_call(
        body, out_shape=jax.ShapeDtypeStruct((n, r, c), BF16), grid=(n, r // tr), in_specs=[blk, blk], out_specs=blk,
        compiler_params=_cp("parallel", "parallel"), name=name,
    )(a, b)


_ANY = pl.BlockSpec(memory_space=pl.ANY)


def _place():
    return lax.axis_index("x"), lax.axis_index("y"), lax.axis_index("c")


def _other_chips(x, y):
    return [(1 - x, y), (x, 1 - y), (1 - x, 1 - y)]


def _rows_half(ref, c, axis):
    rh = ref.shape[axis] // 2
    idx = [slice(None)] * len(ref.shape)
    idx[axis] = pl.ds(pl.multiple_of(c * rh, 16), rh)
    return ref.at[tuple(idx)]


def _remote(src, dst, send_sems, recv_sems, kk, to):
    return pltpu.make_async_remote_copy(src_ref=src, dst_ref=dst, send_sem=send_sems.at[kk], recv_sem=recv_sems.at[kk],
                                        device_id=to, device_id_type=MESH)


def _gather_shards(shards, name):
    n = len(shards)

    def body(*refs):
        in_refs, out_refs = refs[:n], refs[n:2 * n]
        send_sems, recv_sems, local_sems = refs[2 * n:]
        x, y, c = _place()
        me = 2 * x + y
        sib = (x, y, 1 - c)
        chips = _other_chips(x, y)
        local = [pltpu.make_async_copy(in_refs[a], out_refs[a].at[:, me], local_sems.at[a]) for a in range(n)]
        for cp in local:
            cp.start()
        first = []
        for a in range(n):
            for j, (cx, cy) in enumerate(chips):
                first.append(_remote(_rows_half(in_refs[a], c, 1), _rows_half(out_refs[a].at[:, me], c, 1),
                                     send_sems, recv_sems, 6 * a + j, (cx, cy, c)))
        for cp in first:
            cp.start()
        passed = []
        for a in range(n):
            for j, (cx, cy) in enumerate(chips):
                slot = _rows_half(out_refs[a].at[:, 2 * cx + cy], c, 1)
                _remote(slot, slot, send_sems, recv_sems, 6 * a + j, (x, y, c)).wait_recv()
                fwd = _remote(slot, slot, send_sems, recv_sems, 6 * a + 3 + j, sib)
                fwd.start()
                passed.append(fwd)
        for a in range(n):
            for j, (cx, cy) in enumerate(chips):
                slot = _rows_half(out_refs[a].at[:, 2 * cx + cy], 1 - c, 1)
                _remote(slot, slot, send_sems, recv_sems, 6 * a + 3 + j, (x, y, c)).wait_recv()
        for cp in first + passed:
            cp.wait_send()
        for cp in local:
            cp.wait()

    return pl.pallas_call(
        body,
        out_shape=tuple(jax.ShapeDtypeStruct((s.shape[0], N_CHIPS) + s.shape[1:], s.dtype) for s in shards),
        in_specs=[_ANY] * n, out_specs=(_ANY,) * n,
        scratch_shapes=[pltpu.SemaphoreType.DMA((6 * n,)), pltpu.SemaphoreType.DMA((6 * n,)), pltpu.SemaphoreType.DMA((n,))],
        name=name,
    )(*shards)


def _swap_halves(gs, name):
    n = len(gs)

    def body(*refs):
        in_refs, out_refs = refs[:n], refs[n:2 * n]
        send_sems, recv_sems = refs[2 * n:]
        x, y, c = _place()
        copies = [_remote(_rows_half(in_refs[a], 1 - c, 2), out_refs[a], send_sems, recv_sems, a, (x, y, 1 - c)) for a in range(n)]
        for cp in copies:
            cp.start()
        for cp in copies:
            cp.wait()

    return pl.pallas_call(
        body,
        out_shape=tuple(jax.ShapeDtypeStruct(g.shape[:2] + (g.shape[2] // 2, g.shape[3]), g.dtype) for g in gs),
        in_specs=[_ANY] * n, out_specs=(_ANY,) * n,
        scratch_shapes=[pltpu.SemaphoreType.DMA((n,)), pltpu.SemaphoreType.DMA((n,))], name=name,
    )(*gs)


def _scatter_to_chips(ps, name):
    n = len(ps)

    def body(*refs):
        in_refs, out_refs = refs[:n], refs[n:2 * n]
        send_sems, recv_sems, local_sems = refs[2 * n:]
        x, y, c = _place()
        me = 2 * x + y
        chips = _other_chips(x, y)
        local = [pltpu.make_async_copy(in_refs[a].at[:, me], out_refs[a].at[me], local_sems.at[a]) for a in range(n)]
        for cp in local:
            cp.start()
        sends = []
        for a in range(n):
            for j, (cx, cy) in enumerate(chips):
                cp = _remote(in_refs[a].at[:, 2 * cx + cy], out_refs[a].at[me], send_sems, recv_sems, 3 * a + j, (cx, cy, c))
                cp.start()
                sends.append(cp)
        for a in range(n):
            for j, (cx, cy) in enumerate(chips):
                slot = out_refs[a].at[2 * cx + cy]
                _remote(slot, slot, send_sems, recv_sems, 3 * a + j, (x, y, c)).wait_recv()
        for cp in sends:
            cp.wait_send()
        for cp in local:
            cp.wait()

    return pl.pallas_call(
        body,
        out_shape=tuple(jax.ShapeDtypeStruct((N_CHIPS, p.shape[0]) + p.shape[2:], p.dtype) for p in ps),
        in_specs=[_ANY] * n, out_specs=(_ANY,) * n,
        scratch_shapes=[pltpu.SemaphoreType.DMA((3 * n,)), pltpu.SemaphoreType.DMA((3 * n,)), pltpu.SemaphoreType.DMA((n,))],
        name=name,
    )(*ps)


def _join_halves(rs, name):
    n = len(rs)

    def body(*refs):
        in_refs, out_refs = refs[:n], refs[n:2 * n]
        send_sems, recv_sems, local_sems = refs[2 * n:]
        x, y, c = _place()
        local, sends = [], []
        for a in range(n):
            mine = _rows_half(out_refs[a], c, 1)
            local.append(pltpu.make_async_copy(in_refs[a], mine, local_sems.at[a]))
            sends.append(_remote(in_refs[a], mine, send_sems, recv_sems, a, (x, y, 1 - c)))
        for cp in local + sends:
            cp.start()
        for a in range(n):
            other = _rows_half(out_refs[a], 1 - c, 1)
            _remote(other, other, send_sems, recv_sems, a, (x, y, c)).wait_recv()
        for cp in sends:
            cp.wait_send()
        for cp in local:
            cp.wait()

    return pl.pallas_call(
        body,
        out_shape=tuple(jax.ShapeDtypeStruct((r.shape[0], 2 * r.shape[1], r.shape[2]), r.dtype) for r in rs),
        in_specs=[_ANY] * n, out_specs=(_ANY,) * n,
        scratch_shapes=[pltpu.SemaphoreType.DMA((n,)), pltpu.SemaphoreType.DMA((n,)), pltpu.SemaphoreType.DMA((n,))], name=name,
    )(*rs)


def _gather_all(small, name):
    r, w = small.shape
    flips = [(dx, dy, dc) for dx in (0, 1) for dy in (0, 1) for dc in (0, 1) if (dx, dy, dc) != (0, 0, 0)]

    def body(in_ref, out_ref, send_sems, recv_sems, local_sem):
        x, y, c = _place()
        me = 4 * x + 2 * y + c
        mine = pltpu.make_async_copy(in_ref, out_ref.at[me], local_sem)
        mine.start()
        sends = []
        for j, (dx, dy, dc) in enumerate(flips):
            to = (x ^ dx, y ^ dy, c ^ dc)
            cp = pltpu.make_async_remote_copy(src_ref=in_ref, dst_ref=out_ref.at[me], send_sem=send_sems.at[j],
                                              recv_sem=recv_sems.at[j], device_id=to, device_id_type=MESH)
            cp.start()
            sends.append(cp)
        for j, (dx, dy, dc) in enumerate(flips):
            slot = out_ref.at[4 * (x ^ dx) + 2 * (y ^ dy) + (c ^ dc)]
            pltpu.make_async_remote_copy(src_ref=slot, dst_ref=slot, send_sem=send_sems.at[j], recv_sem=recv_sems.at[j],
                                         device_id=(x, y, c), device_id_type=MESH).wait_recv()
        for cp in sends:
            cp.wait_send()
        mine.wait()

    return pl.pallas_call(
        body, out_shape=jax.ShapeDtypeStruct((8, r, w), small.dtype), in_specs=[_ANY], out_specs=_ANY,
        scratch_shapes=[pltpu.SemaphoreType.DMA((7,)), pltpu.SemaphoreType.DMA((7,)), pltpu.SemaphoreType.DMA], name=name,
    )(small)


_BIG = ("w_in_fox", "w_in_dil", "w_mem_kv", "w_out", "w_up", "w_down")
_CONVW_SHARD = (DEPTH, 3, 2 * D_FF // N_CHIPS)
_COL_SHARDED = ("w_in_fox", "w_in_dil", "w_up")

_SMALL = (("norm_mix", (DEPTH, D_MODEL)), ("norm_mem", (DEPTH, D_MODEL)), ("norm_ffn", (DEPTH, D_MODEL)),
          ("conv_b", (DEPTH, 2 * D_FF)), ("norm_final", (D_MODEL,)), ("b_forget", (2, N_MIX)), ("conv_w", (DEPTH, 3, 2 * D_FF)))


def _pack_small(vals, spec):
    flat = jnp.concatenate([vals[n].reshape(-1).astype(F32) for n, _ in spec])
    rows = -(-flat.shape[0] // (8 * 128)) * 8
    return jnp.pad(flat, (0, rows * 128 - flat.shape[0])).reshape(rows, 128)


def _unpack_small(buf, spec):
    flat, out, off = buf.reshape(-1), {}, 0
    for n, shp in spec:
        k = int(np.prod(shp))
        out[n] = flat[off:off + k].reshape(shp)
        off += k
    return out


def _interleave_cols(a):
    lead = a.shape[:-1]
    nf = D_FF // CONV_TF
    return a.reshape(lead + (2, nf, CONV_TF)).swapaxes(-3, -2).reshape(lead + (2 * D_FF,))


def _deinterleave_cols(a):
    lead = a.shape[:-1]
    nf = D_FF // CONV_TF
    return a.reshape(lead + (nf, 2, CONV_TF)).swapaxes(-3, -2).reshape(lead + (2 * D_FF,))


def _fox_cols_to_kernel(w):
    qkv, f, qm = w[:, :3 * D_MIX], w[:, 3 * D_MIX:3 * D_MIX + N_MIX], w[:, 3 * D_MIX + N_MIX:]
    return jnp.concatenate([qkv, qm, f, jnp.zeros((w.shape[0], 128 - N_MIX), w.dtype)], axis=1)


def _fox_cols_from_kernel(w):
    qkv, qm, f = w[:, :3 * D_MIX], w[:, 3 * D_MIX:3 * D_MIX + D_MEMQ], w[:, 3 * D_MIX + D_MEMQ:3 * D_MIX + D_MEMQ + N_MIX]
    return jnp.concatenate([qkv, f, qm], axis=1)


def _rope_pair_tables(s):
    inv = 1.0 / (ROPE_THETA ** (jnp.arange(0, HEAD_DIM, 2, dtype=F32) / HEAD_DIM))
    ang = jnp.arange(s, dtype=F32)[:, None] * inv[None, :]
    cos, sin = jnp.cos(ang), jnp.sin(ang)
    return jnp.concatenate([cos, cos, cos, cos], axis=1), jnp.concatenate([-sin, sin, -sin, sin], axis=1)


def _local_step(x, mem, target, small, wts):
    s = x.shape[0]
    cos2, sin2 = _rope_pair_tables(s)
    saved = []
    h = x
    for l in range(DEPTH):
        fox = l % 2 == 0
        slot = l // 2
        tag = f"L{l}"
        g_mix, g_mem, g_ffn = (small[n][l:l + 1] for n in ("norm_mix", "norm_mem", "norm_ffn"))
        xn = _rms_fwd(h, g_mix, f"rms_mix_{tag}")
        proj = _matmul(xn, wts["w_in"][l], "nn", F32, f"mm_in_{tag}")
        if fox:
            bfg = jnp.pad(small["b_forget"][slot:slot + 1], ((0, 0), (0, 128 - N_MIX)))
            qp, kp, vp, qmp = _prep_fox_fwd(proj, bfg, f"prep_fox_{tag}")
        else:
            bfg = None
            qp, kp, vp, qmp = _prep_dil_fwd(proj, cos2, sin2, f"prep_dil_{tag}")
        mn = _rms_fwd(mem, g_mem, f"rms_mem_{tag}")
        kvm = _matmul(mn, wts["w_mem_kv"][l], "nn", F32, f"mm_memkv_{tag}")
        kmp, vmp = _pad_heads([(kvm, 0, N_MEM), (kvm, D_MEMQ, N_MEM)], (1.0, 1.0), (N_MEM * HP,) * 2, f"pad_memkv_{tag}")
        mode = "causal" if fox else "dilated"
        o_mix, lse_mix = _flash_fwd(qp, kp, vp, mode, f"flash_{mode}_fwd_{tag}")
        o_mem, lse_mem = _flash_fwd(qmp, kmp, vmp, "full", f"flash_mem_fwd_{tag}")
        heads = _unpad_heads([o_mix, o_mem], (1.0, 1.0), BF16, f"unpad_heads_{tag}")
        h_mid = _matmul(heads, wts["w_out"][l], "nn", F32, f"mm_out_{tag}", residual=h)
        xn2 = _rms_fwd(h_mid, g_ffn, f"rms_ffn_{tag}")
        u = _matmul(xn2, wts["w_up"][l], "nn", F32, f"mm_up_{tag}")
        act = _conv_fwd(u, wts["conv_w"][l], wts["conv_b"][l], f"conv_fwd_{tag}")
        h_out = _matmul(act, wts["w_down"][l], "nn", F32, f"mm_down_{tag}", residual=h_mid)
        saved.append(dict(h=h, xn=xn, proj=proj, bfg=bfg, qp=qp, kp=kp, vp=vp, qmp=qmp, mn=mn, kmp=kmp, vmp=vmp, o_mix=o_mix,
                          lse_mix=lse_mix, o_mem=o_mem, lse_mem=lse_mem, heads=heads, h_mid=h_mid, xn2=xn2, u=u, act=act))
        h = h_out

    loss_blk, dh, dg_final = _loss_head(h, small["norm_final"].reshape(1, D_MODEL), target, "loss_head")

    grads = {k: [None] * DEPTH for k in ("w_in", "w_mem_kv", "w_out", "w_up", "w_down", "conv_w", "conv_b", "norm_mix", "norm_mem",
                                         "norm_ffn")}
    grads["b_forget"] = [None, None]
    for l in reversed(range(DEPTH)):
        fox = l % 2 == 0
        tag = f"L{l}"
        sv = saved[l]
        g_mix, g_mem, g_ffn = (small[n][l:l + 1] for n in ("norm_mix", "norm_mem", "norm_ffn"))
        da = _matmul(dh, wts["w_down"][l], "nt", F32, f"mm_da_{tag}")
        grads["w_down"][l] = _matmul(sv["act"], dh, "tn", BF16, f"mm_dwdown_{tag}")
        du, dcw, dcb = _conv_bwd(sv["u"], da, wts["conv_w"][l], wts["conv_b"][l], f"conv_bwd_{tag}")
        grads["conv_w"][l], grads["conv_b"][l] = dcw, dcb
        dxn2 = _matmul(du, wts["w_up"][l], "nt", F32, f"mm_dxn2_{tag}")
        grads["w_up"][l] = _matmul(sv["xn2"], du, "tn", BF16, f"mm_dwup_{tag}")
        dh_mid, grads["norm_ffn"][l] = _rms_bwd(sv["h_mid"], g_ffn, dxn2, dh, f"rms_ffn_bwd_{tag}")
        dheads = _matmul(dh_mid, wts["w_out"][l], "nt", F32, f"mm_dheads_{tag}")
        grads["w_out"][l] = _matmul(sv["heads"], dh_mid, "tn", BF16, f"mm_dwout_{tag}")
        do_mix, do_mem = _pad_heads([(dheads, 0, N_MIX), (dheads, D_MIX, N_MEM)], (1.0, 1.0), (N_MIX * HP, N_MEM * HP),
                                    f"pad_dheads_{tag}")
        mode = "causal" if fox else "dilated"
        dqp, dkp, dvp = _flash_bwd(sv["qp"], sv["kp"], sv["vp"], sv["o_mix"], do_mix, sv["lse_mix"], mode, f"flash_{mode}_bwd_{tag}")
        dqmp, dkmp, dvmp = _flash_bwd(sv["qmp"], sv["kmp"], sv["vmp"], sv["o_mem"], do_mem, sv["lse_mem"], "full",
                                      f"flash_mem_bwd_{tag}")
        if fox:
            dproj, dbf = _prep_fox_bwd(dqp, dkp, dvp, dqmp, sv["proj"], sv["bfg"], f"prep_fox_bwd_{tag}")
            grads["b_forget"][l // 2] = dbf[0, :N_MIX]
        else:
            dproj = _prep_dil_bwd(dqp, dkp, dvp, dqmp, cos2, sin2, f"prep_dil_bwd_{tag}")
        dkvm = _unpad_heads([dkmp, dvmp], (LN2, 1.0), BF16, f"unpad_dkvm_{tag}")
        grads["w_mem_kv"][l] = _matmul(sv["mn"], dkvm, "tn", BF16, f"mm_dwmemkv_{tag}")
        dmn = _matmul(dkvm, wts["w_mem_kv"][l], "nt", F32, f"mm_dmn_{tag}")
        _, grads["norm_mem"][l] = _rms_bwd(mem, g_mem, dmn, None, f"rms_mem_bwd_{tag}")
        dxn = _matmul(dproj, wts["w_in"][l], "nt", F32, f"mm_dxn_{tag}")
        grads["w_in"][l] = _matmul(sv["xn"], dproj, "tn", BF16, f"mm_dwin_{tag}")
        dh, grads["norm_mix"][l] = _rms_bwd(sv["h"], g_mix, dxn, dh_mid, f"rms_mix_bwd_{tag}")
    grads["norm_final"] = dg_final
    return loss_blk, dh, grads


def _full_from_chips(per_chip, name):
    n_layers, _, r, c = per_chip.shape
    if name in _COL_SHARDED:
        return [jnp.concatenate([per_chip[l, j] for j in range(N_CHIPS)], axis=1) for l in range(n_layers)]
    return [per_chip[l].reshape(N_CHIPS * r, c) for l in range(n_layers)]


def _shards_from_full(layers, name):
    if name in _COL_SHARDED:
        return jnp.stack([jnp.stack(jnp.split(g, N_CHIPS, axis=1)) for g in layers])
    return jnp.stack([g.reshape(N_CHIPS, g.shape[0] // N_CHIPS, g.shape[1]) for g in layers])


def kernel(x, mem, norm_mix, norm_mem, norm_ffn, w_in_fox, b_forget, w_in_dil, w_mem_kv, w_out, w_up, conv_w, conv_b, w_down, norm_final, loss_target, m_norm_mix, m_norm_mem, m_norm_ffn, m_w_in_fox, m_b_forget, m_w_in_dil, m_w_mem_kv, m_w_out, m_w_up, m_conv_w, m_conv_b, m_w_down, m_norm_final, v_norm_mix, v_norm_mem, v_norm_ffn, v_w_in_fox, v_b_forget, v_w_in_dil, v_w_mem_kv, v_w_out, v_w_up, v_conv_w, v_conv_b, v_w_down, v_norm_final):
    w_sh = dict(w_in_fox=w_in_fox, w_in_dil=w_in_dil, w_mem_kv=w_mem_kv, w_out=w_out, w_up=w_up, w_down=w_down, conv_w=conv_w)
    m_sh = dict(w_in_fox=m_w_in_fox, w_in_dil=m_w_in_dil, w_mem_kv=m_w_mem_kv, w_out=m_w_out, w_up=m_w_up, w_down=m_w_down, conv_w=m_conv_w)
    v_sh = dict(w_in_fox=v_w_in_fox, w_in_dil=v_w_in_dil, w_mem_kv=v_w_mem_kv, w_out=v_w_out, w_up=v_w_up, w_down=v_w_down, conv_w=v_conv_w)
    small = dict(norm_mix=norm_mix, norm_mem=norm_mem, norm_ffn=norm_ffn, conv_b=conv_b, norm_final=norm_final, b_forget=b_forget)
    m_small = dict(norm_mix=m_norm_mix, norm_mem=m_norm_mem, norm_ffn=m_norm_ffn, conv_b=m_conv_b, norm_final=m_norm_final, b_forget=m_b_forget)
    v_small = dict(norm_mix=v_norm_mix, norm_mem=v_norm_mem, norm_ffn=v_norm_ffn, conv_b=v_conv_b, norm_final=v_norm_final, b_forget=v_b_forget)
    chip = 2 * lax.axis_index("x") + lax.axis_index("y")
    core = lax.axis_index("c")

    per_chip = dict(zip(_BIG, _gather_shards([w_sh[n].astype(BF16) for n in _BIG], "gather_weights")))
    conv_spec = (("conv_w", _CONVW_SHARD),)
    conv_all = _gather_all(_pack_small(dict(conv_w=conv_w), conv_spec), "gather_conv_w")
    conv_w_full = jnp.concatenate([_unpack_small(conv_all[2 * j], conv_spec)["conv_w"] for j in range(N_CHIPS)], axis=-1)
    fox_full = [_fox_cols_to_kernel(w) for w in _full_from_chips(per_chip["w_in_fox"], "w_in_fox")]
    dil_full = _full_from_chips(per_chip["w_in_dil"], "w_in_dil")
    wts = dict(
        w_in=[fox_full[0], dil_full[0], fox_full[1], dil_full[1]],
        w_mem_kv=_full_from_chips(per_chip["w_mem_kv"], "w_mem_kv"),
        w_out=_full_from_chips(per_chip["w_out"], "w_out"),
        w_up=[_interleave_cols(w) for w in _full_from_chips(per_chip["w_up"], "w_up")],
        w_down=_full_from_chips(per_chip["w_down"], "w_down"),
        conv_w=[_interleave_cols(conv_w_full[l]) for l in range(DEPTH)],
        conv_b=[_interleave_cols(conv_b[l:l + 1]) for l in range(DEPTH)],
    )

    loss_blk, dx, grads = _local_step(x[0], mem[0], loss_target[0], small, wts)
    loss = lax.psum(loss_blk[0, 0], ("x", "y", "c"))

    g_layers = dict(
        w_in_fox=[_fox_cols_from_kernel(grads["w_in"][0]), _fox_cols_from_kernel(grads["w_in"][2])],
        w_in_dil=[grads["w_in"][1], grads["w_in"][3]],
        w_mem_kv=grads["w_mem_kv"], w_out=grads["w_out"],
        w_up=[_deinterleave_cols(g) for g in grads["w_up"]], w_down=grads["w_down"],
    )
    g_sh = [_shards_from_full(g_layers[n], n) for n in _BIG]
    from_sibling = _swap_halves(g_sh, "grad_swap_halves")
    pair_sum, halves = [], []
    for n, g, f in zip(_BIG, g_sh, from_sibling):
        mine = lax.dynamic_slice_in_dim(g, core * f.shape[2], f.shape[2], axis=2)
        flat3 = (-1,) + f.shape[2:]
        pair_sum.append(_add_pairs(mine.reshape(flat3), f.reshape(flat3), f"grad_add_pairs_{n}").reshape(f.shape))
    from_chips = _scatter_to_chips(pair_sum, "grad_scatter_chips")
    for n, fc in zip(_BIG, from_chips):
        halves.append(_sum_rows(fc.reshape(N_CHIPS, -1, fc.shape[-1]), F32, f"grad_sum_chips_{n}").reshape(fc.shape[1:]))
    g_big = dict(zip(_BIG, _join_halves(halves, "grad_join_halves")))

    g_small_local = dict(
        norm_mix=jnp.concatenate(grads["norm_mix"]), norm_mem=jnp.concatenate(grads["norm_mem"]),
        norm_ffn=jnp.concatenate(grads["norm_ffn"]),
        conv_b=jnp.concatenate([_deinterleave_cols(g) for g in grads["conv_b"]]),
        norm_final=grads["norm_final"], b_forget=jnp.stack(grads["b_forget"]),
        conv_w=jnp.stack([_deinterleave_cols(g) for g in grads["conv_w"]]),
    )
    small_all = _gather_all(_pack_small(g_small_local, _SMALL), "small_gather_all")
    g_small = _unpack_small(_sum_rows(small_all, F32, "small_sum"), _SMALL)
    ncol = 2 * D_FF // N_CHIPS
    g_big["conv_w"] = lax.dynamic_slice_in_dim(g_small["conv_w"], chip * ncol, ncol, axis=2)

    out_g, out_d, out_m, out_v = {}, {}, {}, {}
    for n in _BIG + ("conv_w",):
        shp = w_sh[n].shape
        two_d = (-1, shp[-1])
        d, mo, vo = _adamw(w_sh[n].reshape(two_d), g_big[n].reshape(two_d), m_sh[n].reshape(two_d), v_sh[n].reshape(two_d), f"adamw_{n}")
        out_g[n], out_d[n], out_m[n], out_v[n] = g_big[n].reshape(shp), d.reshape(shp), mo.reshape(shp), vo.reshape(shp)
    spec = _SMALL[:-1]
    d, mo, vo = _adamw(_pack_small(small, spec), _pack_small(g_small, spec), _pack_small(m_small, spec), _pack_small(v_small, spec),
                       "adamw_small")
    d, mo, vo = _unpack_small(d, spec), _unpack_small(mo, spec), _unpack_small(vo, spec)
    for n, shp in spec:
        out_g[n], out_d[n], out_m[n], out_v[n] = g_small[n].reshape(shp), d[n], mo[n], vo[n]

    order = ("norm_mix", "norm_mem", "norm_ffn", "w_in_fox", "b_forget", "w_in_dil", "w_mem_kv", "w_out", "w_up", "conv_w", "conv_b",
             "w_down", "norm_final")
    return (loss, dx[None], *[out_g[n] for n in order], *[out_d[n] for n in order], *[out_m[n] for n in order],
            *[out_v[n] for n in order])
```

```python
import functools
import math

import numpy as np
import jax
import jax.numpy as jnp
from jax import lax
from jax.experimental import pallas as pl
from jax.experimental.pallas import tpu as pltpu

F32 = jnp.float32
BF16 = jnp.bfloat16

D_MODEL = 1024
DEPTH = 4
HEAD_DIM = 64
N_MIX = 12
N_MEM = 4
D_MIX = N_MIX * HEAD_DIM
D_MEMQ = N_MEM * HEAD_DIM
D_FF = 2816
FOX_IN = 3 * D_MIX + N_MIX + D_MEMQ
DIL_IN = 3 * D_MIX + D_MEMQ
HP = 128
SCALE = HEAD_DIM ** -0.5
NEG = -1e30
NORM_EPS = 1e-6
DIL_MAX = 2048
LOG2E = 1.0 / math.log(2.0)
LN2 = math.log(2.0)
QSCALE = SCALE * LOG2E
HEADS_PER_STEP = 2
ROPE_THETA = 10000.0
N_CHIPS = 4
CONV_TF = 128
FLASH_BLK = 512
VMEM_LIMIT = 48 * 1024 * 1024

ADAM_LR = 0.001
ADAM_B1 = 0.9
ADAM_B2 = 0.999
ADAM_EPS = 1e-08
ADAM_WD = 0.01
ADAM_STEP = 10

MESH = pl.DeviceIdType.MESH


def _cp(*sem):
    return pltpu.CompilerParams(dimension_semantics=tuple(sem), vmem_limit_bytes=VMEM_LIMIT)


def _tile(n, cap, mult=128):
    if n <= cap:
        return n
    t = (cap // mult) * mult
    while t >= mult:
        if n % t == 0:
            return t
        t -= mult
    raise ValueError(f"no tile for {n} under {cap}")


_DIMS = {"nn": (((1,), (0,)), ((), ())), "nt": (((1,), (1,)), ((), ())), "tn": (((0,), (0,)), ((), ()))}


def _matmul(a, b, mode, out_dtype, name, residual=None):
    if mode == "nn":
        (m, k), n = a.shape, b.shape[1]
    elif mode == "nt":
        (m, k), n = a.shape, b.shape[0]
    else:
        (k, m), n = a.shape, b.shape[1]
    tn = 1408 if n % 1408 == 0 else _tile(n, 1024)
    tm = 1408 if (m % 1408 == 0 and tn <= 1024) else _tile(m, 1024 if tn <= 1024 else 512)
    tk = _tile(k, 1408)
    nk = k // tk
    dims = _DIMS[mode]
    has_res = residual is not None

    def body(*refs):
        if has_res:
            a_ref, b_ref, r_ref, o_ref = refs[:4]
        else:
            a_ref, b_ref, o_ref = refs[:3]
        part = lax.dot_general(a_ref[...].astype(BF16), b_ref[...].astype(BF16), dims, preferred_element_type=F32)
        if nk == 1:
            if has_res:
                part = part + r_ref[...]
            o_ref[...] = part.astype(o_ref.dtype)
            return
        acc_ref = refs[-1]
        kk = pl.program_id(2)

        @pl.when(kk == 0)
        def _():
            acc_ref[...] = part

        @pl.when(kk > 0)
        def _():
            acc_ref[...] += part

        @pl.when(kk == nk - 1)
        def _():
            tot = acc_ref[...]
            if has_res:
                tot = tot + r_ref[...]
            o_ref[...] = tot.astype(o_ref.dtype)

    if mode == "nn":
        a_spec = pl.BlockSpec((tm, tk), lambda i, j, kk: (i, kk))
        b_spec = pl.BlockSpec((tk, tn), lambda i, j, kk: (kk, j))
    elif mode == "nt":
        a_spec = pl.BlockSpec((tm, tk), lambda i, j, kk: (i, kk))
        b_spec = pl.BlockSpec((tn, tk), lambda i, j, kk: (j, kk))
    else:
        a_spec = pl.BlockSpec((tk, tm), lambda i, j, kk: (kk, i))
        b_spec = pl.BlockSpec((tk, tn), lambda i, j, kk: (kk, j))
    in_specs = [a_spec, b_spec]
    args = [a, b]
    if has_res:
        in_specs.append(pl.BlockSpec((tm, tn), lambda i, j, kk: (i, j)))
        args.append(residual)
    return pl.pallas_call(
        body,
        out_shape=jax.ShapeDtypeStruct((m, n), out_dtype),
        grid=(m // tm, n // tn, nk),
        in_specs=in_specs,
        out_specs=pl.BlockSpec((tm, tn), lambda i, j, kk: (i, j)),
        scratch_shapes=[pltpu.VMEM((tm, tn), F32)] if nk > 1 else [],
        compiler_params=_cp("parallel", "parallel", "arbitrary"),
        name=name,
    )(*args)


def _rms_fwd(h, g, name):
    r, d = h.shape
    tr = _tile(r, 512, 8)

    def body(h_ref, g_ref, o_ref):
        x = h_ref[...]
        rstd = lax.rsqrt(jnp.mean(x * x, axis=-1, keepdims=True) + NORM_EPS)
        o_ref[...] = ((x * rstd) * g_ref[...]).astype(o_ref.dtype)

    return pl.pallas_call(
        body,
        out_shape=jax.ShapeDtypeStruct((r, d), BF16),
        grid=(r // tr,),
        in_specs=[pl.BlockSpec((tr, d), lambda i: (i, 0)), pl.BlockSpec((1, d), lambda i: (0, 0))],
        out_specs=pl.BlockSpec((tr, d), lambda i: (i, 0)),
        compiler_params=_cp("parallel"),
        name=name,
    )(h, g)


def _rms_bwd(h, g, dy, dres, name):
    r, d = h.shape
    tr = _tile(r, 512, 8)
    need_dh = dres is not None

    def body(*refs):
        if need_dh:
            h_ref, g_ref, dy_ref, dres_ref, dh_ref, dg_ref = refs
        else:
            h_ref, g_ref, dy_ref, dg_ref = refs
        i = pl.program_id(0)
        x = h_ref[...]
        rstd = lax.rsqrt(jnp.mean(x * x, axis=-1, keepdims=True) + NORM_EPS)
        nrm = x * rstd
        dyv = dy_ref[...].astype(F32)
        part = jnp.sum(dyv * nrm, axis=0, keepdims=True)

        @pl.when(i == 0)
        def _():
            dg_ref[...] = part

        @pl.when(i > 0)
        def _():
            dg_ref[...] += part

        if need_dh:
            gy = dyv * g_ref[...]
            dx = rstd * (gy - nrm * jnp.mean(gy * nrm, axis=-1, keepdims=True))
            dh_ref[...] = dres_ref[...] + dx

    row = pl.BlockSpec((tr, d), lambda i: (i, 0))
    vec = pl.BlockSpec((1, d), lambda i: (0, 0))
    if need_dh:
        return pl.pallas_call(
            body,
            out_shape=(jax.ShapeDtypeStruct((r, d), F32), jax.ShapeDtypeStruct((1, d), F32)),
            grid=(r // tr,),
            in_specs=[row, vec, row, row],
            out_specs=(row, vec),
            compiler_params=_cp("arbitrary"),
            name=name,
        )(h, g, dy, dres)
    return None, pl.pallas_call(
        body,
        out_shape=jax.ShapeDtypeStruct((1, d), F32),
        grid=(r // tr,),
        in_specs=[row, vec, row],
        out_specs=vec,
        compiler_params=_cp("arbitrary"),
        name=name,
    )(h, g, dy)


def _loss_head(h, g, target, name):
    r, d = h.shape
    tr = _tile(r, 512, 8)

    def body(h_ref, g_ref, t_ref, loss_ref, dh_ref, dg_ref):
        i = pl.program_id(0)
        x = h_ref[...]
        gv = g_ref[...]
        rstd = lax.rsqrt(jnp.mean(x * x, axis=-1, keepdims=True) + NORM_EPS)
        nrm = x * rstd
        err = nrm * gv - t_ref[...]
        lpart = 0.5 * jnp.sum(jnp.mean(err * err, axis=-1, keepdims=True), axis=0, keepdims=True)
        dyv = err * (1.0 / d)
        gpart = jnp.sum(dyv * nrm, axis=0, keepdims=True)

        @pl.when(i == 0)
        def _():
            loss_ref[...] = jnp.broadcast_to(lpart, loss_ref.shape)
            dg_ref[...] = gpart

        @pl.when(i > 0)
        def _():
            loss_ref[...] += jnp.broadcast_to(lpart, loss_ref.shape)
            dg_ref[...] += gpart

        gy = dyv * gv
        dh_ref[...] = rstd * (gy - nrm * jnp.mean(gy * nrm, axis=-1, keepdims=True))

    row = pl.BlockSpec((tr, d), lambda i: (i, 0))
    vec = pl.BlockSpec((1, d), lambda i: (0, 0))
    lsp = pl.BlockSpec((1, 128), lambda i: (0, 0))
    return pl.pallas_call(
        body,
        out_shape=(jax.ShapeDtypeStruct((1, 128), F32), jax.ShapeDtypeStruct((r, d), F32), jax.ShapeDtypeStruct((1, d), F32)),
        grid=(r // tr,),
        in_specs=[row, vec, row],
        out_specs=(lsp, row, vec),
        compiler_params=_cp("arbitrary"),
        name=name,
    )(h, g, target)


def _lane(shape):
    return lax.broadcasted_iota(jnp.int32, shape, 1)


def _head_tile(pair_tile, odd):
    return pltpu.roll(pair_tile, 64, 1) if odd else pair_tile


def _pair_tile(even_tile, odd_tile):
    lane = _lane(even_tile.shape)
    return jnp.where(lane < 64, even_tile, pltpu.roll(odd_tile, 64, 1))


def _pad_heads(xs, scales, out_widths, name):
    r = xs[0][0].shape[0]
    tr = _tile(r, 256, 8)
    n_in = len(xs)

    def body(*refs):
        for idx in range(n_in):
            x_ref, o_ref = refs[idx], refs[n_in + idx]
            nh = xs[idx][2]
            for p in range(nh // 2):
                t = x_ref[:, p * 128:(p + 1) * 128].astype(F32) * scales[idx]
                lane = _lane(t.shape)
                o_ref[:, (2 * p) * HP:(2 * p + 1) * HP] = jnp.where(lane < 64, t, 0.0).astype(o_ref.dtype)
                o_ref[:, (2 * p + 1) * HP:(2 * p + 2) * HP] = jnp.where(lane < 64, pltpu.roll(t, 64, 1), 0.0).astype(o_ref.dtype)

    in_specs, args, out_specs, out_shape = [], [], [], []
    for (arr, c0, nh), w in zip(xs, out_widths):
        wcols = nh * 64
        assert c0 % wcols == 0 or c0 == 0
        blk = c0 // wcols if wcols else 0
        in_specs.append(pl.BlockSpec((tr, wcols), functools.partial(lambda i, b: (i, b), b=blk)))
        args.append(arr)
        out_specs.append(pl.BlockSpec((tr, w), lambda i: (i, 0)))
        out_shape.append(jax.ShapeDtypeStruct((r, w), BF16))
    return pl.pallas_call(
        body,
        out_shape=tuple(out_shape),
        grid=(r // tr,),
        in_specs=in_specs,
        out_specs=tuple(out_specs),
        compiler_params=_cp("parallel"),
        name=name,
    )(*args)


def _unpad_heads(xs, scales, out_dtype, name):
    r = xs[0].shape[0]
    tr = _tile(r, 256, 8)
    nhs = [x.shape[1] // HP for x in xs]
    total = sum(nhs) * 64

    def body(*refs):
        o_ref = refs[-1]
        col = 0
        for x_ref, nh, sc in zip(refs[:-1], nhs, scales):
            for p in range(nh // 2):
                ev = x_ref[:, (2 * p) * HP:(2 * p + 1) * HP].astype(F32)
                od = x_ref[:, (2 * p + 1) * HP:(2 * p + 2) * HP].astype(F32)
                o_ref[:, col:col + 128] = (_pair_tile(ev, od) * sc).astype(o_ref.dtype)
                col += 128

    return pl.pallas_call(
        body,
        out_shape=jax.ShapeDtypeStruct((r, total), out_dtype),
        grid=(r // tr,),
        in_specs=[pl.BlockSpec((tr, x.shape[1]), lambda i: (i, 0)) for x in xs],
        out_specs=pl.BlockSpec((tr, total), lambda i: (i, 0)),
        compiler_params=_cp("parallel"),
        name=name,
    )(*xs)


def _bf16_split3(c):
    hi = c.astype(BF16).astype(F32)
    r1 = c - hi
    mid = r1.astype(BF16).astype(F32)
    lo = (r1 - mid).astype(BF16).astype(F32)
    return hi, mid, lo


def _log_sigmoid(z):
    return jnp.minimum(z, 0.0) - jnp.log(1.0 + jnp.exp(-jnp.abs(z)))


def _prep_fox_fwd(proj, bfg, name):
    s = proj.shape[0]
    ts = _tile(s, 256, 8)
    fcol = 3 * D_MIX + D_MEMQ

    def body(p_ref, b_ref, q_ref, k_ref, v_ref, qm_ref, carry_ref):
        i = pl.program_id(0)

        @pl.when(i == 0)
        def _():
            carry_ref[...] = jnp.zeros_like(carry_ref)

        lane = _lane((ts, 128))
        z = p_ref[:, fcol:fcol + 128] + b_ref[...]
        logf = jnp.where(lane < N_MIX, _log_sigmoid(z), 0.0)
        rr = lax.broadcasted_iota(jnp.int32, (ts, ts), 0)
        cc = lax.broadcasted_iota(jnp.int32, (ts, ts), 1)
        tri = jnp.where(cc <= rr, 1.0, 0.0).astype(F32)
        c = jnp.dot(tri, logf, preferred_element_type=F32, precision=lax.Precision.HIGHEST) + carry_ref[0:1, :]
        carry_ref[...] = jnp.broadcast_to(c[ts - 1:ts, :], carry_ref.shape)
        for hh in range(N_MIX):
            p, odd = hh // 2, hh % 2
            ch = jnp.sum(jnp.where(lane == hh, c, 0.0), axis=-1, keepdims=True) * LOG2E
            hi, mid, lo = _bf16_split3(ch)
            qt = _head_tile(p_ref[:, p * 128:(p + 1) * 128], odd) * QSCALE
            kt = _head_tile(p_ref[:, D_MIX + p * 128:D_MIX + (p + 1) * 128], odd)
            vt = _head_tile(p_ref[:, 2 * D_MIX + p * 128:2 * D_MIX + (p + 1) * 128], odd)
            qa = jnp.where(lane == 64, hi, jnp.where(lane == 65, mid, jnp.where(lane == 66, lo, jnp.where(lane < 70, 1.0, 0.0))))
            ka = jnp.where(lane < 67, 1.0, jnp.where(lane == 67, -hi, jnp.where(lane == 68, -mid, jnp.where(lane == 69, -lo, 0.0))))
            q_ref[:, hh * HP:(hh + 1) * HP] = jnp.where(lane < 64, qt, qa).astype(BF16)
            k_ref[:, hh * HP:(hh + 1) * HP] = jnp.where(lane < 64, kt, ka).astype(BF16)
            v_ref[:, hh * HP:(hh + 1) * HP] = jnp.where(lane < 64, vt, 0.0).astype(BF16)
        for hh in range(N_MEM):
            p, odd = hh // 2, hh % 2
            t = _head_tile(p_ref[:, 3 * D_MIX + p * 128:3 * D_MIX + (p + 1) * 128], odd) * QSCALE
            qm_ref[:, hh * HP:(hh + 1) * HP] = jnp.where(lane < 64, t, 0.0).astype(BF16)

    wmix, wmem = N_MIX * HP, N_MEM * HP
    return pl.pallas_call(
        body,
        out_shape=(jax.ShapeDtypeStruct((s, wmix), BF16),) * 3 + (jax.ShapeDtypeStruct((s, wmem), BF16),),
        grid=(s // ts,),
        in_specs=[pl.BlockSpec((ts, proj.shape[1]), lambda i: (i, 0)), pl.BlockSpec((1, 128), lambda i: (0, 0))],
        out_specs=(pl.BlockSpec((ts, wmix), lambda i: (i, 0)),) * 3 + (pl.BlockSpec((ts, wmem), lambda i: (i, 0)),),
        scratch_shapes=[pltpu.VMEM((8, 128), F32)],
        compiler_params=_cp("arbitrary"),
        name=name,
    )(proj, bfg)


def _prep_fox_bwd(dq, dk, dv, dqm, proj, bfg, name):
    s = proj.shape[0]
    ts = _tile(s, 256, 8)
    nb = s // ts
    fcol = 3 * D_MIX + D_MEMQ

    def body(dq_ref, dk_ref, dv_ref, dqm_ref, p_ref, b_ref, o_ref, db_ref, carry_ref):
        i = pl.program_id(0)

        @pl.when(i == 0)
        def _():
            carry_ref[...] = jnp.zeros_like(carry_ref)
            db_ref[...] = jnp.zeros_like(db_ref)

        lane = _lane((ts, 128))
        dc = jnp.zeros((ts, 128), F32)
        for p in range(N_MIX // 2):
            tq, tk, tv = [], [], []
            for odd in (0, 1):
                hh = 2 * p + odd
                dqt = dq_ref[:, hh * HP:(hh + 1) * HP]
                dkt = dk_ref[:, hh * HP:(hh + 1) * HP]
                col = jnp.sum(jnp.where(lane == 64, dqt, 0.0) - jnp.where(lane == 67, dkt, 0.0), axis=-1, keepdims=True)
                dc = dc + jnp.where(lane == hh, col, 0.0)
                tq.append(dqt)
                tk.append(dkt)
                tv.append(dv_ref[:, hh * HP:(hh + 1) * HP])
            o_ref[:, p * 128:(p + 1) * 128] = (_pair_tile(tq[0], tq[1]) * SCALE).astype(o_ref.dtype)
            o_ref[:, D_MIX + p * 128:D_MIX + (p + 1) * 128] = (_pair_tile(tk[0], tk[1]) * LN2).astype(o_ref.dtype)
            o_ref[:, 2 * D_MIX + p * 128:2 * D_MIX + (p + 1) * 128] = _pair_tile(tv[0], tv[1]).astype(o_ref.dtype)
        for p in range(N_MEM // 2):
            ev = dqm_ref[:, (2 * p) * HP:(2 * p + 1) * HP]
            od = dqm_ref[:, (2 * p + 1) * HP:(2 * p + 2) * HP]
            o_ref[:, 3 * D_MIX + p * 128:3 * D_MIX + (p + 1) * 128] = (_pair_tile(ev, od) * SCALE).astype(o_ref.dtype)
        rr = lax.broadcasted_iota(jnp.int32, (ts, ts), 0)
        cc = lax.broadcasted_iota(jnp.int32, (ts, ts), 1)
        triu = jnp.where(cc >= rr, 1.0, 0.0).astype(F32)
        dlogf = jnp.dot(triu, dc, preferred_element_type=F32, precision=lax.Precision.HIGHEST) + carry_ref[0:1, :]
        carry_ref[...] = jnp.broadcast_to(dlogf[0:1, :], carry_ref.shape)
        z = p_ref[:, fcol:fcol + 128] + b_ref[...]
        dz = jnp.where(lane < N_MIX, dlogf / (1.0 + jnp.exp(z)), 0.0)
        o_ref[:, fcol:fcol + 128] = dz.astype(o_ref.dtype)
        db_ref[...] += jnp.sum(dz, axis=0, keepdims=True)

    wmix, wmem = N_MIX * HP, N_MEM * HP
    rev = lambda i: (nb - 1 - i, 0)
    return pl.pallas_call(
        body,
        out_shape=(jax.ShapeDtypeStruct(proj.shape, BF16), jax.ShapeDtypeStruct((1, 128), F32)),
        grid=(nb,),
        in_specs=[pl.BlockSpec((ts, wmix), rev)] * 3 + [pl.BlockSpec((ts, wmem), rev), pl.BlockSpec((ts, proj.shape[1]), rev),
                                                         pl.BlockSpec((1, 128), lambda i: (0, 0))],
        out_specs=(pl.BlockSpec((ts, proj.shape[1]), rev), pl.BlockSpec((1, 128), lambda i: (0, 0))),
        scratch_shapes=[pltpu.VMEM((8, 128), F32)],
        compiler_params=_cp("arbitrary"),
        name=name,
    )(dq, dk, dv, dqm, proj, bfg)


def _rope_partner(x):
    lane = _lane(x.shape)
    return jnp.where((lane % 64) < 32, pltpu.roll(x, 96, 1), pltpu.roll(x, 32, 1))


def _prep_dil_fwd(proj, cos2, sin2, name):
    s = proj.shape[0]
    ts = _tile(s, 256, 8)

    def body(p_ref, c_ref, s_ref, q_ref, k_ref, v_ref, qm_ref):
        lane = _lane((ts, 128))
        cosv, sinv = c_ref[...], s_ref[...]
        for p in range(N_MIX // 2):
            xq = p_ref[:, p * 128:(p + 1) * 128]
            xk = p_ref[:, D_MIX + p * 128:D_MIX + (p + 1) * 128]
            xv = p_ref[:, 2 * D_MIX + p * 128:2 * D_MIX + (p + 1) * 128]
            yq = (xq * cosv + _rope_partner(xq) * sinv) * QSCALE
            yk = xk * cosv + _rope_partner(xk) * sinv
            for odd in (0, 1):
                hh = 2 * p + odd
                q_ref[:, hh * HP:(hh + 1) * HP] = jnp.where(lane < 64, _head_tile(yq, odd), 0.0).astype(BF16)
                k_ref[:, hh * HP:(hh + 1) * HP] = jnp.where(lane < 64, _head_tile(yk, odd), 0.0).astype(BF16)
                v_ref[:, hh * HP:(hh + 1) * HP] = jnp.where(lane < 64, _head_tile(xv, odd), 0.0).astype(BF16)
        for p in range(N_MEM // 2):
            t = p_ref[:, 3 * D_MIX + p * 128:3 * D_MIX + (p + 1) * 128] * QSCALE
            for odd in (0, 1):
                hh = 2 * p + odd
                qm_ref[:, hh * HP:(hh + 1) * HP] = jnp.where(lane < 64, _head_tile(t, odd), 0.0).astype(BF16)

    wmix, wmem = N_MIX * HP, N_MEM * HP
    return pl.pallas_call(
        body,
        out_shape=(jax.ShapeDtypeStruct((s, wmix), BF16),) * 3 + (jax.ShapeDtypeStruct((s, wmem), BF16),),
        grid=(s // ts,),
        in_specs=[pl.BlockSpec((ts, proj.shape[1]), lambda i: (i, 0)), pl.BlockSpec((ts, 128), lambda i: (i, 0)),
                  pl.BlockSpec((ts, 128), lambda i: (i, 0))],
        out_specs=(pl.BlockSpec((ts, wmix), lambda i: (i, 0)),) * 3 + (pl.BlockSpec((ts, wmem), lambda i: (i, 0)),),
        compiler_params=_cp("parallel"),
        name=name,
    )(proj, cos2, sin2)


def _prep_dil_bwd(dq, dk, dv, dqm, cos2, sin2, name):
    s = dq.shape[0]
    ts = _tile(s, 256, 8)

    def body(dq_ref, dk_ref, dv_ref, dqm_ref, c_ref, s_ref, o_ref):
        cosv, sinv = c_ref[...], s_ref[...]
        for p in range(N_MIX // 2):
            e, o = 2 * p, 2 * p + 1
            dyq = _pair_tile(dq_ref[:, e * HP:(e + 1) * HP], dq_ref[:, o * HP:(o + 1) * HP]) * SCALE
            dyk = _pair_tile(dk_ref[:, e * HP:(e + 1) * HP], dk_ref[:, o * HP:(o + 1) * HP]) * LN2
            dxv = _pair_tile(dv_ref[:, e * HP:(e + 1) * HP], dv_ref[:, o * HP:(o + 1) * HP])
            o_ref[:, p * 128:(p + 1) * 128] = (dyq * cosv - _rope_partner(dyq) * sinv).astype(o_ref.dtype)
            o_ref[:, D_MIX + p * 128:D_MIX + (p + 1) * 128] = (dyk * cosv - _rope_partner(dyk) * sinv).astype(o_ref.dtype)
            o_ref[:, 2 * D_MIX + p * 128:2 * D_MIX + (p + 1) * 128] = dxv.astype(o_ref.dtype)
        for p in range(N_MEM // 2):
            e, o = 2 * p, 2 * p + 1
            t = _pair_tile(dqm_ref[:, e * HP:(e + 1) * HP], dqm_ref[:, o * HP:(o + 1) * HP]) * SCALE
            o_ref[:, 3 * D_MIX + p * 128:3 * D_MIX + (p + 1) * 128] = t.astype(o_ref.dtype)

    wmix, wmem = N_MIX * HP, N_MEM * HP
    row = lambda w: pl.BlockSpec((ts, w), lambda i: (i, 0))
    return pl.pallas_call(
        body,
        out_shape=jax.ShapeDtypeStruct((s, DIL_IN), BF16),
        grid=(s // ts,),
        in_specs=[row(wmix)] * 3 + [row(wmem), row(128), row(128)],
        out_specs=row(DIL_IN),
        compiler_params=_cp("parallel"),
        name=name,
    )(dq, dk, dv, dqm, cos2, sin2)


def _mask_bias(mode, blk):
    n = 1 if mode == "causal" else DIL_MAX // blk + 1
    idx = jnp.arange(blk, dtype=jnp.int32)
    dist = jnp.arange(n, dtype=jnp.int32)[:, None, None] * blk + idx[None, :, None] - idx[None, None, :]
    if mode == "causal":
        return jnp.where(dist >= 0, 0.0, NEG).astype(F32)
    cnt = ((dist <= 128).astype(jnp.int32) + ((dist <= 512) & (dist % 4 == 0)).astype(jnp.int32)
           + ((dist <= DIL_MAX) & (dist % 16 == 0)).astype(jnp.int32))
    bias = jnp.where(cnt == 3, math.log2(3.0), jnp.where(cnt == 2, 1.0, 0.0))
    return jnp.where((dist >= 0) & (cnt > 0), bias, NEG).astype(F32)


def _flash_dims(q, k, mode):
    sq, w = q.shape
    sk = k.shape[0]
    tq = _tile(sq, FLASH_BLK, 8)
    tk = sk if mode == "full" else tq
    band = DIL_MAX // tk if mode == "dilated" else None
    return sq, sk, w, tq, tk, sq // tq, sk // tk, band


def _flash_fwd(q, k, v, mode, name):
    sq, sk, w, tq, tk, nq, nk, band = _flash_dims(q, k, mode)
    hb = HEADS_PER_STEP
    wb = hb * HP
    nch = tk // 128
    has_bias = mode != "full"

    def kidx(hp, qi, kj):
        if mode == "full":
            return (kj, hp)
        lo = jnp.maximum(qi - band, 0) if mode == "dilated" else 0
        return (jnp.clip(kj, lo, qi), hp)

    def body(*refs):
        if has_bias:
            q_ref, k_ref, v_ref, b_ref, o_ref, lse_ref, m_ref, l_ref, acc_ref = refs
        else:
            q_ref, k_ref, v_ref, o_ref, lse_ref, m_ref, l_ref, acc_ref = refs
        qi, kj = pl.program_id(1), pl.program_id(2)

        @pl.when(kj == 0)
        def _():
            m_ref[...] = jnp.full_like(m_ref, -jnp.inf)
            l_ref[...] = jnp.zeros_like(l_ref)
            acc_ref[...] = jnp.zeros_like(acc_ref)

        def step(bias_tile):
            for h in range(hb):
                cols = slice(h * HP, (h + 1) * HP)
                sc = lax.dot_general(q_ref[:, cols], k_ref[:, cols], _DIMS["nt"], preferred_element_type=F32)
                if bias_tile is not None:
                    sc = sc + bias_tile()
                m_prev = m_ref[h]
                m_new = jnp.maximum(m_prev, jnp.max(sc, axis=-1, keepdims=True))
                alpha = jnp.exp2(m_prev - m_new)
                psum, chunks = None, []
                for c in range(nch):
                    pc = jnp.exp2(sc[:, c * 128:(c + 1) * 128] - m_new)
                    psum = pc if psum is None else psum + pc
                    chunks.append(pc.astype(BF16))
                p = chunks[0] if nch == 1 else jnp.concatenate(chunks, axis=1)
                l_ref[h] = alpha * l_ref[h] + psum
                acc_ref[h] = alpha * acc_ref[h] + jnp.dot(p, v_ref[:, cols], preferred_element_type=F32)
                m_ref[h] = m_new

        if mode == "full":
            step(None)
        elif mode == "causal":
            pl.when(kj == qi)(lambda: step(lambda: b_ref[0]))
            pl.when(kj < qi)(lambda: step(None))
        else:
            pl.when((kj <= qi) & (kj >= qi - band))(lambda: step(lambda: b_ref[qi - kj]))

        @pl.when(kj == nk - 1)
        def _():
            for h in range(hb):
                cols = slice(h * HP, (h + 1) * HP)
                l = jnp.sum(l_ref[h], axis=-1, keepdims=True)
                o_ref[:, cols] = (acc_ref[h] / l).astype(o_ref.dtype)
                lse_ref[:, cols] = m_ref[h] + jnp.log2(l)

    qspec = pl.BlockSpec((tq, wb), lambda hp, qi, kj: (qi, hp))
    in_specs = [qspec, pl.BlockSpec((tk, wb), kidx), pl.BlockSpec((tk, wb), kidx)]
    args = [q, k, v]
    if has_bias:
        bias = _mask_bias(mode, tq)
        in_specs.append(pl.BlockSpec(bias.shape, lambda hp, qi, kj: (0, 0, 0)))
        args.append(bias)
    return pl.pallas_call(
        body,
        out_shape=(jax.ShapeDtypeStruct((sq, w), BF16), jax.ShapeDtypeStruct((sq, w), F32)),
        grid=(w // wb, nq, nk),
        in_specs=in_specs,
        out_specs=(qspec, qspec),
        scratch_shapes=[pltpu.VMEM((hb, tq, HP), F32), pltpu.VMEM((hb, tq, HP), F32), pltpu.VMEM((hb, tq, HP), F32)],
        compiler_params=_cp("parallel", "parallel", "arbitrary"),
        name=name,
    )(*args)


def _flash_bwd(q, k, v, o, do, lse, mode, name):
    sq, sk, w, tq, tk, nq, nk, band = _flash_dims(q, k, mode)
    hb = HEADS_PER_STEP
    wb = hb * HP
    nch = tk // 128
    has_bias = mode != "full"

    def qidx(hp, kj, qi):
        if mode == "full":
            return (qi, hp)
        hi = jnp.minimum(kj + band, nq - 1) if mode == "dilated" else nq - 1
        return (jnp.clip(qi, kj, hi), hp)

    def body(*refs):
        if has_bias:
            q_ref, k_ref, v_ref, o_ref, do_ref, lse_ref, b_ref, dq_ref, dk_ref, dv_ref, delta_ref = refs
        else:
            q_ref, k_ref, v_ref, o_ref, do_ref, lse_ref, dq_ref, dk_ref, dv_ref, delta_ref = refs
        kj, qi = pl.program_id(1), pl.program_id(2)
        rows = pl.ds(pl.multiple_of(qi * tq, tq), tq)

        @pl.when((kj == 0) & (qi == 0))
        def _():
            dq_ref[...] = jnp.zeros_like(dq_ref)

        @pl.when(qi == 0)
        def _():
            dk_ref[...] = jnp.zeros_like(dk_ref)
            dv_ref[...] = jnp.zeros_like(dv_ref)

        first = kj == (jnp.maximum(qi - band, 0) if mode == "dilated" else 0)

        @pl.when(first)
        def _():
            for h in range(hb):
                cols = slice(h * HP, (h + 1) * HP)
                dl = jnp.sum(do_ref[:, cols].astype(F32) * o_ref[:, cols].astype(F32), axis=-1, keepdims=True)
                delta_ref[h, rows, :] = jnp.broadcast_to(dl, (tq, HP))

        def step(bias_tile):
            for h in range(hb):
                cols = slice(h * HP, (h + 1) * HP)
                qv, kv, dov = q_ref[:, cols], k_ref[:, cols], do_ref[:, cols]
                sc = lax.dot_general(qv, kv, _DIMS["nt"], preferred_element_type=F32)
                if bias_tile is not None:
                    sc = sc + bias_tile()
                dp = lax.dot_general(dov, v_ref[:, cols], _DIMS["nt"], preferred_element_type=F32)
                lse_b = lse_ref[:, cols]
                dlt = delta_ref[h, rows, :]
                pch, dsch = [], []
                for c in range(nch):
                    lanes = slice(c * 128, (c + 1) * 128)
                    pc = jnp.exp2(sc[:, lanes] - lse_b)
                    pch.append(pc.astype(BF16))
                    dsch.append((pc * (dp[:, lanes] - dlt)).astype(BF16))
                p = pch[0] if nch == 1 else jnp.concatenate(pch, axis=1)
                ds = dsch[0] if nch == 1 else jnp.concatenate(dsch, axis=1)
                dv_ref[:, cols] += lax.dot_general(p, dov, _DIMS["tn"], preferred_element_type=F32)
                dk_ref[:, cols] += lax.dot_general(ds, qv, _DIMS["tn"], preferred_element_type=F32)
                dq_ref[rows, cols] += jnp.dot(ds, kv, preferred_element_type=F32)

        if mode == "full":
            step(None)
        elif mode == "causal":
            pl.when(qi == kj)(lambda: step(lambda: b_ref[0]))
            pl.when(qi > kj)(lambda: step(None))
        else:
            pl.when((qi >= kj) & (qi <= kj + band))(lambda: step(lambda: b_ref[qi - kj]))

    qspec = pl.BlockSpec((tq, wb), qidx)
    kspec = pl.BlockSpec((tk, wb), lambda hp, kj, qi: (kj, hp))
    in_specs = [qspec, kspec, kspec, qspec, qspec, qspec]
    args = [q, k, v, o, do, lse]
    if has_bias:
        bias = _mask_bias(mode, tq)
        in_specs.append(pl.BlockSpec(bias.shape, lambda hp, kj, qi: (0, 0, 0)))
        args.append(bias)
    return pl.pallas_call(
        body,
        out_shape=(jax.ShapeDtypeStruct((sq, w), F32), jax.ShapeDtypeStruct((sk, w), F32), jax.ShapeDtypeStruct((sk, w), F32)),
        grid=(w // wb, nk, nq),
        in_specs=in_specs,
        out_specs=(pl.BlockSpec((sq, wb), lambda hp, kj, qi: (0, hp)), kspec, kspec),
        scratch_shapes=[pltpu.VMEM((hb, sq, HP), F32)],
        compiler_params=_cp("parallel", "arbitrary", "arbitrary"),
        name=name,
    )(*args)


def _conv_rc(s):
    return _tile(s, 256, 8)


def _shift_down(x, prev8, nrows):
    rows = lax.broadcasted_iota(jnp.int32, x.shape, 0)
    out = pltpu.roll(x, nrows, 0)
    for i in range(nrows):
        out = jnp.where(rows == i, prev8[8 - nrows + i:8 - nrows + i + 1, :], out)
    return out


def _shift_up(x, next8, nrows):
    n = x.shape[0]
    rows = lax.broadcasted_iota(jnp.int32, x.shape, 0)
    out = pltpu.roll(x, n - nrows, 0)
    for i in range(nrows):
        out = jnp.where(rows == n - nrows + i, next8[i:i + 1, :], out)
    return out


def _conv_taps(uv_ref, ug_ref, r, rc):
    r0 = pl.multiple_of(r * rc, rc)
    x = jnp.concatenate([uv_ref[pl.ds(r0, rc), :], ug_ref[pl.ds(r0, rc), :]], axis=1)
    p0 = pl.multiple_of(jnp.maximum(r0 - 8, 0), 8)
    prev8 = jnp.where(r > 0, jnp.concatenate([uv_ref[pl.ds(p0, 8), :], ug_ref[pl.ds(p0, 8), :]], axis=1), 0.0)
    return r0, x, _shift_down(x, prev8, 1), _shift_down(x, prev8, 2)


def _conv_specs(s, tf, nf):
    strip = pl.BlockSpec((s, tf), lambda j: (0, j))
    return strip, [pl.BlockSpec((3, tf), lambda j: (0, j)), pl.BlockSpec((3, tf), lambda j: (0, j + nf)),
                   pl.BlockSpec((1, tf), lambda j: (0, j)), pl.BlockSpec((1, tf), lambda j: (0, j + nf))]


def _conv_params(wv_ref, wg_ref, bv_ref, bg_ref):
    w = jnp.concatenate([wv_ref[...], wg_ref[...]], axis=1)
    return w[0:1, :], w[1:2, :], w[2:3, :], jnp.concatenate([bv_ref[...], bg_ref[...]], axis=1)


def _conv_fwd(u_val, u_gate, cw, cb, name):
    s, f = u_val.shape
    tf = CONV_TF
    nf = f // tf
    rc = _conv_rc(s)

    def body(uv_ref, ug_ref, wv_ref, wg_ref, bv_ref, bg_ref, a_ref):
        w0, w1, w2, b = _conv_params(wv_ref, wg_ref, bv_ref, bg_ref)

        def chunk(r, carry):
            r0, x, x1, x2 = _conv_taps(uv_ref, ug_ref, r, rc)
            c = b + w0 * x2 + w1 * x1 + w2 * x
            val, gate = c[:, :tf], c[:, tf:]
            a_ref[pl.ds(r0, rc), :] = (gate * jax.nn.sigmoid(gate) * val).astype(a_ref.dtype)
            return carry

        lax.fori_loop(0, s // rc, chunk, 0)

    strip, params = _conv_specs(s, tf, nf)
    return pl.pallas_call(
        body,
        out_shape=jax.ShapeDtypeStruct((s, f), BF16),
        grid=(nf,),
        in_specs=[strip, strip] + params,
        out_specs=strip,
        compiler_params=_cp("parallel"),
        name=name,
    )(u_val, u_gate, cw, cw, cb, cb)


def _conv_bwd(u_val, u_gate, da, cw, cb, name):
    s, f = u_val.shape
    tf = CONV_TF
    nf = f // tf
    rc = _conv_rc(s)
    nchunk = s // rc

    def body(uv_ref, ug_ref, da_ref, wv_ref, wg_ref, bv_ref, bg_ref, duv_ref, dug_ref, dwv_ref, dwg_ref, dbv_ref, dbg_ref, next_ref):
        w0, w1, w2, b = _conv_params(wv_ref, wg_ref, bv_ref, bg_ref)
        next_ref[...] = jnp.zeros_like(next_ref)

        def chunk(it, carry):
            g0, g1, g2, gb = carry
            r = nchunk - 1 - it
            r0, x, x1, x2 = _conv_taps(uv_ref, ug_ref, r, rc)
            c = b + w0 * x2 + w1 * x1 + w2 * x
            val, gate = c[:, :tf], c[:, tf:]
            sg = jax.nn.sigmoid(gate)
            dav = da_ref[pl.ds(r0, rc), :]
            dc = jnp.concatenate([dav * (gate * sg), dav * val * (sg * (1.0 + gate * (1.0 - sg)))], axis=1)
            nxt = next_ref[...]
            du = (w2 * dc + w1 * _shift_up(dc, nxt, 1) + w0 * _shift_up(dc, nxt, 2)).astype(duv_ref.dtype)
            duv_ref[pl.ds(r0, rc), :] = du[:, :tf]
            dug_ref[pl.ds(r0, rc), :] = du[:, tf:]
            next_ref[...] = dc[0:8, :]
            return (g0 + jnp.sum(dc * x2, axis=0, keepdims=True), g1 + jnp.sum(dc * x1, axis=0, keepdims=True),
                    g2 + jnp.sum(dc * x, axis=0, keepdims=True), gb + jnp.sum(dc, axis=0, keepdims=True))

        zero = jnp.zeros((1, 2 * tf), F32)
        g0, g1, g2, gb = lax.fori_loop(0, nchunk, chunk, (zero, zero, zero, zero))
        for i, gi in enumerate((g0, g1, g2)):
            dwv_ref[i:i + 1, :] = gi[:, :tf]
            dwg_ref[i:i + 1, :] = gi[:, tf:]
        dbv_ref[...] = gb[:, :tf]
        dbg_ref[...] = gb[:, tf:]

    strip, params = _conv_specs(s, tf, nf)
    taps = pl.BlockSpec((3, tf), lambda j: (0, j))
    bias = pl.BlockSpec((1, tf), lambda j: (0, j))
    act = jax.ShapeDtypeStruct((s, f), BF16)
    return pl.pallas_call(
        body,
        out_shape=(act, act, jax.ShapeDtypeStruct((3, f), F32), jax.ShapeDtypeStruct((3, f), F32),
                   jax.ShapeDtypeStruct((1, f), F32), jax.ShapeDtypeStruct((1, f), F32)),
        grid=(nf,),
        in_specs=[strip, strip, strip] + params,
        out_specs=(strip, strip, taps, taps, bias, bias),
        scratch_shapes=[pltpu.VMEM((8, 2 * tf), F32)],
        compiler_params=_cp("parallel"),
        name=name,
    )(u_val, u_gate, da, cw, cw, cb, cb)


def _adamw(w, g, m, v, name):
    r, c = w.shape
    tr = _tile(r, 256, 8) if r % 8 == 0 else r
    c1 = 1.0 - ADAM_B1 ** ADAM_STEP
    c2 = 1.0 - ADAM_B2 ** ADAM_STEP

    def body(w_ref, g_ref, m_ref, v_ref, d_ref, mo_ref, vo_ref):
        gv = g_ref[...]
        mn = ADAM_B1 * m_ref[...] + (1.0 - ADAM_B1) * gv
        vn = ADAM_B2 * v_ref[...] + (1.0 - ADAM_B2) * (gv * gv)
        d_ref[...] = -ADAM_LR * ((mn / c1) / (jnp.sqrt(vn / c2) + ADAM_EPS) + ADAM_WD * w_ref[...])
        mo_ref[...] = mn
        vo_ref[...] = vn

    blk = pl.BlockSpec((tr, c), lambda i: (i, 0))
    shp = jax.ShapeDtypeStruct((r, c), F32)
    return pl.pallas_call(
        body, out_shape=(shp, shp, shp), grid=(r // tr,), in_specs=[blk] * 4, out_specs=(blk,) * 3,
        compiler_params=_cp("parallel"), name=name,
    )(w, g, m, v)


def _sum_rows(parts, out_dtype, name):
    n, r, c = parts.shape
    tr = _tile(r, 256, 8)

    def body(p_ref, o_ref):
        tot = p_ref[0].astype(F32)
        for i in range(1, n):
            tot = tot + p_ref[i].astype(F32)
        o_ref[...] = tot.astype(o_ref.dtype)

    return pl.pallas_call(
        body, out_shape=jax.ShapeDtypeStruct((r, c), out_dtype), grid=(r // tr,),
        in_specs=[pl.BlockSpec((n, tr, c), lambda i: (0, i, 0))], out_specs=pl.BlockSpec((tr, c), lambda i: (i, 0)),
        compiler_params=_cp("parallel"), name=name,
    )(parts)


def _add_pairs(a, b, name):
    n, r, c = a.shape
    tr = _tile(r, 256, 8)

    def body(a_ref, b_ref, o_ref):
        o_ref[...] = (a_ref[...].astype(F32) + b_ref[...].astype(F32)).astype(o_ref.dtype)

    blk = pl.BlockSpec((1, tr, c), lambda j, i: (j, i, 0))
    return pl.pallas_call(
        body, out_shape=jax.ShapeDtypeStruct((n, r, c), BF16), grid=(n, r // tr), in_specs=[blk, blk], out_specs=blk,
        compiler_params=_cp("parallel", "parallel"), name=name,
    )(a, b)


_ANY = pl.BlockSpec(memory_space=pl.ANY)


def _place():
    return lax.axis_index("x"), lax.axis_index("y"), lax.axis_index("c")


def _other_chips(x, y):
    return [(1 - x, y), (x, 1 - y), (1 - x, 1 - y)]


def _rows_half(ref, c, axis):
    rh = ref.shape[axis] // 2
    idx = [slice(None)] * len(ref.shape)
    idx[axis] = pl.ds(pl.multiple_of(c * rh, 16), rh)
    return ref.at[tuple(idx)]


def _remote(src, dst, send_sems, recv_sems, kk, to):
    return pltpu.make_async_remote_copy(src_ref=src, dst_ref=dst, send_sem=send_sems.at[kk], recv_sem=recv_sems.at[kk],
                                        device_id=to, device_id_type=MESH)


def _gather_shards(shards, name):
    n = len(shards)

    def body(*refs):
        in_refs, out_refs = refs[:n], refs[n:2 * n]
        send_sems, recv_sems = refs[2 * n:]
        x, y, c = _place()
        me = 2 * x + y
        sib = (x, y, 1 - c)
        chips = _other_chips(x, y)
        first = []
        for a in range(n):
            for j, (cx, cy) in enumerate(chips):
                first.append(_remote(_rows_half(in_refs[a], c, 1), _rows_half(out_refs[a].at[:, me], c, 1),
                                     send_sems, recv_sems, 6 * a + j, (cx, cy, c)))
        for cp in first:
            cp.start()
        passed = []
        for a in range(n):
            for j, (cx, cy) in enumerate(chips):
                slot = _rows_half(out_refs[a].at[:, 2 * cx + cy], c, 1)
                _remote(slot, slot, send_sems, recv_sems, 6 * a + j, (x, y, c)).wait_recv()
                fwd = _remote(slot, slot, send_sems, recv_sems, 6 * a + 3 + j, sib)
                fwd.start()
                passed.append(fwd)
        for a in range(n):
            for j, (cx, cy) in enumerate(chips):
                slot = _rows_half(out_refs[a].at[:, 2 * cx + cy], 1 - c, 1)
                _remote(slot, slot, send_sems, recv_sems, 6 * a + 3 + j, (x, y, c)).wait_recv()
        for cp in first + passed:
            cp.wait_send()

    return pl.pallas_call(
        body,
        out_shape=tuple(jax.ShapeDtypeStruct((s.shape[0], N_CHIPS) + s.shape[1:], s.dtype) for s in shards),
        in_specs=[_ANY] * n, out_specs=(_ANY,) * n,
        scratch_shapes=[pltpu.SemaphoreType.DMA((6 * n,)), pltpu.SemaphoreType.DMA((6 * n,))],
        name=name,
    )(*shards)


def _swap_halves(gs, name):
    n = len(gs)

    def body(*refs):
        in_refs, out_refs = refs[:n], refs[n:2 * n]
        send_sems, recv_sems = refs[2 * n:]
        x, y, c = _place()
        copies = [_remote(_rows_half(in_refs[a], 1 - c, 2), out_refs[a], send_sems, recv_sems, a, (x, y, 1 - c)) for a in range(n)]
        for cp in copies:
            cp.start()
        for cp in copies:
            cp.wait()

    return pl.pallas_call(
        body,
        out_shape=tuple(jax.ShapeDtypeStruct(g.shape[:2] + (g.shape[2] // 2, g.shape[3]), g.dtype) for g in gs),
        in_specs=[_ANY] * n, out_specs=(_ANY,) * n,
        scratch_shapes=[pltpu.SemaphoreType.DMA((n,)), pltpu.SemaphoreType.DMA((n,))], name=name,
    )(*gs)


def _scatter_to_chips(ps, name):
    n = len(ps)

    def body(*refs):
        in_refs, out_refs = refs[:n], refs[n:2 * n]
        send_sems, recv_sems = refs[2 * n:]
        x, y, c = _place()
        me = 2 * x + y
        chips = _other_chips(x, y)
        sends = []
        for a in range(n):
            for j, (cx, cy) in enumerate(chips):
                cp = _remote(in_refs[a].at[:, 2 * cx + cy], out_refs[a].at[me], send_sems, recv_sems, 3 * a + j, (cx, cy, c))
                cp.start()
                sends.append(cp)
        for a in range(n):
            for j, (cx, cy) in enumerate(chips):
                slot = out_refs[a].at[2 * cx + cy]
                _remote(slot, slot, send_sems, recv_sems, 3 * a + j, (x, y, c)).wait_recv()
        for cp in sends:
            cp.wait_send()

    return pl.pallas_call(
        body,
        out_shape=tuple(jax.ShapeDtypeStruct((N_CHIPS, p.shape[0]) + p.shape[2:], p.dtype) for p in ps),
        in_specs=[_ANY] * n, out_specs=(_ANY,) * n,
        scratch_shapes=[pltpu.SemaphoreType.DMA((3 * n,)), pltpu.SemaphoreType.DMA((3 * n,))],
        name=name,
    )(*ps)


def _join_halves(rs, name):
    n = len(rs)

    def body(*refs):
        in_refs, out_refs = refs[:n], refs[n:2 * n]
        send_sems, recv_sems = refs[2 * n:]
        x, y, c = _place()
        copies = [_remote(in_refs[a], out_refs[a], send_sems, recv_sems, a, (x, y, 1 - c)) for a in range(n)]
        for cp in copies:
            cp.start()
        for cp in copies:
            cp.wait()

    return pl.pallas_call(
        body,
        out_shape=tuple(jax.ShapeDtypeStruct(r.shape, r.dtype) for r in rs),
        in_specs=[_ANY] * n, out_specs=(_ANY,) * n,
        scratch_shapes=[pltpu.SemaphoreType.DMA((n,)), pltpu.SemaphoreType.DMA((n,))], name=name,
    )(*rs)


def _gather_all(small, name):
    r, w = small.shape
    flips = [(dx, dy, dc) for dx in (0, 1) for dy in (0, 1) for dc in (0, 1) if (dx, dy, dc) != (0, 0, 0)]

    def body(in_ref, out_ref, send_sems, recv_sems, local_sem):
        x, y, c = _place()
        me = 4 * x + 2 * y + c
        mine = pltpu.make_async_copy(in_ref, out_ref.at[me], local_sem)
        mine.start()
        sends = []
        for j, (dx, dy, dc) in enumerate(flips):
            to = (x ^ dx, y ^ dy, c ^ dc)
            cp = pltpu.make_async_remote_copy(src_ref=in_ref, dst_ref=out_ref.at[me], send_sem=send_sems.at[j],
                                              recv_sem=recv_sems.at[j], device_id=to, device_id_type=MESH)
            cp.start()
            sends.append(cp)
        for j, (dx, dy, dc) in enumerate(flips):
            slot = out_ref.at[4 * (x ^ dx) + 2 * (y ^ dy) + (c ^ dc)]
            pltpu.make_async_remote_copy(src_ref=slot, dst_ref=slot, send_sem=send_sems.at[j], recv_sem=recv_sems.at[j],
                                         device_id=(x, y, c), device_id_type=MESH).wait_recv()
        for cp in sends:
            cp.wait_send()
        mine.wait()

    return pl.pallas_call(
        body, out_shape=jax.ShapeDtypeStruct((8, r, w), small.dtype), in_specs=[_ANY], out_specs=_ANY,
        scratch_shapes=[pltpu.SemaphoreType.DMA((7,)), pltpu.SemaphoreType.DMA((7,)), pltpu.SemaphoreType.DMA], name=name,
    )(small)


_BIG = ("w_in_fox", "w_in_dil", "w_mem_kv", "w_out", "w_up", "w_down")
_CONVW_SHARD = (DEPTH, 3, 2 * D_FF // N_CHIPS)
_COL_SHARDED = ("w_in_fox", "w_in_dil", "w_up")

_SMALL = (("norm_mix", (DEPTH, D_MODEL)), ("norm_mem", (DEPTH, D_MODEL)), ("norm_ffn", (DEPTH, D_MODEL)),
          ("conv_b", (DEPTH, 2 * D_FF)), ("norm_final", (D_MODEL,)), ("b_forget", (2, N_MIX)), ("conv_w", (DEPTH, 3, 2 * D_FF)))


def _pack_small(vals, spec):
    flat = jnp.concatenate([vals[n].reshape(-1).astype(F32) for n, _ in spec])
    rows = -(-flat.shape[0] // (8 * 128)) * 8
    return jnp.pad(flat, (0, rows * 128 - flat.shape[0])).reshape(rows, 128)


def _unpack_small(buf, spec):
    flat, out, off = buf.reshape(-1), {}, 0
    for n, shp in spec:
        k = int(np.prod(shp))
        out[n] = flat[off:off + k].reshape(shp)
        off += k
    return out


def _fox_cols_to_kernel(w):
    qkv, f, qm = w[:, :3 * D_MIX], w[:, 3 * D_MIX:3 * D_MIX + N_MIX], w[:, 3 * D_MIX + N_MIX:]
    return jnp.concatenate([qkv, qm, f, jnp.zeros((w.shape[0], 128 - N_MIX), w.dtype)], axis=1)


def _fox_cols_from_kernel(w):
    qkv, qm, f = w[:, :3 * D_MIX], w[:, 3 * D_MIX:3 * D_MIX + D_MEMQ], w[:, 3 * D_MIX + D_MEMQ:3 * D_MIX + D_MEMQ + N_MIX]
    return jnp.concatenate([qkv, f, qm], axis=1)


def _rope_pair_tables(s):
    inv = 1.0 / (ROPE_THETA ** (jnp.arange(0, HEAD_DIM, 2, dtype=F32) / HEAD_DIM))
    ang = jnp.arange(s, dtype=F32)[:, None] * inv[None, :]
    cos, sin = jnp.cos(ang), jnp.sin(ang)
    return jnp.concatenate([cos, cos, cos, cos], axis=1), jnp.concatenate([-sin, sin, -sin, sin], axis=1)


def _local_step(x, mem, target, small, wts):
    s = x.shape[0]
    cos2, sin2 = _rope_pair_tables(s)
    saved = []
    h = x
    for l in range(DEPTH):
        fox = l % 2 == 0
        slot = l // 2
        tag = f"L{l}"
        g_mix, g_mem, g_ffn = (small[n][l:l + 1] for n in ("norm_mix", "norm_mem", "norm_ffn"))
        xn = _rms_fwd(h, g_mix, f"rms_mix_{tag}")
        proj = _matmul(xn, wts["w_in"][l], "nn", F32, f"mm_in_{tag}")
        if fox:
            bfg = jnp.pad(small["b_forget"][slot:slot + 1], ((0, 0), (0, 128 - N_MIX)))
            qp, kp, vp, qmp = _prep_fox_fwd(proj, bfg, f"prep_fox_{tag}")
        else:
            bfg = None
            qp, kp, vp, qmp = _prep_dil_fwd(proj, cos2, sin2, f"prep_dil_{tag}")
        mn = _rms_fwd(mem, g_mem, f"rms_mem_{tag}")
        kvm = _matmul(mn, wts["w_mem_kv"][l], "nn", F32, f"mm_memkv_{tag}")
        kmp, vmp = _pad_heads([(kvm, 0, N_MEM), (kvm, D_MEMQ, N_MEM)], (1.0, 1.0), (N_MEM * HP,) * 2, f"pad_memkv_{tag}")
        mode = "causal" if fox else "dilated"
        o_mix, lse_mix = _flash_fwd(qp, kp, vp, mode, f"flash_{mode}_fwd_{tag}")
        o_mem, lse_mem = _flash_fwd(qmp, kmp, vmp, "full", f"flash_mem_fwd_{tag}")
        heads = _unpad_heads([o_mix, o_mem], (1.0, 1.0), BF16, f"unpad_heads_{tag}")
        h_mid = _matmul(heads, wts["w_out"][l], "nn", F32, f"mm_out_{tag}", residual=h)
        xn2 = _rms_fwd(h_mid, g_ffn, f"rms_ffn_{tag}")
        w_val, w_gate = wts["w_up"][l]
        u_val = _matmul(xn2, w_val, "nn", F32, f"mm_up_val_{tag}")
        u_gate = _matmul(xn2, w_gate, "nn", F32, f"mm_up_gate_{tag}")
        act = _conv_fwd(u_val, u_gate, wts["conv_w"][l], wts["conv_b"][l], f"conv_fwd_{tag}")
        h_out = _matmul(act, wts["w_down"][l], "nn", F32, f"mm_down_{tag}", residual=h_mid)
        saved.append(dict(h=h, xn=xn, proj=proj, bfg=bfg, qp=qp, kp=kp, vp=vp, qmp=qmp, mn=mn, kmp=kmp, vmp=vmp, o_mix=o_mix,
                          lse_mix=lse_mix, o_mem=o_mem, lse_mem=lse_mem, heads=heads, h_mid=h_mid, xn2=xn2, u_val=u_val, u_gate=u_gate, act=act))
        h = h_out

    loss_blk, dh, dg_final = _loss_head(h, small["norm_final"].reshape(1, D_MODEL), target, "loss_head")

    grads = {k: [None] * DEPTH for k in ("w_in", "w_mem_kv", "w_out", "w_up", "w_down", "conv_w", "conv_b", "norm_mix", "norm_mem",
                                         "norm_ffn")}
    grads["b_forget"] = [None, None]
    for l in reversed(range(DEPTH)):
        fox = l % 2 == 0
        tag = f"L{l}"
        sv = saved[l]
        g_mix, g_mem, g_ffn = (small[n][l:l + 1] for n in ("norm_mix", "norm_mem", "norm_ffn"))
        da = _matmul(dh, wts["w_down"][l], "nt", F32, f"mm_da_{tag}")
        grads["w_down"][l] = _matmul(sv["act"], dh, "tn", BF16, f"mm_dwdown_{tag}")
        du_val, du_gate, dcw_v, dcw_g, dcb_v, dcb_g = _conv_bwd(sv["u_val"], sv["u_gate"], da, wts["conv_w"][l], wts["conv_b"][l],
                                                                 f"conv_bwd_{tag}")
        grads["conv_w"][l] = jnp.concatenate([dcw_v, dcw_g], axis=1)
        grads["conv_b"][l] = jnp.concatenate([dcb_v, dcb_g], axis=1)
        w_val, w_gate = wts["w_up"][l]
        dxn2 = _matmul(du_val, w_val, "nt", F32, f"mm_dxn2_val_{tag}")
        dxn2 = _matmul(du_gate, w_gate, "nt", F32, f"mm_dxn2_gate_{tag}", residual=dxn2)
        grads["w_up"][l] = (_matmul(sv["xn2"], du_val, "tn", BF16, f"mm_dwup_val_{tag}"),
                            _matmul(sv["xn2"], du_gate, "tn", BF16, f"mm_dwup_gate_{tag}"))
        dh_mid, grads["norm_ffn"][l] = _rms_bwd(sv["h_mid"], g_ffn, dxn2, dh, f"rms_ffn_bwd_{tag}")
        dheads = _matmul(dh_mid, wts["w_out"][l], "nt", F32, f"mm_dheads_{tag}")
        grads["w_out"][l] = _matmul(sv["heads"], dh_mid, "tn", BF16, f"mm_dwout_{tag}")
        do_mix, do_mem = _pad_heads([(dheads, 0, N_MIX), (dheads, D_MIX, N_MEM)], (1.0, 1.0), (N_MIX * HP, N_MEM * HP),
                                    f"pad_dheads_{tag}")
        mode = "causal" if fox else "dilated"
        dqp, dkp, dvp = _flash_bwd(sv["qp"], sv["kp"], sv["vp"], sv["o_mix"], do_mix, sv["lse_mix"], mode, f"flash_{mode}_bwd_{tag}")
        dqmp, dkmp, dvmp = _flash_bwd(sv["qmp"], sv["kmp"], sv["vmp"], sv["o_mem"], do_mem, sv["lse_mem"], "full",
                                      f"flash_mem_bwd_{tag}")
        if fox:
            dproj, dbf = _prep_fox_bwd(dqp, dkp, dvp, dqmp, sv["proj"], sv["bfg"], f"prep_fox_bwd_{tag}")
            grads["b_forget"][l // 2] = dbf[0, :N_MIX]
        else:
            dproj = _prep_dil_bwd(dqp, dkp, dvp, dqmp, cos2, sin2, f"prep_dil_bwd_{tag}")
        dkvm = _unpad_heads([dkmp, dvmp], (LN2, 1.0), BF16, f"unpad_dkvm_{tag}")
        grads["w_mem_kv"][l] = _matmul(sv["mn"], dkvm, "tn", BF16, f"mm_dwmemkv_{tag}")
        dmn = _matmul(dkvm, wts["w_mem_kv"][l], "nt", F32, f"mm_dmn_{tag}")
        _, grads["norm_mem"][l] = _rms_bwd(mem, g_mem, dmn, None, f"rms_mem_bwd_{tag}")
        dxn = _matmul(dproj, wts["w_in"][l], "nt", F32, f"mm_dxn_{tag}")
        grads["w_in"][l] = _matmul(sv["xn"], dproj, "tn", BF16, f"mm_dwin_{tag}")
        dh, grads["norm_mix"][l] = _rms_bwd(sv["h"], g_mix, dxn, dh_mid, f"rms_mix_bwd_{tag}")
    grads["norm_final"] = dg_final
    return loss_blk, dh, grads


def _full_from_chips(per_chip, name):
    n_layers, _, r, c = per_chip.shape
    if name in _COL_SHARDED:
        return [jnp.concatenate([per_chip[l, j] for j in range(N_CHIPS)], axis=1) for l in range(n_layers)]
    return [per_chip[l].reshape(N_CHIPS * r, c) for l in range(n_layers)]


def _shards_from_full(layers, name):
    if name in _COL_SHARDED:
        return jnp.stack([jnp.stack(jnp.split(g, N_CHIPS, axis=1)) for g in layers])
    return jnp.stack([g.reshape(N_CHIPS, g.shape[0] // N_CHIPS, g.shape[1]) for g in layers])


def kernel(x, mem, norm_mix, norm_mem, norm_ffn, w_in_fox, b_forget, w_in_dil, w_mem_kv, w_out, w_up, conv_w, conv_b, w_down, norm_final, loss_target, m_norm_mix, m_norm_mem, m_norm_ffn, m_w_in_fox, m_b_forget, m_w_in_dil, m_w_mem_kv, m_w_out, m_w_up, m_conv_w, m_conv_b, m_w_down, m_norm_final, v_norm_mix, v_norm_mem, v_norm_ffn, v_w_in_fox, v_b_forget, v_w_in_dil, v_w_mem_kv, v_w_out, v_w_up, v_conv_w, v_conv_b, v_w_down, v_norm_final):
    w_sh = dict(w_in_fox=w_in_fox, w_in_dil=w_in_dil, w_mem_kv=w_mem_kv, w_out=w_out, w_up=w_up, w_down=w_down, conv_w=conv_w)
    m_sh = dict(w_in_fox=m_w_in_fox, w_in_dil=m_w_in_dil, w_mem_kv=m_w_mem_kv, w_out=m_w_out, w_up=m_w_up, w_down=m_w_down, conv_w=m_conv_w)
    v_sh = dict(w_in_fox=v_w_in_fox, w_in_dil=v_w_in_dil, w_mem_kv=v_w_mem_kv, w_out=v_w_out, w_up=v_w_up, w_down=v_w_down, conv_w=v_conv_w)
    small = dict(norm_mix=norm_mix, norm_mem=norm_mem, norm_ffn=norm_ffn, conv_b=conv_b, norm_final=norm_final, b_forget=b_forget)
    m_small = dict(norm_mix=m_norm_mix, norm_mem=m_norm_mem, norm_ffn=m_norm_ffn, conv_b=m_conv_b, norm_final=m_norm_final, b_forget=m_b_forget)
    v_small = dict(norm_mix=v_norm_mix, norm_mem=v_norm_mem, norm_ffn=v_norm_ffn, conv_b=v_conv_b, norm_final=v_norm_final, b_forget=v_b_forget)
    chip = 2 * lax.axis_index("x") + lax.axis_index("y")
    core = lax.axis_index("c")

    own = [w_sh[n].astype(BF16) for n in _BIG]
    others = _gather_shards(own, "gather_weights")
    per_chip = {n: lax.dynamic_update_slice_in_dim(o, w[:, None], chip, axis=1) for n, o, w in zip(_BIG, others, own)}
    conv_spec = (("conv_w", _CONVW_SHARD),)
    conv_all = _gather_all(_pack_small(dict(conv_w=conv_w), conv_spec), "gather_conv_w")
    conv_w_full = jnp.concatenate([_unpack_small(conv_all[2 * j], conv_spec)["conv_w"] for j in range(N_CHIPS)], axis=-1)
    fox_full = [_fox_cols_to_kernel(w) for w in _full_from_chips(per_chip["w_in_fox"], "w_in_fox")]
    dil_full = _full_from_chips(per_chip["w_in_dil"], "w_in_dil")
    wts = dict(
        w_in=[fox_full[0], dil_full[0], fox_full[1], dil_full[1]],
        w_mem_kv=_full_from_chips(per_chip["w_mem_kv"], "w_mem_kv"),
        w_out=_full_from_chips(per_chip["w_out"], "w_out"),
        w_up=[(jnp.concatenate([per_chip["w_up"][l, 0], per_chip["w_up"][l, 1]], axis=1),
               jnp.concatenate([per_chip["w_up"][l, 2], per_chip["w_up"][l, 3]], axis=1)) for l in range(DEPTH)],
        w_down=_full_from_chips(per_chip["w_down"], "w_down"),
        conv_w=[conv_w_full[l] for l in range(DEPTH)],
        conv_b=[conv_b[l:l + 1] for l in range(DEPTH)],
    )

    loss_blk, dx, grads = _local_step(x[0], mem[0], loss_target[0], small, wts)
    loss = lax.psum(loss_blk[0, 0], ("x", "y", "c"))

    g_layers = dict(
        w_in_fox=[_fox_cols_from_kernel(grads["w_in"][0]), _fox_cols_from_kernel(grads["w_in"][2])],
        w_in_dil=[grads["w_in"][1], grads["w_in"][3]],
        w_mem_kv=grads["w_mem_kv"], w_out=grads["w_out"],
        w_up=[jnp.concatenate([g_val, g_gate], axis=1) for g_val, g_gate in grads["w_up"]], w_down=grads["w_down"],
    )
    g_sh = [_shards_from_full(g_layers[n], n) for n in _BIG]
    from_sibling = _swap_halves(g_sh, "grad_swap_halves")
    pair_sum, halves = [], []
    for n, g, f in zip(_BIG, g_sh, from_sibling):
        mine = lax.dynamic_slice_in_dim(g, core * f.shape[2], f.shape[2], axis=2)
        flat3 = (-1,) + f.shape[2:]
        pair_sum.append(_add_pairs(mine.reshape(flat3), f.reshape(flat3), f"grad_add_pairs_{n}").reshape(f.shape))
    from_chips = _scatter_to_chips(pair_sum, "grad_scatter_chips")
    for n, p, fc in zip(_BIG, pair_sum, from_chips):
        kept = lax.dynamic_slice_in_dim(p, chip, 1, axis=1).reshape((1,) + fc.shape[1:])
        fc = lax.dynamic_update_slice_in_dim(fc, kept, chip, axis=0)
        halves.append(_sum_rows(fc.reshape(N_CHIPS, -1, fc.shape[-1]), F32, f"grad_sum_chips_{n}").reshape(fc.shape[1:]))
    g_big = {}
    for n, mine, other in zip(_BIG, halves, _join_halves(halves, "grad_join_halves")):
        g_big[n] = jnp.concatenate([jnp.where(core == 0, mine, other), jnp.where(core == 0, other, mine)], axis=1)

    g_small_local = dict(
        norm_mix=jnp.concatenate(grads["norm_mix"]), norm_mem=jnp.concatenate(grads["norm_mem"]),
        norm_ffn=jnp.concatenate(grads["norm_ffn"]),
        conv_b=jnp.concatenate(grads["conv_b"]),
        norm_final=grads["norm_final"], b_forget=jnp.stack(grads["b_forget"]),
        conv_w=jnp.stack(grads["conv_w"]),
    )
    small_all = _gather_all(_pack_small(g_small_local, _SMALL), "small_gather_all")
    g_small = _unpack_small(_sum_rows(small_all, F32, "small_sum"), _SMALL)
    ncol = 2 * D_FF // N_CHIPS
    g_big["conv_w"] = lax.dynamic_slice_in_dim(g_small["conv_w"], chip * ncol, ncol, axis=2)

    out_g, out_d, out_m, out_v = {}, {}, {}, {}
    for n in _BIG + ("conv_w",):
        shp = w_sh[n].shape
        two_d = (-1, shp[-1])
        d, mo, vo = _adamw(w_sh[n].reshape(two_d), g_big[n].reshape(two_d), m_sh[n].reshape(two_d), v_sh[n].reshape(two_d), f"adamw_{n}")
        out_g[n], out_d[n], out_m[n], out_v[n] = g_big[n].reshape(shp), d.reshape(shp), mo.reshape(shp), vo.reshape(shp)
    spec = _SMALL[:-1]
    d, mo, vo = _adamw(_pack_small(small, spec), _pack_small(g_small, spec), _pack_small(m_small, spec), _pack_small(v_small, spec),
                       "adamw_small")
    d, mo, vo = _unpack_small(d, spec), _unpack_small(mo, spec), _unpack_small(vo, spec)
    for n, shp in spec:
        out_g[n], out_d[n], out_m[n], out_v[n] = g_small[n].reshape(shp), d[n], mo[n], vo[n]

    order = ("norm_mix", "norm_mem", "norm_ffn", "w_in_fox", "b_forget", "w_in_dil", "w_mem_kv", "w_out", "w_up", "conv_w", "conv_b",
             "w_down", "norm_final")
    return (loss, dx[None], *[out_g[n] for n in order], *[out_d[n] for n in order], *[out_m[n] for n in order],
            *[out_v[n] for n in order])
```

```python
import functools
import math

import numpy as np
import jax
import jax.numpy as jnp
from jax import lax
from jax.experimental import pallas as pl
from jax.experimental.pallas import tpu as pltpu

F32 = jnp.float32
BF16 = jnp.bfloat16

D_MODEL = 1024
DEPTH = 4
HEAD_DIM = 64
N_MIX = 12
N_MEM = 4
D_MIX = N_MIX * HEAD_DIM
D_MEMQ = N_MEM * HEAD_DIM
D_FF = 2816
FOX_IN = 3 * D_MIX + N_MIX + D_MEMQ
DIL_IN = 3 * D_MIX + D_MEMQ
HP = 128
SCALE = HEAD_DIM ** -0.5
NEG = -1e30
NORM_EPS = 1e-6
DIL_MAX = 2048
LOG2E = 1.0 / math.log(2.0)
LN2 = math.log(2.0)
QSCALE = SCALE * LOG2E
HEADS_PER_STEP = 2
ROPE_THETA = 10000.0
N_CHIPS = 4
CONV_TF = 128
FLASH_BLK = 512
VMEM_LIMIT = 48 * 1024 * 1024

ADAM_LR = 0.001
ADAM_B1 = 0.9
ADAM_B2 = 0.999
ADAM_EPS = 1e-08
ADAM_WD = 0.01
ADAM_STEP = 10

MESH = pl.DeviceIdType.MESH


def _cp(*sem):
    return pltpu.CompilerParams(dimension_semantics=tuple(sem), vmem_limit_bytes=VMEM_LIMIT)


def _tile(n, cap, mult=128):
    if n <= cap:
        return n
    t = (cap // mult) * mult
    while t >= mult:
        if n % t == 0:
            return t
        t -= mult
    raise ValueError(f"no tile for {n} under {cap}")


_DIMS = {"nn": (((1,), (0,)), ((), ())), "nt": (((1,), (1,)), ((), ())), "tn": (((0,), (0,)), ((), ()))}


def _matmul(a, b, mode, out_dtype, name, residual=None):
    if mode == "nn":
        (m, k), n = a.shape, b.shape[1]
    elif mode == "nt":
        (m, k), n = a.shape, b.shape[0]
    else:
        (k, m), n = a.shape, b.shape[1]
    tn = 1408 if n % 1408 == 0 else _tile(n, 1024)
    tm = 1408 if (m % 1408 == 0 and tn <= 1024) else _tile(m, 1024 if tn <= 1024 else 512)
    tk = _tile(k, 1408)
    nk = k // tk
    dims = _DIMS[mode]
    has_res = residual is not None

    def body(*refs):
        if has_res:
            a_ref, b_ref, r_ref, o_ref = refs[:4]
        else:
            a_ref, b_ref, o_ref = refs[:3]
        part = lax.dot_general(a_ref[...].astype(BF16), b_ref[...].astype(BF16), dims, preferred_element_type=F32)
        if nk == 1:
            if has_res:
                part = part + r_ref[...]
            o_ref[...] = part.astype(o_ref.dtype)
            return
        acc_ref = refs[-1]
        kk = pl.program_id(2)

        @pl.when(kk == 0)
        def _():
            acc_ref[...] = part

        @pl.when(kk > 0)
        def _():
            acc_ref[...] += part

        @pl.when(kk == nk - 1)
        def _():
            tot = acc_ref[...]
            if has_res:
                tot = tot + r_ref[...]
            o_ref[...] = tot.astype(o_ref.dtype)

    if mode == "nn":
        a_spec = pl.BlockSpec((tm, tk), lambda i, j, kk: (i, kk))
        b_spec = pl.BlockSpec((tk, tn), lambda i, j, kk: (kk, j))
    elif mode == "nt":
        a_spec = pl.BlockSpec((tm, tk), lambda i, j, kk: (i, kk))
        b_spec = pl.BlockSpec((tn, tk), lambda i, j, kk: (j, kk))
    else:
        a_spec = pl.BlockSpec((tk, tm), lambda i, j, kk: (kk, i))
        b_spec = pl.BlockSpec((tk, tn), lambda i, j, kk: (kk, j))
    in_specs = [a_spec, b_spec]
    args = [a, b]
    if has_res:
        in_specs.append(pl.BlockSpec((tm, tn), lambda i, j, kk: (i, j)))
        args.append(residual)
    return pl.pallas_call(
        body,
        out_shape=jax.ShapeDtypeStruct((m, n), out_dtype),
        grid=(m // tm, n // tn, nk),
        in_specs=in_specs,
        out_specs=pl.BlockSpec((tm, tn), lambda i, j, kk: (i, j)),
        scratch_shapes=[pltpu.VMEM((tm, tn), F32)] if nk > 1 else [],
        compiler_params=_cp("parallel", "parallel", "arbitrary"),
        name=name,
    )(*args)


def _rms_fwd(h, g, name):
    r, d = h.shape
    tr = _tile(r, 512, 8)

    def body(h_ref, g_ref, o_ref):
        x = h_ref[...]
        rstd = lax.rsqrt(jnp.mean(x * x, axis=-1, keepdims=True) + NORM_EPS)
        o_ref[...] = ((x * rstd) * g_ref[...]).astype(o_ref.dtype)

    return pl.pallas_call(
        body,
        out_shape=jax.ShapeDtypeStruct((r, d), BF16),
        grid=(r // tr,),
        in_specs=[pl.BlockSpec((tr, d), lambda i: (i, 0)), pl.BlockSpec((1, d), lambda i: (0, 0))],
        out_specs=pl.BlockSpec((tr, d), lambda i: (i, 0)),
        compiler_params=_cp("parallel"),
        name=name,
    )(h, g)


def _rms_bwd(h, g, dy, dres, name):
    r, d = h.shape
    tr = _tile(r, 512, 8)
    need_dh = dres is not None

    def body(*refs):
        if need_dh:
            h_ref, g_ref, dy_ref, dres_ref, dh_ref, dg_ref = refs
        else:
            h_ref, g_ref, dy_ref, dg_ref = refs
        i = pl.program_id(0)
        x = h_ref[...]
        rstd = lax.rsqrt(jnp.mean(x * x, axis=-1, keepdims=True) + NORM_EPS)
        nrm = x * rstd
        dyv = dy_ref[...].astype(F32)
        part = jnp.sum(dyv * nrm, axis=0, keepdims=True)

        @pl.when(i == 0)
        def _():
            dg_ref[...] = part

        @pl.when(i > 0)
        def _():
            dg_ref[...] += part

        if need_dh:
            gy = dyv * g_ref[...]
            dx = rstd * (gy - nrm * jnp.mean(gy * nrm, axis=-1, keepdims=True))
            dh_ref[...] = dres_ref[...] + dx

    row = pl.BlockSpec((tr, d), lambda i: (i, 0))
    vec = pl.BlockSpec((1, d), lambda i: (0, 0))
    if need_dh:
        return pl.pallas_call(
            body,
            out_shape=(jax.ShapeDtypeStruct((r, d), F32), jax.ShapeDtypeStruct((1, d), F32)),
            grid=(r // tr,),
            in_specs=[row, vec, row, row],
            out_specs=(row, vec),
            compiler_params=_cp("arbitrary"),
            name=name,
        )(h, g, dy, dres)
    return None, pl.pallas_call(
        body,
        out_shape=jax.ShapeDtypeStruct((1, d), F32),
        grid=(r // tr,),
        in_specs=[row, vec, row],
        out_specs=vec,
        compiler_params=_cp("arbitrary"),
        name=name,
    )(h, g, dy)


def _loss_head(h, g, target, name):
    r, d = h.shape
    tr = _tile(r, 512, 8)

    def body(h_ref, g_ref, t_ref, loss_ref, dh_ref, dg_ref):
        i = pl.program_id(0)
        x = h_ref[...]
        gv = g_ref[...]
        rstd = lax.rsqrt(jnp.mean(x * x, axis=-1, keepdims=True) + NORM_EPS)
        nrm = x * rstd
        err = nrm * gv - t_ref[...]
        lpart = 0.5 * jnp.sum(jnp.mean(err * err, axis=-1, keepdims=True), axis=0, keepdims=True)
        dyv = err * (1.0 / d)
        gpart = jnp.sum(dyv * nrm, axis=0, keepdims=True)

        @pl.when(i == 0)
        def _():
            loss_ref[...] = jnp.broadcast_to(lpart, loss_ref.shape)
            dg_ref[...] = gpart

        @pl.when(i > 0)
        def _():
            loss_ref[...] += jnp.broadcast_to(lpart, loss_ref.shape)
            dg_ref[...] += gpart

        gy = dyv * gv
        dh_ref[...] = rstd * (gy - nrm * jnp.mean(gy * nrm, axis=-1, keepdims=True))

    row = pl.BlockSpec((tr, d), lambda i: (i, 0))
    vec = pl.BlockSpec((1, d), lambda i: (0, 0))
    lsp = pl.BlockSpec((1, 128), lambda i: (0, 0))
    return pl.pallas_call(
        body,
        out_shape=(jax.ShapeDtypeStruct((1, 128), F32), jax.ShapeDtypeStruct((r, d), F32), jax.ShapeDtypeStruct((1, d), F32)),
        grid=(r // tr,),
        in_specs=[row, vec, row],
        out_specs=(lsp, row, vec),
        compiler_params=_cp("arbitrary"),
        name=name,
    )(h, g, target)


def _lane(shape):
    return lax.broadcasted_iota(jnp.int32, shape, 1)


def _head_tile(pair_tile, odd):
    return pltpu.roll(pair_tile, 64, 1) if odd else pair_tile


def _pair_tile(even_tile, odd_tile):
    lane = _lane(even_tile.shape)
    return jnp.where(lane < 64, even_tile, pltpu.roll(odd_tile, 64, 1))


def _pad_heads(xs, scales, out_widths, name):
    r = xs[0][0].shape[0]
    tr = _tile(r, 256, 8)
    n_in = len(xs)

    def body(*refs):
        for idx in range(n_in):
            x_ref, o_ref = refs[idx], refs[n_in + idx]
            nh = xs[idx][2]
            for p in range(nh // 2):
                t = x_ref[:, p * 128:(p + 1) * 128].astype(F32) * scales[idx]
                lane = _lane(t.shape)
                o_ref[:, (2 * p) * HP:(2 * p + 1) * HP] = jnp.where(lane < 64, t, 0.0).astype(o_ref.dtype)
                o_ref[:, (2 * p + 1) * HP:(2 * p + 2) * HP] = jnp.where(lane < 64, pltpu.roll(t, 64, 1), 0.0).astype(o_ref.dtype)

    in_specs, args, out_specs, out_shape = [], [], [], []
    for (arr, c0, nh), w in zip(xs, out_widths):
        wcols = nh * 64
        assert c0 % wcols == 0 or c0 == 0
        blk = c0 // wcols if wcols else 0
        in_specs.append(pl.BlockSpec((tr, wcols), functools.partial(lambda i, b: (i, b), b=blk)))
        args.append(arr)
        out_specs.append(pl.BlockSpec((tr, w), lambda i: (i, 0)))
        out_shape.append(jax.ShapeDtypeStruct((r, w), BF16))
    return pl.pallas_call(
        body,
        out_shape=tuple(out_shape),
        grid=(r // tr,),
        in_specs=in_specs,
        out_specs=tuple(out_specs),
        compiler_params=_cp("parallel"),
        name=name,
    )(*args)


def _unpad_heads(xs, scales, out_dtype, name):
    r = xs[0].shape[0]
    tr = _tile(r, 256, 8)
    nhs = [x.shape[1] // HP for x in xs]
    total = sum(nhs) * 64

    def body(*refs):
        o_ref = refs[-1]
        col = 0
        for x_ref, nh, sc in zip(refs[:-1], nhs, scales):
            for p in range(nh // 2):
                ev = x_ref[:, (2 * p) * HP:(2 * p + 1) * HP].astype(F32)
                od = x_ref[:, (2 * p + 1) * HP:(2 * p + 2) * HP].astype(F32)
                o_ref[:, col:col + 128] = (_pair_tile(ev, od) * sc).astype(o_ref.dtype)
                col += 128

    return pl.pallas_call(
        body,
        out_shape=jax.ShapeDtypeStruct((r, total), out_dtype),
        grid=(r // tr,),
        in_specs=[pl.BlockSpec((tr, x.shape[1]), lambda i: (i, 0)) for x in xs],
        out_specs=pl.BlockSpec((tr, total), lambda i: (i, 0)),
        compiler_params=_cp("parallel"),
        name=name,
    )(*xs)


def _bf16_split3(c):
    hi = c.astype(BF16).astype(F32)
    r1 = c - hi
    mid = r1.astype(BF16).astype(F32)
    lo = (r1 - mid).astype(BF16).astype(F32)
    return hi, mid, lo


def _log_sigmoid(z):
    return jnp.minimum(z, 0.0) - jnp.log(1.0 + jnp.exp(-jnp.abs(z)))


def _prep_fox_fwd(proj, bfg, name):
    s = proj.shape[0]
    ts = _tile(s, 256, 8)
    fcol = 3 * D_MIX + D_MEMQ

    def body(p_ref, b_ref, q_ref, k_ref, v_ref, qm_ref, carry_ref):
        i = pl.program_id(0)

        @pl.when(i == 0)
        def _():
            carry_ref[...] = jnp.zeros_like(carry_ref)

        lane = _lane((ts, 128))
        z = p_ref[:, fcol:fcol + 128] + b_ref[...]
        logf = jnp.where(lane < N_MIX, _log_sigmoid(z), 0.0)
        rr = lax.broadcasted_iota(jnp.int32, (ts, ts), 0)
        cc = lax.broadcasted_iota(jnp.int32, (ts, ts), 1)
        tri = jnp.where(cc <= rr, 1.0, 0.0).astype(F32)
        c = jnp.dot(tri, logf, preferred_element_type=F32, precision=lax.Precision.HIGHEST) + carry_ref[0:1, :]
        carry_ref[...] = jnp.broadcast_to(c[ts - 1:ts, :], carry_ref.shape)
        for hh in range(N_MIX):
            p, odd = hh // 2, hh % 2
            ch = jnp.sum(jnp.where(lane == hh, c, 0.0), axis=-1, keepdims=True) * LOG2E
            hi, mid, lo = _bf16_split3(ch)
            qt = _head_tile(p_ref[:, p * 128:(p + 1) * 128], odd) * QSCALE
            kt = _head_tile(p_ref[:, D_MIX + p * 128:D_MIX + (p + 1) * 128], odd)
            vt = _head_tile(p_ref[:, 2 * D_MIX + p * 128:2 * D_MIX + (p + 1) * 128], odd)
            qa = jnp.where(lane == 64, hi, jnp.where(lane == 65, mid, jnp.where(lane == 66, lo, jnp.where(lane < 70, 1.0, 0.0))))
            ka = jnp.where(lane < 67, 1.0, jnp.where(lane == 67, -hi, jnp.where(lane == 68, -mid, jnp.where(lane == 69, -lo, 0.0))))
            q_ref[:, hh * HP:(hh + 1) * HP] = jnp.where(lane < 64, qt, qa).astype(BF16)
            k_ref[:, hh * HP:(hh + 1) * HP] = jnp.where(lane < 64, kt, ka).astype(BF16)
            v_ref[:, hh * HP:(hh + 1) * HP] = jnp.where(lane < 64, vt, 0.0).astype(BF16)
        for hh in range(N_MEM):
            p, odd = hh // 2, hh % 2
            t = _head_tile(p_ref[:, 3 * D_MIX + p * 128:3 * D_MIX + (p + 1) * 128], odd) * QSCALE
            qm_ref[:, hh * HP:(hh + 1) * HP] = jnp.where(lane < 64, t, 0.0).astype(BF16)

    wmix, wmem = N_MIX * HP, N_MEM * HP
    return pl.pallas_call(
        body,
        out_shape=(jax.ShapeDtypeStruct((s, wmix), BF16),) * 3 + (jax.ShapeDtypeStruct((s, wmem), BF16),),
        grid=(s // ts,),
        in_specs=[pl.BlockSpec((ts, proj.shape[1]), lambda i: (i, 0)), pl.BlockSpec((1, 128), lambda i: (0, 0))],
        out_specs=(pl.BlockSpec((ts, wmix), lambda i: (i, 0)),) * 3 + (pl.BlockSpec((ts, wmem), lambda i: (i, 0)),),
        scratch_shapes=[pltpu.VMEM((8, 128), F32)],
        compiler_params=_cp("arbitrary"),
        name=name,
    )(proj, bfg)


def _prep_fox_bwd(dq, dk, dv, dqm, proj, bfg, name):
    s = proj.shape[0]
    ts = _tile(s, 256, 8)
    nb = s // ts
    fcol = 3 * D_MIX + D_MEMQ

    def body(dq_ref, dk_ref, dv_ref, dqm_ref, p_ref, b_ref, o_ref, db_ref, carry_ref):
        i = pl.program_id(0)

        @pl.when(i == 0)
        def _():
            carry_ref[...] = jnp.zeros_like(carry_ref)
            db_ref[...] = jnp.zeros_like(db_ref)

        lane = _lane((ts, 128))
        dc = jnp.zeros((ts, 128), F32)
        for p in range(N_MIX // 2):
            tq, tk, tv = [], [], []
            for odd in (0, 1):
                hh = 2 * p + odd
                dqt = dq_ref[:, hh * HP:(hh + 1) * HP]
                dkt = dk_ref[:, hh * HP:(hh + 1) * HP]
                col = jnp.sum(jnp.where(lane == 64, dqt, 0.0) - jnp.where(lane == 67, dkt, 0.0), axis=-1, keepdims=True)
                dc = dc + jnp.where(lane == hh, col, 0.0)
                tq.append(dqt)
                tk.append(dkt)
                tv.append(dv_ref[:, hh * HP:(hh + 1) * HP])
            o_ref[:, p * 128:(p + 1) * 128] = (_pair_tile(tq[0], tq[1]) * SCALE).astype(o_ref.dtype)
            o_ref[:, D_MIX + p * 128:D_MIX + (p + 1) * 128] = (_pair_tile(tk[0], tk[1]) * LN2).astype(o_ref.dtype)
            o_ref[:, 2 * D_MIX + p * 128:2 * D_MIX + (p + 1) * 128] = _pair_tile(tv[0], tv[1]).astype(o_ref.dtype)
        for p in range(N_MEM // 2):
            ev = dqm_ref[:, (2 * p) * HP:(2 * p + 1) * HP]
            od = dqm_ref[:, (2 * p + 1) * HP:(2 * p + 2) * HP]
            o_ref[:, 3 * D_MIX + p * 128:3 * D_MIX + (p + 1) * 128] = (_pair_tile(ev, od) * SCALE).astype(o_ref.dtype)
        rr = lax.broadcasted_iota(jnp.int32, (ts, ts), 0)
        cc = lax.broadcasted_iota(jnp.int32, (ts, ts), 1)
        triu = jnp.where(cc >= rr, 1.0, 0.0).astype(F32)
        dlogf = jnp.dot(triu, dc, preferred_element_type=F32, precision=lax.Precision.HIGHEST) + carry_ref[0:1, :]
        carry_ref[...] = jnp.broadcast_to(dlogf[0:1, :], carry_ref.shape)
        z = p_ref[:, fcol:fcol + 128] + b_ref[...]
        dz = jnp.where(lane < N_MIX, dlogf / (1.0 + jnp.exp(z)), 0.0)
        o_ref[:, fcol:fcol + 128] = dz.astype(o_ref.dtype)
        db_ref[...] += jnp.sum(dz, axis=0, keepdims=True)

    wmix, wmem = N_MIX * HP, N_MEM * HP
    rev = lambda i: (nb - 1 - i, 0)
    return pl.pallas_call(
        body,
        out_shape=(jax.ShapeDtypeStruct(proj.shape, BF16), jax.ShapeDtypeStruct((1, 128), F32)),
        grid=(nb,),
        in_specs=[pl.BlockSpec((ts, wmix), rev)] * 3 + [pl.BlockSpec((ts, wmem), rev), pl.BlockSpec((ts, proj.shape[1]), rev),
                                                         pl.BlockSpec((1, 128), lambda i: (0, 0))],
        out_specs=(pl.BlockSpec((ts, proj.shape[1]), rev), pl.BlockSpec((1, 128), lambda i: (0, 0))),
        scratch_shapes=[pltpu.VMEM((8, 128), F32)],
        compiler_params=_cp("arbitrary"),
        name=name,
    )(dq, dk, dv, dqm, proj, bfg)


def _rope_partner(x):
    lane = _lane(x.shape)
    return jnp.where((lane % 64) < 32, pltpu.roll(x, 96, 1), pltpu.roll(x, 32, 1))


def _prep_dil_fwd(proj, cos2, sin2, name):
    s = proj.shape[0]
    ts = _tile(s, 256, 8)

    def body(p_ref, c_ref, s_ref, q_ref, k_ref, v_ref, qm_ref):
        lane = _lane((ts, 128))
        cosv, sinv = c_ref[...], s_ref[...]
        for p in range(N_MIX // 2):
            xq = p_ref[:, p * 128:(p + 1) * 128]
            xk = p_ref[:, D_MIX + p * 128:D_MIX + (p + 1) * 128]
            xv = p_ref[:, 2 * D_MIX + p * 128:2 * D_MIX + (p + 1) * 128]
            yq = (xq * cosv + _rope_partner(xq) * sinv) * QSCALE
            yk = xk * cosv + _rope_partner(xk) * sinv
            for odd in (0, 1):
                hh = 2 * p + odd
                q_ref[:, hh * HP:(hh + 1) * HP] = jnp.where(lane < 64, _head_tile(yq, odd), 0.0).astype(BF16)
                k_ref[:, hh * HP:(hh + 1) * HP] = jnp.where(lane < 64, _head_tile(yk, odd), 0.0).astype(BF16)
                v_ref[:, hh * HP:(hh + 1) * HP] = jnp.where(lane < 64, _head_tile(xv, odd), 0.0).astype(BF16)
        for p in range(N_MEM // 2):
            t = p_ref[:, 3 * D_MIX + p * 128:3 * D_MIX + (p + 1) * 128] * QSCALE
            for odd in (0, 1):
                hh = 2 * p + odd
                qm_ref[:, hh * HP:(hh + 1) * HP] = jnp.where(lane < 64, _head_tile(t, odd), 0.0).astype(BF16)

    wmix, wmem = N_MIX * HP, N_MEM * HP
    return pl.pallas_call(
        body,
        out_shape=(jax.ShapeDtypeStruct((s, wmix), BF16),) * 3 + (jax.ShapeDtypeStruct((s, wmem), BF16),),
        grid=(s // ts,),
        in_specs=[pl.BlockSpec((ts, proj.shape[1]), lambda i: (i, 0)), pl.BlockSpec((ts, 128), lambda i: (i, 0)),
                  pl.BlockSpec((ts, 128), lambda i: (i, 0))],
        out_specs=(pl.BlockSpec((ts, wmix), lambda i: (i, 0)),) * 3 + (pl.BlockSpec((ts, wmem), lambda i: (i, 0)),),
        compiler_params=_cp("parallel"),
        name=name,
    )(proj, cos2, sin2)


def _prep_dil_bwd(dq, dk, dv, dqm, cos2, sin2, name):
    s = dq.shape[0]
    ts = _tile(s, 256, 8)

    def body(dq_ref, dk_ref, dv_ref, dqm_ref, c_ref, s_ref, o_ref):
        cosv, sinv = c_ref[...], s_ref[...]
        for p in range(N_MIX // 2):
            e, o = 2 * p, 2 * p + 1
            dyq = _pair_tile(dq_ref[:, e * HP:(e + 1) * HP], dq_ref[:, o * HP:(o + 1) * HP]) * SCALE
            dyk = _pair_tile(dk_ref[:, e * HP:(e + 1) * HP], dk_ref[:, o * HP:(o + 1) * HP]) * LN2
            dxv = _pair_tile(dv_ref[:, e * HP:(e + 1) * HP], dv_ref[:, o * HP:(o + 1) * HP])
            o_ref[:, p * 128:(p + 1) * 128] = (dyq * cosv - _rope_partner(dyq) * sinv).astype(o_ref.dtype)
            o_ref[:, D_MIX + p * 128:D_MIX + (p + 1) * 128] = (dyk * cosv - _rope_partner(dyk) * sinv).astype(o_ref.dtype)
            o_ref[:, 2 * D_MIX + p * 128:2 * D_MIX + (p + 1) * 128] = dxv.astype(o_ref.dtype)
        for p in range(N_MEM // 2):
            e, o = 2 * p, 2 * p + 1
            t = _pair_tile(dqm_ref[:, e * HP:(e + 1) * HP], dqm_ref[:, o * HP:(o + 1) * HP]) * SCALE
            o_ref[:, 3 * D_MIX + p * 128:3 * D_MIX + (p + 1) * 128] = t.astype(o_ref.dtype)

    wmix, wmem = N_MIX * HP, N_MEM * HP
    row = lambda w: pl.BlockSpec((ts, w), lambda i: (i, 0))
    return pl.pallas_call(
        body,
        out_shape=jax.ShapeDtypeStruct((s, DIL_IN), BF16),
        grid=(s // ts,),
        in_specs=[row(wmix)] * 3 + [row(wmem), row(128), row(128)],
        out_specs=row(DIL_IN),
        compiler_params=_cp("parallel"),
        name=name,
    )(dq, dk, dv, dqm, cos2, sin2)


def _mask_bias(mode, blk):
    n = 1 if mode == "causal" else DIL_MAX // blk + 1
    idx = jnp.arange(blk, dtype=jnp.int32)
    dist = jnp.arange(n, dtype=jnp.int32)[:, None, None] * blk + idx[None, :, None] - idx[None, None, :]
    if mode == "causal":
        return jnp.where(dist >= 0, 0.0, NEG).astype(F32)
    cnt = ((dist <= 128).astype(jnp.int32) + ((dist <= 512) & (dist % 4 == 0)).astype(jnp.int32)
           + ((dist <= DIL_MAX) & (dist % 16 == 0)).astype(jnp.int32))
    bias = jnp.where(cnt == 3, math.log2(3.0), jnp.where(cnt == 2, 1.0, 0.0))
    return jnp.where((dist >= 0) & (cnt > 0), bias, NEG).astype(F32)


def _flash_dims(q, k, mode):
    sq, w = q.shape
    sk = k.shape[0]
    tq = _tile(sq, FLASH_BLK, 8)
    tk = sk if mode == "full" else tq
    band = DIL_MAX // tk if mode == "dilated" else None
    return sq, sk, w, tq, tk, sq // tq, sk // tk, band


def _carried(comm, grid):
    if comm is None:
        return 0, 0, [], (lambda *a: None), (lambda *a: None)

    def edge(last):
        cond = None
        for d, n in enumerate(grid):
            c = pl.program_id(d) == (n - 1 if last else 0)
            cond = c if cond is None else cond & c
        return cond

    def start(cins, couts, sems):
        pl.when(edge(False))(lambda: comm.start(cins, couts, *sems))

    def finish(cins, couts, sems):
        pl.when(edge(True))(lambda: comm.finish(cins, couts, *sems))

    sems = [pltpu.SemaphoreType.DMA((comm.n_sems,)), pltpu.SemaphoreType.DMA((comm.n_sems,))]
    return len(comm.inputs), len(comm.out_shapes), sems, start, finish


def _split_refs(refs, n_in, n_cin, n_out, n_cout, n_scratch):
    bounds = np.cumsum([0, n_in, n_cin, n_out, n_cout, n_scratch])
    return [refs[a:b] for a, b in zip(bounds[:-1], bounds[1:])] + [refs[bounds[-1]:]]


def _flash_fwd(q, k, v, mode, name, comm=None):
    sq, sk, w, tq, tk, nq, nk, band = _flash_dims(q, k, mode)
    hb = HEADS_PER_STEP
    wb = hb * HP
    nch = tk // 128
    has_bias = mode != "full"
    grid = (w // wb, nq, nk)
    n_cin, n_cout, comm_sems, comm_start, comm_finish = _carried(comm, grid)

    def kidx(hp, qi, kj):
        if mode == "full":
            return (kj, hp)
        lo = jnp.maximum(qi - band, 0) if mode == "dilated" else 0
        return (jnp.clip(kj, lo, qi), hp)

    def body(*refs):
        ins, cins, (o_ref, lse_ref), couts, (m_ref, l_ref, acc_ref), sems = _split_refs(refs, 3 + has_bias, n_cin, 2, n_cout, 3)
        q_ref, k_ref, v_ref = ins[:3]
        b_ref = ins[3] if has_bias else None
        qi, kj = pl.program_id(1), pl.program_id(2)
        comm_start(cins, couts, sems)

        @pl.when(kj == 0)
        def _():
            m_ref[...] = jnp.full_like(m_ref, -jnp.inf)
            l_ref[...] = jnp.zeros_like(l_ref)
            acc_ref[...] = jnp.zeros_like(acc_ref)

        def step(bias_tile):
            for h in range(hb):
                cols = slice(h * HP, (h + 1) * HP)
                sc = lax.dot_general(q_ref[:, cols], k_ref[:, cols], _DIMS["nt"], preferred_element_type=F32)
                if bias_tile is not None:
                    sc = sc + bias_tile()
                m_prev = m_ref[h]
                m_new = jnp.maximum(m_prev, jnp.max(sc, axis=-1, keepdims=True))
                alpha = jnp.exp2(m_prev - m_new)
                psum, chunks = None, []
                for c in range(nch):
                    pc = jnp.exp2(sc[:, c * 128:(c + 1) * 128] - m_new)
                    psum = pc if psum is None else psum + pc
                    chunks.append(pc.astype(BF16))
                p = chunks[0] if nch == 1 else jnp.concatenate(chunks, axis=1)
                l_ref[h] = alpha * l_ref[h] + psum
                acc_ref[h] = alpha * acc_ref[h] + jnp.dot(p, v_ref[:, cols], preferred_element_type=F32)
                m_ref[h] = m_new

        if mode == "full":
            step(None)
        elif mode == "causal":
            pl.when(kj == qi)(lambda: step(lambda: b_ref[0]))
            pl.when(kj < qi)(lambda: step(None))
        else:
            pl.when((kj <= qi) & (kj >= qi - band))(lambda: step(lambda: b_ref[qi - kj]))

        @pl.when(kj == nk - 1)
        def _():
            for h in range(hb):
                cols = slice(h * HP, (h + 1) * HP)
                l = jnp.sum(l_ref[h], axis=-1, keepdims=True)
                o_ref[:, cols] = (acc_ref[h] / l).astype(o_ref.dtype)
                lse_ref[:, cols] = m_ref[h] + jnp.log2(l)

        comm_finish(cins, couts, sems)

    qspec = pl.BlockSpec((tq, wb), lambda hp, qi, kj: (qi, hp))
    in_specs = [qspec, pl.BlockSpec((tk, wb), kidx), pl.BlockSpec((tk, wb), kidx)]
    args = [q, k, v]
    if has_bias:
        bias = _mask_bias(mode, tq)
        in_specs.append(pl.BlockSpec(bias.shape, lambda hp, qi, kj: (0, 0, 0)))
        args.append(bias)
    out_shape = [jax.ShapeDtypeStruct((sq, w), BF16), jax.ShapeDtypeStruct((sq, w), F32)]
    out_specs = [qspec, qspec]
    if comm is not None:
        in_specs += [_ANY] * n_cin
        args += list(comm.inputs)
        out_shape += list(comm.out_shapes)
        out_specs += [_ANY] * n_cout
    res = pl.pallas_call(
        body,
        out_shape=tuple(out_shape),
        grid=grid,
        in_specs=in_specs,
        out_specs=tuple(out_specs),
        scratch_shapes=[pltpu.VMEM((hb, tq, HP), F32), pltpu.VMEM((hb, tq, HP), F32), pltpu.VMEM((hb, tq, HP), F32)] + comm_sems,
        compiler_params=_cp("parallel", "parallel", "arbitrary") if comm is None else _cp("arbitrary", "arbitrary", "arbitrary"),
        name=name,
    )(*args)
    return res[0], res[1], list(res[2:])


def _flash_bwd(q, k, v, o, do, lse, mode, name, comm=None):
    sq, sk, w, tq, tk, nq, nk, band = _flash_dims(q, k, mode)
    hb = HEADS_PER_STEP
    wb = hb * HP
    nch = tk // 128
    has_bias = mode != "full"
    grid = (w // wb, nk, nq)
    n_cin, n_cout, comm_sems, comm_start, comm_finish = _carried(comm, grid)

    def qidx(hp, kj, qi):
        if mode == "full":
            return (qi, hp)
        hi = jnp.minimum(kj + band, nq - 1) if mode == "dilated" else nq - 1
        return (jnp.clip(qi, kj, hi), hp)

    def body(*refs):
        ins, cins, (dq_ref, dk_ref, dv_ref), couts, (delta_ref,), sems = _split_refs(refs, 6 + has_bias, n_cin, 3, n_cout, 1)
        q_ref, k_ref, v_ref, o_ref, do_ref, lse_ref = ins[:6]
        b_ref = ins[6] if has_bias else None
        kj, qi = pl.program_id(1), pl.program_id(2)
        rows = pl.ds(pl.multiple_of(qi * tq, tq), tq)
        comm_start(cins, couts, sems)

        @pl.when((kj == 0) & (qi == 0))
        def _():
            dq_ref[...] = jnp.zeros_like(dq_ref)

        @pl.when(qi == 0)
        def _():
            dk_ref[...] = jnp.zeros_like(dk_ref)
            dv_ref[...] = jnp.zeros_like(dv_ref)

        first = kj == (jnp.maximum(qi - band, 0) if mode == "dilated" else 0)

        @pl.when(first)
        def _():
            for h in range(hb):
                cols = slice(h * HP, (h + 1) * HP)
                dl = jnp.sum(do_ref[:, cols].astype(F32) * o_ref[:, cols].astype(F32), axis=-1, keepdims=True)
                delta_ref[h, rows, :] = jnp.broadcast_to(dl, (tq, HP))

        def step(bias_tile):
            for h in range(hb):
                cols = slice(h * HP, (h + 1) * HP)
                qv, kv, dov = q_ref[:, cols], k_ref[:, cols], do_ref[:, cols]
                sc = lax.dot_general(qv, kv, _DIMS["nt"], preferred_element_type=F32)
                if bias_tile is not None:
                    sc = sc + bias_tile()
                dp = lax.dot_general(dov, v_ref[:, cols], _DIMS["nt"], preferred_element_type=F32)
                lse_b = lse_ref[:, cols]
                dlt = delta_ref[h, rows, :]
                pch, dsch = [], []
                for c in range(nch):
                    lanes = slice(c * 128, (c + 1) * 128)
                    pc = jnp.exp2(sc[:, lanes] - lse_b)
                    pch.append(pc.astype(BF16))
                    dsch.append((pc * (dp[:, lanes] - dlt)).astype(BF16))
                p = pch[0] if nch == 1 else jnp.concatenate(pch, axis=1)
                ds = dsch[0] if nch == 1 else jnp.concatenate(dsch, axis=1)
                dv_ref[:, cols] += lax.dot_general(p, dov, _DIMS["tn"], preferred_element_type=F32)
                dk_ref[:, cols] += lax.dot_general(ds, qv, _DIMS["tn"], preferred_element_type=F32)
                dq_ref[rows, cols] += jnp.dot(ds, kv, preferred_element_type=F32)

        if mode == "full":
            step(None)
        elif mode == "causal":
            pl.when(qi == kj)(lambda: step(lambda: b_ref[0]))
            pl.when(qi > kj)(lambda: step(None))
        else:
            pl.when((qi >= kj) & (qi <= kj + band))(lambda: step(lambda: b_ref[qi - kj]))

        comm_finish(cins, couts, sems)

    qspec = pl.BlockSpec((tq, wb), qidx)
    kspec = pl.BlockSpec((tk, wb), lambda hp, kj, qi: (kj, hp))
    in_specs = [qspec, kspec, kspec, qspec, qspec, qspec]
    args = [q, k, v, o, do, lse]
    if has_bias:
        bias = _mask_bias(mode, tq)
        in_specs.append(pl.BlockSpec(bias.shape, lambda hp, kj, qi: (0, 0, 0)))
        args.append(bias)
    out_shape = [jax.ShapeDtypeStruct((sq, w), F32), jax.ShapeDtypeStruct((sk, w), F32), jax.ShapeDtypeStruct((sk, w), F32)]
    out_specs = [pl.BlockSpec((sq, wb), lambda hp, kj, qi: (0, hp)), kspec, kspec]
    if comm is not None:
        in_specs += [_ANY] * n_cin
        args += list(comm.inputs)
        out_shape += list(comm.out_shapes)
        out_specs += [_ANY] * n_cout
    res = pl.pallas_call(
        body,
        out_shape=tuple(out_shape),
        grid=grid,
        in_specs=in_specs,
        out_specs=tuple(out_specs),
        scratch_shapes=[pltpu.VMEM((hb, sq, HP), F32)] + comm_sems,
        compiler_params=_cp("parallel", "arbitrary", "arbitrary") if comm is None else _cp("arbitrary", "arbitrary", "arbitrary"),
        name=name,
    )(*args)
    return res[0], res[1], res[2], list(res[3:])


def _conv_rc(s):
    return _tile(s, 256, 8)


def _shift_down(x, prev8, nrows):
    rows = lax.broadcasted_iota(jnp.int32, x.shape, 0)
    out = pltpu.roll(x, nrows, 0)
    for i in range(nrows):
        out = jnp.where(rows == i, prev8[8 - nrows + i:8 - nrows + i + 1, :], out)
    return out


def _shift_up(x, next8, nrows):
    n = x.shape[0]
    rows = lax.broadcasted_iota(jnp.int32, x.shape, 0)
    out = pltpu.roll(x, n - nrows, 0)
    for i in range(nrows):
        out = jnp.where(rows == n - nrows + i, next8[i:i + 1, :], out)
    return out


def _conv_taps(uv_ref, ug_ref, r, rc):
    r0 = pl.multiple_of(r * rc, rc)
    x = jnp.concatenate([uv_ref[pl.ds(r0, rc), :], ug_ref[pl.ds(r0, rc), :]], axis=1)
    p0 = pl.multiple_of(jnp.maximum(r0 - 8, 0), 8)
    prev8 = jnp.where(r > 0, jnp.concatenate([uv_ref[pl.ds(p0, 8), :], ug_ref[pl.ds(p0, 8), :]], axis=1), 0.0)
    return r0, x, _shift_down(x, prev8, 1), _shift_down(x, prev8, 2)


def _conv_specs(s, tf, nf):
    strip = pl.BlockSpec((s, tf), lambda j: (0, j))
    return strip, [pl.BlockSpec((3, tf), lambda j: (0, j)), pl.BlockSpec((3, tf), lambda j: (0, j + nf)),
                   pl.BlockSpec((1, tf), lambda j: (0, j)), pl.BlockSpec((1, tf), lambda j: (0, j + nf))]


def _conv_params(wv_ref, wg_ref, bv_ref, bg_ref):
    w = jnp.concatenate([wv_ref[...], wg_ref[...]], axis=1)
    return w[0:1, :], w[1:2, :], w[2:3, :], jnp.concatenate([bv_ref[...], bg_ref[...]], axis=1)


def _conv_fwd(u_val, u_gate, cw, cb, name):
    s, f = u_val.shape
    tf = CONV_TF
    nf = f // tf
    rc = _conv_rc(s)

    def body(uv_ref, ug_ref, wv_ref, wg_ref, bv_ref, bg_ref, a_ref):
        w0, w1, w2, b = _conv_params(wv_ref, wg_ref, bv_ref, bg_ref)

        def chunk(r, carry):
            r0, x, x1, x2 = _conv_taps(uv_ref, ug_ref, r, rc)
            c = b + w0 * x2 + w1 * x1 + w2 * x
            val, gate = c[:, :tf], c[:, tf:]
            a_ref[pl.ds(r0, rc), :] = (gate * jax.nn.sigmoid(gate) * val).astype(a_ref.dtype)
            return carry

        lax.fori_loop(0, s // rc, chunk, 0)

    strip, params = _conv_specs(s, tf, nf)
    return pl.pallas_call(
        body,
        out_shape=jax.ShapeDtypeStruct((s, f), BF16),
        grid=(nf,),
        in_specs=[strip, strip] + params,
        out_specs=strip,
        compiler_params=_cp("parallel"),
        name=name,
    )(u_val, u_gate, cw, cw, cb, cb)


def _conv_bwd(u_val, u_gate, da, cw, cb, name):
    s, f = u_val.shape
    tf = CONV_TF
    nf = f // tf
    rc = _conv_rc(s)
    nchunk = s // rc

    def body(uv_ref, ug_ref, da_ref, wv_ref, wg_ref, bv_ref, bg_ref, duv_ref, dug_ref, dwv_ref, dwg_ref, dbv_ref, dbg_ref, next_ref):
        w0, w1, w2, b = _conv_params(wv_ref, wg_ref, bv_ref, bg_ref)
        next_ref[...] = jnp.zeros_like(next_ref)

        def chunk(it, carry):
            g0, g1, g2, gb = carry
            r = nchunk - 1 - it
            r0, x, x1, x2 = _conv_taps(uv_ref, ug_ref, r, rc)
            c = b + w0 * x2 + w1 * x1 + w2 * x
            val, gate = c[:, :tf], c[:, tf:]
            sg = jax.nn.sigmoid(gate)
            dav = da_ref[pl.ds(r0, rc), :]
            dc = jnp.concatenate([dav * (gate * sg), dav * val * (sg * (1.0 + gate * (1.0 - sg)))], axis=1)
            nxt = next_ref[...]
            du = (w2 * dc + w1 * _shift_up(dc, nxt, 1) + w0 * _shift_up(dc, nxt, 2)).astype(duv_ref.dtype)
            duv_ref[pl.ds(r0, rc), :] = du[:, :tf]
            dug_ref[pl.ds(r0, rc), :] = du[:, tf:]
            next_ref[...] = dc[0:8, :]
            return (g0 + jnp.sum(dc * x2, axis=0, keepdims=True), g1 + jnp.sum(dc * x1, axis=0, keepdims=True),
                    g2 + jnp.sum(dc * x, axis=0, keepdims=True), gb + jnp.sum(dc, axis=0, keepdims=True))

        zero = jnp.zeros((1, 2 * tf), F32)
        g0, g1, g2, gb = lax.fori_loop(0, nchunk, chunk, (zero, zero, zero, zero))
        for i, gi in enumerate((g0, g1, g2)):
            dwv_ref[i:i + 1, :] = gi[:, :tf]
            dwg_ref[i:i + 1, :] = gi[:, tf:]
        dbv_ref[...] = gb[:, :tf]
        dbg_ref[...] = gb[:, tf:]

    strip, params = _conv_specs(s, tf, nf)
    taps = pl.BlockSpec((3, tf), lambda j: (0, j))
    bias = pl.BlockSpec((1, tf), lambda j: (0, j))
    act = jax.ShapeDtypeStruct((s, f), BF16)
    return pl.pallas_call(
        body,
        out_shape=(act, act, jax.ShapeDtypeStruct((3, f), F32), jax.ShapeDtypeStruct((3, f), F32),
                   jax.ShapeDtypeStruct((1, f), F32), jax.ShapeDtypeStruct((1, f), F32)),
        grid=(nf,),
        in_specs=[strip, strip, strip] + params,
        out_specs=(strip, strip, taps, taps, bias, bias),
        scratch_shapes=[pltpu.VMEM((8, 2 * tf), F32)],
        compiler_params=_cp("parallel"),
        name=name,
    )(u_val, u_gate, da, cw, cw, cb, cb)


def _adamw(w, g, m, v, name):
    r, c = w.shape
    tr = _tile(r, 256, 8) if r % 8 == 0 else r
    c1 = 1.0 - ADAM_B1 ** ADAM_STEP
    c2 = 1.0 - ADAM_B2 ** ADAM_STEP

    def body(w_ref, g_ref, m_ref, v_ref, d_ref, mo_ref, vo_ref):
        gv = g_ref[...]
        mn = ADAM_B1 * m_ref[...] + (1.0 - ADAM_B1) * gv
        vn = ADAM_B2 * v_ref[...] + (1.0 - ADAM_B2) * (gv * gv)
        d_ref[...] = -ADAM_LR * ((mn / c1) / (jnp.sqrt(vn / c2) + ADAM_EPS) + ADAM_WD * w_ref[...])
        mo_ref[...] = mn
        vo_ref[...] = vn

    blk = pl.BlockSpec((tr, c), lambda i: (i, 0))
    shp = jax.ShapeDtypeStruct((r, c), F32)
    return pl.pallas_call(
        body, out_shape=(shp, shp, shp), grid=(r // tr,), in_specs=[blk] * 4, out_specs=(blk,) * 3,
        compiler_params=_cp("parallel"), name=name,
    )(w, g, m, v)


def _sum_rows(parts, out_dtype, name):
    n, r, c = parts.shape
    tr = _tile(r, 256, 8)

    def body(p_ref, o_ref):
        tot = p_ref[0].astype(F32)
        for i in range(1, n):
            tot = tot + p_ref[i].astype(F32)
        o_ref[...] = tot.astype(o_ref.dtype)

    return pl.pallas_call(
        body, out_shape=jax.ShapeDtypeStruct((r, c), out_dtype), grid=(r // tr,),
        in_specs=[pl.BlockSpec((n, tr, c), lambda i: (0, i, 0))], out_specs=pl.BlockSpec((tr, c), lambda i: (i, 0)),
        compiler_params=_cp("parallel"), name=name,
    )(parts)


_ANY = pl.BlockSpec(memory_space=pl.ANY)


def _place():
    return lax.axis_index("x"), lax.axis_index("y"), lax.axis_index("c")


def _other_chips(x, y):
    return [(1 - x, y), (x, 1 - y), (1 - x, 1 - y)]


def _rows_half(ref, c, axis):
    rh = ref.shape[axis] // 2
    idx = [slice(None)] * len(ref.shape)
    idx[axis] = pl.ds(pl.multiple_of(c * rh, 16), rh)
    return ref.at[tuple(idx)]


def _remote(src, dst, send_sems, recv_sems, kk, to):
    return pltpu.make_async_remote_copy(src_ref=src, dst_ref=dst, send_sem=send_sems.at[kk], recv_sem=recv_sems.at[kk],
                                        device_id=to, device_id_type=MESH)


class _GatherSpec:
    def __init__(self, shards):
        self.inputs = list(shards)
        self.out_shapes = [jax.ShapeDtypeStruct((N_CHIPS,) + s.shape, s.dtype) for s in shards]
        self.n_sems = 6 * len(shards)

    def _sends(self, ins, outs, send_sems, recv_sems):
        x, y, c = _place()
        me = 2 * x + y
        return [_remote(_rows_half(ins[a], c, 0), _rows_half(outs[a].at[me], c, 0), send_sems, recv_sems, 6 * a + j, (cx, cy, c))
                for a in range(len(ins)) for j, (cx, cy) in enumerate(_other_chips(x, y))]

    def _forwards(self, outs, send_sems, recv_sems):
        x, y, c = _place()
        return [_remote(_rows_half(outs[a].at[2 * cx + cy], c, 0), _rows_half(outs[a].at[2 * cx + cy], c, 0), send_sems, recv_sems,
                        6 * a + 3 + j, (x, y, 1 - c))
                for a in range(len(outs)) for j, (cx, cy) in enumerate(_other_chips(x, y))]

    def start(self, ins, outs, send_sems, recv_sems):
        for cp in self._sends(ins, outs, send_sems, recv_sems):
            cp.start()

    def finish(self, ins, outs, send_sems, recv_sems):
        x, y, c = _place()
        chips = _other_chips(x, y)
        forwards = self._forwards(outs, send_sems, recv_sems)
        for a in range(len(outs)):
            for j, (cx, cy) in enumerate(chips):
                slot = _rows_half(outs[a].at[2 * cx + cy], c, 0)
                _remote(slot, slot, send_sems, recv_sems, 6 * a + j, (x, y, c)).wait_recv()
                forwards[3 * a + j].start()
        for a in range(len(outs)):
            for j, (cx, cy) in enumerate(chips):
                slot = _rows_half(outs[a].at[2 * cx + cy], 1 - c, 0)
                _remote(slot, slot, send_sems, recv_sems, 6 * a + 3 + j, (x, y, c)).wait_recv()
        for cp in self._sends(ins, outs, send_sems, recv_sems) + forwards:
            cp.wait_send()


_FLIPS = [(dx, dy, dc) for dx in (0, 1) for dy in (0, 1) for dc in (0, 1) if (dx, dy, dc) != (0, 0, 0)]


class _ScatterSpec:
    def __init__(self, parts):
        self.inputs = list(parts)
        self.out_shapes = [jax.ShapeDtypeStruct((8, p.shape[1] // 2, p.shape[2]), p.dtype) for p in parts]
        self.n_sems = 7 * len(parts)

    def _sends(self, ins, outs, send_sems, recv_sems):
        x, y, c = _place()
        me = 4 * x + 2 * y + c
        copies = []
        for a in range(len(ins)):
            for k, (dx, dy, dc) in enumerate(_FLIPS):
                tx, ty, tc = x ^ dx, y ^ dy, c ^ dc
                copies.append(_remote(_rows_half(ins[a].at[2 * tx + ty], tc, 0), outs[a].at[me], send_sems, recv_sems,
                                      7 * a + k, (tx, ty, tc)))
        return copies

    def start(self, ins, outs, send_sems, recv_sems):
        for cp in self._sends(ins, outs, send_sems, recv_sems):
            cp.start()

    def finish(self, ins, outs, send_sems, recv_sems):
        x, y, c = _place()
        for a in range(len(outs)):
            for k, (dx, dy, dc) in enumerate(_FLIPS):
                slot = outs[a].at[4 * (x ^ dx) + 2 * (y ^ dy) + (c ^ dc)]
                _remote(slot, slot, send_sems, recv_sems, 7 * a + k, (x, y, c)).wait_recv()
        for cp in self._sends(ins, outs, send_sems, recv_sems):
            cp.wait_send()


def _run_comm(comm, name):
    n_in, n_out = len(comm.inputs), len(comm.out_shapes)

    def body(*refs):
        ins, outs, sems = refs[:n_in], refs[n_in:n_in + n_out], refs[n_in + n_out:]
        comm.start(ins, outs, *sems)
        comm.finish(ins, outs, *sems)

    return list(pl.pallas_call(
        body, out_shape=tuple(comm.out_shapes), in_specs=[_ANY] * n_in, out_specs=(_ANY,) * n_out,
        scratch_shapes=[pltpu.SemaphoreType.DMA((comm.n_sems,)), pltpu.SemaphoreType.DMA((comm.n_sems,))], name=name,
    )(*comm.inputs))


def _join_halves(rs, name):
    n = len(rs)

    def body(*refs):
        in_refs, out_refs = refs[:n], refs[n:2 * n]
        send_sems, recv_sems = refs[2 * n:]
        x, y, c = _place()
        copies = [_remote(in_refs[a], out_refs[a], send_sems, recv_sems, a, (x, y, 1 - c)) for a in range(n)]
        for cp in copies:
            cp.start()
        for cp in copies:
            cp.wait()

    return pl.pallas_call(
        body,
        out_shape=tuple(jax.ShapeDtypeStruct(r.shape, r.dtype) for r in rs),
        in_specs=[_ANY] * n, out_specs=(_ANY,) * n,
        scratch_shapes=[pltpu.SemaphoreType.DMA((n,)), pltpu.SemaphoreType.DMA((n,))], name=name,
    )(*rs)


def _gather_all(small, name):
    r, w = small.shape
    flips = [(dx, dy, dc) for dx in (0, 1) for dy in (0, 1) for dc in (0, 1) if (dx, dy, dc) != (0, 0, 0)]

    def body(in_ref, out_ref, send_sems, recv_sems, local_sem):
        x, y, c = _place()
        me = 4 * x + 2 * y + c
        mine = pltpu.make_async_copy(in_ref, out_ref.at[me], local_sem)
        mine.start()
        sends = []
        for j, (dx, dy, dc) in enumerate(flips):
            to = (x ^ dx, y ^ dy, c ^ dc)
            cp = pltpu.make_async_remote_copy(src_ref=in_ref, dst_ref=out_ref.at[me], send_sem=send_sems.at[j],
                                              recv_sem=recv_sems.at[j], device_id=to, device_id_type=MESH)
            cp.start()
            sends.append(cp)
        for j, (dx, dy, dc) in enumerate(flips):
            slot = out_ref.at[4 * (x ^ dx) + 2 * (y ^ dy) + (c ^ dc)]
            pltpu.make_async_remote_copy(src_ref=slot, dst_ref=slot, send_sem=send_sems.at[j], recv_sem=recv_sems.at[j],
                                         device_id=(x, y, c), device_id_type=MESH).wait_recv()
        for cp in sends:
            cp.wait_send()
        mine.wait()

    return pl.pallas_call(
        body, out_shape=jax.ShapeDtypeStruct((8, r, w), small.dtype), in_specs=[_ANY], out_specs=_ANY,
        scratch_shapes=[pltpu.SemaphoreType.DMA((7,)), pltpu.SemaphoreType.DMA((7,)), pltpu.SemaphoreType.DMA], name=name,
    )(small)


_BIG = ("w_in_fox", "w_in_dil", "w_mem_kv", "w_out", "w_up", "w_down")
_CONVW_SHARD = (DEPTH, 3, 2 * D_FF // N_CHIPS)

_SMALL = (("norm_mix", (DEPTH, D_MODEL)), ("norm_mem", (DEPTH, D_MODEL)), ("norm_ffn", (DEPTH, D_MODEL)),
          ("conv_b", (DEPTH, 2 * D_FF)), ("norm_final", (D_MODEL,)), ("b_forget", (2, N_MIX)), ("conv_w", (DEPTH, 3, 2 * D_FF)))


def _pack_small(vals, spec):
    flat = jnp.concatenate([vals[n].reshape(-1).astype(F32) for n, _ in spec])
    rows = -(-flat.shape[0] // (8 * 128)) * 8
    return jnp.pad(flat, (0, rows * 128 - flat.shape[0])).reshape(rows, 128)


def _unpack_small(buf, spec):
    flat, out, off = buf.reshape(-1), {}, 0
    for n, shp in spec:
        k = int(np.prod(shp))
        out[n] = flat[off:off + k].reshape(shp)
        off += k
    return out


def _fox_cols_to_kernel(w):
    qkv, f, qm = w[:, :3 * D_MIX], w[:, 3 * D_MIX:3 * D_MIX + N_MIX], w[:, 3 * D_MIX + N_MIX:]
    return jnp.concatenate([qkv, qm, f, jnp.zeros((w.shape[0], 128 - N_MIX), w.dtype)], axis=1)


def _fox_cols_from_kernel(w):
    qkv, qm, f = w[:, :3 * D_MIX], w[:, 3 * D_MIX:3 * D_MIX + D_MEMQ], w[:, 3 * D_MIX + D_MEMQ:3 * D_MIX + D_MEMQ + N_MIX]
    return jnp.concatenate([qkv, f, qm], axis=1)


def _rope_pair_tables(s):
    inv = 1.0 / (ROPE_THETA ** (jnp.arange(0, HEAD_DIM, 2, dtype=F32) / HEAD_DIM))
    ang = jnp.arange(s, dtype=F32)[:, None] * inv[None, :]
    cos, sin = jnp.cos(ang), jnp.sin(ang)
    return jnp.concatenate([cos, cos, cos, cos], axis=1), jnp.concatenate([-sin, sin, -sin, sin], axis=1)


def _local_step(x, mem, target, small, ex):
    s = x.shape[0]
    cos2, sin2 = _rope_pair_tables(s)
    saved = []
    h = x
    keys = [("w_in", 0), ("w_mem_kv", 0)]
    spec = ex.gather_spec(keys)
    if spec is not None:
        ex.put_gathered(keys, _run_comm(spec, "gather_first"))
    for l in range(DEPTH):
        fox = l % 2 == 0
        slot = l // 2
        tag = f"L{l}"
        g_mix, g_mem, g_ffn = (small[n][l:l + 1] for n in ("norm_mix", "norm_mem", "norm_ffn"))
        xn = _rms_fwd(h, g_mix, f"rms_mix_{tag}")
        proj = _matmul(xn, ex.weight(("w_in", l)), "nn", F32, f"mm_in_{tag}")
        if fox:
            bfg = jnp.pad(small["b_forget"][slot:slot + 1], ((0, 0), (0, 128 - N_MIX)))
            qp, kp, vp, qmp = _prep_fox_fwd(proj, bfg, f"prep_fox_{tag}")
        else:
            bfg = None
            qp, kp, vp, qmp = _prep_dil_fwd(proj, cos2, sin2, f"prep_dil_{tag}")
        mn = _rms_fwd(mem, g_mem, f"rms_mem_{tag}")
        kvm = _matmul(mn, ex.weight(("w_mem_kv", l)), "nn", F32, f"mm_memkv_{tag}")
        kmp, vmp = _pad_heads([(kvm, 0, N_MEM), (kvm, D_MEMQ, N_MEM)], (1.0, 1.0), (N_MEM * HP,) * 2, f"pad_memkv_{tag}")
        mode = "causal" if fox else "dilated"
        keys = [("w_out", l), ("w_up", l), ("w_down", l)] + ([("w_in", l + 1), ("w_mem_kv", l + 1)] if l + 1 < DEPTH else [])
        spec = ex.gather_spec(keys)
        o_mix, lse_mix, got = _flash_fwd(qp, kp, vp, mode, f"flash_{mode}_fwd_{tag}", comm=spec)
        if spec is not None:
            ex.put_gathered(keys, got)
        o_mem, lse_mem, _ = _flash_fwd(qmp, kmp, vmp, "full", f"flash_mem_fwd_{tag}")
        heads = _unpad_heads([o_mix, o_mem], (1.0, 1.0), BF16, f"unpad_heads_{tag}")
        h_mid = _matmul(heads, ex.weight(("w_out", l)), "nn", F32, f"mm_out_{tag}", residual=h)
        xn2 = _rms_fwd(h_mid, g_ffn, f"rms_ffn_{tag}")
        w_val, w_gate = ex.weight(("w_up", l))
        u_val = _matmul(xn2, w_val, "nn", F32, f"mm_up_val_{tag}")
        u_gate = _matmul(xn2, w_gate, "nn", F32, f"mm_up_gate_{tag}")
        act = _conv_fwd(u_val, u_gate, *ex.conv(l), f"conv_fwd_{tag}")
        h_out = _matmul(act, ex.weight(("w_down", l)), "nn", F32, f"mm_down_{tag}", residual=h_mid)
        saved.append(dict(h=h, xn=xn, proj=proj, bfg=bfg, qp=qp, kp=kp, vp=vp, qmp=qmp, mn=mn, kmp=kmp, vmp=vmp, o_mix=o_mix,
                          lse_mix=lse_mix, o_mem=o_mem, lse_mem=lse_mem, heads=heads, h_mid=h_mid, xn2=xn2, u_val=u_val, u_gate=u_gate, act=act))
        h = h_out

    loss_blk, dh, dg_final = _loss_head(h, small["norm_final"].reshape(1, D_MODEL), target, "loss_head")

    grads = {k: [None] * DEPTH for k in ("conv_w", "conv_b", "norm_mix", "norm_mem", "norm_ffn")}
    grads["b_forget"] = [None, None]
    waiting = []
    for l in reversed(range(DEPTH)):
        fox = l % 2 == 0
        tag = f"L{l}"
        sv = saved[l]
        g_mix, g_mem, g_ffn = (small[n][l:l + 1] for n in ("norm_mix", "norm_mem", "norm_ffn"))
        da = _matmul(dh, ex.weight(("w_down", l)), "nt", F32, f"mm_da_{tag}")
        g_down = _matmul(sv["act"], dh, "tn", BF16, f"mm_dwdown_{tag}")
        du_val, du_gate, dcw_v, dcw_g, dcb_v, dcb_g = _conv_bwd(sv["u_val"], sv["u_gate"], da, *ex.conv(l), f"conv_bwd_{tag}")
        grads["conv_w"][l] = jnp.concatenate([dcw_v, dcw_g], axis=1)
        grads["conv_b"][l] = jnp.concatenate([dcb_v, dcb_g], axis=1)
        w_val, w_gate = ex.weight(("w_up", l))
        dxn2 = _matmul(du_val, w_val, "nt", F32, f"mm_dxn2_val_{tag}")
        dxn2 = _matmul(du_gate, w_gate, "nt", F32, f"mm_dxn2_gate_{tag}", residual=dxn2)
        g_up = (_matmul(sv["xn2"], du_val, "tn", BF16, f"mm_dwup_val_{tag}"), _matmul(sv["xn2"], du_gate, "tn", BF16, f"mm_dwup_gate_{tag}"))
        dh_mid, grads["norm_ffn"][l] = _rms_bwd(sv["h_mid"], g_ffn, dxn2, dh, f"rms_ffn_bwd_{tag}")
        dheads = _matmul(dh_mid, ex.weight(("w_out", l)), "nt", F32, f"mm_dheads_{tag}")
        g_out = _matmul(sv["heads"], dh_mid, "tn", BF16, f"mm_dwout_{tag}")
        do_mix, do_mem = _pad_heads([(dheads, 0, N_MIX), (dheads, D_MIX, N_MEM)], (1.0, 1.0), (N_MIX * HP, N_MEM * HP),
                                    f"pad_dheads_{tag}")
        mode = "causal" if fox else "dilated"
        items = waiting + [(("w_down", l), g_down), (("w_up", l), g_up), (("w_out", l), g_out)]
        spec = ex.scatter_spec(items)
        dqp, dkp, dvp, got = _flash_bwd(sv["qp"], sv["kp"], sv["vp"], sv["o_mix"], do_mix, sv["lse_mix"], mode, f"flash_{mode}_bwd_{tag}",
                                        comm=spec)
        if spec is not None:
            ex.put_received(items, spec, got)
        dqmp, dkmp, dvmp, _ = _flash_bwd(sv["qmp"], sv["kmp"], sv["vmp"], sv["o_mem"], do_mem, sv["lse_mem"], "full",
                                         f"flash_mem_bwd_{tag}")
        if fox:
            dproj, dbf = _prep_fox_bwd(dqp, dkp, dvp, dqmp, sv["proj"], sv["bfg"], f"prep_fox_bwd_{tag}")
            grads["b_forget"][l // 2] = dbf[0, :N_MIX]
        else:
            dproj = _prep_dil_bwd(dqp, dkp, dvp, dqmp, cos2, sin2, f"prep_dil_bwd_{tag}")
        dkvm = _unpad_heads([dkmp, dvmp], (LN2, 1.0), BF16, f"unpad_dkvm_{tag}")
        g_memkv = _matmul(sv["mn"], dkvm, "tn", BF16, f"mm_dwmemkv_{tag}")
        dmn = _matmul(dkvm, ex.weight(("w_mem_kv", l)), "nt", F32, f"mm_dmn_{tag}")
        _, grads["norm_mem"][l] = _rms_bwd(mem, g_mem, dmn, None, f"rms_mem_bwd_{tag}")
        dxn = _matmul(dproj, ex.weight(("w_in", l)), "nt", F32, f"mm_dxn_{tag}")
        g_in = _matmul(sv["xn"], dproj, "tn", BF16, f"mm_dwin_{tag}")
        waiting = [(("w_mem_kv", l), g_memkv), (("w_in", l), g_in)]
        dh, grads["norm_mix"][l] = _rms_bwd(sv["h"], g_mix, dxn, dh_mid, f"rms_mix_bwd_{tag}")
    spec = ex.scatter_spec(waiting)
    if spec is not None:
        ex.put_received(waiting, spec, _run_comm(spec, "grad_scatter_last"))
    grads["norm_final"] = dg_final
    return loss_blk, dh, grads


class _Exchange:
    def __init__(self, own, conv_w_full, conv_b, chip, core):
        self.own, self.conv_w_full, self.conv_b, self.chip, self.core = own, conv_w_full, conv_b, chip, core
        self.full, self.recv = {}, {}

    def _shard(self, key):
        name, l = key
        if name == "w_in":
            return self.own["w_in_fox" if l % 2 == 0 else "w_in_dil"][l // 2]
        return self.own[name][l]

    def gather_spec(self, keys):
        return _GatherSpec([self._shard(k) for k in keys])

    def put_gathered(self, keys, outs):
        for key, o in zip(keys, outs):
            name, l = key
            g = lax.dynamic_update_slice_in_dim(o, self._shard(key)[None], self.chip, axis=0)
            if name == "w_in":
                w = jnp.concatenate([g[j] for j in range(N_CHIPS)], axis=1)
                self.full[key] = _fox_cols_to_kernel(w) if l % 2 == 0 else w
            elif name == "w_up":
                self.full[key] = (jnp.concatenate([g[0], g[1]], axis=1), jnp.concatenate([g[2], g[3]], axis=1))
            else:
                self.full[key] = g.reshape(N_CHIPS * g.shape[1], g.shape[2])

    def weight(self, key):
        return self.full[key]

    def conv(self, l):
        return self.conv_w_full[l], self.conv_b[l:l + 1]

    def scatter_spec(self, items):
        parts = []
        for (name, l), g in items:
            if name == "w_in":
                g = _fox_cols_from_kernel(g) if l % 2 == 0 else g
                parts.append(jnp.stack(jnp.split(g, N_CHIPS, axis=1)))
            elif name == "w_up":
                parts.append(jnp.stack(jnp.split(g[0], 2, axis=1) + jnp.split(g[1], 2, axis=1)))
            else:
                parts.append(g.reshape(N_CHIPS, g.shape[0] // N_CHIPS, g.shape[1]))
        return _ScatterSpec(parts)

    def put_received(self, items, spec, outs):
        for (key, _), part, o in zip(items, spec.inputs, outs):
            rh = o.shape[1]
            mine = lax.dynamic_slice_in_dim(lax.dynamic_index_in_dim(part, self.chip, 0, keepdims=False), self.core * rh, rh, axis=0)
            self.recv[key] = lax.dynamic_update_slice_in_dim(o, mine[None], 2 * self.chip + self.core, axis=0)


def kernel(x, mem, norm_mix, norm_mem, norm_ffn, w_in_fox, b_forget, w_in_dil, w_mem_kv, w_out, w_up, conv_w, conv_b, w_down, norm_final, loss_target, m_norm_mix, m_norm_mem, m_norm_ffn, m_w_in_fox, m_b_forget, m_w_in_dil, m_w_mem_kv, m_w_out, m_w_up, m_conv_w, m_conv_b, m_w_down, m_norm_final, v_norm_mix, v_norm_mem, v_norm_ffn, v_w_in_fox, v_b_forget, v_w_in_dil, v_w_mem_kv, v_w_out, v_w_up, v_conv_w, v_conv_b, v_w_down, v_norm_final):
    w_sh = dict(w_in_fox=w_in_fox, w_in_dil=w_in_dil, w_mem_kv=w_mem_kv, w_out=w_out, w_up=w_up, w_down=w_down, conv_w=conv_w)
    m_sh = dict(w_in_fox=m_w_in_fox, w_in_dil=m_w_in_dil, w_mem_kv=m_w_mem_kv, w_out=m_w_out, w_up=m_w_up, w_down=m_w_down, conv_w=m_conv_w)
    v_sh = dict(w_in_fox=v_w_in_fox, w_in_dil=v_w_in_dil, w_mem_kv=v_w_mem_kv, w_out=v_w_out, w_up=v_w_up, w_down=v_w_down, conv_w=v_conv_w)
    small = dict(norm_mix=norm_mix, norm_mem=norm_mem, norm_ffn=norm_ffn, conv_b=conv_b, norm_final=norm_final, b_forget=b_forget)
    m_small = dict(norm_mix=m_norm_mix, norm_mem=m_norm_mem, norm_ffn=m_norm_ffn, conv_b=m_conv_b, norm_final=m_norm_final, b_forget=m_b_forget)
    v_small = dict(norm_mix=v_norm_mix, norm_mem=v_norm_mem, norm_ffn=v_norm_ffn, conv_b=v_conv_b, norm_final=v_norm_final, b_forget=v_b_forget)
    chip = 2 * lax.axis_index("x") + lax.axis_index("y")
    core = lax.axis_index("c")

    conv_spec = (("conv_w", _CONVW_SHARD),)
    conv_all = _gather_all(_pack_small(dict(conv_w=conv_w), conv_spec), "gather_conv_w")
    conv_w_full = jnp.concatenate([_unpack_small(conv_all[2 * j], conv_spec)["conv_w"] for j in range(N_CHIPS)], axis=-1)
    ex = _Exchange({n: w_sh[n].astype(BF16) for n in _BIG}, conv_w_full, conv_b, chip, core)

    loss_blk, dx, grads = _local_step(x[0], mem[0], loss_target[0], small, ex)
    loss = lax.psum(loss_blk[0, 0], ("x", "y", "c"))

    layer_keys = {"w_in_fox": [("w_in", 0), ("w_in", 2)], "w_in_dil": [("w_in", 1), ("w_in", 3)]}
    keys = [k for n in _BIG for k in layer_keys.get(n, [(n, l) for l in range(DEPTH)])]
    halves = [_sum_rows(ex.recv[k], F32, f"grad_sum_{k[0]}_L{k[1]}") for k in keys]
    whole = {k: jnp.concatenate([jnp.where(core == 0, mine, other), jnp.where(core == 0, other, mine)], axis=0)
             for k, mine, other in zip(keys, halves, _join_halves(halves, "grad_join_halves"))}
    g_big = {n: jnp.stack([whole[k] for k in layer_keys.get(n, [(n, l) for l in range(DEPTH)])]) for n in _BIG}

    g_small_local = dict(
        norm_mix=jnp.concatenate(grads["norm_mix"]), norm_mem=jnp.concatenate(grads["norm_mem"]),
        norm_ffn=jnp.concatenate(grads["norm_ffn"]),
        conv_b=jnp.concatenate(grads["conv_b"]),
        norm_final=grads["norm_final"], b_forget=jnp.stack(grads["b_forget"]),
        conv_w=jnp.stack(grads["conv_w"]),
    )
    small_all = _gather_all(_pack_small(g_small_local, _SMALL), "small_gather_all")
    g_small = _unpack_small(_sum_rows(small_all, F32, "small_sum"), _SMALL)
    ncol = 2 * D_FF // N_CHIPS
    g_big["conv_w"] = lax.dynamic_slice_in_dim(g_small["conv_w"], chip * ncol, ncol, axis=2)

    out_g, out_d, out_m, out_v = {}, {}, {}, {}
    for n in _BIG + ("conv_w",):
        shp = w_sh[n].shape
        two_d = (-1, shp[-1])
        d, mo, vo = _adamw(w_sh[n].reshape(two_d), g_big[n].reshape(two_d), m_sh[n].reshape(two_d), v_sh[n].reshape(two_d), f"adamw_{n}")
        out_g[n], out_d[n], out_m[n], out_v[n] = g_big[n].reshape(shp), d.reshape(shp), mo.reshape(shp), vo.reshape(shp)
    spec = _SMALL[:-1]
    d, mo, vo = _adamw(_pack_small(small, spec), _pack_small(g_small, spec), _pack_small(m_small, spec), _pack_small(v_small, spec),
                       "adamw_small")
    d, mo, vo = _unpack_small(d, spec), _unpack_small(mo, spec), _unpack_small(vo, spec)
    for n, shp in spec:
        out_g[n], out_d[n], out_m[n], out_v[n] = g_small[n].reshape(shp), d[n], mo[n], vo[n]

    order = ("norm_mix", "norm_mem", "norm_ffn", "w_in_fox", "b_forget", "w_in_dil", "w_mem_kv", "w_out", "w_up", "conv_w", "conv_b",
             "w_down", "norm_final")
    return (loss, dx[None], *[out_g[n] for n in order], *[out_d[n] for n in order], *[out_m[n] for n in order],
            *[out_v[n] for n in order])
```

```python
import functools
import math

import numpy as np
import jax
import jax.numpy as jnp
from jax import lax
from jax.experimental import pallas as pl
from jax.experimental.pallas import tpu as pltpu

F32 = jnp.float32
BF16 = jnp.bfloat16

D_MODEL = 1024
DEPTH = 4
HEAD_DIM = 64
N_MIX = 12
N_MEM = 4
D_MIX = N_MIX * HEAD_DIM
D_MEMQ = N_MEM * HEAD_DIM
D_FF = 2816
FOX_IN = 3 * D_MIX + N_MIX + D_MEMQ
DIL_IN = 3 * D_MIX + D_MEMQ
HP = 128
SCALE = HEAD_DIM ** -0.5
NEG = -1e30
NORM_EPS = 1e-6
DIL_MAX = 2048
LOG2E = 1.0 / math.log(2.0)
LN2 = math.log(2.0)
QSCALE = SCALE * LOG2E
HEADS_PER_STEP = 2
ROPE_THETA = 10000.0
N_CHIPS = 4
CONV_TF = 128
FLASH_BLK = 512
VMEM_LIMIT = 48 * 1024 * 1024

ADAM_LR = 0.001
ADAM_B1 = 0.9
ADAM_B2 = 0.999
ADAM_EPS = 1e-08
ADAM_WD = 0.01
ADAM_STEP = 10

MESH = pl.DeviceIdType.MESH


def _cp(*sem):
    return pltpu.CompilerParams(dimension_semantics=tuple(sem), vmem_limit_bytes=VMEM_LIMIT)


def _tile(n, cap, mult=128):
    if n <= cap:
        return n
    t = (cap // mult) * mult
    while t >= mult:
        if n % t == 0:
            return t
        t -= mult
    raise ValueError(f"no tile for {n} under {cap}")


_DIMS = {"nn": (((1,), (0,)), ((), ())), "nt": (((1,), (1,)), ((), ())), "tn": (((0,), (0,)), ((), ()))}


def _matmul(a, b, mode, out_dtype, name, residual=None):
    if mode == "nn":
        (m, k), n = a.shape, b.shape[1]
    elif mode == "nt":
        (m, k), n = a.shape, b.shape[0]
    else:
        (k, m), n = a.shape, b.shape[1]
    tn = 1408 if n % 1408 == 0 else _tile(n, 1024)
    tm = 1408 if (m % 1408 == 0 and tn <= 1024) else _tile(m, 1024 if tn <= 1024 else 512)
    tk = _tile(k, 1408)
    nk = k // tk
    dims = _DIMS[mode]
    has_res = residual is not None

    def body(*refs):
        if has_res:
            a_ref, b_ref, r_ref, o_ref = refs[:4]
        else:
            a_ref, b_ref, o_ref = refs[:3]
        part = lax.dot_general(a_ref[...].astype(BF16), b_ref[...].astype(BF16), dims, preferred_element_type=F32)
        if nk == 1:
            if has_res:
                part = part + r_ref[...]
            o_ref[...] = part.astype(o_ref.dtype)
            return
        acc_ref = refs[-1]
        kk = pl.program_id(2)

        @pl.when(kk == 0)
        def _():
            acc_ref[...] = part

        @pl.when(kk > 0)
        def _():
            acc_ref[...] += part

        @pl.when(kk == nk - 1)
        def _():
            tot = acc_ref[...]
            if has_res:
                tot = tot + r_ref[...]
            o_ref[...] = tot.astype(o_ref.dtype)

    if mode == "nn":
        a_spec = pl.BlockSpec((tm, tk), lambda i, j, kk: (i, kk))
        b_spec = pl.BlockSpec((tk, tn), lambda i, j, kk: (kk, j))
    elif mode == "nt":
        a_spec = pl.BlockSpec((tm, tk), lambda i, j, kk: (i, kk))
        b_spec = pl.BlockSpec((tn, tk), lambda i, j, kk: (j, kk))
    else:
        a_spec = pl.BlockSpec((tk, tm), lambda i, j, kk: (kk, i))
        b_spec = pl.BlockSpec((tk, tn), lambda i, j, kk: (kk, j))
    in_specs = [a_spec, b_spec]
    args = [a, b]
    if has_res:
        in_specs.append(pl.BlockSpec((tm, tn), lambda i, j, kk: (i, j)))
        args.append(residual)
    return pl.pallas_call(
        body,
        out_shape=jax.ShapeDtypeStruct((m, n), out_dtype),
        grid=(m // tm, n // tn, nk),
        in_specs=in_specs,
        out_specs=pl.BlockSpec((tm, tn), lambda i, j, kk: (i, j)),
        scratch_shapes=[pltpu.VMEM((tm, tn), F32)] if nk > 1 else [],
        compiler_params=_cp("parallel", "parallel", "arbitrary"),
        name=name,
    )(*args)


def _rms_fwd(h, g, name):
    r, d = h.shape
    tr = _tile(r, 512, 8)

    def body(h_ref, g_ref, o_ref):
        x = h_ref[...]
        rstd = lax.rsqrt(jnp.mean(x * x, axis=-1, keepdims=True) + NORM_EPS)
        o_ref[...] = ((x * rstd) * g_ref[...]).astype(o_ref.dtype)

    return pl.pallas_call(
        body,
        out_shape=jax.ShapeDtypeStruct((r, d), BF16),
        grid=(r // tr,),
        in_specs=[pl.BlockSpec((tr, d), lambda i: (i, 0)), pl.BlockSpec((1, d), lambda i: (0, 0))],
        out_specs=pl.BlockSpec((tr, d), lambda i: (i, 0)),
        compiler_params=_cp("parallel"),
        name=name,
    )(h, g)


def _rms_bwd(h, g, dy, dres, name):
    r, d = h.shape
    tr = _tile(r, 512, 8)
    need_dh = dres is not None

    def body(*refs):
        if need_dh:
            h_ref, g_ref, dy_ref, dres_ref, dh_ref, dg_ref = refs
        else:
            h_ref, g_ref, dy_ref, dg_ref = refs
        i = pl.program_id(0)
        x = h_ref[...]
        rstd = lax.rsqrt(jnp.mean(x * x, axis=-1, keepdims=True) + NORM_EPS)
        nrm = x * rstd
        dyv = dy_ref[...].astype(F32)
        part = jnp.sum(dyv * nrm, axis=0, keepdims=True)

        @pl.when(i == 0)
        def _():
            dg_ref[...] = part

        @pl.when(i > 0)
        def _():
            dg_ref[...] += part

        if need_dh:
            gy = dyv * g_ref[...]
            dx = rstd * (gy - nrm * jnp.mean(gy * nrm, axis=-1, keepdims=True))
            dh_ref[...] = dres_ref[...] + dx

    row = pl.BlockSpec((tr, d), lambda i: (i, 0))
    vec = pl.BlockSpec((1, d), lambda i: (0, 0))
    if need_dh:
        return pl.pallas_call(
            body,
            out_shape=(jax.ShapeDtypeStruct((r, d), F32), jax.ShapeDtypeStruct((1, d), F32)),
            grid=(r // tr,),
            in_specs=[row, vec, row, row],
            out_specs=(row, vec),
            compiler_params=_cp("arbitrary"),
            name=name,
        )(h, g, dy, dres)
    return None, pl.pallas_call(
        body,
        out_shape=jax.ShapeDtypeStruct((1, d), F32),
        grid=(r // tr,),
        in_specs=[row, vec, row],
        out_specs=vec,
        compiler_params=_cp("arbitrary"),
        name=name,
    )(h, g, dy)


def _loss_head(h, g, target, name):
    r, d = h.shape
    tr = _tile(r, 512, 8)

    def body(h_ref, g_ref, t_ref, loss_ref, dh_ref, dg_ref):
        i = pl.program_id(0)
        x = h_ref[...]
        gv = g_ref[...]
        rstd = lax.rsqrt(jnp.mean(x * x, axis=-1, keepdims=True) + NORM_EPS)
        nrm = x * rstd
        err = nrm * gv - t_ref[...]
        lpart = 0.5 * jnp.sum(jnp.mean(err * err, axis=-1, keepdims=True), axis=0, keepdims=True)
        dyv = err * (1.0 / d)
        gpart = jnp.sum(dyv * nrm, axis=0, keepdims=True)

        @pl.when(i == 0)
        def _():
            loss_ref[...] = jnp.broadcast_to(lpart, loss_ref.shape)
            dg_ref[...] = gpart

        @pl.when(i > 0)
        def _():
            loss_ref[...] += jnp.broadcast_to(lpart, loss_ref.shape)
            dg_ref[...] += gpart

        gy = dyv * gv
        dh_ref[...] = rstd * (gy - nrm * jnp.mean(gy * nrm, axis=-1, keepdims=True))

    row = pl.BlockSpec((tr, d), lambda i: (i, 0))
    vec = pl.BlockSpec((1, d), lambda i: (0, 0))
    lsp = pl.BlockSpec((1, 128), lambda i: (0, 0))
    return pl.pallas_call(
        body,
        out_shape=(jax.ShapeDtypeStruct((1, 128), F32), jax.ShapeDtypeStruct((r, d), F32), jax.ShapeDtypeStruct((1, d), F32)),
        grid=(r // tr,),
        in_specs=[row, vec, row],
        out_specs=(lsp, row, vec),
        compiler_params=_cp("arbitrary"),
        name=name,
    )(h, g, target)


def _lane(shape):
    return lax.broadcasted_iota(jnp.int32, shape, 1)


def _head_tile(pair_tile, odd):
    return pltpu.roll(pair_tile, 64, 1) if odd else pair_tile


def _pair_tile(even_tile, odd_tile):
    lane = _lane(even_tile.shape)
    return jnp.where(lane < 64, even_tile, pltpu.roll(odd_tile, 64, 1))


def _pad_heads(xs, scales, out_widths, name):
    r = xs[0][0].shape[0]
    tr = _tile(r, 256, 8)
    n_in = len(xs)

    def body(*refs):
        for idx in range(n_in):
            x_ref, o_ref = refs[idx], refs[n_in + idx]
            nh = xs[idx][2]
            for p in range(nh // 2):
                t = x_ref[:, p * 128:(p + 1) * 128].astype(F32) * scales[idx]
                lane = _lane(t.shape)
                o_ref[:, (2 * p) * HP:(2 * p + 1) * HP] = jnp.where(lane < 64, t, 0.0).astype(o_ref.dtype)
                o_ref[:, (2 * p + 1) * HP:(2 * p + 2) * HP] = jnp.where(lane < 64, pltpu.roll(t, 64, 1), 0.0).astype(o_ref.dtype)

    in_specs, args, out_specs, out_shape = [], [], [], []
    for (arr, c0, nh), w in zip(xs, out_widths):
        wcols = nh * 64
        assert c0 % wcols == 0 or c0 == 0
        blk = c0 // wcols if wcols else 0
        in_specs.append(pl.BlockSpec((tr, wcols), functools.partial(lambda i, b: (i, b), b=blk)))
        args.append(arr)
        out_specs.append(pl.BlockSpec((tr, w), lambda i: (i, 0)))
        out_shape.append(jax.ShapeDtypeStruct((r, w), BF16))
    return pl.pallas_call(
        body,
        out_shape=tuple(out_shape),
        grid=(r // tr,),
        in_specs=in_specs,
        out_specs=tuple(out_specs),
        compiler_params=_cp("parallel"),
        name=name,
    )(*args)


def _unpad_heads(xs, scales, out_dtype, name):
    r = xs[0].shape[0]
    tr = _tile(r, 256, 8)
    nhs = [x.shape[1] // HP for x in xs]
    total = sum(nhs) * 64

    def body(*refs):
        o_ref = refs[-1]
        col = 0
        for x_ref, nh, sc in zip(refs[:-1], nhs, scales):
            for p in range(nh // 2):
                ev = x_ref[:, (2 * p) * HP:(2 * p + 1) * HP].astype(F32)
                od = x_ref[:, (2 * p + 1) * HP:(2 * p + 2) * HP].astype(F32)
                o_ref[:, col:col + 128] = (_pair_tile(ev, od) * sc).astype(o_ref.dtype)
                col += 128

    return pl.pallas_call(
        body,
        out_shape=jax.ShapeDtypeStruct((r, total), out_dtype),
        grid=(r // tr,),
        in_specs=[pl.BlockSpec((tr, x.shape[1]), lambda i: (i, 0)) for x in xs],
        out_specs=pl.BlockSpec((tr, total), lambda i: (i, 0)),
        compiler_params=_cp("parallel"),
        name=name,
    )(*xs)


def _bf16_split3(c):
    hi = c.astype(BF16).astype(F32)
    r1 = c - hi
    mid = r1.astype(BF16).astype(F32)
    lo = (r1 - mid).astype(BF16).astype(F32)
    return hi, mid, lo


def _log_sigmoid(z):
    return jnp.minimum(z, 0.0) - jnp.log(1.0 + jnp.exp(-jnp.abs(z)))


def _prep_fox_fwd(proj, bfg, name):
    s = proj.shape[0]
    ts = _tile(s, 256, 8)
    fcol = 3 * D_MIX + D_MEMQ

    def body(p_ref, b_ref, q_ref, k_ref, v_ref, qm_ref, carry_ref):
        i = pl.program_id(0)

        @pl.when(i == 0)
        def _():
            carry_ref[...] = jnp.zeros_like(carry_ref)

        lane = _lane((ts, 128))
        z = p_ref[:, fcol:fcol + 128] + b_ref[...]
        logf = jnp.where(lane < N_MIX, _log_sigmoid(z), 0.0)
        rr = lax.broadcasted_iota(jnp.int32, (ts, ts), 0)
        cc = lax.broadcasted_iota(jnp.int32, (ts, ts), 1)
        tri = jnp.where(cc <= rr, 1.0, 0.0).astype(F32)
        c = jnp.dot(tri, logf, preferred_element_type=F32, precision=lax.Precision.HIGHEST) + carry_ref[0:1, :]
        carry_ref[...] = jnp.broadcast_to(c[ts - 1:ts, :], carry_ref.shape)
        for hh in range(N_MIX):
            p, odd = hh // 2, hh % 2
            ch = jnp.sum(jnp.where(lane == hh, c, 0.0), axis=-1, keepdims=True) * LOG2E
            hi, mid, lo = _bf16_split3(ch)
            qt = _head_tile(p_ref[:, p * 128:(p + 1) * 128], odd) * QSCALE
            kt = _head_tile(p_ref[:, D_MIX + p * 128:D_MIX + (p + 1) * 128], odd)
            vt = _head_tile(p_ref[:, 2 * D_MIX + p * 128:2 * D_MIX + (p + 1) * 128], odd)
            qa = jnp.where(lane == 64, hi, jnp.where(lane == 65, mid, jnp.where(lane == 66, lo, jnp.where(lane < 70, 1.0, 0.0))))
            ka = jnp.where(lane < 67, 1.0, jnp.where(lane == 67, -hi, jnp.where(lane == 68, -mid, jnp.where(lane == 69, -lo, 0.0))))
            q_ref[:, hh * HP:(hh + 1) * HP] = jnp.where(lane < 64, qt, qa).astype(BF16)
            k_ref[:, hh * HP:(hh + 1) * HP] = jnp.where(lane < 64, kt, ka).astype(BF16)
            v_ref[:, hh * HP:(hh + 1) * HP] = jnp.where(lane < 64, vt, 0.0).astype(BF16)
        for hh in range(N_MEM):
            p, odd = hh // 2, hh % 2
            t = _head_tile(p_ref[:, 3 * D_MIX + p * 128:3 * D_MIX + (p + 1) * 128], odd) * QSCALE
            qm_ref[:, hh * HP:(hh + 1) * HP] = jnp.where(lane < 64, t, 0.0).astype(BF16)

    wmix, wmem = N_MIX * HP, N_MEM * HP
    return pl.pallas_call(
        body,
        out_shape=(jax.ShapeDtypeStruct((s, wmix), BF16),) * 3 + (jax.ShapeDtypeStruct((s, wmem), BF16),),
        grid=(s // ts,),
        in_specs=[pl.BlockSpec((ts, proj.shape[1]), lambda i: (i, 0)), pl.BlockSpec((1, 128), lambda i: (0, 0))],
        out_specs=(pl.BlockSpec((ts, wmix), lambda i: (i, 0)),) * 3 + (pl.BlockSpec((ts, wmem), lambda i: (i, 0)),),
        scratch_shapes=[pltpu.VMEM((8, 128), F32)],
        compiler_params=_cp("arbitrary"),
        name=name,
    )(proj, bfg)


def _prep_fox_bwd(dq, dk, dv, dqm, proj, bfg, name):
    s = proj.shape[0]
    ts = _tile(s, 256, 8)
    nb = s // ts
    fcol = 3 * D_MIX + D_MEMQ

    def body(dq_ref, dk_ref, dv_ref, dqm_ref, p_ref, b_ref, o_ref, db_ref, carry_ref):
        i = pl.program_id(0)

        @pl.when(i == 0)
        def _():
            carry_ref[...] = jnp.zeros_like(carry_ref)
            db_ref[...] = jnp.zeros_like(db_ref)

        lane = _lane((ts, 128))
        dc = jnp.zeros((ts, 128), F32)
        for p in range(N_MIX // 2):
            tq, tk, tv = [], [], []
            for odd in (0, 1):
                hh = 2 * p + odd
                dqt = dq_ref[:, hh * HP:(hh + 1) * HP]
                dkt = dk_ref[:, hh * HP:(hh + 1) * HP]
                col = jnp.sum(jnp.where(lane == 64, dqt, 0.0) - jnp.where(lane == 67, dkt, 0.0), axis=-1, keepdims=True)
                dc = dc + jnp.where(lane == hh, col, 0.0)
                tq.append(dqt)
                tk.append(dkt)
                tv.append(dv_ref[:, hh * HP:(hh + 1) * HP])
            o_ref[:, p * 128:(p + 1) * 128] = (_pair_tile(tq[0], tq[1]) * SCALE).astype(o_ref.dtype)
            o_ref[:, D_MIX + p * 128:D_MIX + (p + 1) * 128] = (_pair_tile(tk[0], tk[1]) * LN2).astype(o_ref.dtype)
            o_ref[:, 2 * D_MIX + p * 128:2 * D_MIX + (p + 1) * 128] = _pair_tile(tv[0], tv[1]).astype(o_ref.dtype)
        for p in range(N_MEM // 2):
            ev = dqm_ref[:, (2 * p) * HP:(2 * p + 1) * HP]
            od = dqm_ref[:, (2 * p + 1) * HP:(2 * p + 2) * HP]
            o_ref[:, 3 * D_MIX + p * 128:3 * D_MIX + (p + 1) * 128] = (_pair_tile(ev, od) * SCALE).astype(o_ref.dtype)
        rr = lax.broadcasted_iota(jnp.int32, (ts, ts), 0)
        cc = lax.broadcasted_iota(jnp.int32, (ts, ts), 1)
        triu = jnp.where(cc >= rr, 1.0, 0.0).astype(F32)
        dlogf = jnp.dot(triu, dc, preferred_element_type=F32, precision=lax.Precision.HIGHEST) + carry_ref[0:1, :]
        carry_ref[...] = jnp.broadcast_to(dlogf[0:1, :], carry_ref.shape)
        z = p_ref[:, fcol:fcol + 128] + b_ref[...]
        dz = jnp.where(lane < N_MIX, dlogf / (1.0 + jnp.exp(z)), 0.0)
        o_ref[:, fcol:fcol + 128] = dz.astype(o_ref.dtype)
        db_ref[...] += jnp.sum(dz, axis=0, keepdims=True)

    wmix, wmem = N_MIX * HP, N_MEM * HP
    rev = lambda i: (nb - 1 - i, 0)
    return pl.pallas_call(
        body,
        out_shape=(jax.ShapeDtypeStruct(proj.shape, BF16), jax.ShapeDtypeStruct((1, 128), F32)),
        grid=(nb,),
        in_specs=[pl.BlockSpec((ts, wmix), rev)] * 3 + [pl.BlockSpec((ts, wmem), rev), pl.BlockSpec((ts, proj.shape[1]), rev),
                                                         pl.BlockSpec((1, 128), lambda i: (0, 0))],
        out_specs=(pl.BlockSpec((ts, proj.shape[1]), rev), pl.BlockSpec((1, 128), lambda i: (0, 0))),
        scratch_shapes=[pltpu.VMEM((8, 128), F32)],
        compiler_params=_cp("arbitrary"),
        name=name,
    )(dq, dk, dv, dqm, proj, bfg)


def _rope_partner(x):
    lane = _lane(x.shape)
    return jnp.where((lane % 64) < 32, pltpu.roll(x, 96, 1), pltpu.roll(x, 32, 1))


def _prep_dil_fwd(proj, cos2, sin2, name):
    s = proj.shape[0]
    ts = _tile(s, 256, 8)

    def body(p_ref, c_ref, s_ref, q_ref, k_ref, v_ref, qm_ref):
        lane = _lane((ts, 128))
        cosv, sinv = c_ref[...], s_ref[...]
        for p in range(N_MIX // 2):
            xq = p_ref[:, p * 128:(p + 1) * 128]
            xk = p_ref[:, D_MIX + p * 128:D_MIX + (p + 1) * 128]
            xv = p_ref[:, 2 * D_MIX + p * 128:2 * D_MIX + (p + 1) * 128]
            yq = (xq * cosv + _rope_partner(xq) * sinv) * QSCALE
            yk = xk * cosv + _rope_partner(xk) * sinv
            for odd in (0, 1):
                hh = 2 * p + odd
                q_ref[:, hh * HP:(hh + 1) * HP] = jnp.where(lane < 64, _head_tile(yq, odd), 0.0).astype(BF16)
                k_ref[:, hh * HP:(hh + 1) * HP] = jnp.where(lane < 64, _head_tile(yk, odd), 0.0).astype(BF16)
                v_ref[:, hh * HP:(hh + 1) * HP] = jnp.where(lane < 64, _head_tile(xv, odd), 0.0).astype(BF16)
        for p in range(N_MEM // 2):
            t = p_ref[:, 3 * D_MIX + p * 128:3 * D_MIX + (p + 1) * 128] * QSCALE
            for odd in (0, 1):
                hh = 2 * p + odd
                qm_ref[:, hh * HP:(hh + 1) * HP] = jnp.where(lane < 64, _head_tile(t, odd), 0.0).astype(BF16)

    wmix, wmem = N_MIX * HP, N_MEM * HP
    return pl.pallas_call(
        body,
        out_shape=(jax.ShapeDtypeStruct((s, wmix), BF16),) * 3 + (jax.ShapeDtypeStruct((s, wmem), BF16),),
        grid=(s // ts,),
        in_specs=[pl.BlockSpec((ts, proj.shape[1]), lambda i: (i, 0)), pl.BlockSpec((ts, 128), lambda i: (i, 0)),
                  pl.BlockSpec((ts, 128), lambda i: (i, 0))],
        out_specs=(pl.BlockSpec((ts, wmix), lambda i: (i, 0)),) * 3 + (pl.BlockSpec((ts, wmem), lambda i: (i, 0)),),
        compiler_params=_cp("parallel"),
        name=name,
    )(proj, cos2, sin2)


def _prep_dil_bwd(dq, dk, dv, dqm, cos2, sin2, name):
    s = dq.shape[0]
    ts = _tile(s, 256, 8)

    def body(dq_ref, dk_ref, dv_ref, dqm_ref, c_ref, s_ref, o_ref):
        cosv, sinv = c_ref[...], s_ref[...]
        for p in range(N_MIX // 2):
            e, o = 2 * p, 2 * p + 1
            dyq = _pair_tile(dq_ref[:, e * HP:(e + 1) * HP], dq_ref[:, o * HP:(o + 1) * HP]) * SCALE
            dyk = _pair_tile(dk_ref[:, e * HP:(e + 1) * HP], dk_ref[:, o * HP:(o + 1) * HP]) * LN2
            dxv = _pair_tile(dv_ref[:, e * HP:(e + 1) * HP], dv_ref[:, o * HP:(o + 1) * HP])
            o_ref[:, p * 128:(p + 1) * 128] = (dyq * cosv - _rope_partner(dyq) * sinv).astype(o_ref.dtype)
            o_ref[:, D_MIX + p * 128:D_MIX + (p + 1) * 128] = (dyk * cosv - _rope_partner(dyk) * sinv).astype(o_ref.dtype)
            o_ref[:, 2 * D_MIX + p * 128:2 * D_MIX + (p + 1) * 128] = dxv.astype(o_ref.dtype)
        for p in range(N_MEM // 2):
            e, o = 2 * p, 2 * p + 1
            t = _pair_tile(dqm_ref[:, e * HP:(e + 1) * HP], dqm_ref[:, o * HP:(o + 1) * HP]) * SCALE
            o_ref[:, 3 * D_MIX + p * 128:3 * D_MIX + (p + 1) * 128] = t.astype(o_ref.dtype)

    wmix, wmem = N_MIX * HP, N_MEM * HP
    row = lambda w: pl.BlockSpec((ts, w), lambda i: (i, 0))
    return pl.pallas_call(
        body,
        out_shape=jax.ShapeDtypeStruct((s, DIL_IN), BF16),
        grid=(s // ts,),
        in_specs=[row(wmix)] * 3 + [row(wmem), row(128), row(128)],
        out_specs=row(DIL_IN),
        compiler_params=_cp("parallel"),
        name=name,
    )(dq, dk, dv, dqm, cos2, sin2)


def _mask_bias(mode, blk):
    n = 1 if mode == "causal" else DIL_MAX // blk + 1
    idx = jnp.arange(blk, dtype=jnp.int32)
    dist = jnp.arange(n, dtype=jnp.int32)[:, None, None] * blk + idx[None, :, None] - idx[None, None, :]
    if mode == "causal":
        return jnp.where(dist >= 0, 0.0, NEG).astype(F32)
    cnt = ((dist <= 128).astype(jnp.int32) + ((dist <= 512) & (dist % 4 == 0)).astype(jnp.int32)
           + ((dist <= DIL_MAX) & (dist % 16 == 0)).astype(jnp.int32))
    bias = jnp.where(cnt == 3, math.log2(3.0), jnp.where(cnt == 2, 1.0, 0.0))
    return jnp.where((dist >= 0) & (cnt > 0), bias, NEG).astype(F32)


def _flash_dims(q, k, mode):
    sq, w = q.shape
    sk = k.shape[0]
    tq = _tile(sq, FLASH_BLK, 8)
    tk = sk if mode == "full" else tq
    band = DIL_MAX // tk if mode == "dilated" else None
    return sq, sk, w, tq, tk, sq // tq, sk // tk, band


def _pair_starts(mode, nq, band, key_major):
    if mode == "full":
        counts = [nq] if key_major else [1] * nq
    elif key_major:
        counts = [(min(kj + band, nq - 1) if mode == "dilated" else nq - 1) - kj + 1 for kj in range(nq)]
    else:
        counts = [qi - (max(qi - band, 0) if mode == "dilated" else 0) + 1 for qi in range(nq)]
    starts = [int(v) for v in np.cumsum([0] + counts)]
    return starts[:-1], starts[-1]


def _pair_at(t, starts, mode, band, key_major):
    if mode == "full":
        return t, jnp.zeros((), jnp.int32)
    major, begin = 0, 0
    for prev, s in zip(starts[:-1], starts[1:]):
        reached = t >= s
        major = major + reached.astype(jnp.int32)
        begin = begin + jnp.where(reached, s - prev, 0)
    if key_major:
        return major + (t - begin), major
    return major, _first_key(mode, major, band) + (t - begin)


def _first_key(mode, qi, band):
    return jnp.maximum(qi - band, 0) if mode == "dilated" else 0


def _carried(comm, grid):
    if comm is None:
        return 0, 0, [], (lambda *a: None), (lambda *a: None)

    def edge(last):
        cond = None
        for d, n in enumerate(grid):
            c = pl.program_id(d) == (n - 1 if last else 0)
            cond = c if cond is None else cond & c
        return cond

    def start(cins, couts, sems):
        pl.when(edge(False))(lambda: comm.start(cins, couts, *sems))

    def finish(cins, couts, sems):
        pl.when(edge(True))(lambda: comm.finish(cins, couts, *sems))

    sems = [pltpu.SemaphoreType.DMA((comm.n_sems,)), pltpu.SemaphoreType.DMA((comm.n_sems,))]
    return len(comm.inputs), len(comm.out_shapes), sems, start, finish


def _split_refs(refs, n_in, n_cin, n_out, n_cout, n_scratch):
    bounds = np.cumsum([0, n_in, n_cin, n_out, n_cout, n_scratch])
    return [refs[a:b] for a, b in zip(bounds[:-1], bounds[1:])] + [refs[bounds[-1]:]]


def _flash_fwd(q, k, v, mode, name, comm=None):
    sq, sk, w, tq, tk, nq, nk, band = _flash_dims(q, k, mode)
    hb = HEADS_PER_STEP
    wb = hb * HP
    nch = tk // 128
    has_bias = mode != "full"
    starts, n_pairs = _pair_starts(mode, nq, band, key_major=False)
    pair = functools.partial(_pair_at, starts=starts, mode=mode, band=band, key_major=False)
    grid = (w // wb, n_pairs)
    n_cin, n_cout, comm_sems, comm_start, comm_finish = _carried(comm, grid)

    def body(*refs):
        ins, cins, (o_ref, lse_ref), couts, (m_ref, l_ref, acc_ref), sems = _split_refs(refs, 3 + has_bias, n_cin, 2, n_cout, 3)
        q_ref, k_ref, v_ref = ins[:3]
        b_ref = ins[3] if has_bias else None
        qi, kj = pair(pl.program_id(1))
        comm_start(cins, couts, sems)

        @pl.when(kj == _first_key(mode, qi, band))
        def _():
            m_ref[...] = jnp.full_like(m_ref, -jnp.inf)
            l_ref[...] = jnp.zeros_like(l_ref)
            acc_ref[...] = jnp.zeros_like(acc_ref)

        def step(bias_tile):
            for h in range(hb):
                cols = slice(h * HP, (h + 1) * HP)
                sc = lax.dot_general(q_ref[:, cols], k_ref[:, cols], _DIMS["nt"], preferred_element_type=F32)
                if bias_tile is not None:
                    sc = sc + bias_tile()
                m_prev = m_ref[h]
                m_new = jnp.maximum(m_prev, jnp.max(sc, axis=-1, keepdims=True))
                alpha = jnp.exp2(m_prev - m_new)
                psum, chunks = None, []
                for c in range(nch):
                    pc = jnp.exp2(sc[:, c * 128:(c + 1) * 128] - m_new)
                    psum = pc if psum is None else psum + pc
                    chunks.append(pc.astype(BF16))
                p = chunks[0] if nch == 1 else jnp.concatenate(chunks, axis=1)
                l_ref[h] = alpha * l_ref[h] + psum
                acc_ref[h] = alpha * acc_ref[h] + jnp.dot(p, v_ref[:, cols], preferred_element_type=F32)
                m_ref[h] = m_new

        if mode == "full":
            step(None)
        elif mode == "causal":
            pl.when(kj == qi)(lambda: step(lambda: b_ref[0]))
            pl.when(kj < qi)(lambda: step(None))
        else:
            step(lambda: b_ref[qi - kj])

        @pl.when(kj == (0 if mode == "full" else qi))
        def _():
            for h in range(hb):
                cols = slice(h * HP, (h + 1) * HP)
                l = jnp.sum(l_ref[h], axis=-1, keepdims=True)
                o_ref[:, cols] = (acc_ref[h] / l).astype(o_ref.dtype)
                lse_ref[:, cols] = m_ref[h] + jnp.log2(l)

        comm_finish(cins, couts, sems)

    qspec = pl.BlockSpec((tq, wb), lambda hp, t: (pair(t)[0], hp))
    kspec = pl.BlockSpec((tk, wb), lambda hp, t: (pair(t)[1], hp))
    in_specs = [qspec, kspec, kspec]
    args = [q, k, v]
    if has_bias:
        bias = _mask_bias(mode, tq)
        in_specs.append(pl.BlockSpec(bias.shape, lambda hp, t: (0, 0, 0)))
        args.append(bias)
    out_shape = [jax.ShapeDtypeStruct((sq, w), BF16), jax.ShapeDtypeStruct((sq, w), F32)]
    out_specs = [qspec, qspec]
    if comm is not None:
        in_specs += [_ANY] * n_cin
        args += list(comm.inputs)
        out_shape += list(comm.out_shapes)
        out_specs += [_ANY] * n_cout
    res = pl.pallas_call(
        body,
        out_shape=tuple(out_shape),
        grid=grid,
        in_specs=in_specs,
        out_specs=tuple(out_specs),
        scratch_shapes=[pltpu.VMEM((hb, tq, HP), F32), pltpu.VMEM((hb, tq, HP), F32), pltpu.VMEM((hb, tq, HP), F32)] + comm_sems,
        compiler_params=_cp("parallel", "arbitrary") if comm is None else _cp("arbitrary", "arbitrary"),
        name=name,
    )(*args)
    return res[0], res[1], list(res[2:])


def _flash_bwd(q, k, v, o, do, lse, mode, name, comm=None):
    sq, sk, w, tq, tk, nq, nk, band = _flash_dims(q, k, mode)
    hb = HEADS_PER_STEP
    wb = hb * HP
    nch = tk // 128
    has_bias = mode != "full"
    starts, n_pairs = _pair_starts(mode, nq, band, key_major=True)
    pair = functools.partial(_pair_at, starts=starts, mode=mode, band=band, key_major=True)
    grid = (w // wb, n_pairs)
    n_cin, n_cout, comm_sems, comm_start, comm_finish = _carried(comm, grid)

    def body(*refs):
        ins, cins, (dq_ref, dk_ref, dv_ref), couts, (delta_ref,), sems = _split_refs(refs, 6 + has_bias, n_cin, 3, n_cout, 1)
        q_ref, k_ref, v_ref, o_ref, do_ref, lse_ref = ins[:6]
        b_ref = ins[6] if has_bias else None
        qi, kj = pair(pl.program_id(1))
        rows = pl.ds(pl.multiple_of(qi * tq, tq), tq)
        comm_start(cins, couts, sems)

        @pl.when(pl.program_id(1) == 0)
        def _():
            dq_ref[...] = jnp.zeros_like(dq_ref)

        @pl.when(qi == (0 if mode == "full" else kj))
        def _():
            dk_ref[...] = jnp.zeros_like(dk_ref)
            dv_ref[...] = jnp.zeros_like(dv_ref)

        @pl.when(kj == _first_key(mode, qi, band))
        def _():
            for h in range(hb):
                cols = slice(h * HP, (h + 1) * HP)
                dl = jnp.sum(do_ref[:, cols].astype(F32) * o_ref[:, cols].astype(F32), axis=-1, keepdims=True)
                delta_ref[h, rows, :] = jnp.broadcast_to(dl, (tq, HP))

        def step(bias_tile):
            for h in range(hb):
                cols = slice(h * HP, (h + 1) * HP)
                qv, kv, dov = q_ref[:, cols], k_ref[:, cols], do_ref[:, cols]
                sc = lax.dot_general(qv, kv, _DIMS["nt"], preferred_element_type=F32)
                if bias_tile is not None:
                    sc = sc + bias_tile()
                dp = lax.dot_general(dov, v_ref[:, cols], _DIMS["nt"], preferred_element_type=F32)
                lse_b = lse_ref[:, cols]
                dlt = delta_ref[h, rows, :]
                pch, dsch = [], []
                for c in range(nch):
                    lanes = slice(c * 128, (c + 1) * 128)
                    pc = jnp.exp2(sc[:, lanes] - lse_b)
                    pch.append(pc.astype(BF16))
                    dsch.append((pc * (dp[:, lanes] - dlt)).astype(BF16))
                p = pch[0] if nch == 1 else jnp.concatenate(pch, axis=1)
                ds = dsch[0] if nch == 1 else jnp.concatenate(dsch, axis=1)
                dv_ref[:, cols] += lax.dot_general(p, dov, _DIMS["tn"], preferred_element_type=F32)
                dk_ref[:, cols] += lax.dot_general(ds, qv, _DIMS["tn"], preferred_element_type=F32)
                dq_ref[rows, cols] += jnp.dot(ds, kv, preferred_element_type=F32)

        if mode == "full":
            step(None)
        elif mode == "causal":
            pl.when(qi == kj)(lambda: step(lambda: b_ref[0]))
            pl.when(qi > kj)(lambda: step(None))
        else:
            step(lambda: b_ref[qi - kj])

        comm_finish(cins, couts, sems)

    qspec = pl.BlockSpec((tq, wb), lambda hp, t: (pair(t)[0], hp))
    kspec = pl.BlockSpec((tk, wb), lambda hp, t: (pair(t)[1], hp))
    in_specs = [qspec, kspec, kspec, qspec, qspec, qspec]
    args = [q, k, v, o, do, lse]
    if has_bias:
        bias = _mask_bias(mode, tq)
        in_specs.append(pl.BlockSpec(bias.shape, lambda hp, t: (0, 0, 0)))
        args.append(bias)
    out_shape = [jax.ShapeDtypeStruct((sq, w), F32), jax.ShapeDtypeStruct((sk, w), F32), jax.ShapeDtypeStruct((sk, w), F32)]
    out_specs = [pl.BlockSpec((sq, wb), lambda hp, t: (0, hp)), kspec, kspec]
    if comm is not None:
        in_specs += [_ANY] * n_cin
        args += list(comm.inputs)
        out_shape += list(comm.out_shapes)
        out_specs += [_ANY] * n_cout
    res = pl.pallas_call(
        body,
        out_shape=tuple(out_shape),
        grid=grid,
        in_specs=in_specs,
        out_specs=tuple(out_specs),
        scratch_shapes=[pltpu.VMEM((hb, sq, HP), F32)] + comm_sems,
        compiler_params=_cp("parallel", "arbitrary") if comm is None else _cp("arbitrary", "arbitrary"),
        name=name,
    )(*args)
    return res[0], res[1], res[2], list(res[3:])


def _conv_rc(s):
    return _tile(s, 256, 8)


def _shift_down(x, prev8, nrows):
    rows = lax.broadcasted_iota(jnp.int32, x.shape, 0)
    out = pltpu.roll(x, nrows, 0)
    for i in range(nrows):
        out = jnp.where(rows == i, prev8[8 - nrows + i:8 - nrows + i + 1, :], out)
    return out


def _shift_up(x, next8, nrows):
    n = x.shape[0]
    rows = lax.broadcasted_iota(jnp.int32, x.shape, 0)
    out = pltpu.roll(x, n - nrows, 0)
    for i in range(nrows):
        out = jnp.where(rows == n - nrows + i, next8[i:i + 1, :], out)
    return out


def _conv_taps(uv_ref, ug_ref, r, rc):
    r0 = pl.multiple_of(r * rc, rc)
    x = jnp.concatenate([uv_ref[pl.ds(r0, rc), :], ug_ref[pl.ds(r0, rc), :]], axis=1)
    p0 = pl.multiple_of(jnp.maximum(r0 - 8, 0), 8)
    prev8 = jnp.where(r > 0, jnp.concatenate([uv_ref[pl.ds(p0, 8), :], ug_ref[pl.ds(p0, 8), :]], axis=1), 0.0)
    return r0, x, _shift_down(x, prev8, 1), _shift_down(x, prev8, 2)


def _conv_specs(s, tf, nf):
    strip = pl.BlockSpec((s, tf), lambda j: (0, j))
    return strip, [pl.BlockSpec((3, tf), lambda j: (0, j)), pl.BlockSpec((3, tf), lambda j: (0, j + nf)),
                   pl.BlockSpec((1, tf), lambda j: (0, j)), pl.BlockSpec((1, tf), lambda j: (0, j + nf))]


def _conv_params(wv_ref, wg_ref, bv_ref, bg_ref):
    w = jnp.concatenate([wv_ref[...], wg_ref[...]], axis=1)
    return w[0:1, :], w[1:2, :], w[2:3, :], jnp.concatenate([bv_ref[...], bg_ref[...]], axis=1)


def _conv_fwd(u_val, u_gate, cw, cb, name):
    s, f = u_val.shape
    tf = CONV_TF
    nf = f // tf
    rc = _conv_rc(s)

    def body(uv_ref, ug_ref, wv_ref, wg_ref, bv_ref, bg_ref, a_ref):
        w0, w1, w2, b = _conv_params(wv_ref, wg_ref, bv_ref, bg_ref)

        def chunk(r, carry):
            r0, x, x1, x2 = _conv_taps(uv_ref, ug_ref, r, rc)
            c = b + w0 * x2 + w1 * x1 + w2 * x
            val, gate = c[:, :tf], c[:, tf:]
            a_ref[pl.ds(r0, rc), :] = (gate * jax.nn.sigmoid(gate) * val).astype(a_ref.dtype)
            return carry

        lax.fori_loop(0, s // rc, chunk, 0)

    strip, params = _conv_specs(s, tf, nf)
    return pl.pallas_call(
        body,
        out_shape=jax.ShapeDtypeStruct((s, f), BF16),
        grid=(nf,),
        in_specs=[strip, strip] + params,
        out_specs=strip,
        compiler_params=_cp("parallel"),
        name=name,
    )(u_val, u_gate, cw, cw, cb, cb)


def _conv_bwd(u_val, u_gate, da, cw, cb, name):
    s, f = u_val.shape
    tf = CONV_TF
    nf = f // tf
    rc = _conv_rc(s)
    nchunk = s // rc

    def body(uv_ref, ug_ref, da_ref, wv_ref, wg_ref, bv_ref, bg_ref, duv_ref, dug_ref, dwv_ref, dwg_ref, dbv_ref, dbg_ref, next_ref):
        w0, w1, w2, b = _conv_params(wv_ref, wg_ref, bv_ref, bg_ref)
        next_ref[...] = jnp.zeros_like(next_ref)

        def chunk(it, carry):
            g0, g1, g2, gb = carry
            r = nchunk - 1 - it
            r0, x, x1, x2 = _conv_taps(uv_ref, ug_ref, r, rc)
            c = b + w0 * x2 + w1 * x1 + w2 * x
            val, gate = c[:, :tf], c[:, tf:]
            sg = jax.nn.sigmoid(gate)
            dav = da_ref[pl.ds(r0, rc), :]
            dc = jnp.concatenate([dav * (gate * sg), dav * val * (sg * (1.0 + gate * (1.0 - sg)))], axis=1)
            nxt = next_ref[...]
            du = (w2 * dc + w1 * _shift_up(dc, nxt, 1) + w0 * _shift_up(dc, nxt, 2)).astype(duv_ref.dtype)
            duv_ref[pl.ds(r0, rc), :] = du[:, :tf]
            dug_ref[pl.ds(r0, rc), :] = du[:, tf:]
            next_ref[...] = dc[0:8, :]
            return (g0 + jnp.sum(dc * x2, axis=0, keepdims=True), g1 + jnp.sum(dc * x1, axis=0, keepdims=True),
                    g2 + jnp.sum(dc * x, axis=0, keepdims=True), gb + jnp.sum(dc, axis=0, keepdims=True))

        zero = jnp.zeros((1, 2 * tf), F32)
        g0, g1, g2, gb = lax.fori_loop(0, nchunk, chunk, (zero, zero, zero, zero))
        for i, gi in enumerate((g0, g1, g2)):
            dwv_ref[i:i + 1, :] = gi[:, :tf]
            dwg_ref[i:i + 1, :] = gi[:, tf:]
        dbv_ref[...] = gb[:, :tf]
        dbg_ref[...] = gb[:, tf:]

    strip, params = _conv_specs(s, tf, nf)
    taps = pl.BlockSpec((3, tf), lambda j: (0, j))
    bias = pl.BlockSpec((1, tf), lambda j: (0, j))
    act = jax.ShapeDtypeStruct((s, f), BF16)
    return pl.pallas_call(
        body,
        out_shape=(act, act, jax.ShapeDtypeStruct((3, f), F32), jax.ShapeDtypeStruct((3, f), F32),
                   jax.ShapeDtypeStruct((1, f), F32), jax.ShapeDtypeStruct((1, f), F32)),
        grid=(nf,),
        in_specs=[strip, strip, strip] + params,
        out_specs=(strip, strip, taps, taps, bias, bias),
        scratch_shapes=[pltpu.VMEM((8, 2 * tf), F32)],
        compiler_params=_cp("parallel"),
        name=name,
    )(u_val, u_gate, da, cw, cw, cb, cb)


def _adamw(w, g, m, v, name):
    r, c = w.shape
    tr = _tile(r, 256, 8) if r % 8 == 0 else r
    c1 = 1.0 - ADAM_B1 ** ADAM_STEP
    c2 = 1.0 - ADAM_B2 ** ADAM_STEP

    def body(w_ref, g_ref, m_ref, v_ref, d_ref, mo_ref, vo_ref):
        gv = g_ref[...]
        mn = ADAM_B1 * m_ref[...] + (1.0 - ADAM_B1) * gv
        vn = ADAM_B2 * v_ref[...] + (1.0 - ADAM_B2) * (gv * gv)
        d_ref[...] = -ADAM_LR * ((mn / c1) / (jnp.sqrt(vn / c2) + ADAM_EPS) + ADAM_WD * w_ref[...])
        mo_ref[...] = mn
        vo_ref[...] = vn

    blk = pl.BlockSpec((tr, c), lambda i: (i, 0))
    shp = jax.ShapeDtypeStruct((r, c), F32)
    return pl.pallas_call(
        body, out_shape=(shp, shp, shp), grid=(r // tr,), in_specs=[blk] * 4, out_specs=(blk,) * 3,
        compiler_params=_cp("parallel"), name=name,
    )(w, g, m, v)


def _sum_rows(parts, out_dtype, name):
    n, r, c = parts.shape
    tr = _tile(r, 256, 8)

    def body(p_ref, o_ref):
        tot = p_ref[0].astype(F32)
        for i in range(1, n):
            tot = tot + p_ref[i].astype(F32)
        o_ref[...] = tot.astype(o_ref.dtype)

    return pl.pallas_call(
        body, out_shape=jax.ShapeDtypeStruct((r, c), out_dtype), grid=(r // tr,),
        in_specs=[pl.BlockSpec((n, tr, c), lambda i: (0, i, 0))], out_specs=pl.BlockSpec((tr, c), lambda i: (i, 0)),
        compiler_params=_cp("parallel"), name=name,
    )(parts)


_ANY = pl.BlockSpec(memory_space=pl.ANY)


def _place():
    return lax.axis_index("x"), lax.axis_index("y"), lax.axis_index("c")


def _other_chips(x, y):
    return [(1 - x, y), (x, 1 - y), (1 - x, 1 - y)]


def _rows_half(ref, c, axis):
    rh = ref.shape[axis] // 2
    idx = [slice(None)] * len(ref.shape)
    idx[axis] = pl.ds(pl.multiple_of(c * rh, 16), rh)
    return ref.at[tuple(idx)]


def _remote(src, dst, send_sems, recv_sems, kk, to):
    return pltpu.make_async_remote_copy(src_ref=src, dst_ref=dst, send_sem=send_sems.at[kk], recv_sem=recv_sems.at[kk],
                                        device_id=to, device_id_type=MESH)


class _GatherSpec:
    def __init__(self, shards):
        self.inputs = list(shards)
        self.out_shapes = [jax.ShapeDtypeStruct((N_CHIPS,) + s.shape, s.dtype) for s in shards]
        self.n_sems = 6 * len(shards)

    def _sends(self, ins, outs, send_sems, recv_sems):
        x, y, c = _place()
        me = 2 * x + y
        return [_remote(_rows_half(ins[a], c, 0), _rows_half(outs[a].at[me], c, 0), send_sems, recv_sems, 6 * a + j, (cx, cy, c))
                for a in range(len(ins)) for j, (cx, cy) in enumerate(_other_chips(x, y))]

    def _forwards(self, outs, send_sems, recv_sems):
        x, y, c = _place()
        return [_remote(_rows_half(outs[a].at[2 * cx + cy], c, 0), _rows_half(outs[a].at[2 * cx + cy], c, 0), send_sems, recv_sems,
                        6 * a + 3 + j, (x, y, 1 - c))
                for a in range(len(outs)) for j, (cx, cy) in enumerate(_other_chips(x, y))]

    def start(self, ins, outs, send_sems, recv_sems):
        for cp in self._sends(ins, outs, send_sems, recv_sems):
            cp.start()

    def finish(self, ins, outs, send_sems, recv_sems):
        x, y, c = _place()
        chips = _other_chips(x, y)
        forwards = self._forwards(outs, send_sems, recv_sems)
        for a in range(len(outs)):
            for j, (cx, cy) in enumerate(chips):
                slot = _rows_half(outs[a].at[2 * cx + cy], c, 0)
                _remote(slot, slot, send_sems, recv_sems, 6 * a + j, (x, y, c)).wait_recv()
                forwards[3 * a + j].start()
        for a in range(len(outs)):
            for j, (cx, cy) in enumerate(chips):
                slot = _rows_half(outs[a].at[2 * cx + cy], 1 - c, 0)
                _remote(slot, slot, send_sems, recv_sems, 6 * a + 3 + j, (x, y, c)).wait_recv()
        for cp in self._sends(ins, outs, send_sems, recv_sems) + forwards:
            cp.wait_send()


_FLIPS = [(dx, dy, dc) for dx in (0, 1) for dy in (0, 1) for dc in (0, 1) if (dx, dy, dc) != (0, 0, 0)]


class _ScatterSpec:
    def __init__(self, parts):
        self.inputs = list(parts)
        self.out_shapes = [jax.ShapeDtypeStruct((8, p.shape[1] // 2, p.shape[2]), p.dtype) for p in parts]
        self.n_sems = 7 * len(parts)

    def _sends(self, ins, outs, send_sems, recv_sems):
        x, y, c = _place()
        me = 4 * x + 2 * y + c
        copies = []
        for a in range(len(ins)):
            for k, (dx, dy, dc) in enumerate(_FLIPS):
                tx, ty, tc = x ^ dx, y ^ dy, c ^ dc
                copies.append(_remote(_rows_half(ins[a].at[2 * tx + ty], tc, 0), outs[a].at[me], send_sems, recv_sems,
                                      7 * a + k, (tx, ty, tc)))
        return copies

    def start(self, ins, outs, send_sems, recv_sems):
        for cp in self._sends(ins, outs, send_sems, recv_sems):
            cp.start()

    def finish(self, ins, outs, send_sems, recv_sems):
        x, y, c = _place()
        for a in range(len(outs)):
            for k, (dx, dy, dc) in enumerate(_FLIPS):
                slot = outs[a].at[4 * (x ^ dx) + 2 * (y ^ dy) + (c ^ dc)]
                _remote(slot, slot, send_sems, recv_sems, 7 * a + k, (x, y, c)).wait_recv()
        for cp in self._sends(ins, outs, send_sems, recv_sems):
            cp.wait_send()


def _run_comm(comm, name):
    n_in, n_out = len(comm.inputs), len(comm.out_shapes)

    def body(*refs):
        ins, outs, sems = refs[:n_in], refs[n_in:n_in + n_out], refs[n_in + n_out:]
        comm.start(ins, outs, *sems)
        comm.finish(ins, outs, *sems)

    return list(pl.pallas_call(
        body, out_shape=tuple(comm.out_shapes), in_specs=[_ANY] * n_in, out_specs=(_ANY,) * n_out,
        scratch_shapes=[pltpu.SemaphoreType.DMA((comm.n_sems,)), pltpu.SemaphoreType.DMA((comm.n_sems,))], name=name,
    )(*comm.inputs))


def _join_halves(rs, name):
    n = len(rs)

    def body(*refs):
        in_refs, out_refs = refs[:n], refs[n:2 * n]
        send_sems, recv_sems = refs[2 * n:]
        x, y, c = _place()
        copies = [_remote(in_refs[a], out_refs[a], send_sems, recv_sems, a, (x, y, 1 - c)) for a in range(n)]
        for cp in copies:
            cp.start()
        for cp in copies:
            cp.wait()

    return pl.pallas_call(
        body,
        out_shape=tuple(jax.ShapeDtypeStruct(r.shape, r.dtype) for r in rs),
        in_specs=[_ANY] * n, out_specs=(_ANY,) * n,
        scratch_shapes=[pltpu.SemaphoreType.DMA((n,)), pltpu.SemaphoreType.DMA((n,))], name=name,
    )(*rs)


def _gather_all(small, name):
    r, w = small.shape
    flips = [(dx, dy, dc) for dx in (0, 1) for dy in (0, 1) for dc in (0, 1) if (dx, dy, dc) != (0, 0, 0)]

    def body(in_ref, out_ref, send_sems, recv_sems, local_sem):
        x, y, c = _place()
        me = 4 * x + 2 * y + c
        mine = pltpu.make_async_copy(in_ref, out_ref.at[me], local_sem)
        mine.start()
        sends = []
        for j, (dx, dy, dc) in enumerate(flips):
            to = (x ^ dx, y ^ dy, c ^ dc)
            cp = pltpu.make_async_remote_copy(src_ref=in_ref, dst_ref=out_ref.at[me], send_sem=send_sems.at[j],
                                              recv_sem=recv_sems.at[j], device_id=to, device_id_type=MESH)
            cp.start()
            sends.append(cp)
        for j, (dx, dy, dc) in enumerate(flips):
            slot = out_ref.at[4 * (x ^ dx) + 2 * (y ^ dy) + (c ^ dc)]
            pltpu.make_async_remote_copy(src_ref=slot, dst_ref=slot, send_sem=send_sems.at[j], recv_sem=recv_sems.at[j],
                                         device_id=(x, y, c), device_id_type=MESH).wait_recv()
        for cp in sends:
            cp.wait_send()
        mine.wait()

    return pl.pallas_call(
        body, out_shape=jax.ShapeDtypeStruct((8, r, w), small.dtype), in_specs=[_ANY], out_specs=_ANY,
        scratch_shapes=[pltpu.SemaphoreType.DMA((7,)), pltpu.SemaphoreType.DMA((7,)), pltpu.SemaphoreType.DMA], name=name,
    )(small)


_BIG = ("w_in_fox", "w_in_dil", "w_mem_kv", "w_out", "w_up", "w_down")
_CONVW_SHARD = (DEPTH, 3, 2 * D_FF // N_CHIPS)

_SMALL = (("norm_mix", (DEPTH, D_MODEL)), ("norm_mem", (DEPTH, D_MODEL)), ("norm_ffn", (DEPTH, D_MODEL)),
          ("conv_b", (DEPTH, 2 * D_FF)), ("norm_final", (D_MODEL,)), ("b_forget", (2, N_MIX)), ("conv_w", (DEPTH, 3, 2 * D_FF)))


def _pack_small(vals, spec):
    flat = jnp.concatenate([vals[n].reshape(-1).astype(F32) for n, _ in spec])
    rows = -(-flat.shape[0] // (8 * 128)) * 8
    return jnp.pad(flat, (0, rows * 128 - flat.shape[0])).reshape(rows, 128)


def _unpack_small(buf, spec):
    flat, out, off = buf.reshape(-1), {}, 0
    for n, shp in spec:
        k = int(np.prod(shp))
        out[n] = flat[off:off + k].reshape(shp)
        off += k
    return out


def _fox_cols_to_kernel(w):
    qkv, f, qm = w[:, :3 * D_MIX], w[:, 3 * D_MIX:3 * D_MIX + N_MIX], w[:, 3 * D_MIX + N_MIX:]
    return jnp.concatenate([qkv, qm, f, jnp.zeros((w.shape[0], 128 - N_MIX), w.dtype)], axis=1)


def _fox_cols_from_kernel(w):
    qkv, qm, f = w[:, :3 * D_MIX], w[:, 3 * D_MIX:3 * D_MIX + D_MEMQ], w[:, 3 * D_MIX + D_MEMQ:3 * D_MIX + D_MEMQ + N_MIX]
    return jnp.concatenate([qkv, f, qm], axis=1)


def _rope_pair_tables(s):
    inv = 1.0 / (ROPE_THETA ** (jnp.arange(0, HEAD_DIM, 2, dtype=F32) / HEAD_DIM))
    ang = jnp.arange(s, dtype=F32)[:, None] * inv[None, :]
    cos, sin = jnp.cos(ang), jnp.sin(ang)
    return jnp.concatenate([cos, cos, cos, cos], axis=1), jnp.concatenate([-sin, sin, -sin, sin], axis=1)


def _local_step(x, mem, target, small, ex):
    s = x.shape[0]
    cos2, sin2 = _rope_pair_tables(s)
    saved = []
    h = x
    keys = [("w_in", 0), ("w_mem_kv", 0)]
    spec = ex.gather_spec(keys)
    if spec is not None:
        ex.put_gathered(keys, _run_comm(spec, "gather_first"))
    for l in range(DEPTH):
        fox = l % 2 == 0
        slot = l // 2
        tag = f"L{l}"
        g_mix, g_mem, g_ffn = (small[n][l:l + 1] for n in ("norm_mix", "norm_mem", "norm_ffn"))
        xn = _rms_fwd(h, g_mix, f"rms_mix_{tag}")
        proj = _matmul(xn, ex.weight(("w_in", l)), "nn", F32, f"mm_in_{tag}")
        if fox:
            bfg = jnp.pad(small["b_forget"][slot:slot + 1], ((0, 0), (0, 128 - N_MIX)))
            qp, kp, vp, qmp = _prep_fox_fwd(proj, bfg, f"prep_fox_{tag}")
        else:
            bfg = None
            qp, kp, vp, qmp = _prep_dil_fwd(proj, cos2, sin2, f"prep_dil_{tag}")
        mn = _rms_fwd(mem, g_mem, f"rms_mem_{tag}")
        kvm = _matmul(mn, ex.weight(("w_mem_kv", l)), "nn", F32, f"mm_memkv_{tag}")
        kmp, vmp = _pad_heads([(kvm, 0, N_MEM), (kvm, D_MEMQ, N_MEM)], (1.0, 1.0), (N_MEM * HP,) * 2, f"pad_memkv_{tag}")
        mode = "causal" if fox else "dilated"
        keys = [("w_out", l), ("w_up", l), ("w_down", l)] + ([("w_in", l + 1), ("w_mem_kv", l + 1)] if l + 1 < DEPTH else [])
        spec = ex.gather_spec(keys)
        o_mix, lse_mix, got = _flash_fwd(qp, kp, vp, mode, f"flash_{mode}_fwd_{tag}", comm=spec)
        if spec is not None:
            ex.put_gathered(keys, got)
        o_mem, lse_mem, _ = _flash_fwd(qmp, kmp, vmp, "full", f"flash_mem_fwd_{tag}")
        heads = _unpad_heads([o_mix, o_mem], (1.0, 1.0), BF16, f"unpad_heads_{tag}")
        h_mid = _matmul(heads, ex.weight(("w_out", l)), "nn", F32, f"mm_out_{tag}", residual=h)
        xn2 = _rms_fwd(h_mid, g_ffn, f"rms_ffn_{tag}")
        w_val, w_gate = ex.weight(("w_up", l))
        u_val = _matmul(xn2, w_val, "nn", F32, f"mm_up_val_{tag}")
        u_gate = _matmul(xn2, w_gate, "nn", F32, f"mm_up_gate_{tag}")
        act = _conv_fwd(u_val, u_gate, *ex.conv(l), f"conv_fwd_{tag}")
        h_out = _matmul(act, ex.weight(("w_down", l)), "nn", F32, f"mm_down_{tag}", residual=h_mid)
        saved.append(dict(h=h, xn=xn, proj=proj, bfg=bfg, qp=qp, kp=kp, vp=vp, qmp=qmp, mn=mn, kmp=kmp, vmp=vmp, o_mix=o_mix,
                          lse_mix=lse_mix, o_mem=o_mem, lse_mem=lse_mem, heads=heads, h_mid=h_mid, xn2=xn2, u_val=u_val, u_gate=u_gate, act=act))
        h = h_out

    loss_blk, dh, dg_final = _loss_head(h, small["norm_final"].reshape(1, D_MODEL), target, "loss_head")

    grads = {k: [None] * DEPTH for k in ("conv_w", "conv_b", "norm_mix", "norm_mem", "norm_ffn")}
    grads["b_forget"] = [None, None]
    waiting = []
    for l in reversed(range(DEPTH)):
        fox = l % 2 == 0
        tag = f"L{l}"
        sv = saved[l]
        g_mix, g_mem, g_ffn = (small[n][l:l + 1] for n in ("norm_mix", "norm_mem", "norm_ffn"))
        da = _matmul(dh, ex.weight(("w_down", l)), "nt", F32, f"mm_da_{tag}")
        g_down = _matmul(sv["act"], dh, "tn", BF16, f"mm_dwdown_{tag}")
        du_val, du_gate, dcw_v, dcw_g, dcb_v, dcb_g = _conv_bwd(sv["u_val"], sv["u_gate"], da, *ex.conv(l), f"conv_bwd_{tag}")
        grads["conv_w"][l] = jnp.concatenate([dcw_v, dcw_g], axis=1)
        grads["conv_b"][l] = jnp.concatenate([dcb_v, dcb_g], axis=1)
        w_val, w_gate = ex.weight(("w_up", l))
        dxn2 = _matmul(du_val, w_val, "nt", F32, f"mm_dxn2_val_{tag}")
        dxn2 = _matmul(du_gate, w_gate, "nt", F32, f"mm_dxn2_gate_{tag}", residual=dxn2)
        g_up = (_matmul(sv["xn2"], du_val, "tn", BF16, f"mm_dwup_val_{tag}"), _matmul(sv["xn2"], du_gate, "tn", BF16, f"mm_dwup_gate_{tag}"))
        dh_mid, grads["norm_ffn"][l] = _rms_bwd(sv["h_mid"], g_ffn, dxn2, dh, f"rms_ffn_bwd_{tag}")
        dheads = _matmul(dh_mid, ex.weight(("w_out", l)), "nt", F32, f"mm_dheads_{tag}")
        g_out = _matmul(sv["heads"], dh_mid, "tn", BF16, f"mm_dwout_{tag}")
        do_mix, do_mem = _pad_heads([(dheads, 0, N_MIX), (dheads, D_MIX, N_MEM)], (1.0, 1.0), (N_MIX * HP, N_MEM * HP),
                                    f"pad_dheads_{tag}")
        mode = "causal" if fox else "dilated"
        items = waiting + [(("w_down", l), g_down), (("w_up", l), g_up), (("w_out", l), g_out)]
        spec = ex.scatter_spec(items)
        dqp, dkp, dvp, got = _flash_bwd(sv["qp"], sv["kp"], sv["vp"], sv["o_mix"], do_mix, sv["lse_mix"], mode, f"flash_{mode}_bwd_{tag}",
                                        comm=spec)
        if spec is not None:
            ex.put_received(items, spec, got)
        dqmp, dkmp, dvmp, _ = _flash_bwd(sv["qmp"], sv["kmp"], sv["vmp"], sv["o_mem"], do_mem, sv["lse_mem"], "full",
                                         f"flash_mem_bwd_{tag}")
        if fox:
            dproj, dbf = _prep_fox_bwd(dqp, dkp, dvp, dqmp, sv["proj"], sv["bfg"], f"prep_fox_bwd_{tag}")
            grads["b_forget"][l // 2] = dbf[0, :N_MIX]
        else:
            dproj = _prep_dil_bwd(dqp, dkp, dvp, dqmp, cos2, sin2, f"prep_dil_bwd_{tag}")
        dkvm = _unpad_heads([dkmp, dvmp], (LN2, 1.0), BF16, f"unpad_dkvm_{tag}")
        g_memkv = _matmul(sv["mn"], dkvm, "tn", BF16, f"mm_dwmemkv_{tag}")
        dmn = _matmul(dkvm, ex.weight(("w_mem_kv", l)), "nt", F32, f"mm_dmn_{tag}")
        _, grads["norm_mem"][l] = _rms_bwd(mem, g_mem, dmn, None, f"rms_mem_bwd_{tag}")
        dxn = _matmul(dproj, ex.weight(("w_in", l)), "nt", F32, f"mm_dxn_{tag}")
        g_in = _matmul(sv["xn"], dproj, "tn", BF16, f"mm_dwin_{tag}")
        waiting = [(("w_mem_kv", l), g_memkv), (("w_in", l), g_in)]
        dh, grads["norm_mix"][l] = _rms_bwd(sv["h"], g_mix, dxn, dh_mid, f"rms_mix_bwd_{tag}")
    spec = ex.scatter_spec(waiting)
    if spec is not None:
        ex.put_received(waiting, spec, _run_comm(spec, "grad_scatter_last"))
    grads["norm_final"] = dg_final
    return loss_blk, dh, grads


class _Exchange:
    def __init__(self, own, conv_w_full, conv_b, chip, core):
        self.own, self.conv_w_full, self.conv_b, self.chip, self.core = own, conv_w_full, conv_b, chip, core
        self.full, self.recv = {}, {}

    def _shard(self, key):
        name, l = key
        if name == "w_in":
            return self.own["w_in_fox" if l % 2 == 0 else "w_in_dil"][l // 2]
        return self.own[name][l]

    def gather_spec(self, keys):
        return _GatherSpec([self._shard(k) for k in keys])

    def put_gathered(self, keys, outs):
        for key, o in zip(keys, outs):
            name, l = key
            g = lax.dynamic_update_slice_in_dim(o, self._shard(key)[None], self.chip, axis=0)
            if name == "w_in":
                w = jnp.concatenate([g[j] for j in range(N_CHIPS)], axis=1)
                self.full[key] = _fox_cols_to_kernel(w) if l % 2 == 0 else w
            elif name == "w_up":
                self.full[key] = (jnp.concatenate([g[0], g[1]], axis=1), jnp.concatenate([g[2], g[3]], axis=1))
            else:
                self.full[key] = g.reshape(N_CHIPS * g.shape[1], g.shape[2])

    def weight(self, key):
        return self.full[key]

    def conv(self, l):
        return self.conv_w_full[l], self.conv_b[l:l + 1]

    def scatter_spec(self, items):
        parts = []
        for (name, l), g in items:
            if name == "w_in":
                g = _fox_cols_from_kernel(g) if l % 2 == 0 else g
                parts.append(jnp.stack(jnp.split(g, N_CHIPS, axis=1)))
            elif name == "w_up":
                parts.append(jnp.stack(jnp.split(g[0], 2, axis=1) + jnp.split(g[1], 2, axis=1)))
            else:
                parts.append(g.reshape(N_CHIPS, g.shape[0] // N_CHIPS, g.shape[1]))
        return _ScatterSpec(parts)

    def put_received(self, items, spec, outs):
        for (key, _), part, o in zip(items, spec.inputs, outs):
            rh = o.shape[1]
            mine = lax.dynamic_slice_in_dim(lax.dynamic_index_in_dim(part, self.chip, 0, keepdims=False), self.core * rh, rh, axis=0)
            self.recv[key] = lax.dynamic_update_slice_in_dim(o, mine[None], 2 * self.chip + self.core, axis=0)


def kernel(x, mem, norm_mix, norm_mem, norm_ffn, w_in_fox, b_forget, w_in_dil, w_mem_kv, w_out, w_up, conv_w, conv_b, w_down, norm_final, loss_target, m_norm_mix, m_norm_mem, m_norm_ffn, m_w_in_fox, m_b_forget, m_w_in_dil, m_w_mem_kv, m_w_out, m_w_up, m_conv_w, m_conv_b, m_w_down, m_norm_final, v_norm_mix, v_norm_mem, v_norm_ffn, v_w_in_fox, v_b_forget, v_w_in_dil, v_w_mem_kv, v_w_out, v_w_up, v_conv_w, v_conv_b, v_w_down, v_norm_final):
    w_sh = dict(w_in_fox=w_in_fox, w_in_dil=w_in_dil, w_mem_kv=w_mem_kv, w_out=w_out, w_up=w_up, w_down=w_down, conv_w=conv_w)
    m_sh = dict(w_in_fox=m_w_in_fox, w_in_dil=m_w_in_dil, w_mem_kv=m_w_mem_kv, w_out=m_w_out, w_up=m_w_up, w_down=m_w_down, conv_w=m_conv_w)
    v_sh = dict(w_in_fox=v_w_in_fox, w_in_dil=v_w_in_dil, w_mem_kv=v_w_mem_kv, w_out=v_w_out, w_up=v_w_up, w_down=v_w_down, conv_w=v_conv_w)
    small = dict(norm_mix=norm_mix, norm_mem=norm_mem, norm_ffn=norm_ffn, conv_b=conv_b, norm_final=norm_final, b_forget=b_forget)
    m_small = dict(norm_mix=m_norm_mix, norm_mem=m_norm_mem, norm_ffn=m_norm_ffn, conv_b=m_conv_b, norm_final=m_norm_final, b_forget=m_b_forget)
    v_small = dict(norm_mix=v_norm_mix, norm_mem=v_norm_mem, norm_ffn=v_norm_ffn, conv_b=v_conv_b, norm_final=v_norm_final, b_forget=v_b_forget)
    chip = 2 * lax.axis_index("x") + lax.axis_index("y")
    core = lax.axis_index("c")

    conv_spec = (("conv_w", _CONVW_SHARD),)
    conv_all = _gather_all(_pack_small(dict(conv_w=conv_w), conv_spec), "gather_conv_w")
    conv_w_full = jnp.concatenate([_unpack_small(conv_all[2 * j], conv_spec)["conv_w"] for j in range(N_CHIPS)], axis=-1)
    ex = _Exchange({n: w_sh[n].astype(BF16) for n in _BIG}, conv_w_full, conv_b, chip, core)

    loss_blk, dx, grads = _local_step(x[0], mem[0], loss_target[0], small, ex)
    loss = lax.psum(loss_blk[0, 0], ("x", "y", "c"))

    layer_keys = {"w_in_fox": [("w_in", 0), ("w_in", 2)], "w_in_dil": [("w_in", 1), ("w_in", 3)]}
    keys = [k for n in _BIG for k in layer_keys.get(n, [(n, l) for l in range(DEPTH)])]
    halves = [_sum_rows(ex.recv[k], F32, f"grad_sum_{k[0]}_L{k[1]}") for k in keys]
    whole = {k: jnp.concatenate([jnp.where(core == 0, mine, other), jnp.where(core == 0, other, mine)], axis=0)
             for k, mine, other in zip(keys, halves, _join_halves(halves, "grad_join_halves"))}
    g_big = {n: jnp.stack([whole[k] for k in layer_keys.get(n, [(n, l) for l in range(DEPTH)])]) for n in _BIG}

    g_small_local = dict(
        norm_mix=jnp.concatenate(grads["norm_mix"]), norm_mem=jnp.concatenate(grads["norm_mem"]),
        norm_ffn=jnp.concatenate(grads["norm_ffn"]),
        conv_b=jnp.concatenate(grads["conv_b"]),
        norm_final=grads["norm_final"], b_forget=jnp.stack(grads["b_forget"]),
        conv_w=jnp.stack(grads["conv_w"]),
    )
    small_all = _gather_all(_pack_small(g_small_local, _SMALL), "small_gather_all")
    g_small = _unpack_small(_sum_rows(small_all, F32, "small_sum"), _SMALL)
    ncol = 2 * D_FF // N_CHIPS
    g_big["conv_w"] = lax.dynamic_slice_in_dim(g_small["conv_w"], chip * ncol, ncol, axis=2)

    out_g, out_d, out_m, out_v = {}, {}, {}, {}
    for n in _BIG + ("conv_w",):
        shp = w_sh[n].shape
        two_d = (-1, shp[-1])
        d, mo, vo = _adamw(w_sh[n].reshape(two_d), g_big[n].reshape(two_d), m_sh[n].reshape(two_d), v_sh[n].reshape(two_d), f"adamw_{n}")
        out_g[n], out_d[n], out_m[n], out_v[n] = g_big[n].reshape(shp), d.reshape(shp), mo.reshape(shp), vo.reshape(shp)
    spec = _SMALL[:-1]
    d, mo, vo = _adamw(_pack_small(small, spec), _pack_small(g_small, spec), _pack_small(m_small, spec), _pack_small(v_small, spec),
                       "adamw_small")
    d, mo, vo = _unpack_small(d, spec), _unpack_small(mo, spec), _unpack_small(vo, spec)
    for n, shp in spec:
        out_g[n], out_d[n], out_m[n], out_v[n] = g_small[n].reshape(shp), d[n], mo[n], vo[n]

    order = ("norm_mix", "norm_mem", "norm_ffn", "w_in_fox", "b_forget", "w_in_dil", "w_mem_kv", "w_out", "w_up", "conv_w", "conv_b",
             "w_down", "norm_final")
    return (loss, dx[None], *[out_g[n] for n in order], *[out_d[n] for n in order], *[out_m[n] for n in order],
            *[out_v[n] for n in order])
```

```python
import functools
import math

import numpy as np
import jax
import jax.numpy as jnp
from jax import lax
from jax.experimental import pallas as pl
from jax.experimental.pallas import tpu as pltpu

F32 = jnp.float32
BF16 = jnp.bfloat16

D_MODEL = 1024
DEPTH = 4
HEAD_DIM = 64
N_MIX = 12
N_MEM = 4
D_MIX = N_MIX * HEAD_DIM
D_MEMQ = N_MEM * HEAD_DIM
D_FF = 2816
FOX_IN = 3 * D_MIX + N_MIX + D_MEMQ
DIL_IN = 3 * D_MIX + D_MEMQ
HP = 128
SCALE = HEAD_DIM ** -0.5
NEG = -1e30
NORM_EPS = 1e-6
DIL_MAX = 2048
LOG2E = 1.0 / math.log(2.0)
LN2 = math.log(2.0)
QSCALE = SCALE * LOG2E
HEADS_PER_STEP = 2
FWD_HEADS_PER_STEP = 4
ROPE_THETA = 10000.0
N_CHIPS = 4
CONV_TF = 128
FLASH_BLK = 512
VMEM_LIMIT = 48 * 1024 * 1024

ADAM_LR = 0.001
ADAM_B1 = 0.9
ADAM_B2 = 0.999
ADAM_EPS = 1e-08
ADAM_WD = 0.01
ADAM_STEP = 10

MESH = pl.DeviceIdType.MESH


def _cp(*sem):
    return pltpu.CompilerParams(dimension_semantics=tuple(sem), vmem_limit_bytes=VMEM_LIMIT)


def _tile(n, cap, mult=128):
    if n <= cap:
        return n
    t = (cap // mult) * mult
    while t >= mult:
        if n % t == 0:
            return t
        t -= mult
    raise ValueError(f"no tile for {n} under {cap}")


_DIMS = {"nn": (((1,), (0,)), ((), ())), "nt": (((1,), (1,)), ((), ())), "tn": (((0,), (0,)), ((), ()))}


def _matmul(a, b, mode, out_dtype, name, residual=None):
    if mode == "nn":
        (m, k), n = a.shape, b.shape[1]
    elif mode == "nt":
        (m, k), n = a.shape, b.shape[0]
    else:
        (k, m), n = a.shape, b.shape[1]
    tn = 1408 if n % 1408 == 0 else _tile(n, 1024)
    tm = 1408 if (m % 1408 == 0 and tn <= 1024) else _tile(m, 1024)
    tk = _tile(k, 1408)
    nk = k // tk
    dims = _DIMS[mode]
    has_res = residual is not None

    def body(*refs):
        if has_res:
            a_ref, b_ref, r_ref, o_ref = refs[:4]
        else:
            a_ref, b_ref, o_ref = refs[:3]
        part = lax.dot_general(a_ref[...].astype(BF16), b_ref[...].astype(BF16), dims, preferred_element_type=F32)
        if nk == 1:
            if has_res:
                part = part + r_ref[...]
            o_ref[...] = part.astype(o_ref.dtype)
            return
        acc_ref = refs[-1]
        kk = pl.program_id(2)

        @pl.when(kk == 0)
        def _():
            acc_ref[...] = part

        @pl.when(kk > 0)
        def _():
            acc_ref[...] += part

        @pl.when(kk == nk - 1)
        def _():
            tot = acc_ref[...]
            if has_res:
                tot = tot + r_ref[...]
            o_ref[...] = tot.astype(o_ref.dtype)

    if mode == "nn":
        a_spec = pl.BlockSpec((tm, tk), lambda i, j, kk: (i, kk))
        b_spec = pl.BlockSpec((tk, tn), lambda i, j, kk: (kk, j))
    elif mode == "nt":
        a_spec = pl.BlockSpec((tm, tk), lambda i, j, kk: (i, kk))
        b_spec = pl.BlockSpec((tn, tk), lambda i, j, kk: (j, kk))
    else:
        a_spec = pl.BlockSpec((tk, tm), lambda i, j, kk: (kk, i))
        b_spec = pl.BlockSpec((tk, tn), lambda i, j, kk: (kk, j))
    in_specs = [a_spec, b_spec]
    args = [a, b]
    if has_res:
        in_specs.append(pl.BlockSpec((tm, tn), lambda i, j, kk: (i, j)))
        args.append(residual)
    return pl.pallas_call(
        body,
        out_shape=jax.ShapeDtypeStruct((m, n), out_dtype),
        grid=(m // tm, n // tn, nk),
        in_specs=in_specs,
        out_specs=pl.BlockSpec((tm, tn), lambda i, j, kk: (i, j)),
        scratch_shapes=[pltpu.VMEM((tm, tn), F32)] if nk > 1 else [],
        compiler_params=_cp("parallel", "parallel", "arbitrary"),
        name=name,
    )(*args)


def _rms_fwd(h, g, name):
    r, d = h.shape
    tr = _tile(r, 512, 8)

    def body(h_ref, g_ref, o_ref):
        x = h_ref[...]
        rstd = lax.rsqrt(jnp.mean(x * x, axis=-1, keepdims=True) + NORM_EPS)
        o_ref[...] = ((x * rstd) * g_ref[...]).astype(o_ref.dtype)

    return pl.pallas_call(
        body,
        out_shape=jax.ShapeDtypeStruct((r, d), BF16),
        grid=(r // tr,),
        in_specs=[pl.BlockSpec((tr, d), lambda i: (i, 0)), pl.BlockSpec((1, d), lambda i: (0, 0))],
        out_specs=pl.BlockSpec((tr, d), lambda i: (i, 0)),
        compiler_params=_cp("parallel"),
        name=name,
    )(h, g)


def _rms_bwd(h, g, dy, dres, name):
    r, d = h.shape
    tr = _tile(r, 512, 8)
    need_dh = dres is not None

    def body(*refs):
        if need_dh:
            h_ref, g_ref, dy_ref, dres_ref, dh_ref, dg_ref = refs
        else:
            h_ref, g_ref, dy_ref, dg_ref = refs
        i = pl.program_id(0)
        x = h_ref[...]
        rstd = lax.rsqrt(jnp.mean(x * x, axis=-1, keepdims=True) + NORM_EPS)
        nrm = x * rstd
        dyv = dy_ref[...].astype(F32)
        part = jnp.sum(dyv * nrm, axis=0, keepdims=True)

        @pl.when(i == 0)
        def _():
            dg_ref[...] = part

        @pl.when(i > 0)
        def _():
            dg_ref[...] += part

        if need_dh:
            gy = dyv * g_ref[...]
            dx = rstd * (gy - nrm * jnp.mean(gy * nrm, axis=-1, keepdims=True))
            dh_ref[...] = dres_ref[...] + dx

    row = pl.BlockSpec((tr, d), lambda i: (i, 0))
    vec = pl.BlockSpec((1, d), lambda i: (0, 0))
    if need_dh:
        return pl.pallas_call(
            body,
            out_shape=(jax.ShapeDtypeStruct((r, d), F32), jax.ShapeDtypeStruct((1, d), F32)),
            grid=(r // tr,),
            in_specs=[row, vec, row, row],
            out_specs=(row, vec),
            compiler_params=_cp("arbitrary"),
            name=name,
        )(h, g, dy, dres)
    return None, pl.pallas_call(
        body,
        out_shape=jax.ShapeDtypeStruct((1, d), F32),
        grid=(r // tr,),
        in_specs=[row, vec, row],
        out_specs=vec,
        compiler_params=_cp("arbitrary"),
        name=name,
    )(h, g, dy)


def _loss_head(h, g, target, name):
    r, d = h.shape
    tr = _tile(r, 512, 8)

    def body(h_ref, g_ref, t_ref, loss_ref, dh_ref, dg_ref):
        i = pl.program_id(0)
        x = h_ref[...]
        gv = g_ref[...]
        rstd = lax.rsqrt(jnp.mean(x * x, axis=-1, keepdims=True) + NORM_EPS)
        nrm = x * rstd
        err = nrm * gv - t_ref[...]
        lpart = 0.5 * jnp.sum(jnp.mean(err * err, axis=-1, keepdims=True), axis=0, keepdims=True)
        dyv = err * (1.0 / d)
        gpart = jnp.sum(dyv * nrm, axis=0, keepdims=True)

        @pl.when(i == 0)
        def _():
            loss_ref[...] = jnp.broadcast_to(lpart, loss_ref.shape)
            dg_ref[...] = gpart

        @pl.when(i > 0)
        def _():
            loss_ref[...] += jnp.broadcast_to(lpart, loss_ref.shape)
            dg_ref[...] += gpart

        gy = dyv * gv
        dh_ref[...] = rstd * (gy - nrm * jnp.mean(gy * nrm, axis=-1, keepdims=True))

    row = pl.BlockSpec((tr, d), lambda i: (i, 0))
    vec = pl.BlockSpec((1, d), lambda i: (0, 0))
    lsp = pl.BlockSpec((1, 128), lambda i: (0, 0))
    return pl.pallas_call(
        body,
        out_shape=(jax.ShapeDtypeStruct((1, 128), F32), jax.ShapeDtypeStruct((r, d), F32), jax.ShapeDtypeStruct((1, d), F32)),
        grid=(r // tr,),
        in_specs=[row, vec, row],
        out_specs=(lsp, row, vec),
        compiler_params=_cp("arbitrary"),
        name=name,
    )(h, g, target)


def _lane(shape):
    return lax.broadcasted_iota(jnp.int32, shape, 1)


def _head_tile(pair_tile, odd):
    return pltpu.roll(pair_tile, 64, 1) if odd else pair_tile


def _pair_tile(even_tile, odd_tile):
    lane = _lane(even_tile.shape)
    return jnp.where(lane < 64, even_tile, pltpu.roll(odd_tile, 64, 1))


def _pad_heads(xs, scales, out_widths, name):
    r = xs[0][0].shape[0]
    tr = _tile(r, 256, 8)
    n_in = len(xs)

    def body(*refs):
        for idx in range(n_in):
            x_ref, o_ref = refs[idx], refs[n_in + idx]
            nh = xs[idx][2]
            for p in range(nh // 2):
                t = x_ref[:, p * 128:(p + 1) * 128].astype(F32) * scales[idx]
                lane = _lane(t.shape)
                o_ref[:, (2 * p) * HP:(2 * p + 1) * HP] = jnp.where(lane < 64, t, 0.0).astype(o_ref.dtype)
                o_ref[:, (2 * p + 1) * HP:(2 * p + 2) * HP] = jnp.where(lane < 64, pltpu.roll(t, 64, 1), 0.0).astype(o_ref.dtype)

    in_specs, args, out_specs, out_shape = [], [], [], []
    for (arr, c0, nh), w in zip(xs, out_widths):
        wcols = nh * 64
        assert c0 % wcols == 0 or c0 == 0
        blk = c0 // wcols if wcols else 0
        in_specs.append(pl.BlockSpec((tr, wcols), functools.partial(lambda i, b: (i, b), b=blk)))
        args.append(arr)
        out_specs.append(pl.BlockSpec((tr, w), lambda i: (i, 0)))
        out_shape.append(jax.ShapeDtypeStruct((r, w), BF16))
    return pl.pallas_call(
        body,
        out_shape=tuple(out_shape),
        grid=(r // tr,),
        in_specs=in_specs,
        out_specs=tuple(out_specs),
        compiler_params=_cp("parallel"),
        name=name,
    )(*args)


def _unpad_heads(xs, scales, out_dtype, name):
    r = xs[0].shape[0]
    tr = _tile(r, 256, 8)
    nhs = [x.shape[1] // HP for x in xs]
    total = sum(nhs) * 64

    def body(*refs):
        o_ref = refs[-1]
        col = 0
        for x_ref, nh, sc in zip(refs[:-1], nhs, scales):
            for p in range(nh // 2):
                ev = x_ref[:, (2 * p) * HP:(2 * p + 1) * HP].astype(F32)
                od = x_ref[:, (2 * p + 1) * HP:(2 * p + 2) * HP].astype(F32)
                o_ref[:, col:col + 128] = (_pair_tile(ev, od) * sc).astype(o_ref.dtype)
                col += 128

    return pl.pallas_call(
        body,
        out_shape=jax.ShapeDtypeStruct((r, total), out_dtype),
        grid=(r // tr,),
        in_specs=[pl.BlockSpec((tr, x.shape[1]), lambda i: (i, 0)) for x in xs],
        out_specs=pl.BlockSpec((tr, total), lambda i: (i, 0)),
        compiler_params=_cp("parallel"),
        name=name,
    )(*xs)


def _bf16_split3(c):
    hi = c.astype(BF16).astype(F32)
    r1 = c - hi
    mid = r1.astype(BF16).astype(F32)
    lo = (r1 - mid).astype(BF16).astype(F32)
    return hi, mid, lo


def _log_sigmoid(z):
    return jnp.minimum(z, 0.0) - jnp.log(1.0 + jnp.exp(-jnp.abs(z)))


def _prep_fox_fwd(proj, bfg, name):
    s = proj.shape[0]
    ts = _tile(s, 256, 8)
    fcol = 3 * D_MIX + D_MEMQ

    def body(p_ref, b_ref, q_ref, k_ref, v_ref, qm_ref, carry_ref):
        i = pl.program_id(0)

        @pl.when(i == 0)
        def _():
            carry_ref[...] = jnp.zeros_like(carry_ref)

        lane = _lane((ts, 128))
        z = p_ref[:, fcol:fcol + 128] + b_ref[...]
        logf = jnp.where(lane < N_MIX, _log_sigmoid(z), 0.0)
        rr = lax.broadcasted_iota(jnp.int32, (ts, ts), 0)
        cc = lax.broadcasted_iota(jnp.int32, (ts, ts), 1)
        tri = jnp.where(cc <= rr, 1.0, 0.0).astype(F32)
        c = jnp.dot(tri, logf, preferred_element_type=F32, precision=lax.Precision.HIGHEST) + carry_ref[0:1, :]
        carry_ref[...] = jnp.broadcast_to(c[ts - 1:ts, :], carry_ref.shape)
        for hh in range(N_MIX):
            p, odd = hh // 2, hh % 2
            ch = jnp.sum(jnp.where(lane == hh, c, 0.0), axis=-1, keepdims=True) * LOG2E
            hi, mid, lo = _bf16_split3(ch)
            qt = _head_tile(p_ref[:, p * 128:(p + 1) * 128], odd) * QSCALE
            kt = _head_tile(p_ref[:, D_MIX + p * 128:D_MIX + (p + 1) * 128], odd)
            vt = _head_tile(p_ref[:, 2 * D_MIX + p * 128:2 * D_MIX + (p + 1) * 128], odd)
            qa = jnp.where(lane == 64, hi, jnp.where(lane == 65, mid, jnp.where(lane == 66, lo, jnp.where(lane < 70, 1.0, 0.0))))
            ka = jnp.where(lane < 67, 1.0, jnp.where(lane == 67, -hi, jnp.where(lane == 68, -mid, jnp.where(lane == 69, -lo, 0.0))))
            q_ref[:, hh * HP:(hh + 1) * HP] = jnp.where(lane < 64, qt, qa).astype(BF16)
            k_ref[:, hh * HP:(hh + 1) * HP] = jnp.where(lane < 64, kt, ka).astype(BF16)
            v_ref[:, hh * HP:(hh + 1) * HP] = jnp.where(lane < 64, vt, 0.0).astype(BF16)
        for hh in range(N_MEM):
            p, odd = hh // 2, hh % 2
            t = _head_tile(p_ref[:, 3 * D_MIX + p * 128:3 * D_MIX + (p + 1) * 128], odd) * QSCALE
            qm_ref[:, hh * HP:(hh + 1) * HP] = jnp.where(lane < 64, t, 0.0).astype(BF16)

    wmix, wmem = N_MIX * HP, N_MEM * HP
    return pl.pallas_call(
        body,
        out_shape=(jax.ShapeDtypeStruct((s, wmix), BF16),) * 3 + (jax.ShapeDtypeStruct((s, wmem), BF16),),
        grid=(s // ts,),
        in_specs=[pl.BlockSpec((ts, proj.shape[1]), lambda i: (i, 0)), pl.BlockSpec((1, 128), lambda i: (0, 0))],
        out_specs=(pl.BlockSpec((ts, wmix), lambda i: (i, 0)),) * 3 + (pl.BlockSpec((ts, wmem), lambda i: (i, 0)),),
        scratch_shapes=[pltpu.VMEM((8, 128), F32)],
        compiler_params=_cp("arbitrary"),
        name=name,
    )(proj, bfg)


def _prep_fox_bwd(dq, dk, dv, dqm, proj, bfg, name):
    s = proj.shape[0]
    ts = _tile(s, 256, 8)
    nb = s // ts
    fcol = 3 * D_MIX + D_MEMQ

    def body(dq_ref, dk_ref, dv_ref, dqm_ref, p_ref, b_ref, o_ref, db_ref, carry_ref):
        i = pl.program_id(0)

        @pl.when(i == 0)
        def _():
            carry_ref[...] = jnp.zeros_like(carry_ref)
            db_ref[...] = jnp.zeros_like(db_ref)

        lane = _lane((ts, 128))
        dc = jnp.zeros((ts, 128), F32)
        for p in range(N_MIX // 2):
            tq, tk, tv = [], [], []
            for odd in (0, 1):
                hh = 2 * p + odd
                dqt = dq_ref[:, hh * HP:(hh + 1) * HP]
                dkt = dk_ref[:, hh * HP:(hh + 1) * HP]
                col = jnp.sum(jnp.where(lane == 64, dqt, 0.0) - jnp.where(lane == 67, dkt, 0.0), axis=-1, keepdims=True)
                dc = dc + jnp.where(lane == hh, col, 0.0)
                tq.append(dqt)
                tk.append(dkt)
                tv.append(dv_ref[:, hh * HP:(hh + 1) * HP])
            o_ref[:, p * 128:(p + 1) * 128] = (_pair_tile(tq[0], tq[1]) * SCALE).astype(o_ref.dtype)
            o_ref[:, D_MIX + p * 128:D_MIX + (p + 1) * 128] = (_pair_tile(tk[0], tk[1]) * LN2).astype(o_ref.dtype)
            o_ref[:, 2 * D_MIX + p * 128:2 * D_MIX + (p + 1) * 128] = _pair_tile(tv[0], tv[1]).astype(o_ref.dtype)
        for p in range(N_MEM // 2):
            ev = dqm_ref[:, (2 * p) * HP:(2 * p + 1) * HP]
            od = dqm_ref[:, (2 * p + 1) * HP:(2 * p + 2) * HP]
            o_ref[:, 3 * D_MIX + p * 128:3 * D_MIX + (p + 1) * 128] = (_pair_tile(ev, od) * SCALE).astype(o_ref.dtype)
        rr = lax.broadcasted_iota(jnp.int32, (ts, ts), 0)
        cc = lax.broadcasted_iota(jnp.int32, (ts, ts), 1)
        triu = jnp.where(cc >= rr, 1.0, 0.0).astype(F32)
        dlogf = jnp.dot(triu, dc, preferred_element_type=F32, precision=lax.Precision.HIGHEST) + carry_ref[0:1, :]
        carry_ref[...] = jnp.broadcast_to(dlogf[0:1, :], carry_ref.shape)
        z = p_ref[:, fcol:fcol + 128] + b_ref[...]
        dz = jnp.where(lane < N_MIX, dlogf / (1.0 + jnp.exp(z)), 0.0)
        o_ref[:, fcol:fcol + 128] = dz.astype(o_ref.dtype)
        db_ref[...] += jnp.sum(dz, axis=0, keepdims=True)

    wmix, wmem = N_MIX * HP, N_MEM * HP
    rev = lambda i: (nb - 1 - i, 0)
    return pl.pallas_call(
        body,
        out_shape=(jax.ShapeDtypeStruct(proj.shape, BF16), jax.ShapeDtypeStruct((1, 128), F32)),
        grid=(nb,),
        in_specs=[pl.BlockSpec((ts, wmix), rev)] * 3 + [pl.BlockSpec((ts, wmem), rev), pl.BlockSpec((ts, proj.shape[1]), rev),
                                                         pl.BlockSpec((1, 128), lambda i: (0, 0))],
        out_specs=(pl.BlockSpec((ts, proj.shape[1]), rev), pl.BlockSpec((1, 128), lambda i: (0, 0))),
        scratch_shapes=[pltpu.VMEM((8, 128), F32)],
        compiler_params=_cp("arbitrary"),
        name=name,
    )(dq, dk, dv, dqm, proj, bfg)


def _rope_partner(x):
    lane = _lane(x.shape)
    return jnp.where((lane % 64) < 32, pltpu.roll(x, 96, 1), pltpu.roll(x, 32, 1))


def _prep_dil_fwd(proj, cos2, sin2, name):
    s = proj.shape[0]
    ts = _tile(s, 256, 8)

    def body(p_ref, c_ref, s_ref, q_ref, k_ref, v_ref, qm_ref):
        lane = _lane((ts, 128))
        cosv, sinv = c_ref[...], s_ref[...]
        for p in range(N_MIX // 2):
            xq = p_ref[:, p * 128:(p + 1) * 128]
            xk = p_ref[:, D_MIX + p * 128:D_MIX + (p + 1) * 128]
            xv = p_ref[:, 2 * D_MIX + p * 128:2 * D_MIX + (p + 1) * 128]
            yq = (xq * cosv + _rope_partner(xq) * sinv) * QSCALE
            yk = xk * cosv + _rope_partner(xk) * sinv
            for odd in (0, 1):
                hh = 2 * p + odd
                q_ref[:, hh * HP:(hh + 1) * HP] = jnp.where(lane < 64, _head_tile(yq, odd), 0.0).astype(BF16)
                k_ref[:, hh * HP:(hh + 1) * HP] = jnp.where(lane < 64, _head_tile(yk, odd), 0.0).astype(BF16)
                v_ref[:, hh * HP:(hh + 1) * HP] = jnp.where(lane < 64, _head_tile(xv, odd), 0.0).astype(BF16)
        for p in range(N_MEM // 2):
            t = p_ref[:, 3 * D_MIX + p * 128:3 * D_MIX + (p + 1) * 128] * QSCALE
            for odd in (0, 1):
                hh = 2 * p + odd
                qm_ref[:, hh * HP:(hh + 1) * HP] = jnp.where(lane < 64, _head_tile(t, odd), 0.0).astype(BF16)

    wmix, wmem = N_MIX * HP, N_MEM * HP
    return pl.pallas_call(
        body,
        out_shape=(jax.ShapeDtypeStruct((s, wmix), BF16),) * 3 + (jax.ShapeDtypeStruct((s, wmem), BF16),),
        grid=(s // ts,),
        in_specs=[pl.BlockSpec((ts, proj.shape[1]), lambda i: (i, 0)), pl.BlockSpec((ts, 128), lambda i: (i, 0)),
                  pl.BlockSpec((ts, 128), lambda i: (i, 0))],
        out_specs=(pl.BlockSpec((ts, wmix), lambda i: (i, 0)),) * 3 + (pl.BlockSpec((ts, wmem), lambda i: (i, 0)),),
        compiler_params=_cp("parallel"),
        name=name,
    )(proj, cos2, sin2)


def _prep_dil_bwd(dq, dk, dv, dqm, cos2, sin2, name):
    s = dq.shape[0]
    ts = _tile(s, 256, 8)

    def body(dq_ref, dk_ref, dv_ref, dqm_ref, c_ref, s_ref, o_ref):
        cosv, sinv = c_ref[...], s_ref[...]
        for p in range(N_MIX // 2):
            e, o = 2 * p, 2 * p + 1
            dyq = _pair_tile(dq_ref[:, e * HP:(e + 1) * HP], dq_ref[:, o * HP:(o + 1) * HP]) * SCALE
            dyk = _pair_tile(dk_ref[:, e * HP:(e + 1) * HP], dk_ref[:, o * HP:(o + 1) * HP]) * LN2
            dxv = _pair_tile(dv_ref[:, e * HP:(e + 1) * HP], dv_ref[:, o * HP:(o + 1) * HP])
            o_ref[:, p * 128:(p + 1) * 128] = (dyq * cosv - _rope_partner(dyq) * sinv).astype(o_ref.dtype)
            o_ref[:, D_MIX + p * 128:D_MIX + (p + 1) * 128] = (dyk * cosv - _rope_partner(dyk) * sinv).astype(o_ref.dtype)
            o_ref[:, 2 * D_MIX + p * 128:2 * D_MIX + (p + 1) * 128] = dxv.astype(o_ref.dtype)
        for p in range(N_MEM // 2):
            e, o = 2 * p, 2 * p + 1
            t = _pair_tile(dqm_ref[:, e * HP:(e + 1) * HP], dqm_ref[:, o * HP:(o + 1) * HP]) * SCALE
            o_ref[:, 3 * D_MIX + p * 128:3 * D_MIX + (p + 1) * 128] = t.astype(o_ref.dtype)

    wmix, wmem = N_MIX * HP, N_MEM * HP
    row = lambda w: pl.BlockSpec((ts, w), lambda i: (i, 0))
    return pl.pallas_call(
        body,
        out_shape=jax.ShapeDtypeStruct((s, DIL_IN), BF16),
        grid=(s // ts,),
        in_specs=[row(wmix)] * 3 + [row(wmem), row(128), row(128)],
        out_specs=row(DIL_IN),
        compiler_params=_cp("parallel"),
        name=name,
    )(dq, dk, dv, dqm, cos2, sin2)


def _mask_bias(mode, blk):
    n = 1 if mode == "causal" else DIL_MAX // blk + 1
    idx = jnp.arange(blk, dtype=jnp.int32)
    dist = jnp.arange(n, dtype=jnp.int32)[:, None, None] * blk + idx[None, :, None] - idx[None, None, :]
    if mode == "causal":
        return jnp.where(dist >= 0, 0.0, NEG).astype(F32)
    cnt = ((dist <= 128).astype(jnp.int32) + ((dist <= 512) & (dist % 4 == 0)).astype(jnp.int32)
           + ((dist <= DIL_MAX) & (dist % 16 == 0)).astype(jnp.int32))
    bias = jnp.where(cnt == 3, math.log2(3.0), jnp.where(cnt == 2, 1.0, 0.0))
    return jnp.where((dist >= 0) & (cnt > 0), bias, NEG).astype(F32)


def _flash_dims(q, k, mode):
    sq, w = q.shape
    sk = k.shape[0]
    tq = _tile(sq, FLASH_BLK, 8)
    tk = sk if mode == "full" else tq
    band = DIL_MAX // tk if mode == "dilated" else None
    return sq, sk, w, tq, tk, sq // tq, sk // tk, band


def _pair_starts(mode, nq, band, key_major):
    if mode == "full":
        counts = [nq] if key_major else [1] * nq
    elif key_major:
        counts = [(min(kj + band, nq - 1) if mode == "dilated" else nq - 1) - kj + 1 for kj in range(nq)]
    else:
        counts = [qi - (max(qi - band, 0) if mode == "dilated" else 0) + 1 for qi in range(nq)]
    starts = [int(v) for v in np.cumsum([0] + counts)]
    return starts[:-1], starts[-1]


def _pair_at(t, starts, mode, band, key_major):
    if mode == "full":
        return t, jnp.zeros((), jnp.int32)
    major, begin = 0, 0
    for prev, s in zip(starts[:-1], starts[1:]):
        reached = t >= s
        major = major + reached.astype(jnp.int32)
        begin = begin + jnp.where(reached, s - prev, 0)
    if key_major:
        return major + (t - begin), major
    return major, _first_key(mode, major, band) + (t - begin)


def _first_key(mode, qi, band):
    return jnp.maximum(qi - band, 0) if mode == "dilated" else 0


def _carried(comm, grid):
    if comm is None:
        return 0, 0, [], (lambda *a: None), (lambda *a: None)

    def edge(last):
        cond = None
        for d, n in enumerate(grid):
            c = pl.program_id(d) == (n - 1 if last else 0)
            cond = c if cond is None else cond & c
        return cond

    def start(cins, couts, sems):
        pl.when(edge(False))(lambda: comm.start(cins, couts, *sems))

    def finish(cins, couts, sems):
        pl.when(edge(True))(lambda: comm.finish(cins, couts, *sems))

    sems = [pltpu.SemaphoreType.DMA((comm.n_sems,)), pltpu.SemaphoreType.DMA((comm.n_sems,))]
    return len(comm.inputs), len(comm.out_shapes), sems, start, finish


def _split_refs(refs, n_in, n_cin, n_out, n_cout, n_scratch):
    bounds = np.cumsum([0, n_in, n_cin, n_out, n_cout, n_scratch])
    return [refs[a:b] for a, b in zip(bounds[:-1], bounds[1:])] + [refs[bounds[-1]:]]


def _flash_fwd(q, k, v, mode, name, comm=None):
    sq, sk, w, tq, tk, nq, nk, band = _flash_dims(q, k, mode)
    hb = FWD_HEADS_PER_STEP
    wb = hb * HP
    nch = tk // 128
    has_bias = mode != "full"
    starts, n_pairs = _pair_starts(mode, nq, band, key_major=False)
    pair = functools.partial(_pair_at, starts=starts, mode=mode, band=band, key_major=False)
    grid = (w // wb, n_pairs)
    n_cin, n_cout, comm_sems, comm_start, comm_finish = _carried(comm, grid)

    def body(*refs):
        ins, cins, (o_ref, lse_ref), couts, (m_ref, l_ref, acc_ref), sems = _split_refs(refs, 3 + has_bias, n_cin, 2, n_cout, 3)
        q_ref, k_ref, v_ref = ins[:3]
        b_ref = ins[3] if has_bias else None
        qi, kj = pair(pl.program_id(1))
        comm_start(cins, couts, sems)

        @pl.when(kj == _first_key(mode, qi, band))
        def _():
            m_ref[...] = jnp.full_like(m_ref, -jnp.inf)
            l_ref[...] = jnp.zeros_like(l_ref)
            acc_ref[...] = jnp.zeros_like(acc_ref)

        def step(bias_tile):
            for h in range(hb):
                cols = slice(h * HP, (h + 1) * HP)
                sc = lax.dot_general(q_ref[:, cols], k_ref[:, cols], _DIMS["nt"], preferred_element_type=F32)
                if bias_tile is not None:
                    sc = sc + bias_tile()
                m_prev = m_ref[h]
                m_new = jnp.maximum(m_prev, jnp.max(sc, axis=-1, keepdims=True))
                alpha = jnp.exp2(m_prev - m_new)
                psum, chunks = None, []
                for c in range(nch):
                    pc = jnp.exp2(sc[:, c * 128:(c + 1) * 128] - m_new)
                    psum = pc if psum is None else psum + pc
                    chunks.append(pc.astype(BF16))
                p = chunks[0] if nch == 1 else jnp.concatenate(chunks, axis=1)
                l_ref[h] = alpha * l_ref[h] + psum
                acc_ref[h] = alpha * acc_ref[h] + jnp.dot(p, v_ref[:, cols], preferred_element_type=F32)
                m_ref[h] = m_new

        if mode == "full":
            step(None)
        elif mode == "causal":
            pl.when(kj == qi)(lambda: step(lambda: b_ref[0]))
            pl.when(kj < qi)(lambda: step(None))
        else:
            step(lambda: b_ref[qi - kj])

        @pl.when(kj == (0 if mode == "full" else qi))
        def _():
            for h in range(hb):
                cols = slice(h * HP, (h + 1) * HP)
                l = jnp.sum(l_ref[h], axis=-1, keepdims=True)
                o_ref[:, cols] = (acc_ref[h] / l).astype(o_ref.dtype)
                lse_ref[:, cols] = m_ref[h] + jnp.log2(l)

        comm_finish(cins, couts, sems)

    qspec = pl.BlockSpec((tq, wb), lambda hp, t: (pair(t)[0], hp))
    kspec = pl.BlockSpec((tk, wb), lambda hp, t: (pair(t)[1], hp))
    in_specs = [qspec, kspec, kspec]
    args = [q, k, v]
    if has_bias:
        bias = _mask_bias(mode, tq)
        in_specs.append(pl.BlockSpec(bias.shape, lambda hp, t: (0, 0, 0)))
        args.append(bias)
    out_shape = [jax.ShapeDtypeStruct((sq, w), BF16), jax.ShapeDtypeStruct((sq, w), F32)]
    out_specs = [qspec, qspec]
    if comm is not None:
        in_specs += [_ANY] * n_cin
        args += list(comm.inputs)
        out_shape += list(comm.out_shapes)
        out_specs += [_ANY] * n_cout
    res = pl.pallas_call(
        body,
        out_shape=tuple(out_shape),
        grid=grid,
        in_specs=in_specs,
        out_specs=tuple(out_specs),
        scratch_shapes=[pltpu.VMEM((hb, tq, HP), F32), pltpu.VMEM((hb, tq, HP), F32), pltpu.VMEM((hb, tq, HP), F32)] + comm_sems,
        compiler_params=_cp("parallel", "arbitrary") if comm is None else _cp("arbitrary", "arbitrary"),
        name=name,
    )(*args)
    return res[0], res[1], list(res[2:])


def _flash_bwd(q, k, v, o, do, lse, mode, name, comm=None):
    sq, sk, w, tq, tk, nq, nk, band = _flash_dims(q, k, mode)
    hb = HEADS_PER_STEP
    wb = hb * HP
    nch = tk // 128
    has_bias = mode != "full"
    starts, n_pairs = _pair_starts(mode, nq, band, key_major=True)
    pair = functools.partial(_pair_at, starts=starts, mode=mode, band=band, key_major=True)
    grid = (w // wb, n_pairs)
    n_cin, n_cout, comm_sems, comm_start, comm_finish = _carried(comm, grid)

    def body(*refs):
        ins, cins, (dq_ref, dk_ref, dv_ref), couts, (delta_ref,), sems = _split_refs(refs, 6 + has_bias, n_cin, 3, n_cout, 1)
        q_ref, k_ref, v_ref, o_ref, do_ref, lse_ref = ins[:6]
        b_ref = ins[6] if has_bias else None
        qi, kj = pair(pl.program_id(1))
        rows = pl.ds(pl.multiple_of(qi * tq, tq), tq)
        comm_start(cins, couts, sems)

        @pl.when(pl.program_id(1) == 0)
        def _():
            dq_ref[...] = jnp.zeros_like(dq_ref)

        @pl.when(qi == (0 if mode == "full" else kj))
        def _():
            dk_ref[...] = jnp.zeros_like(dk_ref)
            dv_ref[...] = jnp.zeros_like(dv_ref)

        @pl.when(kj == _first_key(mode, qi, band))
        def _():
            for h in range(hb):
                cols = slice(h * HP, (h + 1) * HP)
                dl = jnp.sum(do_ref[:, cols].astype(F32) * o_ref[:, cols].astype(F32), axis=-1, keepdims=True)
                delta_ref[h, rows, :] = jnp.broadcast_to(dl, (tq, HP))

        def step(bias_tile):
            for h in range(hb):
                cols = slice(h * HP, (h + 1) * HP)
                qv, kv, dov = q_ref[:, cols], k_ref[:, cols], do_ref[:, cols]
                sc = lax.dot_general(qv, kv, _DIMS["nt"], preferred_element_type=F32)
                if bias_tile is not None:
                    sc = sc + bias_tile()
                dp = lax.dot_general(dov, v_ref[:, cols], _DIMS["nt"], preferred_element_type=F32)
                lse_b = lse_ref[:, cols]
                dlt = delta_ref[h, rows, :]
                pch, dsch = [], []
                for c in range(nch):
                    lanes = slice(c * 128, (c + 1) * 128)
                    pc = jnp.exp2(sc[:, lanes] - lse_b)
                    pch.append(pc.astype(BF16))
                    dsch.append((pc * (dp[:, lanes] - dlt)).astype(BF16))
                p = pch[0] if nch == 1 else jnp.concatenate(pch, axis=1)
                ds = dsch[0] if nch == 1 else jnp.concatenate(dsch, axis=1)
                dv_ref[:, cols] += lax.dot_general(p, dov, _DIMS["tn"], preferred_element_type=F32)
                dk_ref[:, cols] += lax.dot_general(ds, qv, _DIMS["tn"], preferred_element_type=F32)
                dq_ref[rows, cols] += jnp.dot(ds, kv, preferred_element_type=F32)

        if mode == "full":
            step(None)
        elif mode == "causal":
            pl.when(qi == kj)(lambda: step(lambda: b_ref[0]))
            pl.when(qi > kj)(lambda: step(None))
        else:
            step(lambda: b_ref[qi - kj])

        comm_finish(cins, couts, sems)

    qspec = pl.BlockSpec((tq, wb), lambda hp, t: (pair(t)[0], hp))
    kspec = pl.BlockSpec((tk, wb), lambda hp, t: (pair(t)[1], hp))
    in_specs = [qspec, kspec, kspec, qspec, qspec, qspec]
    args = [q, k, v, o, do, lse]
    if has_bias:
        bias = _mask_bias(mode, tq)
        in_specs.append(pl.BlockSpec(bias.shape, lambda hp, t: (0, 0, 0)))
        args.append(bias)
    out_shape = [jax.ShapeDtypeStruct((sq, w), F32), jax.ShapeDtypeStruct((sk, w), F32), jax.ShapeDtypeStruct((sk, w), F32)]
    out_specs = [pl.BlockSpec((sq, wb), lambda hp, t: (0, hp)), kspec, kspec]
    if comm is not None:
        in_specs += [_ANY] * n_cin
        args += list(comm.inputs)
        out_shape += list(comm.out_shapes)
        out_specs += [_ANY] * n_cout
    res = pl.pallas_call(
        body,
        out_shape=tuple(out_shape),
        grid=grid,
        in_specs=in_specs,
        out_specs=tuple(out_specs),
        scratch_shapes=[pltpu.VMEM((hb, sq, HP), F32)] + comm_sems,
        compiler_params=_cp("parallel", "arbitrary") if comm is None else _cp("arbitrary", "arbitrary"),
        name=name,
    )(*args)
    return res[0], res[1], res[2], list(res[3:])


def _conv_rc(s):
    return _tile(s, 256, 8)


def _shift_down(x, prev8, nrows):
    rows = lax.broadcasted_iota(jnp.int32, x.shape, 0)
    out = pltpu.roll(x, nrows, 0)
    for i in range(nrows):
        out = jnp.where(rows == i, prev8[8 - nrows + i:8 - nrows + i + 1, :], out)
    return out


def _shift_up(x, next8, nrows):
    n = x.shape[0]
    rows = lax.broadcasted_iota(jnp.int32, x.shape, 0)
    out = pltpu.roll(x, n - nrows, 0)
    for i in range(nrows):
        out = jnp.where(rows == n - nrows + i, next8[i:i + 1, :], out)
    return out


def _conv_taps(uv_ref, ug_ref, r, rc):
    r0 = pl.multiple_of(r * rc, rc)
    x = jnp.concatenate([uv_ref[pl.ds(r0, rc), :], ug_ref[pl.ds(r0, rc), :]], axis=1)
    p0 = pl.multiple_of(jnp.maximum(r0 - 8, 0), 8)
    prev8 = jnp.where(r > 0, jnp.concatenate([uv_ref[pl.ds(p0, 8), :], ug_ref[pl.ds(p0, 8), :]], axis=1), 0.0)
    return r0, x, _shift_down(x, prev8, 1), _shift_down(x, prev8, 2)


def _conv_specs(s, tf, nf):
    strip = pl.BlockSpec((s, tf), lambda j: (0, j))
    return strip, [pl.BlockSpec((3, tf), lambda j: (0, j)), pl.BlockSpec((3, tf), lambda j: (0, j + nf)),
                   pl.BlockSpec((1, tf), lambda j: (0, j)), pl.BlockSpec((1, tf), lambda j: (0, j + nf))]


def _conv_params(wv_ref, wg_ref, bv_ref, bg_ref):
    w = jnp.concatenate([wv_ref[...], wg_ref[...]], axis=1)
    return w[0:1, :], w[1:2, :], w[2:3, :], jnp.concatenate([bv_ref[...], bg_ref[...]], axis=1)


def _conv_fwd(u_val, u_gate, cw, cb, name):
    s, f = u_val.shape
    tf = CONV_TF
    nf = f // tf
    rc = _conv_rc(s)

    def body(uv_ref, ug_ref, wv_ref, wg_ref, bv_ref, bg_ref, a_ref):
        w0, w1, w2, b = _conv_params(wv_ref, wg_ref, bv_ref, bg_ref)

        def chunk(r, carry):
            r0, x, x1, x2 = _conv_taps(uv_ref, ug_ref, r, rc)
            c = b + w0 * x2 + w1 * x1 + w2 * x
            val, gate = c[:, :tf], c[:, tf:]
            a_ref[pl.ds(r0, rc), :] = (gate * jax.nn.sigmoid(gate) * val).astype(a_ref.dtype)
            return carry

        lax.fori_loop(0, s // rc, chunk, 0)

    strip, params = _conv_specs(s, tf, nf)
    return pl.pallas_call(
        body,
        out_shape=jax.ShapeDtypeStruct((s, f), BF16),
        grid=(nf,),
        in_specs=[strip, strip] + params,
        out_specs=strip,
        compiler_params=_cp("parallel"),
        name=name,
    )(u_val, u_gate, cw, cw, cb, cb)


def _conv_bwd(u_val, u_gate, da, cw, cb, name):
    s, f = u_val.shape
    tf = CONV_TF
    nf = f // tf
    rc = _conv_rc(s)
    nchunk = s // rc

    def body(uv_ref, ug_ref, da_ref, wv_ref, wg_ref, bv_ref, bg_ref, duv_ref, dug_ref, dwv_ref, dwg_ref, dbv_ref, dbg_ref, next_ref):
        w0, w1, w2, b = _conv_params(wv_ref, wg_ref, bv_ref, bg_ref)
        next_ref[...] = jnp.zeros_like(next_ref)

        def chunk(it, carry):
            g0, g1, g2, gb = carry
            r = nchunk - 1 - it
            r0, x, x1, x2 = _conv_taps(uv_ref, ug_ref, r, rc)
            c = b + w0 * x2 + w1 * x1 + w2 * x
            val, gate = c[:, :tf], c[:, tf:]
            sg = jax.nn.sigmoid(gate)
            dav = da_ref[pl.ds(r0, rc), :]
            dc = jnp.concatenate([dav * (gate * sg), dav * val * (sg * (1.0 + gate * (1.0 - sg)))], axis=1)
            nxt = next_ref[...]
            du = (w2 * dc + w1 * _shift_up(dc, nxt, 1) + w0 * _shift_up(dc, nxt, 2)).astype(duv_ref.dtype)
            duv_ref[pl.ds(r0, rc), :] = du[:, :tf]
            dug_ref[pl.ds(r0, rc), :] = du[:, tf:]
            next_ref[...] = dc[0:8, :]
            return (g0 + jnp.sum(dc * x2, axis=0, keepdims=True), g1 + jnp.sum(dc * x1, axis=0, keepdims=True),
                    g2 + jnp.sum(dc * x, axis=0, keepdims=True), gb + jnp.sum(dc, axis=0, keepdims=True))

        zero = jnp.zeros((1, 2 * tf), F32)
        g0, g1, g2, gb = lax.fori_loop(0, nchunk, chunk, (zero, zero, zero, zero))
        for i, gi in enumerate((g0, g1, g2)):
            dwv_ref[i:i + 1, :] = gi[:, :tf]
            dwg_ref[i:i + 1, :] = gi[:, tf:]
        dbv_ref[...] = gb[:, :tf]
        dbg_ref[...] = gb[:, tf:]

    strip, params = _conv_specs(s, tf, nf)
    taps = pl.BlockSpec((3, tf), lambda j: (0, j))
    bias = pl.BlockSpec((1, tf), lambda j: (0, j))
    act = jax.ShapeDtypeStruct((s, f), BF16)
    return pl.pallas_call(
        body,
        out_shape=(act, act, jax.ShapeDtypeStruct((3, f), F32), jax.ShapeDtypeStruct((3, f), F32),
                   jax.ShapeDtypeStruct((1, f), F32), jax.ShapeDtypeStruct((1, f), F32)),
        grid=(nf,),
        in_specs=[strip, strip, strip] + params,
        out_specs=(strip, strip, taps, taps, bias, bias),
        scratch_shapes=[pltpu.VMEM((8, 2 * tf), F32)],
        compiler_params=_cp("parallel"),
        name=name,
    )(u_val, u_gate, da, cw, cw, cb, cb)


def _adamw(w, g, m, v, name):
    r, c = w.shape
    tr = _tile(r, 256, 8) if r % 8 == 0 else r
    c1 = 1.0 - ADAM_B1 ** ADAM_STEP
    c2 = 1.0 - ADAM_B2 ** ADAM_STEP

    def body(w_ref, g_ref, m_ref, v_ref, d_ref, mo_ref, vo_ref):
        gv = g_ref[...]
        mn = ADAM_B1 * m_ref[...] + (1.0 - ADAM_B1) * gv
        vn = ADAM_B2 * v_ref[...] + (1.0 - ADAM_B2) * (gv * gv)
        d_ref[...] = -ADAM_LR * ((mn / c1) / (jnp.sqrt(vn / c2) + ADAM_EPS) + ADAM_WD * w_ref[...])
        mo_ref[...] = mn
        vo_ref[...] = vn

    blk = pl.BlockSpec((tr, c), lambda i: (i, 0))
    shp = jax.ShapeDtypeStruct((r, c), F32)
    return pl.pallas_call(
        body, out_shape=(shp, shp, shp), grid=(r // tr,), in_specs=[blk] * 4, out_specs=(blk,) * 3,
        compiler_params=_cp("parallel"), name=name,
    )(w, g, m, v)


def _sum_rows(parts, out_dtype, name):
    n, r, c = parts.shape
    tr = _tile(r, 256, 8)

    def body(p_ref, o_ref):
        tot = p_ref[0].astype(F32)
        for i in range(1, n):
            tot = tot + p_ref[i].astype(F32)
        o_ref[...] = tot.astype(o_ref.dtype)

    return pl.pallas_call(
        body, out_shape=jax.ShapeDtypeStruct((r, c), out_dtype), grid=(r // tr,),
        in_specs=[pl.BlockSpec((n, tr, c), lambda i: (0, i, 0))], out_specs=pl.BlockSpec((tr, c), lambda i: (i, 0)),
        compiler_params=_cp("parallel"), name=name,
    )(parts)


_ANY = pl.BlockSpec(memory_space=pl.ANY)


def _place():
    return lax.axis_index("x"), lax.axis_index("y"), lax.axis_index("c")


def _other_chips(x, y):
    return [(1 - x, y), (x, 1 - y), (1 - x, 1 - y)]


def _rows_half(ref, c, axis):
    rh = ref.shape[axis] // 2
    idx = [slice(None)] * len(ref.shape)
    idx[axis] = pl.ds(pl.multiple_of(c * rh, 16), rh)
    return ref.at[tuple(idx)]


def _remote(src, dst, send_sems, recv_sems, kk, to):
    return pltpu.make_async_remote_copy(src_ref=src, dst_ref=dst, send_sem=send_sems.at[kk], recv_sem=recv_sems.at[kk],
                                        device_id=to, device_id_type=MESH)


class _GatherSpec:
    def __init__(self, shards):
        self.inputs = list(shards)
        self.out_shapes = [jax.ShapeDtypeStruct((N_CHIPS,) + s.shape, s.dtype) for s in shards]
        self.n_sems = 6 * len(shards)

    def _sends(self, ins, outs, send_sems, recv_sems):
        x, y, c = _place()
        me = 2 * x + y
        return [_remote(_rows_half(ins[a], c, 0), _rows_half(outs[a].at[me], c, 0), send_sems, recv_sems, 6 * a + j, (cx, cy, c))
                for a in range(len(ins)) for j, (cx, cy) in enumerate(_other_chips(x, y))]

    def _forwards(self, outs, send_sems, recv_sems):
        x, y, c = _place()
        return [_remote(_rows_half(outs[a].at[2 * cx + cy], c, 0), _rows_half(outs[a].at[2 * cx + cy], c, 0), send_sems, recv_sems,
                        6 * a + 3 + j, (x, y, 1 - c))
                for a in range(len(outs)) for j, (cx, cy) in enumerate(_other_chips(x, y))]

    def start(self, ins, outs, send_sems, recv_sems):
        for cp in self._sends(ins, outs, send_sems, recv_sems):
            cp.start()

    def finish(self, ins, outs, send_sems, recv_sems):
        x, y, c = _place()
        chips = _other_chips(x, y)
        forwards = self._forwards(outs, send_sems, recv_sems)
        for a in range(len(outs)):
            for j, (cx, cy) in enumerate(chips):
                slot = _rows_half(outs[a].at[2 * cx + cy], c, 0)
                _remote(slot, slot, send_sems, recv_sems, 6 * a + j, (x, y, c)).wait_recv()
                forwards[3 * a + j].start()
        for a in range(len(outs)):
            for j, (cx, cy) in enumerate(chips):
                slot = _rows_half(outs[a].at[2 * cx + cy], 1 - c, 0)
                _remote(slot, slot, send_sems, recv_sems, 6 * a + 3 + j, (x, y, c)).wait_recv()
        for cp in self._sends(ins, outs, send_sems, recv_sems) + forwards:
            cp.wait_send()


_FLIPS = [(dx, dy, dc) for dx in (0, 1) for dy in (0, 1) for dc in (0, 1) if (dx, dy, dc) != (0, 0, 0)]


class _ScatterSpec:
    def __init__(self, parts):
        self.inputs = list(parts)
        self.out_shapes = [jax.ShapeDtypeStruct((8, p.shape[1] // 2, p.shape[2]), p.dtype) for p in parts]
        self.n_sems = 7 * len(parts)

    def _sends(self, ins, outs, send_sems, recv_sems):
        x, y, c = _place()
        me = 4 * x + 2 * y + c
        copies = []
        for a in range(len(ins)):
            for k, (dx, dy, dc) in enumerate(_FLIPS):
                tx, ty, tc = x ^ dx, y ^ dy, c ^ dc
                copies.append(_remote(_rows_half(ins[a].at[2 * tx + ty], tc, 0), outs[a].at[me], send_sems, recv_sems,
                                      7 * a + k, (tx, ty, tc)))
        return copies

    def start(self, ins, outs, send_sems, recv_sems):
        for cp in self._sends(ins, outs, send_sems, recv_sems):
            cp.start()

    def finish(self, ins, outs, send_sems, recv_sems):
        x, y, c = _place()
        for a in range(len(outs)):
            for k, (dx, dy, dc) in enumerate(_FLIPS):
                slot = outs[a].at[4 * (x ^ dx) + 2 * (y ^ dy) + (c ^ dc)]
                _remote(slot, slot, send_sems, recv_sems, 7 * a + k, (x, y, c)).wait_recv()
        for cp in self._sends(ins, outs, send_sems, recv_sems):
            cp.wait_send()


def _run_comm(comm, name):
    n_in, n_out = len(comm.inputs), len(comm.out_shapes)

    def body(*refs):
        ins, outs, sems = refs[:n_in], refs[n_in:n_in + n_out], refs[n_in + n_out:]
        comm.start(ins, outs, *sems)
        comm.finish(ins, outs, *sems)

    return list(pl.pallas_call(
        body, out_shape=tuple(comm.out_shapes), in_specs=[_ANY] * n_in, out_specs=(_ANY,) * n_out,
        scratch_shapes=[pltpu.SemaphoreType.DMA((comm.n_sems,)), pltpu.SemaphoreType.DMA((comm.n_sems,))], name=name,
    )(*comm.inputs))


def _join_halves(rs, name):
    n = len(rs)

    def body(*refs):
        in_refs, out_refs = refs[:n], refs[n:2 * n]
        send_sems, recv_sems = refs[2 * n:]
        x, y, c = _place()
        copies = [_remote(in_refs[a], out_refs[a], send_sems, recv_sems, a, (x, y, 1 - c)) for a in range(n)]
        for cp in copies:
            cp.start()
        for cp in copies:
            cp.wait()

    return pl.pallas_call(
        body,
        out_shape=tuple(jax.ShapeDtypeStruct(r.shape, r.dtype) for r in rs),
        in_specs=[_ANY] * n, out_specs=(_ANY,) * n,
        scratch_shapes=[pltpu.SemaphoreType.DMA((n,)), pltpu.SemaphoreType.DMA((n,))], name=name,
    )(*rs)


def _gather_all(small, name):
    r, w = small.shape
    flips = [(dx, dy, dc) for dx in (0, 1) for dy in (0, 1) for dc in (0, 1) if (dx, dy, dc) != (0, 0, 0)]

    def body(in_ref, out_ref, send_sems, recv_sems, local_sem):
        x, y, c = _place()
        me = 4 * x + 2 * y + c
        mine = pltpu.make_async_copy(in_ref, out_ref.at[me], local_sem)
        mine.start()
        sends = []
        for j, (dx, dy, dc) in enumerate(flips):
            to = (x ^ dx, y ^ dy, c ^ dc)
            cp = pltpu.make_async_remote_copy(src_ref=in_ref, dst_ref=out_ref.at[me], send_sem=send_sems.at[j],
                                              recv_sem=recv_sems.at[j], device_id=to, device_id_type=MESH)
            cp.start()
            sends.append(cp)
        for j, (dx, dy, dc) in enumerate(flips):
            slot = out_ref.at[4 * (x ^ dx) + 2 * (y ^ dy) + (c ^ dc)]
            pltpu.make_async_remote_copy(src_ref=slot, dst_ref=slot, send_sem=send_sems.at[j], recv_sem=recv_sems.at[j],
                                         device_id=(x, y, c), device_id_type=MESH).wait_recv()
        for cp in sends:
            cp.wait_send()
        mine.wait()

    return pl.pallas_call(
        body, out_shape=jax.ShapeDtypeStruct((8, r, w), small.dtype), in_specs=[_ANY], out_specs=_ANY,
        scratch_shapes=[pltpu.SemaphoreType.DMA((7,)), pltpu.SemaphoreType.DMA((7,)), pltpu.SemaphoreType.DMA], name=name,
    )(small)


_BIG = ("w_in_fox", "w_in_dil", "w_mem_kv", "w_out", "w_up", "w_down")
_CONVW_SHARD = (DEPTH, 3, 2 * D_FF // N_CHIPS)

_SMALL = (("norm_mix", (DEPTH, D_MODEL)), ("norm_mem", (DEPTH, D_MODEL)), ("norm_ffn", (DEPTH, D_MODEL)),
          ("conv_b", (DEPTH, 2 * D_FF)), ("norm_final", (D_MODEL,)), ("b_forget", (2, N_MIX)), ("conv_w", (DEPTH, 3, 2 * D_FF)))


def _pack_small(vals, spec):
    flat = jnp.concatenate([vals[n].reshape(-1).astype(F32) for n, _ in spec])
    rows = -(-flat.shape[0] // (8 * 128)) * 8
    return jnp.pad(flat, (0, rows * 128 - flat.shape[0])).reshape(rows, 128)


def _unpack_small(buf, spec):
    flat, out, off = buf.reshape(-1), {}, 0
    for n, shp in spec:
        k = int(np.prod(shp))
        out[n] = flat[off:off + k].reshape(shp)
        off += k
    return out


def _fox_cols_to_kernel(w):
    qkv, f, qm = w[:, :3 * D_MIX], w[:, 3 * D_MIX:3 * D_MIX + N_MIX], w[:, 3 * D_MIX + N_MIX:]
    return jnp.concatenate([qkv, qm, f, jnp.zeros((w.shape[0], 128 - N_MIX), w.dtype)], axis=1)


def _fox_cols_from_kernel(w):
    qkv, qm, f = w[:, :3 * D_MIX], w[:, 3 * D_MIX:3 * D_MIX + D_MEMQ], w[:, 3 * D_MIX + D_MEMQ:3 * D_MIX + D_MEMQ + N_MIX]
    return jnp.concatenate([qkv, f, qm], axis=1)


def _rope_pair_tables(s):
    inv = 1.0 / (ROPE_THETA ** (jnp.arange(0, HEAD_DIM, 2, dtype=F32) / HEAD_DIM))
    ang = jnp.arange(s, dtype=F32)[:, None] * inv[None, :]
    cos, sin = jnp.cos(ang), jnp.sin(ang)
    return jnp.concatenate([cos, cos, cos, cos], axis=1), jnp.concatenate([-sin, sin, -sin, sin], axis=1)


def _local_step(x, mem, target, small, ex):
    s = x.shape[0]
    cos2, sin2 = _rope_pair_tables(s)
    saved = []
    h = x
    keys = [("w_in", 0), ("w_mem_kv", 0)]
    spec = ex.gather_spec(keys)
    if spec is not None:
        ex.put_gathered(keys, _run_comm(spec, "gather_first"))
    for l in range(DEPTH):
        fox = l % 2 == 0
        slot = l // 2
        tag = f"L{l}"
        g_mix, g_mem, g_ffn = (small[n][l:l + 1] for n in ("norm_mix", "norm_mem", "norm_ffn"))
        xn = _rms_fwd(h, g_mix, f"rms_mix_{tag}")
        proj = _matmul(xn, ex.weight(("w_in", l)), "nn", F32, f"mm_in_{tag}")
        if fox:
            bfg = jnp.pad(small["b_forget"][slot:slot + 1], ((0, 0), (0, 128 - N_MIX)))
            qp, kp, vp, qmp = _prep_fox_fwd(proj, bfg, f"prep_fox_{tag}")
        else:
            bfg = None
            qp, kp, vp, qmp = _prep_dil_fwd(proj, cos2, sin2, f"prep_dil_{tag}")
        mn = _rms_fwd(mem, g_mem, f"rms_mem_{tag}")
        kvm = _matmul(mn, ex.weight(("w_mem_kv", l)), "nn", F32, f"mm_memkv_{tag}")
        kmp, vmp = _pad_heads([(kvm, 0, N_MEM), (kvm, D_MEMQ, N_MEM)], (1.0, 1.0), (N_MEM * HP,) * 2, f"pad_memkv_{tag}")
        mode = "causal" if fox else "dilated"
        keys = [("w_out", l), ("w_up", l), ("w_down", l)] + ([("w_in", l + 1), ("w_mem_kv", l + 1)] if l + 1 < DEPTH else [])
        spec = ex.gather_spec(keys)
        o_mix, lse_mix, got = _flash_fwd(qp, kp, vp, mode, f"flash_{mode}_fwd_{tag}", comm=spec)
        if spec is not None:
            ex.put_gathered(keys, got)
        o_mem, lse_mem, _ = _flash_fwd(qmp, kmp, vmp, "full", f"flash_mem_fwd_{tag}")
        heads = _unpad_heads([o_mix, o_mem], (1.0, 1.0), BF16, f"unpad_heads_{tag}")
        h_mid = _matmul(heads, ex.weight(("w_out", l)), "nn", F32, f"mm_out_{tag}", residual=h)
        xn2 = _rms_fwd(h_mid, g_ffn, f"rms_ffn_{tag}")
        w_val, w_gate = ex.weight(("w_up", l))
        u_val = _matmul(xn2, w_val, "nn", F32, f"mm_up_val_{tag}")
        u_gate = _matmul(xn2, w_gate, "nn", F32, f"mm_up_gate_{tag}")
        act = _conv_fwd(u_val, u_gate, *ex.conv(l), f"conv_fwd_{tag}")
        h_out = _matmul(act, ex.weight(("w_down", l)), "nn", F32, f"mm_down_{tag}", residual=h_mid)
        saved.append(dict(h=h, xn=xn, proj=proj, bfg=bfg, qp=qp, kp=kp, vp=vp, qmp=qmp, mn=mn, kmp=kmp, vmp=vmp, o_mix=o_mix,
                          lse_mix=lse_mix, o_mem=o_mem, lse_mem=lse_mem, heads=heads, h_mid=h_mid, xn2=xn2, u_val=u_val, u_gate=u_gate, act=act))
        h = h_out

    loss_blk, dh, dg_final = _loss_head(h, small["norm_final"].reshape(1, D_MODEL), target, "loss_head")

    grads = {k: [None] * DEPTH for k in ("conv_w", "conv_b", "norm_mix", "norm_mem", "norm_ffn")}
    grads["b_forget"] = [None, None]
    waiting = []
    for l in reversed(range(DEPTH)):
        fox = l % 2 == 0
        tag = f"L{l}"
        sv = saved[l]
        g_mix, g_mem, g_ffn = (small[n][l:l + 1] for n in ("norm_mix", "norm_mem", "norm_ffn"))
        da = _matmul(dh, ex.weight(("w_down", l)), "nt", F32, f"mm_da_{tag}")
        g_down = _matmul(sv["act"], dh, "tn", BF16, f"mm_dwdown_{tag}")
        du_val, du_gate, dcw_v, dcw_g, dcb_v, dcb_g = _conv_bwd(sv["u_val"], sv["u_gate"], da, *ex.conv(l), f"conv_bwd_{tag}")
        grads["conv_w"][l] = jnp.concatenate([dcw_v, dcw_g], axis=1)
        grads["conv_b"][l] = jnp.concatenate([dcb_v, dcb_g], axis=1)
        w_val, w_gate = ex.weight(("w_up", l))
        dxn2 = _matmul(du_val, w_val, "nt", F32, f"mm_dxn2_val_{tag}")
        dxn2 = _matmul(du_gate, w_gate, "nt", F32, f"mm_dxn2_gate_{tag}", residual=dxn2)
        g_up = (_matmul(sv["xn2"], du_val, "tn", BF16, f"mm_dwup_val_{tag}"), _matmul(sv["xn2"], du_gate, "tn", BF16, f"mm_dwup_gate_{tag}"))
        dh_mid, grads["norm_ffn"][l] = _rms_bwd(sv["h_mid"], g_ffn, dxn2, dh, f"rms_ffn_bwd_{tag}")
        dheads = _matmul(dh_mid, ex.weight(("w_out", l)), "nt", BF16, f"mm_dheads_{tag}")
        g_out = _matmul(sv["heads"], dh_mid, "tn", BF16, f"mm_dwout_{tag}")
        do_mix, do_mem = _pad_heads([(dheads, 0, N_MIX), (dheads, D_MIX, N_MEM)], (1.0, 1.0), (N_MIX * HP, N_MEM * HP),
                                    f"pad_dheads_{tag}")
        mode = "causal" if fox else "dilated"
        items = waiting + [(("w_down", l), g_down), (("w_up", l), g_up), (("w_out", l), g_out)]
        spec = ex.scatter_spec(items)
        dqp, dkp, dvp, got = _flash_bwd(sv["qp"], sv["kp"], sv["vp"], sv["o_mix"], do_mix, sv["lse_mix"], mode, f"flash_{mode}_bwd_{tag}",
                                        comm=spec)
        if spec is not None:
            ex.put_received(items, spec, got)
        dqmp, dkmp, dvmp, _ = _flash_bwd(sv["qmp"], sv["kmp"], sv["vmp"], sv["o_mem"], do_mem, sv["lse_mem"], "full",
                                         f"flash_mem_bwd_{tag}")
        if fox:
            dproj, dbf = _prep_fox_bwd(dqp, dkp, dvp, dqmp, sv["proj"], sv["bfg"], f"prep_fox_bwd_{tag}")
            grads["b_forget"][l // 2] = dbf[0, :N_MIX]
        else:
            dproj = _prep_dil_bwd(dqp, dkp, dvp, dqmp, cos2, sin2, f"prep_dil_bwd_{tag}")
        dkvm = _unpad_heads([dkmp, dvmp], (LN2, 1.0), BF16, f"unpad_dkvm_{tag}")
        g_memkv = _matmul(sv["mn"], dkvm, "tn", BF16, f"mm_dwmemkv_{tag}")
        dmn = _matmul(dkvm, ex.weight(("w_mem_kv", l)), "nt", F32, f"mm_dmn_{tag}")
        _, grads["norm_mem"][l] = _rms_bwd(mem, g_mem, dmn, None, f"rms_mem_bwd_{tag}")
        dxn = _matmul(dproj, ex.weight(("w_in", l)), "nt", F32, f"mm_dxn_{tag}")
        g_in = _matmul(sv["xn"], dproj, "tn", BF16, f"mm_dwin_{tag}")
        waiting = [(("w_mem_kv", l), g_memkv), (("w_in", l), g_in)]
        dh, grads["norm_mix"][l] = _rms_bwd(sv["h"], g_mix, dxn, dh_mid, f"rms_mix_bwd_{tag}")
    spec = ex.scatter_spec(waiting)
    if spec is not None:
        ex.put_received(waiting, spec, _run_comm(spec, "grad_scatter_last"))
    grads["norm_final"] = dg_final
    return loss_blk, dh, grads


class _Exchange:
    def __init__(self, own, conv_w_full, conv_b, chip, core):
        self.own, self.conv_w_full, self.conv_b, self.chip, self.core = own, conv_w_full, conv_b, chip, core
        self.full, self.recv = {}, {}

    def _shard(self, key):
        name, l = key
        if name == "w_in":
            return self.own["w_in_fox" if l % 2 == 0 else "w_in_dil"][l // 2]
        return self.own[name][l]

    def gather_spec(self, keys):
        return _GatherSpec([self._shard(k) for k in keys])

    def put_gathered(self, keys, outs):
        for key, o in zip(keys, outs):
            name, l = key
            g = lax.dynamic_update_slice_in_dim(o, self._shard(key)[None], self.chip, axis=0)
            if name == "w_in":
                w = jnp.concatenate([g[j] for j in range(N_CHIPS)], axis=1)
                self.full[key] = _fox_cols_to_kernel(w) if l % 2 == 0 else w
            elif name == "w_up":
                self.full[key] = (jnp.concatenate([g[0], g[1]], axis=1), jnp.concatenate([g[2], g[3]], axis=1))
            else:
                self.full[key] = g.reshape(N_CHIPS * g.shape[1], g.shape[2])

    def weight(self, key):
        return self.full[key]

    def conv(self, l):
        return self.conv_w_full[l], self.conv_b[l:l + 1]

    def scatter_spec(self, items):
        parts = []
        for (name, l), g in items:
            if name == "w_in":
                g = _fox_cols_from_kernel(g) if l % 2 == 0 else g
                parts.append(jnp.stack(jnp.split(g, N_CHIPS, axis=1)))
            elif name == "w_up":
                parts.append(jnp.stack(jnp.split(g[0], 2, axis=1) + jnp.split(g[1], 2, axis=1)))
            else:
                parts.append(g.reshape(N_CHIPS, g.shape[0] // N_CHIPS, g.shape[1]))
        return _ScatterSpec(parts)

    def put_received(self, items, spec, outs):
        for (key, _), part, o in zip(items, spec.inputs, outs):
            rh = o.shape[1]
            mine = lax.dynamic_slice_in_dim(lax.dynamic_index_in_dim(part, self.chip, 0, keepdims=False), self.core * rh, rh, axis=0)
            self.recv[key] = lax.dynamic_update_slice_in_dim(o, mine[None], 2 * self.chip + self.core, axis=0)


def kernel(x, mem, norm_mix, norm_mem, norm_ffn, w_in_fox, b_forget, w_in_dil, w_mem_kv, w_out, w_up, conv_w, conv_b, w_down, norm_final, loss_target, m_norm_mix, m_norm_mem, m_norm_ffn, m_w_in_fox, m_b_forget, m_w_in_dil, m_w_mem_kv, m_w_out, m_w_up, m_conv_w, m_conv_b, m_w_down, m_norm_final, v_norm_mix, v_norm_mem, v_norm_ffn, v_w_in_fox, v_b_forget, v_w_in_dil, v_w_mem_kv, v_w_out, v_w_up, v_conv_w, v_conv_b, v_w_down, v_norm_final):
    w_sh = dict(w_in_fox=w_in_fox, w_in_dil=w_in_dil, w_mem_kv=w_mem_kv, w_out=w_out, w_up=w_up, w_down=w_down, conv_w=conv_w)
    m_sh = dict(w_in_fox=m_w_in_fox, w_in_dil=m_w_in_dil, w_mem_kv=m_w_mem_kv, w_out=m_w_out, w_up=m_w_up, w_down=m_w_down, conv_w=m_conv_w)
    v_sh = dict(w_in_fox=v_w_in_fox, w_in_dil=v_w_in_dil, w_mem_kv=v_w_mem_kv, w_out=v_w_out, w_up=v_w_up, w_down=v_w_down, conv_w=v_conv_w)
    small = dict(norm_mix=norm_mix, norm_mem=norm_mem, norm_ffn=norm_ffn, conv_b=conv_b, norm_final=norm_final, b_forget=b_forget)
    m_small = dict(norm_mix=m_norm_mix, norm_mem=m_norm_mem, norm_ffn=m_norm_ffn, conv_b=m_conv_b, norm_final=m_norm_final, b_forget=m_b_forget)
    v_small = dict(norm_mix=v_norm_mix, norm_mem=v_norm_mem, norm_ffn=v_norm_ffn, conv_b=v_conv_b, norm_final=v_norm_final, b_forget=v_b_forget)
    chip = 2 * lax.axis_index("x") + lax.axis_index("y")
    core = lax.axis_index("c")

    conv_spec = (("conv_w", _CONVW_SHARD),)
    conv_all = _gather_all(_pack_small(dict(conv_w=conv_w), conv_spec), "gather_conv_w")
    conv_w_full = jnp.concatenate([_unpack_small(conv_all[2 * j], conv_spec)["conv_w"] for j in range(N_CHIPS)], axis=-1)
    ex = _Exchange({n: w_sh[n].astype(BF16) for n in _BIG}, conv_w_full, conv_b, chip, core)

    loss_blk, dx, grads = _local_step(x[0], mem[0], loss_target[0], small, ex)
    loss = lax.psum(loss_blk[0, 0], ("x", "y", "c"))

    layer_keys = {"w_in_fox": [("w_in", 0), ("w_in", 2)], "w_in_dil": [("w_in", 1), ("w_in", 3)]}
    keys = [k for n in _BIG for k in layer_keys.get(n, [(n, l) for l in range(DEPTH)])]
    halves = [_sum_rows(ex.recv[k], F32, f"grad_sum_{k[0]}_L{k[1]}") for k in keys]
    whole = {k: jnp.concatenate([jnp.where(core == 0, mine, other), jnp.where(core == 0, other, mine)], axis=0)
             for k, mine, other in zip(keys, halves, _join_halves(halves, "grad_join_halves"))}
    g_big = {n: jnp.stack([whole[k] for k in layer_keys.get(n, [(n, l) for l in range(DEPTH)])]) for n in _BIG}

    g_small_local = dict(
        norm_mix=jnp.concatenate(grads["norm_mix"]), norm_mem=jnp.concatenate(grads["norm_mem"]),
        norm_ffn=jnp.concatenate(grads["norm_ffn"]),
        conv_b=jnp.concatenate(grads["conv_b"]),
        norm_final=grads["norm_final"], b_forget=jnp.stack(grads["b_forget"]),
        conv_w=jnp.stack(grads["conv_w"]),
    )
    small_all = _gather_all(_pack_small(g_small_local, _SMALL), "small_gather_all")
    g_small = _unpack_small(_sum_rows(small_all, F32, "small_sum"), _SMALL)
    ncol = 2 * D_FF // N_CHIPS
    g_big["conv_w"] = lax.dynamic_slice_in_dim(g_small["conv_w"], chip * ncol, ncol, axis=2)

    out_g, out_d, out_m, out_v = {}, {}, {}, {}
    for n in _BIG + ("conv_w",):
        shp = w_sh[n].shape
        two_d = (-1, shp[-1])
        d, mo, vo = _adamw(w_sh[n].reshape(two_d), g_big[n].reshape(two_d), m_sh[n].reshape(two_d), v_sh[n].reshape(two_d), f"adamw_{n}")
        out_g[n], out_d[n], out_m[n], out_v[n] = g_big[n].reshape(shp), d.reshape(shp), mo.reshape(shp), vo.reshape(shp)
    spec = _SMALL[:-1]
    d, mo, vo = _adamw(_pack_small(small, spec), _pack_small(g_small, spec), _pack_small(m_small, spec), _pack_small(v_small, spec),
                       "adamw_small")
    d, mo, vo = _unpack_small(d, spec), _unpack_small(mo, spec), _unpack_small(vo, spec)
    for n, shp in spec:
        out_g[n], out_d[n], out_m[n], out_v[n] = g_small[n].reshape(shp), d[n], mo[n], vo[n]

    order = ("norm_mix", "norm_mem", "norm_ffn", "w_in_fox", "b_forget", "w_in_dil", "w_mem_kv", "w_out", "w_up", "conv_w", "conv_b",
             "w_down", "norm_final")
    return (loss, dx[None], *[out_g[n] for n in order], *[out_d[n] for n in order], *[out_m[n] for n in order],
            *[out_v[n] for n in order])
```

```python
import functools
import math

import numpy as np
import jax
import jax.numpy as jnp
from jax import lax
from jax.experimental import pallas as pl
from jax.experimental.pallas import tpu as pltpu

F32 = jnp.float32
BF16 = jnp.bfloat16

D_MODEL = 1024
DEPTH = 4
HEAD_DIM = 64
N_MIX = 12
N_MEM = 4
D_MIX = N_MIX * HEAD_DIM
D_MEMQ = N_MEM * HEAD_DIM
D_FF = 2816
FOX_IN = 3 * D_MIX + N_MIX + D_MEMQ
DIL_IN = 3 * D_MIX + D_MEMQ
HP = 128
SCALE = HEAD_DIM ** -0.5
NEG = -1e30
NORM_EPS = 1e-6
DIL_MAX = 2048
LOG2E = 1.0 / math.log(2.0)
LN2 = math.log(2.0)
QSCALE = SCALE * LOG2E
BWD_HEADS_PER_STEP = 3
FWD_HEADS_PER_STEP = 6
ROPE_THETA = 10000.0
N_CHIPS = 4
CONV_TF = 128
FLASH_BLK = 512
VMEM_LIMIT = 48 * 1024 * 1024

ADAM_LR = 0.001
ADAM_B1 = 0.9
ADAM_B2 = 0.999
ADAM_EPS = 1e-08
ADAM_WD = 0.01
ADAM_STEP = 10

MESH = pl.DeviceIdType.MESH


def _cp(*sem):
    return pltpu.CompilerParams(dimension_semantics=tuple(sem), vmem_limit_bytes=VMEM_LIMIT)


def _tile(n, cap, mult=128):
    if n <= cap:
        return n
    t = (cap // mult) * mult
    while t >= mult:
        if n % t == 0:
            return t
        t -= mult
    raise ValueError(f"no tile for {n} under {cap}")


_DIMS = {"nn": (((1,), (0,)), ((), ())), "nt": (((1,), (1,)), ((), ())), "tn": (((0,), (0,)), ((), ()))}


def _matmul(a, b, mode, out_dtype, name, residual=None):
    if mode == "nn":
        (m, k), n = a.shape, b.shape[1]
    elif mode == "nt":
        (m, k), n = a.shape, b.shape[0]
    else:
        (k, m), n = a.shape, b.shape[1]
    tn = 1408 if n % 1408 == 0 else _tile(n, 1024)
    tm = 1408 if (m % 1408 == 0 and tn <= 1024) else _tile(m, 1024)
    tk = _tile(k, 1408)
    nk = k // tk
    dims = _DIMS[mode]
    has_res = residual is not None

    def body(*refs):
        if has_res:
            a_ref, b_ref, r_ref, o_ref = refs[:4]
        else:
            a_ref, b_ref, o_ref = refs[:3]
        part = lax.dot_general(a_ref[...].astype(BF16), b_ref[...].astype(BF16), dims, preferred_element_type=F32)
        if nk == 1:
            if has_res:
                part = part + r_ref[...]
            o_ref[...] = part.astype(o_ref.dtype)
            return
        acc_ref = refs[-1]
        kk = pl.program_id(2)

        @pl.when(kk == 0)
        def _():
            acc_ref[...] = part

        @pl.when(kk > 0)
        def _():
            acc_ref[...] += part

        @pl.when(kk == nk - 1)
        def _():
            tot = acc_ref[...]
            if has_res:
                tot = tot + r_ref[...]
            o_ref[...] = tot.astype(o_ref.dtype)

    if mode == "nn":
        a_spec = pl.BlockSpec((tm, tk), lambda i, j, kk: (i, kk))
        b_spec = pl.BlockSpec((tk, tn), lambda i, j, kk: (kk, j))
    elif mode == "nt":
        a_spec = pl.BlockSpec((tm, tk), lambda i, j, kk: (i, kk))
        b_spec = pl.BlockSpec((tn, tk), lambda i, j, kk: (j, kk))
    else:
        a_spec = pl.BlockSpec((tk, tm), lambda i, j, kk: (kk, i))
        b_spec = pl.BlockSpec((tk, tn), lambda i, j, kk: (kk, j))
    in_specs = [a_spec, b_spec]
    args = [a, b]
    if has_res:
        in_specs.append(pl.BlockSpec((tm, tn), lambda i, j, kk: (i, j)))
        args.append(residual)
    return pl.pallas_call(
        body,
        out_shape=jax.ShapeDtypeStruct((m, n), out_dtype),
        grid=(m // tm, n // tn, nk),
        in_specs=in_specs,
        out_specs=pl.BlockSpec((tm, tn), lambda i, j, kk: (i, j)),
        scratch_shapes=[pltpu.VMEM((tm, tn), F32)] if nk > 1 else [],
        compiler_params=_cp("parallel", "parallel", "arbitrary"),
        name=name,
    )(*args)


def _rms_fwd(h, g, name):
    r, d = h.shape
    tr = _tile(r, 512, 8)

    def body(h_ref, g_ref, o_ref):
        x = h_ref[...]
        rstd = lax.rsqrt(jnp.mean(x * x, axis=-1, keepdims=True) + NORM_EPS)
        o_ref[...] = ((x * rstd) * g_ref[...]).astype(o_ref.dtype)

    return pl.pallas_call(
        body,
        out_shape=jax.ShapeDtypeStruct((r, d), BF16),
        grid=(r // tr,),
        in_specs=[pl.BlockSpec((tr, d), lambda i: (i, 0)), pl.BlockSpec((1, d), lambda i: (0, 0))],
        out_specs=pl.BlockSpec((tr, d), lambda i: (i, 0)),
        compiler_params=_cp("parallel"),
        name=name,
    )(h, g)


def _rms_bwd(h, g, dy, dres, name):
    r, d = h.shape
    tr = _tile(r, 512, 8)
    need_dh = dres is not None

    def body(*refs):
        if need_dh:
            h_ref, g_ref, dy_ref, dres_ref, dh_ref, dg_ref = refs
        else:
            h_ref, g_ref, dy_ref, dg_ref = refs
        i = pl.program_id(0)
        x = h_ref[...]
        rstd = lax.rsqrt(jnp.mean(x * x, axis=-1, keepdims=True) + NORM_EPS)
        nrm = x * rstd
        dyv = dy_ref[...].astype(F32)
        part = jnp.sum(dyv * nrm, axis=0, keepdims=True)

        @pl.when(i == 0)
        def _():
            dg_ref[...] = part

        @pl.when(i > 0)
        def _():
            dg_ref[...] += part

        if need_dh:
            gy = dyv * g_ref[...]
            dx = rstd * (gy - nrm * jnp.mean(gy * nrm, axis=-1, keepdims=True))
            dh_ref[...] = dres_ref[...] + dx

    row = pl.BlockSpec((tr, d), lambda i: (i, 0))
    vec = pl.BlockSpec((1, d), lambda i: (0, 0))
    if need_dh:
        return pl.pallas_call(
            body,
            out_shape=(jax.ShapeDtypeStruct((r, d), F32), jax.ShapeDtypeStruct((1, d), F32)),
            grid=(r // tr,),
            in_specs=[row, vec, row, row],
            out_specs=(row, vec),
            compiler_params=_cp("arbitrary"),
            name=name,
        )(h, g, dy, dres)
    return None, pl.pallas_call(
        body,
        out_shape=jax.ShapeDtypeStruct((1, d), F32),
        grid=(r // tr,),
        in_specs=[row, vec, row],
        out_specs=vec,
        compiler_params=_cp("arbitrary"),
        name=name,
    )(h, g, dy)


def _loss_head(h, g, target, name):
    r, d = h.shape
    tr = _tile(r, 512, 8)

    def body(h_ref, g_ref, t_ref, loss_ref, dh_ref, dg_ref):
        i = pl.program_id(0)
        x = h_ref[...]
        gv = g_ref[...]
        rstd = lax.rsqrt(jnp.mean(x * x, axis=-1, keepdims=True) + NORM_EPS)
        nrm = x * rstd
        err = nrm * gv - t_ref[...]
        lpart = 0.5 * jnp.sum(jnp.mean(err * err, axis=-1, keepdims=True), axis=0, keepdims=True)
        dyv = err * (1.0 / d)
        gpart = jnp.sum(dyv * nrm, axis=0, keepdims=True)

        @pl.when(i == 0)
        def _():
            loss_ref[...] = jnp.broadcast_to(lpart, loss_ref.shape)
            dg_ref[...] = gpart

        @pl.when(i > 0)
        def _():
            loss_ref[...] += jnp.broadcast_to(lpart, loss_ref.shape)
            dg_ref[...] += gpart

        gy = dyv * gv
        dh_ref[...] = rstd * (gy - nrm * jnp.mean(gy * nrm, axis=-1, keepdims=True))

    row = pl.BlockSpec((tr, d), lambda i: (i, 0))
    vec = pl.BlockSpec((1, d), lambda i: (0, 0))
    lsp = pl.BlockSpec((1, 128), lambda i: (0, 0))
    return pl.pallas_call(
        body,
        out_shape=(jax.ShapeDtypeStruct((1, 128), F32), jax.ShapeDtypeStruct((r, d), F32), jax.ShapeDtypeStruct((1, d), F32)),
        grid=(r // tr,),
        in_specs=[row, vec, row],
        out_specs=(lsp, row, vec),
        compiler_params=_cp("arbitrary"),
        name=name,
    )(h, g, target)


def _lane(shape):
    return lax.broadcasted_iota(jnp.int32, shape, 1)


def _head_tile(pair_tile, odd):
    return pltpu.roll(pair_tile, 64, 1) if odd else pair_tile


def _pair_tile(even_tile, odd_tile):
    lane = _lane(even_tile.shape)
    return jnp.where(lane < 64, even_tile, pltpu.roll(odd_tile, 64, 1))


def _pad_heads(xs, scales, out_widths, name):
    r = xs[0][0].shape[0]
    tr = _tile(r, 256, 8)
    n_in = len(xs)

    def body(*refs):
        for idx in range(n_in):
            x_ref, o_ref = refs[idx], refs[n_in + idx]
            nh = xs[idx][2]
            for p in range(nh // 2):
                t = x_ref[:, p * 128:(p + 1) * 128].astype(F32) * scales[idx]
                lane = _lane(t.shape)
                o_ref[:, (2 * p) * HP:(2 * p + 1) * HP] = jnp.where(lane < 64, t, 0.0).astype(o_ref.dtype)
                o_ref[:, (2 * p + 1) * HP:(2 * p + 2) * HP] = jnp.where(lane < 64, pltpu.roll(t, 64, 1), 0.0).astype(o_ref.dtype)

    in_specs, args, out_specs, out_shape = [], [], [], []
    for (arr, c0, nh), w in zip(xs, out_widths):
        wcols = nh * 64
        assert c0 % wcols == 0 or c0 == 0
        blk = c0 // wcols if wcols else 0
        in_specs.append(pl.BlockSpec((tr, wcols), functools.partial(lambda i, b: (i, b), b=blk)))
        args.append(arr)
        out_specs.append(pl.BlockSpec((tr, w), lambda i: (i, 0)))
        out_shape.append(jax.ShapeDtypeStruct((r, w), BF16))
    return pl.pallas_call(
        body,
        out_shape=tuple(out_shape),
        grid=(r // tr,),
        in_specs=in_specs,
        out_specs=tuple(out_specs),
        compiler_params=_cp("parallel"),
        name=name,
    )(*args)


def _unpad_heads(xs, scales, out_dtype, name):
    r = xs[0].shape[0]
    tr = _tile(r, 256, 8)
    nhs = [x.shape[1] // HP for x in xs]
    total = sum(nhs) * 64

    def body(*refs):
        o_ref = refs[-1]
        col = 0
        for x_ref, nh, sc in zip(refs[:-1], nhs, scales):
            for p in range(nh // 2):
                ev = x_ref[:, (2 * p) * HP:(2 * p + 1) * HP].astype(F32)
                od = x_ref[:, (2 * p + 1) * HP:(2 * p + 2) * HP].astype(F32)
                o_ref[:, col:col + 128] = (_pair_tile(ev, od) * sc).astype(o_ref.dtype)
                col += 128

    return pl.pallas_call(
        body,
        out_shape=jax.ShapeDtypeStruct((r, total), out_dtype),
        grid=(r // tr,),
        in_specs=[pl.BlockSpec((tr, x.shape[1]), lambda i: (i, 0)) for x in xs],
        out_specs=pl.BlockSpec((tr, total), lambda i: (i, 0)),
        compiler_params=_cp("parallel"),
        name=name,
    )(*xs)


def _bf16_split3(c):
    hi = c.astype(BF16).astype(F32)
    r1 = c - hi
    mid = r1.astype(BF16).astype(F32)
    lo = (r1 - mid).astype(BF16).astype(F32)
    return hi, mid, lo


def _log_sigmoid(z):
    return jnp.minimum(z, 0.0) - jnp.log(1.0 + jnp.exp(-jnp.abs(z)))


def _prep_fox_fwd(proj, bfg, name):
    s = proj.shape[0]
    ts = _tile(s, 256, 8)
    fcol = 3 * D_MIX + D_MEMQ

    def body(p_ref, b_ref, q_ref, k_ref, v_ref, qm_ref, carry_ref):
        i = pl.program_id(0)

        @pl.when(i == 0)
        def _():
            carry_ref[...] = jnp.zeros_like(carry_ref)

        lane = _lane((ts, 128))
        z = p_ref[:, fcol:fcol + 128] + b_ref[...]
        logf = jnp.where(lane < N_MIX, _log_sigmoid(z), 0.0)
        rr = lax.broadcasted_iota(jnp.int32, (ts, ts), 0)
        cc = lax.broadcasted_iota(jnp.int32, (ts, ts), 1)
        tri = jnp.where(cc <= rr, 1.0, 0.0).astype(F32)
        c = jnp.dot(tri, logf, preferred_element_type=F32, precision=lax.Precision.HIGHEST) + carry_ref[0:1, :]
        carry_ref[...] = jnp.broadcast_to(c[ts - 1:ts, :], carry_ref.shape)
        for hh in range(N_MIX):
            p, odd = hh // 2, hh % 2
            ch = jnp.sum(jnp.where(lane == hh, c, 0.0), axis=-1, keepdims=True) * LOG2E
            hi, mid, lo = _bf16_split3(ch)
            qt = _head_tile(p_ref[:, p * 128:(p + 1) * 128], odd) * QSCALE
            kt = _head_tile(p_ref[:, D_MIX + p * 128:D_MIX + (p + 1) * 128], odd)
            vt = _head_tile(p_ref[:, 2 * D_MIX + p * 128:2 * D_MIX + (p + 1) * 128], odd)
            qa = jnp.where(lane == 64, hi, jnp.where(lane == 65, mid, jnp.where(lane == 66, lo, jnp.where(lane < 70, 1.0, 0.0))))
            ka = jnp.where(lane < 67, 1.0, jnp.where(lane == 67, -hi, jnp.where(lane == 68, -mid, jnp.where(lane == 69, -lo, 0.0))))
            q_ref[:, hh * HP:(hh + 1) * HP] = jnp.where(lane < 64, qt, qa).astype(BF16)
            k_ref[:, hh * HP:(hh + 1) * HP] = jnp.where(lane < 64, kt, ka).astype(BF16)
            v_ref[:, hh * HP:(hh + 1) * HP] = jnp.where(lane < 64, vt, 0.0).astype(BF16)
        for hh in range(N_MEM):
            p, odd = hh // 2, hh % 2
            t = _head_tile(p_ref[:, 3 * D_MIX + p * 128:3 * D_MIX + (p + 1) * 128], odd) * QSCALE
            qm_ref[:, hh * HP:(hh + 1) * HP] = jnp.where(lane < 64, t, 0.0).astype(BF16)

    wmix, wmem = N_MIX * HP, N_MEM * HP
    return pl.pallas_call(
        body,
        out_shape=(jax.ShapeDtypeStruct((s, wmix), BF16),) * 3 + (jax.ShapeDtypeStruct((s, wmem), BF16),),
        grid=(s // ts,),
        in_specs=[pl.BlockSpec((ts, proj.shape[1]), lambda i: (i, 0)), pl.BlockSpec((1, 128), lambda i: (0, 0))],
        out_specs=(pl.BlockSpec((ts, wmix), lambda i: (i, 0)),) * 3 + (pl.BlockSpec((ts, wmem), lambda i: (i, 0)),),
        scratch_shapes=[pltpu.VMEM((8, 128), F32)],
        compiler_params=_cp("arbitrary"),
        name=name,
    )(proj, bfg)


def _prep_fox_bwd(dq, dk, dv, dqm, proj, bfg, name):
    s = proj.shape[0]
    ts = _tile(s, 256, 8)
    nb = s // ts
    fcol = 3 * D_MIX + D_MEMQ

    def body(dq_ref, dk_ref, dv_ref, dqm_ref, p_ref, b_ref, o_ref, db_ref, carry_ref):
        i = pl.program_id(0)

        @pl.when(i == 0)
        def _():
            carry_ref[...] = jnp.zeros_like(carry_ref)
            db_ref[...] = jnp.zeros_like(db_ref)

        lane = _lane((ts, 128))
        dc = jnp.zeros((ts, 128), F32)
        for p in range(N_MIX // 2):
            tq, tk, tv = [], [], []
            for odd in (0, 1):
                hh = 2 * p + odd
                dqt = dq_ref[:, hh * HP:(hh + 1) * HP]
                dkt = dk_ref[:, hh * HP:(hh + 1) * HP]
                col = jnp.sum(jnp.where(lane == 64, dqt, 0.0) - jnp.where(lane == 67, dkt, 0.0), axis=-1, keepdims=True)
                dc = dc + jnp.where(lane == hh, col, 0.0)
                tq.append(dqt)
                tk.append(dkt)
                tv.append(dv_ref[:, hh * HP:(hh + 1) * HP])
            o_ref[:, p * 128:(p + 1) * 128] = (_pair_tile(tq[0], tq[1]) * SCALE).astype(o_ref.dtype)
            o_ref[:, D_MIX + p * 128:D_MIX + (p + 1) * 128] = (_pair_tile(tk[0], tk[1]) * LN2).astype(o_ref.dtype)
            o_ref[:, 2 * D_MIX + p * 128:2 * D_MIX + (p + 1) * 128] = _pair_tile(tv[0], tv[1]).astype(o_ref.dtype)
        for p in range(N_MEM // 2):
            ev = dqm_ref[:, (2 * p) * HP:(2 * p + 1) * HP]
            od = dqm_ref[:, (2 * p + 1) * HP:(2 * p + 2) * HP]
            o_ref[:, 3 * D_MIX + p * 128:3 * D_MIX + (p + 1) * 128] = (_pair_tile(ev, od) * SCALE).astype(o_ref.dtype)
        rr = lax.broadcasted_iota(jnp.int32, (ts, ts), 0)
        cc = lax.broadcasted_iota(jnp.int32, (ts, ts), 1)
        triu = jnp.where(cc >= rr, 1.0, 0.0).astype(F32)
        dlogf = jnp.dot(triu, dc, preferred_element_type=F32, precision=lax.Precision.HIGHEST) + carry_ref[0:1, :]
        carry_ref[...] = jnp.broadcast_to(dlogf[0:1, :], carry_ref.shape)
        z = p_ref[:, fcol:fcol + 128] + b_ref[...]
        dz = jnp.where(lane < N_MIX, dlogf / (1.0 + jnp.exp(z)), 0.0)
        o_ref[:, fcol:fcol + 128] = dz.astype(o_ref.dtype)
        db_ref[...] += jnp.sum(dz, axis=0, keepdims=True)

    wmix, wmem = N_MIX * HP, N_MEM * HP
    rev = lambda i: (nb - 1 - i, 0)
    return pl.pallas_call(
        body,
        out_shape=(jax.ShapeDtypeStruct(proj.shape, BF16), jax.ShapeDtypeStruct((1, 128), F32)),
        grid=(nb,),
        in_specs=[pl.BlockSpec((ts, wmix), rev)] * 3 + [pl.BlockSpec((ts, wmem), rev), pl.BlockSpec((ts, proj.shape[1]), rev),
                                                         pl.BlockSpec((1, 128), lambda i: (0, 0))],
        out_specs=(pl.BlockSpec((ts, proj.shape[1]), rev), pl.BlockSpec((1, 128), lambda i: (0, 0))),
        scratch_shapes=[pltpu.VMEM((8, 128), F32)],
        compiler_params=_cp("arbitrary"),
        name=name,
    )(dq, dk, dv, dqm, proj, bfg)


def _rope_partner(x):
    lane = _lane(x.shape)
    return jnp.where((lane % 64) < 32, pltpu.roll(x, 96, 1), pltpu.roll(x, 32, 1))


def _prep_dil_fwd(proj, cos2, sin2, name):
    s = proj.shape[0]
    ts = _tile(s, 256, 8)

    def body(p_ref, c_ref, s_ref, q_ref, k_ref, v_ref, qm_ref):
        lane = _lane((ts, 128))
        cosv, sinv = c_ref[...], s_ref[...]
        for p in range(N_MIX // 2):
            xq = p_ref[:, p * 128:(p + 1) * 128]
            xk = p_ref[:, D_MIX + p * 128:D_MIX + (p + 1) * 128]
            xv = p_ref[:, 2 * D_MIX + p * 128:2 * D_MIX + (p + 1) * 128]
            yq = (xq * cosv + _rope_partner(xq) * sinv) * QSCALE
            yk = xk * cosv + _rope_partner(xk) * sinv
            for odd in (0, 1):
                hh = 2 * p + odd
                q_ref[:, hh * HP:(hh + 1) * HP] = jnp.where(lane < 64, _head_tile(yq, odd), 0.0).astype(BF16)
                k_ref[:, hh * HP:(hh + 1) * HP] = jnp.where(lane < 64, _head_tile(yk, odd), 0.0).astype(BF16)
                v_ref[:, hh * HP:(hh + 1) * HP] = jnp.where(lane < 64, _head_tile(xv, odd), 0.0).astype(BF16)
        for p in range(N_MEM // 2):
            t = p_ref[:, 3 * D_MIX + p * 128:3 * D_MIX + (p + 1) * 128] * QSCALE
            for odd in (0, 1):
                hh = 2 * p + odd
                qm_ref[:, hh * HP:(hh + 1) * HP] = jnp.where(lane < 64, _head_tile(t, odd), 0.0).astype(BF16)

    wmix, wmem = N_MIX * HP, N_MEM * HP
    return pl.pallas_call(
        body,
        out_shape=(jax.ShapeDtypeStruct((s, wmix), BF16),) * 3 + (jax.ShapeDtypeStruct((s, wmem), BF16),),
        grid=(s // ts,),
        in_specs=[pl.BlockSpec((ts, proj.shape[1]), lambda i: (i, 0)), pl.BlockSpec((ts, 128), lambda i: (i, 0)),
                  pl.BlockSpec((ts, 128), lambda i: (i, 0))],
        out_specs=(pl.BlockSpec((ts, wmix), lambda i: (i, 0)),) * 3 + (pl.BlockSpec((ts, wmem), lambda i: (i, 0)),),
        compiler_params=_cp("parallel"),
        name=name,
    )(proj, cos2, sin2)


def _prep_dil_bwd(dq, dk, dv, dqm, cos2, sin2, name):
    s = dq.shape[0]
    ts = _tile(s, 256, 8)

    def body(dq_ref, dk_ref, dv_ref, dqm_ref, c_ref, s_ref, o_ref):
        cosv, sinv = c_ref[...], s_ref[...]
        for p in range(N_MIX // 2):
            e, o = 2 * p, 2 * p + 1
            dyq = _pair_tile(dq_ref[:, e * HP:(e + 1) * HP], dq_ref[:, o * HP:(o + 1) * HP]) * SCALE
            dyk = _pair_tile(dk_ref[:, e * HP:(e + 1) * HP], dk_ref[:, o * HP:(o + 1) * HP]) * LN2
            dxv = _pair_tile(dv_ref[:, e * HP:(e + 1) * HP], dv_ref[:, o * HP:(o + 1) * HP])
            o_ref[:, p * 128:(p + 1) * 128] = (dyq * cosv - _rope_partner(dyq) * sinv).astype(o_ref.dtype)
            o_ref[:, D_MIX + p * 128:D_MIX + (p + 1) * 128] = (dyk * cosv - _rope_partner(dyk) * sinv).astype(o_ref.dtype)
            o_ref[:, 2 * D_MIX + p * 128:2 * D_MIX + (p + 1) * 128] = dxv.astype(o_ref.dtype)
        for p in range(N_MEM // 2):
            e, o = 2 * p, 2 * p + 1
            t = _pair_tile(dqm_ref[:, e * HP:(e + 1) * HP], dqm_ref[:, o * HP:(o + 1) * HP]) * SCALE
            o_ref[:, 3 * D_MIX + p * 128:3 * D_MIX + (p + 1) * 128] = t.astype(o_ref.dtype)

    wmix, wmem = N_MIX * HP, N_MEM * HP
    row = lambda w: pl.BlockSpec((ts, w), lambda i: (i, 0))
    return pl.pallas_call(
        body,
        out_shape=jax.ShapeDtypeStruct((s, DIL_IN), BF16),
        grid=(s // ts,),
        in_specs=[row(wmix)] * 3 + [row(wmem), row(128), row(128)],
        out_specs=row(DIL_IN),
        compiler_params=_cp("parallel"),
        name=name,
    )(dq, dk, dv, dqm, cos2, sin2)


def _mask_bias(mode, blk):
    n = 1 if mode == "causal" else DIL_MAX // blk + 1
    idx = jnp.arange(blk, dtype=jnp.int32)
    dist = jnp.arange(n, dtype=jnp.int32)[:, None, None] * blk + idx[None, :, None] - idx[None, None, :]
    if mode == "causal":
        return jnp.where(dist >= 0, 0.0, NEG).astype(F32)
    cnt = ((dist <= 128).astype(jnp.int32) + ((dist <= 512) & (dist % 4 == 0)).astype(jnp.int32)
           + ((dist <= DIL_MAX) & (dist % 16 == 0)).astype(jnp.int32))
    bias = jnp.where(cnt == 3, math.log2(3.0), jnp.where(cnt == 2, 1.0, 0.0))
    return jnp.where((dist >= 0) & (cnt > 0), bias, NEG).astype(F32)


def _flash_dims(q, k, mode):
    sq, w = q.shape
    sk = k.shape[0]
    tq = _tile(sq, FLASH_BLK, 8)
    tk = sk if mode == "full" else tq
    band = DIL_MAX // tk if mode == "dilated" else None
    return sq, sk, w, tq, tk, sq // tq, sk // tk, band


def _heads_per_step(n_heads, most):
    return max(h for h in range(1, most + 1) if n_heads % h == 0)


def _pair_starts(mode, nq, band, key_major):
    if mode == "full":
        counts = [nq] if key_major else [1] * nq
    elif key_major:
        counts = [(min(kj + band, nq - 1) if mode == "dilated" else nq - 1) - kj + 1 for kj in range(nq)]
    else:
        counts = [qi - (max(qi - band, 0) if mode == "dilated" else 0) + 1 for qi in range(nq)]
    starts = [int(v) for v in np.cumsum([0] + counts)]
    return starts[:-1], starts[-1]


def _pair_at(t, starts, mode, band, key_major):
    if mode == "full":
        return t, jnp.zeros((), jnp.int32)
    major, begin = 0, 0
    for prev, s in zip(starts[:-1], starts[1:]):
        reached = t >= s
        major = major + reached.astype(jnp.int32)
        begin = begin + jnp.where(reached, s - prev, 0)
    if key_major:
        return major + (t - begin), major
    return major, _first_key(mode, major, band) + (t - begin)


def _first_key(mode, qi, band):
    return jnp.maximum(qi - band, 0) if mode == "dilated" else 0


def _carried(comm, grid):
    if comm is None:
        return 0, 0, [], (lambda *a: None), (lambda *a: None)

    def edge(last):
        cond = None
        for d, n in enumerate(grid):
            c = pl.program_id(d) == (n - 1 if last else 0)
            cond = c if cond is None else cond & c
        return cond

    def start(cins, couts, sems):
        pl.when(edge(False))(lambda: comm.start(cins, couts, *sems))

    def finish(cins, couts, sems):
        pl.when(edge(True))(lambda: comm.finish(cins, couts, *sems))

    sems = [pltpu.SemaphoreType.DMA((comm.n_sems,)), pltpu.SemaphoreType.DMA((comm.n_sems,))]
    return len(comm.inputs), len(comm.out_shapes), sems, start, finish


def _split_refs(refs, n_in, n_cin, n_out, n_cout, n_scratch):
    bounds = np.cumsum([0, n_in, n_cin, n_out, n_cout, n_scratch])
    return [refs[a:b] for a, b in zip(bounds[:-1], bounds[1:])] + [refs[bounds[-1]:]]


def _flash_fwd(q, k, v, mode, name, comm=None):
    sq, sk, w, tq, tk, nq, nk, band = _flash_dims(q, k, mode)
    hb = _heads_per_step(w // HP, FWD_HEADS_PER_STEP)
    wb = hb * HP
    nch = tk // 128
    has_bias = mode != "full"
    starts, n_pairs = _pair_starts(mode, nq, band, key_major=False)
    pair = functools.partial(_pair_at, starts=starts, mode=mode, band=band, key_major=False)
    grid = (w // wb, n_pairs)
    n_cin, n_cout, comm_sems, comm_start, comm_finish = _carried(comm, grid)

    def body(*refs):
        ins, cins, (o_ref, lse_ref), couts, (m_ref, l_ref, acc_ref), sems = _split_refs(refs, 3 + has_bias, n_cin, 2, n_cout, 3)
        q_ref, k_ref, v_ref = ins[:3]
        b_ref = ins[3] if has_bias else None
        qi, kj = pair(pl.program_id(1))
        comm_start(cins, couts, sems)

        @pl.when(kj == _first_key(mode, qi, band))
        def _():
            m_ref[...] = jnp.full_like(m_ref, -jnp.inf)
            l_ref[...] = jnp.zeros_like(l_ref)
            acc_ref[...] = jnp.zeros_like(acc_ref)

        def step(bias_tile):
            for h in range(hb):
                cols = slice(h * HP, (h + 1) * HP)
                sc = lax.dot_general(q_ref[:, cols], k_ref[:, cols], _DIMS["nt"], preferred_element_type=F32)
                if bias_tile is not None:
                    sc = sc + bias_tile()
                m_prev = m_ref[h]
                m_new = jnp.maximum(m_prev, jnp.max(sc, axis=-1, keepdims=True))
                alpha = jnp.exp2(m_prev - m_new)
                psum, chunks = None, []
                for c in range(nch):
                    pc = jnp.exp2(sc[:, c * 128:(c + 1) * 128] - m_new)
                    psum = pc if psum is None else psum + pc
                    chunks.append(pc.astype(BF16))
                p = chunks[0] if nch == 1 else jnp.concatenate(chunks, axis=1)
                l_ref[h] = alpha * l_ref[h] + psum
                acc_ref[h] = alpha * acc_ref[h] + jnp.dot(p, v_ref[:, cols], preferred_element_type=F32)
                m_ref[h] = m_new

        if mode == "full":
            step(None)
        elif mode == "causal":
            pl.when(kj == qi)(lambda: step(lambda: b_ref[0]))
            pl.when(kj < qi)(lambda: step(None))
        else:
            step(lambda: b_ref[qi - kj])

        @pl.when(kj == (0 if mode == "full" else qi))
        def _():
            for h in range(hb):
                cols = slice(h * HP, (h + 1) * HP)
                l = jnp.sum(l_ref[h], axis=-1, keepdims=True)
                o_ref[:, cols] = (acc_ref[h] / l).astype(o_ref.dtype)
                lse_ref[:, cols] = m_ref[h] + jnp.log2(l)

        comm_finish(cins, couts, sems)

    qspec = pl.BlockSpec((tq, wb), lambda hp, t: (pair(t)[0], hp))
    kspec = pl.BlockSpec((tk, wb), lambda hp, t: (pair(t)[1], hp))
    in_specs = [qspec, kspec, kspec]
    args = [q, k, v]
    if has_bias:
        bias = _mask_bias(mode, tq)
        in_specs.append(pl.BlockSpec(bias.shape, lambda hp, t: (0, 0, 0), pipeline_mode=pl.Buffered(1)))
        args.append(bias)
    out_shape = [jax.ShapeDtypeStruct((sq, w), BF16), jax.ShapeDtypeStruct((sq, w), F32)]
    out_specs = [qspec, qspec]
    if comm is not None:
        in_specs += [_ANY] * n_cin
        args += list(comm.inputs)
        out_shape += list(comm.out_shapes)
        out_specs += [_ANY] * n_cout
    res = pl.pallas_call(
        body,
        out_shape=tuple(out_shape),
        grid=grid,
        in_specs=in_specs,
        out_specs=tuple(out_specs),
        scratch_shapes=[pltpu.VMEM((hb, tq, HP), F32), pltpu.VMEM((hb, tq, HP), F32), pltpu.VMEM((hb, tq, HP), F32)] + comm_sems,
        compiler_params=_cp("parallel", "arbitrary") if comm is None else _cp("arbitrary", "arbitrary"),
        name=name,
    )(*args)
    return res[0], res[1], list(res[2:])


def _flash_bwd(q, k, v, o, do, lse, mode, name, comm=None):
    sq, sk, w, tq, tk, nq, nk, band = _flash_dims(q, k, mode)
    hb = _heads_per_step(w // HP, BWD_HEADS_PER_STEP)
    wb = hb * HP
    nch = tk // 128
    has_bias = mode != "full"
    starts, n_pairs = _pair_starts(mode, nq, band, key_major=True)
    pair = functools.partial(_pair_at, starts=starts, mode=mode, band=band, key_major=True)
    grid = (w // wb, n_pairs)
    n_cin, n_cout, comm_sems, comm_start, comm_finish = _carried(comm, grid)

    def body(*refs):
        ins, cins, (dq_ref, dk_ref, dv_ref), couts, (delta_ref,), sems = _split_refs(refs, 6 + has_bias, n_cin, 3, n_cout, 1)
        q_ref, k_ref, v_ref, o_ref, do_ref, lse_ref = ins[:6]
        b_ref = ins[6] if has_bias else None
        qi, kj = pair(pl.program_id(1))
        rows = pl.ds(pl.multiple_of(qi * tq, tq), tq)
        comm_start(cins, couts, sems)

        @pl.when(pl.program_id(1) == 0)
        def _():
            dq_ref[...] = jnp.zeros_like(dq_ref)

        @pl.when(qi == (0 if mode == "full" else kj))
        def _():
            dk_ref[...] = jnp.zeros_like(dk_ref)
            dv_ref[...] = jnp.zeros_like(dv_ref)

        @pl.when(kj == _first_key(mode, qi, band))
        def _():
            for h in range(hb):
                cols = slice(h * HP, (h + 1) * HP)
                dl = jnp.sum(do_ref[:, cols].astype(F32) * o_ref[:, cols].astype(F32), axis=-1, keepdims=True)
                delta_ref[h, rows, :] = jnp.broadcast_to(dl, (tq, HP))

        def step(bias_tile):
            for h in range(hb):
                cols = slice(h * HP, (h + 1) * HP)
                qv, kv, dov = q_ref[:, cols], k_ref[:, cols], do_ref[:, cols]
                sc = lax.dot_general(qv, kv, _DIMS["nt"], preferred_element_type=F32)
                if bias_tile is not None:
                    sc = sc + bias_tile()
                dp = lax.dot_general(dov, v_ref[:, cols], _DIMS["nt"], preferred_element_type=F32)
                lse_b = lse_ref[:, cols]
                dlt = delta_ref[h, rows, :]
                pch, dsch = [], []
                for c in range(nch):
                    lanes = slice(c * 128, (c + 1) * 128)
                    pc = jnp.exp2(sc[:, lanes] - lse_b)
                    pch.append(pc.astype(BF16))
                    dsch.append((pc * (dp[:, lanes] - dlt)).astype(BF16))
                p = pch[0] if nch == 1 else jnp.concatenate(pch, axis=1)
                ds = dsch[0] if nch == 1 else jnp.concatenate(dsch, axis=1)
                dv_ref[:, cols] += lax.dot_general(p, dov, _DIMS["tn"], preferred_element_type=F32)
                dk_ref[:, cols] += lax.dot_general(ds, qv, _DIMS["tn"], preferred_element_type=F32)
                dq_ref[rows, cols] += jnp.dot(ds, kv, preferred_element_type=F32)

        if mode == "full":
            step(None)
        elif mode == "causal":
            pl.when(qi == kj)(lambda: step(lambda: b_ref[0]))
            pl.when(qi > kj)(lambda: step(None))
        else:
            step(lambda: b_ref[qi - kj])

        comm_finish(cins, couts, sems)

    qspec = pl.BlockSpec((tq, wb), lambda hp, t: (pair(t)[0], hp))
    kspec = pl.BlockSpec((tk, wb), lambda hp, t: (pair(t)[1], hp))
    in_specs = [qspec, kspec, kspec, qspec, qspec, qspec]
    args = [q, k, v, o, do, lse]
    if has_bias:
        bias = _mask_bias(mode, tq)
        in_specs.append(pl.BlockSpec(bias.shape, lambda hp, t: (0, 0, 0), pipeline_mode=pl.Buffered(1)))
        args.append(bias)
    out_shape = [jax.ShapeDtypeStruct((sq, w), F32), jax.ShapeDtypeStruct((sk, w), F32), jax.ShapeDtypeStruct((sk, w), F32)]
    out_specs = [pl.BlockSpec((sq, wb), lambda hp, t: (0, hp)), kspec, kspec]
    if comm is not None:
        in_specs += [_ANY] * n_cin
        args += list(comm.inputs)
        out_shape += list(comm.out_shapes)
        out_specs += [_ANY] * n_cout
    res = pl.pallas_call(
        body,
        out_shape=tuple(out_shape),
        grid=grid,
        in_specs=in_specs,
        out_specs=tuple(out_specs),
        scratch_shapes=[pltpu.VMEM((hb, sq, HP), F32)] + comm_sems,
        compiler_params=_cp("parallel", "arbitrary") if comm is None else _cp("arbitrary", "arbitrary"),
        name=name,
    )(*args)
    return res[0], res[1], res[2], list(res[3:])


def _conv_rc(s):
    return _tile(s, 256, 8)


def _shift_down(x, prev8, nrows):
    rows = lax.broadcasted_iota(jnp.int32, x.shape, 0)
    out = pltpu.roll(x, nrows, 0)
    for i in range(nrows):
        out = jnp.where(rows == i, prev8[8 - nrows + i:8 - nrows + i + 1, :], out)
    return out


def _shift_up(x, next8, nrows):
    n = x.shape[0]
    rows = lax.broadcasted_iota(jnp.int32, x.shape, 0)
    out = pltpu.roll(x, n - nrows, 0)
    for i in range(nrows):
        out = jnp.where(rows == n - nrows + i, next8[i:i + 1, :], out)
    return out


def _conv_taps(uv_ref, ug_ref, r, rc):
    r0 = pl.multiple_of(r * rc, rc)
    x = jnp.concatenate([uv_ref[pl.ds(r0, rc), :], ug_ref[pl.ds(r0, rc), :]], axis=1)
    p0 = pl.multiple_of(jnp.maximum(r0 - 8, 0), 8)
    prev8 = jnp.where(r > 0, jnp.concatenate([uv_ref[pl.ds(p0, 8), :], ug_ref[pl.ds(p0, 8), :]], axis=1), 0.0)
    return r0, x, _shift_down(x, prev8, 1), _shift_down(x, prev8, 2)


def _conv_specs(s, tf, nf):
    strip = pl.BlockSpec((s, tf), lambda j: (0, j))
    return strip, [pl.BlockSpec((3, tf), lambda j: (0, j)), pl.BlockSpec((3, tf), lambda j: (0, j + nf)),
                   pl.BlockSpec((1, tf), lambda j: (0, j)), pl.BlockSpec((1, tf), lambda j: (0, j + nf))]


def _conv_params(wv_ref, wg_ref, bv_ref, bg_ref):
    w = jnp.concatenate([wv_ref[...], wg_ref[...]], axis=1)
    return w[0:1, :], w[1:2, :], w[2:3, :], jnp.concatenate([bv_ref[...], bg_ref[...]], axis=1)


def _conv_fwd(u_val, u_gate, cw, cb, name):
    s, f = u_val.shape
    tf = CONV_TF
    nf = f // tf
    rc = _conv_rc(s)

    def body(uv_ref, ug_ref, wv_ref, wg_ref, bv_ref, bg_ref, a_ref):
        w0, w1, w2, b = _conv_params(wv_ref, wg_ref, bv_ref, bg_ref)

        def chunk(r, carry):
            r0, x, x1, x2 = _conv_taps(uv_ref, ug_ref, r, rc)
            c = b + w0 * x2 + w1 * x1 + w2 * x
            val, gate = c[:, :tf], c[:, tf:]
            a_ref[pl.ds(r0, rc), :] = (gate * jax.nn.sigmoid(gate) * val).astype(a_ref.dtype)
            return carry

        lax.fori_loop(0, s // rc, chunk, 0)

    strip, params = _conv_specs(s, tf, nf)
    return pl.pallas_call(
        body,
        out_shape=jax.ShapeDtypeStruct((s, f), BF16),
        grid=(nf,),
        in_specs=[strip, strip] + params,
        out_specs=strip,
        compiler_params=_cp("parallel"),
        name=name,
    )(u_val, u_gate, cw, cw, cb, cb)


def _conv_bwd(u_val, u_gate, da, cw, cb, name):
    s, f = u_val.shape
    tf = CONV_TF
    nf = f // tf
    rc = _conv_rc(s)
    nchunk = s // rc

    def body(uv_ref, ug_ref, da_ref, wv_ref, wg_ref, bv_ref, bg_ref, duv_ref, dug_ref, dwv_ref, dwg_ref, dbv_ref, dbg_ref, next_ref):
        w0, w1, w2, b = _conv_params(wv_ref, wg_ref, bv_ref, bg_ref)
        next_ref[...] = jnp.zeros_like(next_ref)

        def chunk(it, carry):
            g0, g1, g2, gb = carry
            r = nchunk - 1 - it
            r0, x, x1, x2 = _conv_taps(uv_ref, ug_ref, r, rc)
            c = b + w0 * x2 + w1 * x1 + w2 * x
            val, gate = c[:, :tf], c[:, tf:]
            sg = jax.nn.sigmoid(gate)
            dav = da_ref[pl.ds(r0, rc), :]
            dc = jnp.concatenate([dav * (gate * sg), dav * val * (sg * (1.0 + gate * (1.0 - sg)))], axis=1)
            nxt = next_ref[...]
            du = (w2 * dc + w1 * _shift_up(dc, nxt, 1) + w0 * _shift_up(dc, nxt, 2)).astype(duv_ref.dtype)
            duv_ref[pl.ds(r0, rc), :] = du[:, :tf]
            dug_ref[pl.ds(r0, rc), :] = du[:, tf:]
            next_ref[...] = dc[0:8, :]
            return (g0 + jnp.sum(dc * x2, axis=0, keepdims=True), g1 + jnp.sum(dc * x1, axis=0, keepdims=True),
                    g2 + jnp.sum(dc * x, axis=0, keepdims=True), gb + jnp.sum(dc, axis=0, keepdims=True))

        zero = jnp.zeros((1, 2 * tf), F32)
        g0, g1, g2, gb = lax.fori_loop(0, nchunk, chunk, (zero, zero, zero, zero))
        for i, gi in enumerate((g0, g1, g2)):
            dwv_ref[i:i + 1, :] = gi[:, :tf]
            dwg_ref[i:i + 1, :] = gi[:, tf:]
        dbv_ref[...] = gb[:, :tf]
        dbg_ref[...] = gb[:, tf:]

    strip, params = _conv_specs(s, tf, nf)
    taps = pl.BlockSpec((3, tf), lambda j: (0, j))
    bias = pl.BlockSpec((1, tf), lambda j: (0, j))
    act = jax.ShapeDtypeStruct((s, f), BF16)
    return pl.pallas_call(
        body,
        out_shape=(act, act, jax.ShapeDtypeStruct((3, f), F32), jax.ShapeDtypeStruct((3, f), F32),
                   jax.ShapeDtypeStruct((1, f), F32), jax.ShapeDtypeStruct((1, f), F32)),
        grid=(nf,),
        in_specs=[strip, strip, strip] + params,
        out_specs=(strip, strip, taps, taps, bias, bias),
        scratch_shapes=[pltpu.VMEM((8, 2 * tf), F32)],
        compiler_params=_cp("parallel"),
        name=name,
    )(u_val, u_gate, da, cw, cw, cb, cb)


def _adamw(w, g, m, v, name):
    r, c = w.shape
    tr = _tile(r, 256, 8) if r % 8 == 0 else r
    c1 = 1.0 - ADAM_B1 ** ADAM_STEP
    c2 = 1.0 - ADAM_B2 ** ADAM_STEP

    def body(w_ref, g_ref, m_ref, v_ref, d_ref, mo_ref, vo_ref):
        gv = g_ref[...]
        mn = ADAM_B1 * m_ref[...] + (1.0 - ADAM_B1) * gv
        vn = ADAM_B2 * v_ref[...] + (1.0 - ADAM_B2) * (gv * gv)
        d_ref[...] = -ADAM_LR * ((mn / c1) / (jnp.sqrt(vn / c2) + ADAM_EPS) + ADAM_WD * w_ref[...])
        mo_ref[...] = mn
        vo_ref[...] = vn

    blk = pl.BlockSpec((tr, c), lambda i: (i, 0))
    shp = jax.ShapeDtypeStruct((r, c), F32)
    return pl.pallas_call(
        body, out_shape=(shp, shp, shp), grid=(r // tr,), in_specs=[blk] * 4, out_specs=(blk,) * 3,
        compiler_params=_cp("parallel"), name=name,
    )(w, g, m, v)


def _sum_rows(parts, out_dtype, name):
    n, r, c = parts.shape
    tr = _tile(r, 256, 8)

    def body(p_ref, o_ref):
        tot = p_ref[0].astype(F32)
        for i in range(1, n):
            tot = tot + p_ref[i].astype(F32)
        o_ref[...] = tot.astype(o_ref.dtype)

    return pl.pallas_call(
        body, out_shape=jax.ShapeDtypeStruct((r, c), out_dtype), grid=(r // tr,),
        in_specs=[pl.BlockSpec((n, tr, c), lambda i: (0, i, 0))], out_specs=pl.BlockSpec((tr, c), lambda i: (i, 0)),
        compiler_params=_cp("parallel"), name=name,
    )(parts)


_ANY = pl.BlockSpec(memory_space=pl.ANY)


def _place():
    return lax.axis_index("x"), lax.axis_index("y"), lax.axis_index("c")


def _other_chips(x, y):
    return [(1 - x, y), (x, 1 - y), (1 - x, 1 - y)]


def _rows_half(ref, c, axis):
    rh = ref.shape[axis] // 2
    idx = [slice(None)] * len(ref.shape)
    idx[axis] = pl.ds(pl.multiple_of(c * rh, 16), rh)
    return ref.at[tuple(idx)]


def _remote(src, dst, send_sems, recv_sems, kk, to):
    return pltpu.make_async_remote_copy(src_ref=src, dst_ref=dst, send_sem=send_sems.at[kk], recv_sem=recv_sems.at[kk],
                                        device_id=to, device_id_type=MESH)


class _GatherSpec:
    def __init__(self, shards):
        self.inputs = list(shards)
        self.out_shapes = [jax.ShapeDtypeStruct((N_CHIPS,) + s.shape, s.dtype) for s in shards]
        self.n_sems = 6 * len(shards)

    def _sends(self, ins, outs, send_sems, recv_sems):
        x, y, c = _place()
        me = 2 * x + y
        return [_remote(_rows_half(ins[a], c, 0), _rows_half(outs[a].at[me], c, 0), send_sems, recv_sems, 6 * a + j, (cx, cy, c))
                for a in range(len(ins)) for j, (cx, cy) in enumerate(_other_chips(x, y))]

    def _forwards(self, outs, send_sems, recv_sems):
        x, y, c = _place()
        return [_remote(_rows_half(outs[a].at[2 * cx + cy], c, 0), _rows_half(outs[a].at[2 * cx + cy], c, 0), send_sems, recv_sems,
                        6 * a + 3 + j, (x, y, 1 - c))
                for a in range(len(outs)) for j, (cx, cy) in enumerate(_other_chips(x, y))]

    def start(self, ins, outs, send_sems, recv_sems):
        for cp in self._sends(ins, outs, send_sems, recv_sems):
            cp.start()

    def finish(self, ins, outs, send_sems, recv_sems):
        x, y, c = _place()
        chips = _other_chips(x, y)
        forwards = self._forwards(outs, send_sems, recv_sems)
        for a in range(len(outs)):
            for j, (cx, cy) in enumerate(chips):
                slot = _rows_half(outs[a].at[2 * cx + cy], c, 0)
                _remote(slot, slot, send_sems, recv_sems, 6 * a + j, (x, y, c)).wait_recv()
                forwards[3 * a + j].start()
        for a in range(len(outs)):
            for j, (cx, cy) in enumerate(chips):
                slot = _rows_half(outs[a].at[2 * cx + cy], 1 - c, 0)
                _remote(slot, slot, send_sems, recv_sems, 6 * a + 3 + j, (x, y, c)).wait_recv()
        for cp in self._sends(ins, outs, send_sems, recv_sems) + forwards:
            cp.wait_send()


_FLIPS = [(dx, dy, dc) for dx in (0, 1) for dy in (0, 1) for dc in (0, 1) if (dx, dy, dc) != (0, 0, 0)]


class _ScatterSpec:
    def __init__(self, parts):
        self.inputs = list(parts)
        self.out_shapes = [jax.ShapeDtypeStruct((8, p.shape[1] // 2, p.shape[2]), p.dtype) for p in parts]
        self.n_sems = 7 * len(parts)

    def _sends(self, ins, outs, send_sems, recv_sems):
        x, y, c = _place()
        me = 4 * x + 2 * y + c
        copies = []
        for a in range(len(ins)):
            for k, (dx, dy, dc) in enumerate(_FLIPS):
                tx, ty, tc = x ^ dx, y ^ dy, c ^ dc
                copies.append(_remote(_rows_half(ins[a].at[2 * tx + ty], tc, 0), outs[a].at[me], send_sems, recv_sems,
                                      7 * a + k, (tx, ty, tc)))
        return copies

    def start(self, ins, outs, send_sems, recv_sems):
        for cp in self._sends(ins, outs, send_sems, recv_sems):
            cp.start()

    def finish(self, ins, outs, send_sems, recv_sems):
        x, y, c = _place()
        for a in range(len(outs)):
            for k, (dx, dy, dc) in enumerate(_FLIPS):
                slot = outs[a].at[4 * (x ^ dx) + 2 * (y ^ dy) + (c ^ dc)]
                _remote(slot, slot, send_sems, recv_sems, 7 * a + k, (x, y, c)).wait_recv()
        for cp in self._sends(ins, outs, send_sems, recv_sems):
            cp.wait_send()


def _run_comm(comm, name):
    n_in, n_out = len(comm.inputs), len(comm.out_shapes)

    def body(*refs):
        ins, outs, sems = refs[:n_in], refs[n_in:n_in + n_out], refs[n_in + n_out:]
        comm.start(ins, outs, *sems)
        comm.finish(ins, outs, *sems)

    return list(pl.pallas_call(
        body, out_shape=tuple(comm.out_shapes), in_specs=[_ANY] * n_in, out_specs=(_ANY,) * n_out,
        scratch_shapes=[pltpu.SemaphoreType.DMA((comm.n_sems,)), pltpu.SemaphoreType.DMA((comm.n_sems,))], name=name,
    )(*comm.inputs))


def _join_halves(rs, name):
    n = len(rs)

    def body(*refs):
        in_refs, out_refs = refs[:n], refs[n:2 * n]
        send_sems, recv_sems = refs[2 * n:]
        x, y, c = _place()
        copies = [_remote(in_refs[a], out_refs[a], send_sems, recv_sems, a, (x, y, 1 - c)) for a in range(n)]
        for cp in copies:
            cp.start()
        for cp in copies:
            cp.wait()

    return pl.pallas_call(
        body,
        out_shape=tuple(jax.ShapeDtypeStruct(r.shape, r.dtype) for r in rs),
        in_specs=[_ANY] * n, out_specs=(_ANY,) * n,
        scratch_shapes=[pltpu.SemaphoreType.DMA((n,)), pltpu.SemaphoreType.DMA((n,))], name=name,
    )(*rs)


def _gather_all(small, name):
    r, w = small.shape
    flips = [(dx, dy, dc) for dx in (0, 1) for dy in (0, 1) for dc in (0, 1) if (dx, dy, dc) != (0, 0, 0)]

    def body(in_ref, out_ref, send_sems, recv_sems, local_sem):
        x, y, c = _place()
        me = 4 * x + 2 * y + c
        mine = pltpu.make_async_copy(in_ref, out_ref.at[me], local_sem)
        mine.start()
        sends = []
        for j, (dx, dy, dc) in enumerate(flips):
            to = (x ^ dx, y ^ dy, c ^ dc)
            cp = pltpu.make_async_remote_copy(src_ref=in_ref, dst_ref=out_ref.at[me], send_sem=send_sems.at[j],
                                              recv_sem=recv_sems.at[j], device_id=to, device_id_type=MESH)
            cp.start()
            sends.append(cp)
        for j, (dx, dy, dc) in enumerate(flips):
            slot = out_ref.at[4 * (x ^ dx) + 2 * (y ^ dy) + (c ^ dc)]
            pltpu.make_async_remote_copy(src_ref=slot, dst_ref=slot, send_sem=send_sems.at[j], recv_sem=recv_sems.at[j],
                                         device_id=(x, y, c), device_id_type=MESH).wait_recv()
        for cp in sends:
            cp.wait_send()
        mine.wait()

    return pl.pallas_call(
        body, out_shape=jax.ShapeDtypeStruct((8, r, w), small.dtype), in_specs=[_ANY], out_specs=_ANY,
        scratch_shapes=[pltpu.SemaphoreType.DMA((7,)), pltpu.SemaphoreType.DMA((7,)), pltpu.SemaphoreType.DMA], name=name,
    )(small)


_BIG = ("w_in_fox", "w_in_dil", "w_mem_kv", "w_out", "w_up", "w_down")
_CONVW_SHARD = (DEPTH, 3, 2 * D_FF // N_CHIPS)

_SMALL = (("norm_mix", (DEPTH, D_MODEL)), ("norm_mem", (DEPTH, D_MODEL)), ("norm_ffn", (DEPTH, D_MODEL)),
          ("conv_b", (DEPTH, 2 * D_FF)), ("norm_final", (D_MODEL,)), ("b_forget", (2, N_MIX)), ("conv_w", (DEPTH, 3, 2 * D_FF)))


def _pack_small(vals, spec):
    flat = jnp.concatenate([vals[n].reshape(-1).astype(F32) for n, _ in spec])
    rows = -(-flat.shape[0] // (8 * 128)) * 8
    return jnp.pad(flat, (0, rows * 128 - flat.shape[0])).reshape(rows, 128)


def _unpack_small(buf, spec):
    flat, out, off = buf.reshape(-1), {}, 0
    for n, shp in spec:
        k = int(np.prod(shp))
        out[n] = flat[off:off + k].reshape(shp)
        off += k
    return out


def _fox_cols_to_kernel(w):
    qkv, f, qm = w[:, :3 * D_MIX], w[:, 3 * D_MIX:3 * D_MIX + N_MIX], w[:, 3 * D_MIX + N_MIX:]
    return jnp.concatenate([qkv, qm, f, jnp.zeros((w.shape[0], 128 - N_MIX), w.dtype)], axis=1)


def _fox_cols_from_kernel(w):
    qkv, qm, f = w[:, :3 * D_MIX], w[:, 3 * D_MIX:3 * D_MIX + D_MEMQ], w[:, 3 * D_MIX + D_MEMQ:3 * D_MIX + D_MEMQ + N_MIX]
    return jnp.concatenate([qkv, f, qm], axis=1)


def _rope_pair_tables(s):
    inv = 1.0 / (ROPE_THETA ** (jnp.arange(0, HEAD_DIM, 2, dtype=F32) / HEAD_DIM))
    ang = jnp.arange(s, dtype=F32)[:, None] * inv[None, :]
    cos, sin = jnp.cos(ang), jnp.sin(ang)
    return jnp.concatenate([cos, cos, cos, cos], axis=1), jnp.concatenate([-sin, sin, -sin, sin], axis=1)


def _local_step(x, mem, target, small, ex):
    s = x.shape[0]
    cos2, sin2 = _rope_pair_tables(s)
    saved = []
    h = x
    keys = [("w_in", 0), ("w_mem_kv", 0)]
    spec = ex.gather_spec(keys)
    if spec is not None:
        ex.put_gathered(keys, _run_comm(spec, "gather_first"))
    for l in range(DEPTH):
        fox = l % 2 == 0
        slot = l // 2
        tag = f"L{l}"
        g_mix, g_mem, g_ffn = (small[n][l:l + 1] for n in ("norm_mix", "norm_mem", "norm_ffn"))
        xn = _rms_fwd(h, g_mix, f"rms_mix_{tag}")
        proj = _matmul(xn, ex.weight(("w_in", l)), "nn", F32, f"mm_in_{tag}")
        if fox:
            bfg = jnp.pad(small["b_forget"][slot:slot + 1], ((0, 0), (0, 128 - N_MIX)))
            qp, kp, vp, qmp = _prep_fox_fwd(proj, bfg, f"prep_fox_{tag}")
        else:
            bfg = None
            qp, kp, vp, qmp = _prep_dil_fwd(proj, cos2, sin2, f"prep_dil_{tag}")
        mn = _rms_fwd(mem, g_mem, f"rms_mem_{tag}")
        kvm = _matmul(mn, ex.weight(("w_mem_kv", l)), "nn", F32, f"mm_memkv_{tag}")
        kmp, vmp = _pad_heads([(kvm, 0, N_MEM), (kvm, D_MEMQ, N_MEM)], (1.0, 1.0), (N_MEM * HP,) * 2, f"pad_memkv_{tag}")
        mode = "causal" if fox else "dilated"
        keys = [("w_out", l), ("w_up", l), ("w_down", l)] + ([("w_in", l + 1), ("w_mem_kv", l + 1)] if l + 1 < DEPTH else [])
        spec = ex.gather_spec(keys)
        o_mix, lse_mix, got = _flash_fwd(qp, kp, vp, mode, f"flash_{mode}_fwd_{tag}", comm=spec)
        if spec is not None:
            ex.put_gathered(keys, got)
        o_mem, lse_mem, _ = _flash_fwd(qmp, kmp, vmp, "full", f"flash_mem_fwd_{tag}")
        heads = _unpad_heads([o_mix, o_mem], (1.0, 1.0), BF16, f"unpad_heads_{tag}")
        h_mid = _matmul(heads, ex.weight(("w_out", l)), "nn", F32, f"mm_out_{tag}", residual=h)
        xn2 = _rms_fwd(h_mid, g_ffn, f"rms_ffn_{tag}")
        w_val, w_gate = ex.weight(("w_up", l))
        u_val = _matmul(xn2, w_val, "nn", F32, f"mm_up_val_{tag}")
        u_gate = _matmul(xn2, w_gate, "nn", F32, f"mm_up_gate_{tag}")
        act = _conv_fwd(u_val, u_gate, *ex.conv(l), f"conv_fwd_{tag}")
        h_out = _matmul(act, ex.weight(("w_down", l)), "nn", F32, f"mm_down_{tag}", residual=h_mid)
        saved.append(dict(h=h, xn=xn, proj=proj, bfg=bfg, qp=qp, kp=kp, vp=vp, qmp=qmp, mn=mn, kmp=kmp, vmp=vmp, o_mix=o_mix,
                          lse_mix=lse_mix, o_mem=o_mem, lse_mem=lse_mem, heads=heads, h_mid=h_mid, xn2=xn2, u_val=u_val, u_gate=u_gate, act=act))
        h = h_out

    loss_blk, dh, dg_final = _loss_head(h, small["norm_final"].reshape(1, D_MODEL), target, "loss_head")

    grads = {k: [None] * DEPTH for k in ("conv_w", "conv_b", "norm_mix", "norm_mem", "norm_ffn")}
    grads["b_forget"] = [None, None]
    waiting = []
    for l in reversed(range(DEPTH)):
        fox = l % 2 == 0
        tag = f"L{l}"
        sv = saved[l]
        g_mix, g_mem, g_ffn = (small[n][l:l + 1] for n in ("norm_mix", "norm_mem", "norm_ffn"))
        da = _matmul(dh, ex.weight(("w_down", l)), "nt", F32, f"mm_da_{tag}")
        g_down = _matmul(sv["act"], dh, "tn", BF16, f"mm_dwdown_{tag}")
        du_val, du_gate, dcw_v, dcw_g, dcb_v, dcb_g = _conv_bwd(sv["u_val"], sv["u_gate"], da, *ex.conv(l), f"conv_bwd_{tag}")
        grads["conv_w"][l] = jnp.concatenate([dcw_v, dcw_g], axis=1)
        grads["conv_b"][l] = jnp.concatenate([dcb_v, dcb_g], axis=1)
        w_val, w_gate = ex.weight(("w_up", l))
        dxn2 = _matmul(du_val, w_val, "nt", F32, f"mm_dxn2_val_{tag}")
        dxn2 = _matmul(du_gate, w_gate, "nt", F32, f"mm_dxn2_gate_{tag}", residual=dxn2)
        g_up = (_matmul(sv["xn2"], du_val, "tn", BF16, f"mm_dwup_val_{tag}"), _matmul(sv["xn2"], du_gate, "tn", BF16, f"mm_dwup_gate_{tag}"))
        dh_mid, grads["norm_ffn"][l] = _rms_bwd(sv["h_mid"], g_ffn, dxn2, dh, f"rms_ffn_bwd_{tag}")
        dheads = _matmul(dh_mid, ex.weight(("w_out", l)), "nt", BF16, f"mm_dheads_{tag}")
        g_out = _matmul(sv["heads"], dh_mid, "tn", BF16, f"mm_dwout_{tag}")
        do_mix, do_mem = _pad_heads([(dheads, 0, N_MIX), (dheads, D_MIX, N_MEM)], (1.0, 1.0), (N_MIX * HP, N_MEM * HP),
                                    f"pad_dheads_{tag}")
        mode = "causal" if fox else "dilated"
        items = waiting + [(("w_down", l), g_down), (("w_up", l), g_up), (("w_out", l), g_out)]
        spec = ex.scatter_spec(items)
        dqp, dkp, dvp, got = _flash_bwd(sv["qp"], sv["kp"], sv["vp"], sv["o_mix"], do_mix, sv["lse_mix"], mode, f"flash_{mode}_bwd_{tag}",
                                        comm=spec)
        if spec is not None:
            ex.put_received(items, spec, got)
        dqmp, dkmp, dvmp, _ = _flash_bwd(sv["qmp"], sv["kmp"], sv["vmp"], sv["o_mem"], do_mem, sv["lse_mem"], "full",
                                         f"flash_mem_bwd_{tag}")
        if fox:
            dproj, dbf = _prep_fox_bwd(dqp, dkp, dvp, dqmp, sv["proj"], sv["bfg"], f"prep_fox_bwd_{tag}")
            grads["b_forget"][l // 2] = dbf[0, :N_MIX]
        else:
            dproj = _prep_dil_bwd(dqp, dkp, dvp, dqmp, cos2, sin2, f"prep_dil_bwd_{tag}")
        dkvm = _unpad_heads([dkmp, dvmp], (LN2, 1.0), BF16, f"unpad_dkvm_{tag}")
        g_memkv = _matmul(sv["mn"], dkvm, "tn", BF16, f"mm_dwmemkv_{tag}")
        dmn = _matmul(dkvm, ex.weight(("w_mem_kv", l)), "nt", F32, f"mm_dmn_{tag}")
        _, grads["norm_mem"][l] = _rms_bwd(mem, g_mem, dmn, None, f"rms_mem_bwd_{tag}")
        dxn = _matmul(dproj, ex.weight(("w_in", l)), "nt", F32, f"mm_dxn_{tag}")
        g_in = _matmul(sv["xn"], dproj, "tn", BF16, f"mm_dwin_{tag}")
        waiting = [(("w_mem_kv", l), g_memkv), (("w_in", l), g_in)]
        dh, grads["norm_mix"][l] = _rms_bwd(sv["h"], g_mix, dxn, dh_mid, f"rms_mix_bwd_{tag}")
    spec = ex.scatter_spec(waiting)
    if spec is not None:
        ex.put_received(waiting, spec, _run_comm(spec, "grad_scatter_last"))
    grads["norm_final"] = dg_final
    return loss_blk, dh, grads


class _Exchange:
    def __init__(self, own, conv_w_full, conv_b, chip, core):
        self.own, self.conv_w_full, self.conv_b, self.chip, self.core = own, conv_w_full, conv_b, chip, core
        self.full, self.recv = {}, {}

    def _shard(self, key):
        name, l = key
        if name == "w_in":
            return self.own["w_in_fox" if l % 2 == 0 else "w_in_dil"][l // 2]
        return self.own[name][l]

    def gather_spec(self, keys):
        return _GatherSpec([self._shard(k) for k in keys])

    def put_gathered(self, keys, outs):
        for key, o in zip(keys, outs):
            name, l = key
            g = lax.dynamic_update_slice_in_dim(o, self._shard(key)[None], self.chip, axis=0)
            if name == "w_in":
                w = jnp.concatenate([g[j] for j in range(N_CHIPS)], axis=1)
                self.full[key] = _fox_cols_to_kernel(w) if l % 2 == 0 else w
            elif name == "w_up":
                self.full[key] = (jnp.concatenate([g[0], g[1]], axis=1), jnp.concatenate([g[2], g[3]], axis=1))
            else:
                self.full[key] = g.reshape(N_CHIPS * g.shape[1], g.shape[2])

    def weight(self, key):
        return self.full[key]

    def conv(self, l):
        return self.conv_w_full[l], self.conv_b[l:l + 1]

    def scatter_spec(self, items):
        parts = []
        for (name, l), g in items:
            if name == "w_in":
                g = _fox_cols_from_kernel(g) if l % 2 == 0 else g
                parts.append(jnp.stack(jnp.split(g, N_CHIPS, axis=1)))
            elif name == "w_up":
                parts.append(jnp.stack(jnp.split(g[0], 2, axis=1) + jnp.split(g[1], 2, axis=1)))
            else:
                parts.append(g.reshape(N_CHIPS, g.shape[0] // N_CHIPS, g.shape[1]))
        return _ScatterSpec(parts)

    def put_received(self, items, spec, outs):
        for (key, _), part, o in zip(items, spec.inputs, outs):
            rh = o.shape[1]
            mine = lax.dynamic_slice_in_dim(lax.dynamic_index_in_dim(part, self.chip, 0, keepdims=False), self.core * rh, rh, axis=0)
            self.recv[key] = lax.dynamic_update_slice_in_dim(o, mine[None], 2 * self.chip + self.core, axis=0)


def kernel(x, mem, norm_mix, norm_mem, norm_ffn, w_in_fox, b_forget, w_in_dil, w_mem_kv, w_out, w_up, conv_w, conv_b, w_down, norm_final, loss_target, m_norm_mix, m_norm_mem, m_norm_ffn, m_w_in_fox, m_b_forget, m_w_in_dil, m_w_mem_kv, m_w_out, m_w_up, m_conv_w, m_conv_b, m_w_down, m_norm_final, v_norm_mix, v_norm_mem, v_norm_ffn, v_w_in_fox, v_b_forget, v_w_in_dil, v_w_mem_kv, v_w_out, v_w_up, v_conv_w, v_conv_b, v_w_down, v_norm_final):
    w_sh = dict(w_in_fox=w_in_fox, w_in_dil=w_in_dil, w_mem_kv=w_mem_kv, w_out=w_out, w_up=w_up, w_down=w_down, conv_w=conv_w)
    m_sh = dict(w_in_fox=m_w_in_fox, w_in_dil=m_w_in_dil, w_mem_kv=m_w_mem_kv, w_out=m_w_out, w_up=m_w_up, w_down=m_w_down, conv_w=m_conv_w)
    v_sh = dict(w_in_fox=v_w_in_fox, w_in_dil=v_w_in_dil, w_mem_kv=v_w_mem_kv, w_out=v_w_out, w_up=v_w_up, w_down=v_w_down, conv_w=v_conv_w)
    small = dict(norm_mix=norm_mix, norm_mem=norm_mem, norm_ffn=norm_ffn, conv_b=conv_b, norm_final=norm_final, b_forget=b_forget)
    m_small = dict(norm_mix=m_norm_mix, norm_mem=m_norm_mem, norm_ffn=m_norm_ffn, conv_b=m_conv_b, norm_final=m_norm_final, b_forget=m_b_forget)
    v_small = dict(norm_mix=v_norm_mix, norm_mem=v_norm_mem, norm_ffn=v_norm_ffn, conv_b=v_conv_b, norm_final=v_norm_final, b_forget=v_b_forget)
    chip = 2 * lax.axis_index("x") + lax.axis_index("y")
    core = lax.axis_index("c")

    conv_spec = (("conv_w", _CONVW_SHARD),)
    conv_all = _gather_all(_pack_small(dict(conv_w=conv_w), conv_spec), "gather_conv_w")
    conv_w_full = jnp.concatenate([_unpack_small(conv_all[2 * j], conv_spec)["conv_w"] for j in range(N_CHIPS)], axis=-1)
    ex = _Exchange({n: w_sh[n].astype(BF16) for n in _BIG}, conv_w_full, conv_b, chip, core)

    loss_blk, dx, grads = _local_step(x[0], mem[0], loss_target[0], small, ex)
    loss = lax.psum(loss_blk[0, 0], ("x", "y", "c"))

    layer_keys = {"w_in_fox": [("w_in", 0), ("w_in", 2)], "w_in_dil": [("w_in", 1), ("w_in", 3)]}
    keys = [k for n in _BIG for k in layer_keys.get(n, [(n, l) for l in range(DEPTH)])]
    halves = [_sum_rows(ex.recv[k], F32, f"grad_sum_{k[0]}_L{k[1]}") for k in keys]
    whole = {k: jnp.concatenate([jnp.where(core == 0, mine, other), jnp.where(core == 0, other, mine)], axis=0)
             for k, mine, other in zip(keys, halves, _join_halves(halves, "grad_join_halves"))}
    g_big = {n: jnp.stack([whole[k] for k in layer_keys.get(n, [(n, l) for l in range(DEPTH)])]) for n in _BIG}

    g_small_local = dict(
        norm_mix=jnp.concatenate(grads["norm_mix"]), norm_mem=jnp.concatenate(grads["norm_mem"]),
        norm_ffn=jnp.concatenate(grads["norm_ffn"]),
        conv_b=jnp.concatenate(grads["conv_b"]),
        norm_final=grads["norm_final"], b_forget=jnp.stack(grads["b_forget"]),
        conv_w=jnp.stack(grads["conv_w"]),
    )
    small_all = _gather_all(_pack_small(g_small_local, _SMALL), "small_gather_all")
    g_small = _unpack_small(_sum_rows(small_all, F32, "small_sum"), _SMALL)
    ncol = 2 * D_FF // N_CHIPS
    g_big["conv_w"] = lax.dynamic_slice_in_dim(g_small["conv_w"], chip * ncol, ncol, axis=2)

    out_g, out_d, out_m, out_v = {}, {}, {}, {}
    for n in _BIG + ("conv_w",):
        shp = w_sh[n].shape
        two_d = (-1, shp[-1])
        d, mo, vo = _adamw(w_sh[n].reshape(two_d), g_big[n].reshape(two_d), m_sh[n].reshape(two_d), v_sh[n].reshape(two_d), f"adamw_{n}")
        out_g[n], out_d[n], out_m[n], out_v[n] = g_big[n].reshape(shp), d.reshape(shp), mo.reshape(shp), vo.reshape(shp)
    spec = _SMALL[:-1]
    d, mo, vo = _adamw(_pack_small(small, spec), _pack_small(g_small, spec), _pack_small(m_small, spec), _pack_small(v_small, spec),
                       "adamw_small")
    d, mo, vo = _unpack_small(d, spec), _unpack_small(mo, spec), _unpack_small(vo, spec)
    for n, shp in spec:
        out_g[n], out_d[n], out_m[n], out_v[n] = g_small[n].reshape(shp), d[n], mo[n], vo[n]

    order = ("norm_mix", "norm_mem", "norm_ffn", "w_in_fox", "b_forget", "w_in_dil", "w_mem_kv", "w_out", "w_up", "conv_w", "conv_b",
             "w_down", "norm_final")
    return (loss, dx[None], *[out_g[n] for n in order], *[out_d[n] for n in order], *[out_m[n] for n in order],
            *[out_v[n] for n in order])
```

```python
import functools
import math

import numpy as np
import jax
import jax.numpy as jnp
from jax import lax
from jax.experimental import pallas as pl
from jax.experimental.pallas import tpu as pltpu

F32 = jnp.float32
BF16 = jnp.bfloat16

D_MODEL = 1024
DEPTH = 4
HEAD_DIM = 64
N_MIX = 12
N_MEM = 4
D_MIX = N_MIX * HEAD_DIM
D_MEMQ = N_MEM * HEAD_DIM
D_FF = 2816
FOX_IN = 3 * D_MIX + N_MIX + D_MEMQ
DIL_IN = 3 * D_MIX + D_MEMQ
HP = 128
SCALE = HEAD_DIM ** -0.5
NEG = -1e30
NORM_EPS = 1e-6
DIL_MAX = 2048
LOG2E = 1.0 / math.log(2.0)
LN2 = math.log(2.0)
QSCALE = SCALE * LOG2E
BWD_HEADS_PER_STEP = 3
FWD_HEADS_PER_STEP = 6
ROPE_THETA = 10000.0
N_CHIPS = 4
CONV_TF = 128
NORM_ROWS = 1024
HEAD_ROWS = 512
FLASH_BLK = 512
VMEM_LIMIT = 48 * 1024 * 1024

ADAM_LR = 0.001
ADAM_B1 = 0.9
ADAM_B2 = 0.999
ADAM_EPS = 1e-08
ADAM_WD = 0.01
ADAM_STEP = 10

MESH = pl.DeviceIdType.MESH


def _cp(*sem):
    return pltpu.CompilerParams(dimension_semantics=tuple(sem), vmem_limit_bytes=VMEM_LIMIT)


def _tile(n, cap, mult=128):
    if n <= cap:
        return n
    t = (cap // mult) * mult
    while t >= mult:
        if n % t == 0:
            return t
        t -= mult
    raise ValueError(f"no tile for {n} under {cap}")


_DIMS = {"nn": (((1,), (0,)), ((), ())), "nt": (((1,), (1,)), ((), ())), "tn": (((0,), (0,)), ((), ()))}


def _matmul(a, b, mode, out_dtype, name, residual=None):
    if mode == "nn":
        (m, k), n = a.shape, b.shape[1]
    elif mode == "nt":
        (m, k), n = a.shape, b.shape[0]
    else:
        (k, m), n = a.shape, b.shape[1]
    tn = 1408 if n % 1408 == 0 else _tile(n, 1024)
    tm = 1408 if (m % 1408 == 0 and tn <= 1024) else _tile(m, 1024)
    tk = _tile(k, 1408)
    nk = k // tk
    dims = _DIMS[mode]
    has_res = residual is not None

    def body(*refs):
        if has_res:
            a_ref, b_ref, r_ref, o_ref = refs[:4]
        else:
            a_ref, b_ref, o_ref = refs[:3]
        part = lax.dot_general(a_ref[...].astype(BF16), b_ref[...].astype(BF16), dims, preferred_element_type=F32)
        if nk == 1:
            if has_res:
                part = part + r_ref[...]
            o_ref[...] = part.astype(o_ref.dtype)
            return
        acc_ref = refs[-1]
        kk = pl.program_id(2)

        @pl.when(kk == 0)
        def _():
            acc_ref[...] = part

        @pl.when(kk > 0)
        def _():
            acc_ref[...] += part

        @pl.when(kk == nk - 1)
        def _():
            tot = acc_ref[...]
            if has_res:
                tot = tot + r_ref[...]
            o_ref[...] = tot.astype(o_ref.dtype)

    if mode == "nn":
        a_spec = pl.BlockSpec((tm, tk), lambda i, j, kk: (i, kk))
        b_spec = pl.BlockSpec((tk, tn), lambda i, j, kk: (kk, j))
    elif mode == "nt":
        a_spec = pl.BlockSpec((tm, tk), lambda i, j, kk: (i, kk))
        b_spec = pl.BlockSpec((tn, tk), lambda i, j, kk: (j, kk))
    else:
        a_spec = pl.BlockSpec((tk, tm), lambda i, j, kk: (kk, i))
        b_spec = pl.BlockSpec((tk, tn), lambda i, j, kk: (kk, j))
    in_specs = [a_spec, b_spec]
    args = [a, b]
    if has_res:
        in_specs.append(pl.BlockSpec((tm, tn), lambda i, j, kk: (i, j)))
        args.append(residual)
    return pl.pallas_call(
        body,
        out_shape=jax.ShapeDtypeStruct((m, n), out_dtype),
        grid=(m // tm, n // tn, nk),
        in_specs=in_specs,
        out_specs=pl.BlockSpec((tm, tn), lambda i, j, kk: (i, j)),
        scratch_shapes=[pltpu.VMEM((tm, tn), F32)] if nk > 1 else [],
        compiler_params=_cp("parallel", "parallel", "arbitrary"),
        name=name,
    )(*args)


def _rms_fwd(h, g, name):
    r, d = h.shape
    tr = _tile(r, NORM_ROWS, 8)

    def body(h_ref, g_ref, o_ref):
        x = h_ref[...]
        rstd = lax.rsqrt(jnp.mean(x * x, axis=-1, keepdims=True) + NORM_EPS)
        o_ref[...] = ((x * rstd) * g_ref[...]).astype(o_ref.dtype)

    return pl.pallas_call(
        body,
        out_shape=jax.ShapeDtypeStruct((r, d), BF16),
        grid=(r // tr,),
        in_specs=[pl.BlockSpec((tr, d), lambda i: (i, 0)), pl.BlockSpec((1, d), lambda i: (0, 0))],
        out_specs=pl.BlockSpec((tr, d), lambda i: (i, 0)),
        compiler_params=_cp("parallel"),
        name=name,
    )(h, g)


def _rms_bwd(h, g, dy, dres, name):
    r, d = h.shape
    tr = _tile(r, NORM_ROWS, 8)
    need_dh = dres is not None

    def body(*refs):
        if need_dh:
            h_ref, g_ref, dy_ref, dres_ref, dh_ref, dg_ref = refs
        else:
            h_ref, g_ref, dy_ref, dg_ref = refs
        i = pl.program_id(0)
        x = h_ref[...]
        rstd = lax.rsqrt(jnp.mean(x * x, axis=-1, keepdims=True) + NORM_EPS)
        nrm = x * rstd
        dyv = dy_ref[...].astype(F32)
        part = jnp.sum(dyv * nrm, axis=0, keepdims=True)

        @pl.when(i == 0)
        def _():
            dg_ref[...] = part

        @pl.when(i > 0)
        def _():
            dg_ref[...] += part

        if need_dh:
            gy = dyv * g_ref[...]
            dx = rstd * (gy - nrm * jnp.mean(gy * nrm, axis=-1, keepdims=True))
            dh_ref[...] = dres_ref[...] + dx

    row = pl.BlockSpec((tr, d), lambda i: (i, 0))
    vec = pl.BlockSpec((1, d), lambda i: (0, 0))
    if need_dh:
        return pl.pallas_call(
            body,
            out_shape=(jax.ShapeDtypeStruct((r, d), F32), jax.ShapeDtypeStruct((1, d), F32)),
            grid=(r // tr,),
            in_specs=[row, vec, row, row],
            out_specs=(row, vec),
            compiler_params=_cp("arbitrary"),
            name=name,
        )(h, g, dy, dres)
    return None, pl.pallas_call(
        body,
        out_shape=jax.ShapeDtypeStruct((1, d), F32),
        grid=(r // tr,),
        in_specs=[row, vec, row],
        out_specs=vec,
        compiler_params=_cp("arbitrary"),
        name=name,
    )(h, g, dy)


def _loss_head(h, g, target, name):
    r, d = h.shape
    tr = _tile(r, NORM_ROWS, 8)

    def body(h_ref, g_ref, t_ref, loss_ref, dh_ref, dg_ref):
        i = pl.program_id(0)
        x = h_ref[...]
        gv = g_ref[...]
        rstd = lax.rsqrt(jnp.mean(x * x, axis=-1, keepdims=True) + NORM_EPS)
        nrm = x * rstd
        err = nrm * gv - t_ref[...]
        lpart = 0.5 * jnp.sum(jnp.mean(err * err, axis=-1, keepdims=True), axis=0, keepdims=True)
        dyv = err * (1.0 / d)
        gpart = jnp.sum(dyv * nrm, axis=0, keepdims=True)

        @pl.when(i == 0)
        def _():
            loss_ref[...] = jnp.broadcast_to(lpart, loss_ref.shape)
            dg_ref[...] = gpart

        @pl.when(i > 0)
        def _():
            loss_ref[...] += jnp.broadcast_to(lpart, loss_ref.shape)
            dg_ref[...] += gpart

        gy = dyv * gv
        dh_ref[...] = rstd * (gy - nrm * jnp.mean(gy * nrm, axis=-1, keepdims=True))

    row = pl.BlockSpec((tr, d), lambda i: (i, 0))
    vec = pl.BlockSpec((1, d), lambda i: (0, 0))
    lsp = pl.BlockSpec((1, 128), lambda i: (0, 0))
    return pl.pallas_call(
        body,
        out_shape=(jax.ShapeDtypeStruct((1, 128), F32), jax.ShapeDtypeStruct((r, d), F32), jax.ShapeDtypeStruct((1, d), F32)),
        grid=(r // tr,),
        in_specs=[row, vec, row],
        out_specs=(lsp, row, vec),
        compiler_params=_cp("arbitrary"),
        name=name,
    )(h, g, target)


def _lane(shape):
    return lax.broadcasted_iota(jnp.int32, shape, 1)


def _head_tile(pair_tile, odd):
    return pltpu.roll(pair_tile, 64, 1) if odd else pair_tile


def _pair_tile(even_tile, odd_tile):
    lane = _lane(even_tile.shape)
    return jnp.where(lane < 64, even_tile, pltpu.roll(odd_tile, 64, 1))


def _pad_heads(xs, scales, out_widths, name):
    r = xs[0][0].shape[0]
    tr = _tile(r, HEAD_ROWS, 8)
    n_in = len(xs)

    def body(*refs):
        for idx in range(n_in):
            x_ref, o_ref = refs[idx], refs[n_in + idx]
            nh = xs[idx][2]
            for p in range(nh // 2):
                t = x_ref[:, p * 128:(p + 1) * 128].astype(F32) * scales[idx]
                lane = _lane(t.shape)
                o_ref[:, (2 * p) * HP:(2 * p + 1) * HP] = jnp.where(lane < 64, t, 0.0).astype(o_ref.dtype)
                o_ref[:, (2 * p + 1) * HP:(2 * p + 2) * HP] = jnp.where(lane < 64, pltpu.roll(t, 64, 1), 0.0).astype(o_ref.dtype)

    in_specs, args, out_specs, out_shape = [], [], [], []
    for (arr, c0, nh), w in zip(xs, out_widths):
        wcols = nh * 64
        assert c0 % wcols == 0 or c0 == 0
        blk = c0 // wcols if wcols else 0
        in_specs.append(pl.BlockSpec((tr, wcols), functools.partial(lambda i, b: (i, b), b=blk)))
        args.append(arr)
        out_specs.append(pl.BlockSpec((tr, w), lambda i: (i, 0)))
        out_shape.append(jax.ShapeDtypeStruct((r, w), BF16))
    return pl.pallas_call(
        body,
        out_shape=tuple(out_shape),
        grid=(r // tr,),
        in_specs=in_specs,
        out_specs=tuple(out_specs),
        compiler_params=_cp("parallel"),
        name=name,
    )(*args)


def _unpad_heads(xs, scales, out_dtype, name):
    r = xs[0].shape[0]
    tr = _tile(r, HEAD_ROWS, 8)
    nhs = [x.shape[1] // HP for x in xs]
    total = sum(nhs) * 64

    def body(*refs):
        o_ref = refs[-1]
        col = 0
        for x_ref, nh, sc in zip(refs[:-1], nhs, scales):
            for p in range(nh // 2):
                ev = x_ref[:, (2 * p) * HP:(2 * p + 1) * HP].astype(F32)
                od = x_ref[:, (2 * p + 1) * HP:(2 * p + 2) * HP].astype(F32)
                o_ref[:, col:col + 128] = (_pair_tile(ev, od) * sc).astype(o_ref.dtype)
                col += 128

    return pl.pallas_call(
        body,
        out_shape=jax.ShapeDtypeStruct((r, total), out_dtype),
        grid=(r // tr,),
        in_specs=[pl.BlockSpec((tr, x.shape[1]), lambda i: (i, 0)) for x in xs],
        out_specs=pl.BlockSpec((tr, total), lambda i: (i, 0)),
        compiler_params=_cp("parallel"),
        name=name,
    )(*xs)


def _bf16_split3(c):
    hi = c.astype(BF16).astype(F32)
    r1 = c - hi
    mid = r1.astype(BF16).astype(F32)
    lo = (r1 - mid).astype(BF16).astype(F32)
    return hi, mid, lo


def _log_sigmoid(z):
    return jnp.minimum(z, 0.0) - jnp.log(1.0 + jnp.exp(-jnp.abs(z)))


def _prep_fox_fwd(proj, bfg, name):
    s = proj.shape[0]
    ts = _tile(s, HEAD_ROWS, 8)
    fcol = 3 * D_MIX + D_MEMQ

    def body(p_ref, b_ref, q_ref, k_ref, v_ref, qm_ref, carry_ref):
        i = pl.program_id(0)

        @pl.when(i == 0)
        def _():
            carry_ref[...] = jnp.zeros_like(carry_ref)

        lane = _lane((ts, 128))
        z = p_ref[:, fcol:fcol + 128] + b_ref[...]
        logf = jnp.where(lane < N_MIX, _log_sigmoid(z), 0.0)
        rr = lax.broadcasted_iota(jnp.int32, (ts, ts), 0)
        cc = lax.broadcasted_iota(jnp.int32, (ts, ts), 1)
        tri = jnp.where(cc <= rr, 1.0, 0.0).astype(F32)
        c = jnp.dot(tri, logf, preferred_element_type=F32, precision=lax.Precision.HIGHEST) + carry_ref[0:1, :]
        carry_ref[...] = jnp.broadcast_to(c[ts - 1:ts, :], carry_ref.shape)
        for hh in range(N_MIX):
            p, odd = hh // 2, hh % 2
            ch = jnp.sum(jnp.where(lane == hh, c, 0.0), axis=-1, keepdims=True) * LOG2E
            hi, mid, lo = _bf16_split3(ch)
            qt = _head_tile(p_ref[:, p * 128:(p + 1) * 128], odd) * QSCALE
            kt = _head_tile(p_ref[:, D_MIX + p * 128:D_MIX + (p + 1) * 128], odd)
            vt = _head_tile(p_ref[:, 2 * D_MIX + p * 128:2 * D_MIX + (p + 1) * 128], odd)
            qa = jnp.where(lane == 64, hi, jnp.where(lane == 65, mid, jnp.where(lane == 66, lo, jnp.where(lane < 70, 1.0, 0.0))))
            ka = jnp.where(lane < 67, 1.0, jnp.where(lane == 67, -hi, jnp.where(lane == 68, -mid, jnp.where(lane == 69, -lo, 0.0))))
            q_ref[:, hh * HP:(hh + 1) * HP] = jnp.where(lane < 64, qt, qa).astype(BF16)
            k_ref[:, hh * HP:(hh + 1) * HP] = jnp.where(lane < 64, kt, ka).astype(BF16)
            v_ref[:, hh * HP:(hh + 1) * HP] = jnp.where(lane < 64, vt, 0.0).astype(BF16)
        for hh in range(N_MEM):
            p, odd = hh // 2, hh % 2
            t = _head_tile(p_ref[:, 3 * D_MIX + p * 128:3 * D_MIX + (p + 1) * 128], odd) * QSCALE
            qm_ref[:, hh * HP:(hh + 1) * HP] = jnp.where(lane < 64, t, 0.0).astype(BF16)

    wmix, wmem = N_MIX * HP, N_MEM * HP
    return pl.pallas_call(
        body,
        out_shape=(jax.ShapeDtypeStruct((s, wmix), BF16),) * 3 + (jax.ShapeDtypeStruct((s, wmem), BF16),),
        grid=(s // ts,),
        in_specs=[pl.BlockSpec((ts, proj.shape[1]), lambda i: (i, 0)), pl.BlockSpec((1, 128), lambda i: (0, 0))],
        out_specs=(pl.BlockSpec((ts, wmix), lambda i: (i, 0)),) * 3 + (pl.BlockSpec((ts, wmem), lambda i: (i, 0)),),
        scratch_shapes=[pltpu.VMEM((8, 128), F32)],
        compiler_params=_cp("arbitrary"),
        name=name,
    )(proj, bfg)


def _prep_fox_bwd(dq, dk, dv, dqm, proj, bfg, name):
    s = proj.shape[0]
    ts = _tile(s, 256, 8)
    nb = s // ts
    fcol = 3 * D_MIX + D_MEMQ

    def body(dq_ref, dk_ref, dv_ref, dqm_ref, p_ref, b_ref, o_ref, db_ref, carry_ref):
        i = pl.program_id(0)

        @pl.when(i == 0)
        def _():
            carry_ref[...] = jnp.zeros_like(carry_ref)
            db_ref[...] = jnp.zeros_like(db_ref)

        lane = _lane((ts, 128))
        dc = jnp.zeros((ts, 128), F32)
        for p in range(N_MIX // 2):
            tq, tk, tv = [], [], []
            for odd in (0, 1):
                hh = 2 * p + odd
                dqt = dq_ref[:, hh * HP:(hh + 1) * HP]
                dkt = dk_ref[:, hh * HP:(hh + 1) * HP]
                col = jnp.sum(jnp.where(lane == 64, dqt, 0.0) - jnp.where(lane == 67, dkt, 0.0), axis=-1, keepdims=True)
                dc = dc + jnp.where(lane == hh, col, 0.0)
                tq.append(dqt)
                tk.append(dkt)
                tv.append(dv_ref[:, hh * HP:(hh + 1) * HP])
            o_ref[:, p * 128:(p + 1) * 128] = (_pair_tile(tq[0], tq[1]) * SCALE).astype(o_ref.dtype)
            o_ref[:, D_MIX + p * 128:D_MIX + (p + 1) * 128] = (_pair_tile(tk[0], tk[1]) * LN2).astype(o_ref.dtype)
            o_ref[:, 2 * D_MIX + p * 128:2 * D_MIX + (p + 1) * 128] = _pair_tile(tv[0], tv[1]).astype(o_ref.dtype)
        for p in range(N_MEM // 2):
            ev = dqm_ref[:, (2 * p) * HP:(2 * p + 1) * HP]
            od = dqm_ref[:, (2 * p + 1) * HP:(2 * p + 2) * HP]
            o_ref[:, 3 * D_MIX + p * 128:3 * D_MIX + (p + 1) * 128] = (_pair_tile(ev, od) * SCALE).astype(o_ref.dtype)
        rr = lax.broadcasted_iota(jnp.int32, (ts, ts), 0)
        cc = lax.broadcasted_iota(jnp.int32, (ts, ts), 1)
        triu = jnp.where(cc >= rr, 1.0, 0.0).astype(F32)
        dlogf = jnp.dot(triu, dc, preferred_element_type=F32, precision=lax.Precision.HIGHEST) + carry_ref[0:1, :]
        carry_ref[...] = jnp.broadcast_to(dlogf[0:1, :], carry_ref.shape)
        z = p_ref[:, fcol:fcol + 128] + b_ref[...]
        dz = jnp.where(lane < N_MIX, dlogf / (1.0 + jnp.exp(z)), 0.0)
        o_ref[:, fcol:fcol + 128] = dz.astype(o_ref.dtype)
        db_ref[...] += jnp.sum(dz, axis=0, keepdims=True)

    wmix, wmem = N_MIX * HP, N_MEM * HP
    rev = lambda i: (nb - 1 - i, 0)
    return pl.pallas_call(
        body,
        out_shape=(jax.ShapeDtypeStruct(proj.shape, BF16), jax.ShapeDtypeStruct((1, 128), F32)),
        grid=(nb,),
        in_specs=[pl.BlockSpec((ts, wmix), rev)] * 3 + [pl.BlockSpec((ts, wmem), rev), pl.BlockSpec((ts, proj.shape[1]), rev),
                                                         pl.BlockSpec((1, 128), lambda i: (0, 0))],
        out_specs=(pl.BlockSpec((ts, proj.shape[1]), rev), pl.BlockSpec((1, 128), lambda i: (0, 0))),
        scratch_shapes=[pltpu.VMEM((8, 128), F32)],
        compiler_params=_cp("arbitrary"),
        name=name,
    )(dq, dk, dv, dqm, proj, bfg)


def _rope_partner(x):
    lane = _lane(x.shape)
    return jnp.where((lane % 64) < 32, pltpu.roll(x, 96, 1), pltpu.roll(x, 32, 1))


def _prep_dil_fwd(proj, cos2, sin2, name):
    s = proj.shape[0]
    ts = _tile(s, HEAD_ROWS, 8)

    def body(p_ref, c_ref, s_ref, q_ref, k_ref, v_ref, qm_ref):
        lane = _lane((ts, 128))
        cosv, sinv = c_ref[...], s_ref[...]
        for p in range(N_MIX // 2):
            xq = p_ref[:, p * 128:(p + 1) * 128]
            xk = p_ref[:, D_MIX + p * 128:D_MIX + (p + 1) * 128]
            xv = p_ref[:, 2 * D_MIX + p * 128:2 * D_MIX + (p + 1) * 128]
            yq = (xq * cosv + _rope_partner(xq) * sinv) * QSCALE
            yk = xk * cosv + _rope_partner(xk) * sinv
            for odd in (0, 1):
                hh = 2 * p + odd
                q_ref[:, hh * HP:(hh + 1) * HP] = jnp.where(lane < 64, _head_tile(yq, odd), 0.0).astype(BF16)
                k_ref[:, hh * HP:(hh + 1) * HP] = jnp.where(lane < 64, _head_tile(yk, odd), 0.0).astype(BF16)
                v_ref[:, hh * HP:(hh + 1) * HP] = jnp.where(lane < 64, _head_tile(xv, odd), 0.0).astype(BF16)
        for p in range(N_MEM // 2):
            t = p_ref[:, 3 * D_MIX + p * 128:3 * D_MIX + (p + 1) * 128] * QSCALE
            for odd in (0, 1):
                hh = 2 * p + odd
                qm_ref[:, hh * HP:(hh + 1) * HP] = jnp.where(lane < 64, _head_tile(t, odd), 0.0).astype(BF16)

    wmix, wmem = N_MIX * HP, N_MEM * HP
    return pl.pallas_call(
        body,
        out_shape=(jax.ShapeDtypeStruct((s, wmix), BF16),) * 3 + (jax.ShapeDtypeStruct((s, wmem), BF16),),
        grid=(s // ts,),
        in_specs=[pl.BlockSpec((ts, proj.shape[1]), lambda i: (i, 0)), pl.BlockSpec((ts, 128), lambda i: (i, 0)),
                  pl.BlockSpec((ts, 128), lambda i: (i, 0))],
        out_specs=(pl.BlockSpec((ts, wmix), lambda i: (i, 0)),) * 3 + (pl.BlockSpec((ts, wmem), lambda i: (i, 0)),),
        compiler_params=_cp("parallel"),
        name=name,
    )(proj, cos2, sin2)


def _prep_dil_bwd(dq, dk, dv, dqm, cos2, sin2, name):
    s = dq.shape[0]
    ts = _tile(s, 256, 8)

    def body(dq_ref, dk_ref, dv_ref, dqm_ref, c_ref, s_ref, o_ref):
        cosv, sinv = c_ref[...], s_ref[...]
        for p in range(N_MIX // 2):
            e, o = 2 * p, 2 * p + 1
            dyq = _pair_tile(dq_ref[:, e * HP:(e + 1) * HP], dq_ref[:, o * HP:(o + 1) * HP]) * SCALE
            dyk = _pair_tile(dk_ref[:, e * HP:(e + 1) * HP], dk_ref[:, o * HP:(o + 1) * HP]) * LN2
            dxv = _pair_tile(dv_ref[:, e * HP:(e + 1) * HP], dv_ref[:, o * HP:(o + 1) * HP])
            o_ref[:, p * 128:(p + 1) * 128] = (dyq * cosv - _rope_partner(dyq) * sinv).astype(o_ref.dtype)
            o_ref[:, D_MIX + p * 128:D_MIX + (p + 1) * 128] = (dyk * cosv - _rope_partner(dyk) * sinv).astype(o_ref.dtype)
            o_ref[:, 2 * D_MIX + p * 128:2 * D_MIX + (p + 1) * 128] = dxv.astype(o_ref.dtype)
        for p in range(N_MEM // 2):
            e, o = 2 * p, 2 * p + 1
            t = _pair_tile(dqm_ref[:, e * HP:(e + 1) * HP], dqm_ref[:, o * HP:(o + 1) * HP]) * SCALE
            o_ref[:, 3 * D_MIX + p * 128:3 * D_MIX + (p + 1) * 128] = t.astype(o_ref.dtype)

    wmix, wmem = N_MIX * HP, N_MEM * HP
    row = lambda w: pl.BlockSpec((ts, w), lambda i: (i, 0))
    return pl.pallas_call(
        body,
        out_shape=jax.ShapeDtypeStruct((s, DIL_IN), BF16),
        grid=(s // ts,),
        in_specs=[row(wmix)] * 3 + [row(wmem), row(128), row(128)],
        out_specs=row(DIL_IN),
        compiler_params=_cp("parallel"),
        name=name,
    )(dq, dk, dv, dqm, cos2, sin2)


def _mask_bias(mode, blk):
    n = 1 if mode == "causal" else DIL_MAX // blk + 1
    idx = jnp.arange(blk, dtype=jnp.int32)
    dist = jnp.arange(n, dtype=jnp.int32)[:, None, None] * blk + idx[None, :, None] - idx[None, None, :]
    if mode == "causal":
        return jnp.where(dist >= 0, 0.0, NEG).astype(F32)
    cnt = ((dist <= 128).astype(jnp.int32) + ((dist <= 512) & (dist % 4 == 0)).astype(jnp.int32)
           + ((dist <= DIL_MAX) & (dist % 16 == 0)).astype(jnp.int32))
    bias = jnp.where(cnt == 3, math.log2(3.0), jnp.where(cnt == 2, 1.0, 0.0))
    return jnp.where((dist >= 0) & (cnt > 0), bias, NEG).astype(F32)


def _flash_dims(q, k, mode):
    sq, w = q.shape
    sk = k.shape[0]
    tq = _tile(sq, FLASH_BLK, 8)
    tk = sk if mode == "full" else tq
    band = DIL_MAX // tk if mode == "dilated" else None
    return sq, sk, w, tq, tk, sq // tq, sk // tk, band


def _heads_per_step(n_heads, most):
    return max(h for h in range(1, most + 1) if n_heads % h == 0)


def _pair_starts(mode, nq, band, key_major):
    if mode == "full":
        counts = [nq] if key_major else [1] * nq
    elif key_major:
        counts = [(min(kj + band, nq - 1) if mode == "dilated" else nq - 1) - kj + 1 for kj in range(nq)]
    else:
        counts = [qi - (max(qi - band, 0) if mode == "dilated" else 0) + 1 for qi in range(nq)]
    starts = [int(v) for v in np.cumsum([0] + counts)]
    return starts[:-1], starts[-1]


def _pair_at(t, starts, mode, band, key_major):
    if mode == "full":
        return t, jnp.zeros((), jnp.int32)
    major, begin = 0, 0
    for prev, s in zip(starts[:-1], starts[1:]):
        reached = t >= s
        major = major + reached.astype(jnp.int32)
        begin = begin + jnp.where(reached, s - prev, 0)
    if key_major:
        return major + (t - begin), major
    return major, _first_key(mode, major, band) + (t - begin)


def _first_key(mode, qi, band):
    return jnp.maximum(qi - band, 0) if mode == "dilated" else 0


def _carried(comm, grid):
    if comm is None:
        return 0, 0, [], (lambda *a: None), (lambda *a: None)
    total = int(np.prod(grid))
    relay_at = (3 * total) // 4

    def step_index():
        idx = 0
        for d, n in enumerate(grid):
            idx = idx * n + pl.program_id(d)
        return idx

    def start(cins, couts, sems):
        pl.when(step_index() == 0)(lambda: comm.start(cins, couts, *sems))
        if 0 < relay_at < total - 1:
            pl.when(step_index() == relay_at)(lambda: comm.relay(cins, couts, *sems))

    def finish(cins, couts, sems):
        def last():
            if not 0 < relay_at < total - 1:
                comm.relay(cins, couts, *sems)
            comm.finish(cins, couts, *sems)

        pl.when(step_index() == total - 1)(last)

    sems = [pltpu.SemaphoreType.DMA((comm.n_sems,)), pltpu.SemaphoreType.DMA((comm.n_sems,))]
    return len(comm.inputs), len(comm.out_shapes), sems, start, finish


def _split_refs(refs, n_in, n_cin, n_out, n_cout, n_scratch):
    bounds = np.cumsum([0, n_in, n_cin, n_out, n_cout, n_scratch])
    return [refs[a:b] for a, b in zip(bounds[:-1], bounds[1:])] + [refs[bounds[-1]:]]


def _flash_fwd(q, k, v, mode, name, comm=None):
    sq, sk, w, tq, tk, nq, nk, band = _flash_dims(q, k, mode)
    hb = _heads_per_step(w // HP, FWD_HEADS_PER_STEP)
    wb = hb * HP
    nch = tk // 128
    has_bias = mode != "full"
    starts, n_pairs = _pair_starts(mode, nq, band, key_major=False)
    pair = functools.partial(_pair_at, starts=starts, mode=mode, band=band, key_major=False)
    grid = (w // wb, n_pairs)
    n_cin, n_cout, comm_sems, comm_start, comm_finish = _carried(comm, grid)

    def body(*refs):
        ins, cins, (o_ref, lse_ref), couts, (m_ref, l_ref, acc_ref), sems = _split_refs(refs, 3 + has_bias, n_cin, 2, n_cout, 3)
        q_ref, k_ref, v_ref = ins[:3]
        b_ref = ins[3] if has_bias else None
        qi, kj = pair(pl.program_id(1))
        comm_start(cins, couts, sems)

        @pl.when(kj == _first_key(mode, qi, band))
        def _():
            m_ref[...] = jnp.full_like(m_ref, -jnp.inf)
            l_ref[...] = jnp.zeros_like(l_ref)
            acc_ref[...] = jnp.zeros_like(acc_ref)

        def step(bias_tile):
            for h in range(hb):
                cols = slice(h * HP, (h + 1) * HP)
                sc = lax.dot_general(q_ref[:, cols], k_ref[:, cols], _DIMS["nt"], preferred_element_type=F32)
                if bias_tile is not None:
                    sc = sc + bias_tile()
                m_prev = m_ref[h]
                m_new = jnp.maximum(m_prev, jnp.max(sc, axis=-1, keepdims=True))
                alpha = jnp.exp2(m_prev - m_new)
                psum, chunks = None, []
                for c in range(nch):
                    pc = jnp.exp2(sc[:, c * 128:(c + 1) * 128] - m_new)
                    psum = pc if psum is None else psum + pc
                    chunks.append(pc.astype(BF16))
                p = chunks[0] if nch == 1 else jnp.concatenate(chunks, axis=1)
                l_ref[h] = alpha * l_ref[h] + psum
                acc_ref[h] = alpha * acc_ref[h] + jnp.dot(p, v_ref[:, cols], preferred_element_type=F32)
                m_ref[h] = m_new

        if mode == "full":
            step(None)
        elif mode == "causal":
            pl.when(kj == qi)(lambda: step(lambda: b_ref[0]))
            pl.when(kj < qi)(lambda: step(None))
        else:
            step(lambda: b_ref[qi - kj])

        @pl.when(kj == (0 if mode == "full" else qi))
        def _():
            for h in range(hb):
                cols = slice(h * HP, (h + 1) * HP)
                l = jnp.sum(l_ref[h], axis=-1, keepdims=True)
                o_ref[:, cols] = (acc_ref[h] / l).astype(o_ref.dtype)
                lse_ref[:, cols] = m_ref[h] + jnp.log2(l)

        comm_finish(cins, couts, sems)

    qspec = pl.BlockSpec((tq, wb), lambda hp, t: (pair(t)[0], hp))
    kspec = pl.BlockSpec((tk, wb), lambda hp, t: (pair(t)[1], hp))
    in_specs = [qspec, kspec, kspec]
    args = [q, k, v]
    if has_bias:
        bias = _mask_bias(mode, tq)
        in_specs.append(pl.BlockSpec(bias.shape, lambda hp, t: (0, 0, 0), pipeline_mode=pl.Buffered(1)))
        args.append(bias)
    out_shape = [jax.ShapeDtypeStruct((sq, w), BF16), jax.ShapeDtypeStruct((sq, w), F32)]
    out_specs = [qspec, qspec]
    if comm is not None:
        in_specs += [_ANY] * n_cin
        args += list(comm.inputs)
        out_shape += list(comm.out_shapes)
        out_specs += [_ANY] * n_cout
    res = pl.pallas_call(
        body,
        out_shape=tuple(out_shape),
        grid=grid,
        in_specs=in_specs,
        out_specs=tuple(out_specs),
        scratch_shapes=[pltpu.VMEM((hb, tq, HP), F32), pltpu.VMEM((hb, tq, HP), F32), pltpu.VMEM((hb, tq, HP), F32)] + comm_sems,
        compiler_params=_cp("parallel", "arbitrary") if comm is None else _cp("arbitrary", "arbitrary"),
        name=name,
    )(*args)
    return res[0], res[1], list(res[2:])


def _flash_bwd(q, k, v, o, do, lse, mode, name, comm=None):
    sq, sk, w, tq, tk, nq, nk, band = _flash_dims(q, k, mode)
    hb = _heads_per_step(w // HP, BWD_HEADS_PER_STEP)
    wb = hb * HP
    nch = tk // 128
    has_bias = mode != "full"
    starts, n_pairs = _pair_starts(mode, nq, band, key_major=True)
    pair = functools.partial(_pair_at, starts=starts, mode=mode, band=band, key_major=True)
    grid = (w // wb, n_pairs)
    n_cin, n_cout, comm_sems, comm_start, comm_finish = _carried(comm, grid)

    def body(*refs):
        ins, cins, (dq_ref, dk_ref, dv_ref), couts, (delta_ref,), sems = _split_refs(refs, 6 + has_bias, n_cin, 3, n_cout, 1)
        q_ref, k_ref, v_ref, o_ref, do_ref, lse_ref = ins[:6]
        b_ref = ins[6] if has_bias else None
        qi, kj = pair(pl.program_id(1))
        rows = pl.ds(pl.multiple_of(qi * tq, tq), tq)
        comm_start(cins, couts, sems)

        @pl.when(pl.program_id(1) == 0)
        def _():
            dq_ref[...] = jnp.zeros_like(dq_ref)

        @pl.when(qi == (0 if mode == "full" else kj))
        def _():
            dk_ref[...] = jnp.zeros_like(dk_ref)
            dv_ref[...] = jnp.zeros_like(dv_ref)

        @pl.when(kj == _first_key(mode, qi, band))
        def _():
            for h in range(hb):
                cols = slice(h * HP, (h + 1) * HP)
                dl = jnp.sum(do_ref[:, cols].astype(F32) * o_ref[:, cols].astype(F32), axis=-1, keepdims=True)
                delta_ref[h, rows, :] = jnp.broadcast_to(dl, (tq, HP))

        def step(bias_tile):
            for h in range(hb):
                cols = slice(h * HP, (h + 1) * HP)
                qv, kv, dov = q_ref[:, cols], k_ref[:, cols], do_ref[:, cols]
                sc = lax.dot_general(qv, kv, _DIMS["nt"], preferred_element_type=F32)
                if bias_tile is not None:
                    sc = sc + bias_tile()
                dp = lax.dot_general(dov, v_ref[:, cols], _DIMS["nt"], preferred_element_type=F32)
                lse_b = lse_ref[:, cols]
                dlt = delta_ref[h, rows, :]
                pch, dsch = [], []
                for c in range(nch):
                    lanes = slice(c * 128, (c + 1) * 128)
                    pc = jnp.exp2(sc[:, lanes] - lse_b)
                    pch.append(pc.astype(BF16))
                    dsch.append((pc * (dp[:, lanes] - dlt)).astype(BF16))
                p = pch[0] if nch == 1 else jnp.concatenate(pch, axis=1)
                ds = dsch[0] if nch == 1 else jnp.concatenate(dsch, axis=1)
                dv_ref[:, cols] += lax.dot_general(p, dov, _DIMS["tn"], preferred_element_type=F32)
                dk_ref[:, cols] += lax.dot_general(ds, qv, _DIMS["tn"], preferred_element_type=F32)
                dq_ref[rows, cols] += jnp.dot(ds, kv, preferred_element_type=F32)

        if mode == "full":
            step(None)
        elif mode == "causal":
            pl.when(qi == kj)(lambda: step(lambda: b_ref[0]))
            pl.when(qi > kj)(lambda: step(None))
        else:
            step(lambda: b_ref[qi - kj])

        comm_finish(cins, couts, sems)

    qspec = pl.BlockSpec((tq, wb), lambda hp, t: (pair(t)[0], hp))
    kspec = pl.BlockSpec((tk, wb), lambda hp, t: (pair(t)[1], hp))
    in_specs = [qspec, kspec, kspec, qspec, qspec, qspec]
    args = [q, k, v, o, do, lse]
    if has_bias:
        bias = _mask_bias(mode, tq)
        in_specs.append(pl.BlockSpec(bias.shape, lambda hp, t: (0, 0, 0), pipeline_mode=pl.Buffered(1)))
        args.append(bias)
    out_shape = [jax.ShapeDtypeStruct((sq, w), F32), jax.ShapeDtypeStruct((sk, w), F32), jax.ShapeDtypeStruct((sk, w), F32)]
    out_specs = [pl.BlockSpec((sq, wb), lambda hp, t: (0, hp)), kspec, kspec]
    if comm is not None:
        in_specs += [_ANY] * n_cin
        args += list(comm.inputs)
        out_shape += list(comm.out_shapes)
        out_specs += [_ANY] * n_cout
    res = pl.pallas_call(
        body,
        out_shape=tuple(out_shape),
        grid=grid,
        in_specs=in_specs,
        out_specs=tuple(out_specs),
        scratch_shapes=[pltpu.VMEM((hb, sq, HP), F32)] + comm_sems,
        compiler_params=_cp("parallel", "arbitrary") if comm is None else _cp("arbitrary", "arbitrary"),
        name=name,
    )(*args)
    return res[0], res[1], res[2], list(res[3:])


def _conv_rc(s):
    return _tile(s, 256, 8)


def _shift_down(x, prev8, nrows):
    rows = lax.broadcasted_iota(jnp.int32, x.shape, 0)
    out = pltpu.roll(x, nrows, 0)
    for i in range(nrows):
        out = jnp.where(rows == i, prev8[8 - nrows + i:8 - nrows + i + 1, :], out)
    return out


def _shift_up(x, next8, nrows):
    n = x.shape[0]
    rows = lax.broadcasted_iota(jnp.int32, x.shape, 0)
    out = pltpu.roll(x, n - nrows, 0)
    for i in range(nrows):
        out = jnp.where(rows == n - nrows + i, next8[i:i + 1, :], out)
    return out


def _conv_taps(uv_ref, ug_ref, r, rc):
    r0 = pl.multiple_of(r * rc, rc)
    x = jnp.concatenate([uv_ref[pl.ds(r0, rc), :], ug_ref[pl.ds(r0, rc), :]], axis=1)
    p0 = pl.multiple_of(jnp.maximum(r0 - 8, 0), 8)
    prev8 = jnp.where(r > 0, jnp.concatenate([uv_ref[pl.ds(p0, 8), :], ug_ref[pl.ds(p0, 8), :]], axis=1), 0.0)
    return r0, x, _shift_down(x, prev8, 1), _shift_down(x, prev8, 2)


def _conv_specs(s, tf, nf):
    strip = pl.BlockSpec((s, tf), lambda j: (0, j))
    return strip, [pl.BlockSpec((3, tf), lambda j: (0, j)), pl.BlockSpec((3, tf), lambda j: (0, j + nf)),
                   pl.BlockSpec((1, tf), lambda j: (0, j)), pl.BlockSpec((1, tf), lambda j: (0, j + nf))]


def _conv_params(wv_ref, wg_ref, bv_ref, bg_ref):
    w = jnp.concatenate([wv_ref[...], wg_ref[...]], axis=1)
    return w[0:1, :], w[1:2, :], w[2:3, :], jnp.concatenate([bv_ref[...], bg_ref[...]], axis=1)


def _conv_fwd(u_val, u_gate, cw, cb, name):
    s, f = u_val.shape
    tf = CONV_TF
    nf = f // tf
    rc = _conv_rc(s)

    def body(uv_ref, ug_ref, wv_ref, wg_ref, bv_ref, bg_ref, a_ref):
        w0, w1, w2, b = _conv_params(wv_ref, wg_ref, bv_ref, bg_ref)

        def chunk(r, carry):
            r0, x, x1, x2 = _conv_taps(uv_ref, ug_ref, r, rc)
            c = b + w0 * x2 + w1 * x1 + w2 * x
            val, gate = c[:, :tf], c[:, tf:]
            a_ref[pl.ds(r0, rc), :] = (gate * jax.nn.sigmoid(gate) * val).astype(a_ref.dtype)
            return carry

        lax.fori_loop(0, s // rc, chunk, 0)

    strip, params = _conv_specs(s, tf, nf)
    return pl.pallas_call(
        body,
        out_shape=jax.ShapeDtypeStruct((s, f), BF16),
        grid=(nf,),
        in_specs=[strip, strip] + params,
        out_specs=strip,
        compiler_params=_cp("parallel"),
        name=name,
    )(u_val, u_gate, cw, cw, cb, cb)


def _conv_bwd(u_val, u_gate, da, cw, cb, name):
    s, f = u_val.shape
    tf = CONV_TF
    nf = f // tf
    rc = _conv_rc(s)
    nchunk = s // rc

    def body(uv_ref, ug_ref, da_ref, wv_ref, wg_ref, bv_ref, bg_ref, duv_ref, dug_ref, dwv_ref, dwg_ref, dbv_ref, dbg_ref, next_ref):
        w0, w1, w2, b = _conv_params(wv_ref, wg_ref, bv_ref, bg_ref)
        next_ref[...] = jnp.zeros_like(next_ref)

        def chunk(it, carry):
            g0, g1, g2, gb = carry
            r = nchunk - 1 - it
            r0, x, x1, x2 = _conv_taps(uv_ref, ug_ref, r, rc)
            c = b + w0 * x2 + w1 * x1 + w2 * x
            val, gate = c[:, :tf], c[:, tf:]
            sg = jax.nn.sigmoid(gate)
            dav = da_ref[pl.ds(r0, rc), :]
            dc = jnp.concatenate([dav * (gate * sg), dav * val * (sg * (1.0 + gate * (1.0 - sg)))], axis=1)
            nxt = next_ref[...]
            du = (w2 * dc + w1 * _shift_up(dc, nxt, 1) + w0 * _shift_up(dc, nxt, 2)).astype(duv_ref.dtype)
            duv_ref[pl.ds(r0, rc), :] = du[:, :tf]
            dug_ref[pl.ds(r0, rc), :] = du[:, tf:]
            next_ref[...] = dc[0:8, :]
            return (g0 + jnp.sum(dc * x2, axis=0, keepdims=True), g1 + jnp.sum(dc * x1, axis=0, keepdims=True),
                    g2 + jnp.sum(dc * x, axis=0, keepdims=True), gb + jnp.sum(dc, axis=0, keepdims=True))

        zero = jnp.zeros((1, 2 * tf), F32)
        g0, g1, g2, gb = lax.fori_loop(0, nchunk, chunk, (zero, zero, zero, zero))
        for i, gi in enumerate((g0, g1, g2)):
            dwv_ref[i:i + 1, :] = gi[:, :tf]
            dwg_ref[i:i + 1, :] = gi[:, tf:]
        dbv_ref[...] = gb[:, :tf]
        dbg_ref[...] = gb[:, tf:]

    strip, params = _conv_specs(s, tf, nf)
    taps = pl.BlockSpec((3, tf), lambda j: (0, j))
    bias = pl.BlockSpec((1, tf), lambda j: (0, j))
    act = jax.ShapeDtypeStruct((s, f), BF16)
    return pl.pallas_call(
        body,
        out_shape=(act, act, jax.ShapeDtypeStruct((3, f), F32), jax.ShapeDtypeStruct((3, f), F32),
                   jax.ShapeDtypeStruct((1, f), F32), jax.ShapeDtypeStruct((1, f), F32)),
        grid=(nf,),
        in_specs=[strip, strip, strip] + params,
        out_specs=(strip, strip, taps, taps, bias, bias),
        scratch_shapes=[pltpu.VMEM((8, 2 * tf), F32)],
        compiler_params=_cp("parallel"),
        name=name,
    )(u_val, u_gate, da, cw, cw, cb, cb)


def _adamw(w, g, m, v, name):
    r, c = w.shape
    tr = _tile(r, 256, 8) if r % 8 == 0 else r
    c1 = 1.0 - ADAM_B1 ** ADAM_STEP
    c2 = 1.0 - ADAM_B2 ** ADAM_STEP

    def body(w_ref, g_ref, m_ref, v_ref, d_ref, mo_ref, vo_ref):
        gv = g_ref[...]
        mn = ADAM_B1 * m_ref[...] + (1.0 - ADAM_B1) * gv
        vn = ADAM_B2 * v_ref[...] + (1.0 - ADAM_B2) * (gv * gv)
        d_ref[...] = -ADAM_LR * ((mn / c1) / (jnp.sqrt(vn / c2) + ADAM_EPS) + ADAM_WD * w_ref[...])
        mo_ref[...] = mn
        vo_ref[...] = vn

    blk = pl.BlockSpec((tr, c), lambda i: (i, 0))
    shp = jax.ShapeDtypeStruct((r, c), F32)
    return pl.pallas_call(
        body, out_shape=(shp, shp, shp), grid=(r // tr,), in_specs=[blk] * 4, out_specs=(blk,) * 3,
        compiler_params=_cp("parallel"), name=name,
    )(w, g, m, v)


def _sum_rows(parts, out_dtype, name):
    n, r, c = parts.shape
    tr = _tile(r, 256, 8)

    def body(p_ref, o_ref):
        tot = p_ref[0].astype(F32)
        for i in range(1, n):
            tot = tot + p_ref[i].astype(F32)
        o_ref[...] = tot.astype(o_ref.dtype)

    return pl.pallas_call(
        body, out_shape=jax.ShapeDtypeStruct((r, c), out_dtype), grid=(r // tr,),
        in_specs=[pl.BlockSpec((n, tr, c), lambda i: (0, i, 0))], out_specs=pl.BlockSpec((tr, c), lambda i: (i, 0)),
        compiler_params=_cp("parallel"), name=name,
    )(parts)


_ANY = pl.BlockSpec(memory_space=pl.ANY)


def _place():
    return lax.axis_index("x"), lax.axis_index("y"), lax.axis_index("c")


def _other_chips(x, y):
    return [(1 - x, y), (x, 1 - y), (1 - x, 1 - y)]


def _rows_half(ref, c, axis):
    rh = ref.shape[axis] // 2
    idx = [slice(None)] * len(ref.shape)
    idx[axis] = pl.ds(pl.multiple_of(c * rh, 16), rh)
    return ref.at[tuple(idx)]


def _remote(src, dst, send_sems, recv_sems, kk, to):
    return pltpu.make_async_remote_copy(src_ref=src, dst_ref=dst, send_sem=send_sems.at[kk], recv_sem=recv_sems.at[kk],
                                        device_id=to, device_id_type=MESH)


class _GatherSpec:
    def __init__(self, shards):
        self.inputs = list(shards)
        self.out_shapes = [jax.ShapeDtypeStruct((N_CHIPS,) + s.shape, s.dtype) for s in shards]
        self.n_sems = 6 * len(shards)

    def _sends(self, ins, outs, send_sems, recv_sems):
        x, y, c = _place()
        me = 2 * x + y
        return [_remote(_rows_half(ins[a], c, 0), _rows_half(outs[a].at[me], c, 0), send_sems, recv_sems, 6 * a + j, (cx, cy, c))
                for a in range(len(ins)) for j, (cx, cy) in enumerate(_other_chips(x, y))]

    def _forwards(self, outs, send_sems, recv_sems):
        x, y, c = _place()
        return [_remote(_rows_half(outs[a].at[2 * cx + cy], c, 0), _rows_half(outs[a].at[2 * cx + cy], c, 0), send_sems, recv_sems,
                        6 * a + 3 + j, (x, y, 1 - c))
                for a in range(len(outs)) for j, (cx, cy) in enumerate(_other_chips(x, y))]

    def start(self, ins, outs, send_sems, recv_sems):
        for cp in self._sends(ins, outs, send_sems, recv_sems):
            cp.start()

    def relay(self, ins, outs, send_sems, recv_sems):
        x, y, c = _place()
        forwards = self._forwards(outs, send_sems, recv_sems)
        for a in range(len(outs)):
            for j, (cx, cy) in enumerate(_other_chips(x, y)):
                slot = _rows_half(outs[a].at[2 * cx + cy], c, 0)
                _remote(slot, slot, send_sems, recv_sems, 6 * a + j, (x, y, c)).wait_recv()
                forwards[3 * a + j].start()

    def finish(self, ins, outs, send_sems, recv_sems):
        x, y, c = _place()
        for a in range(len(outs)):
            for j, (cx, cy) in enumerate(_other_chips(x, y)):
                slot = _rows_half(outs[a].at[2 * cx + cy], 1 - c, 0)
                _remote(slot, slot, send_sems, recv_sems, 6 * a + 3 + j, (x, y, c)).wait_recv()
        for cp in self._sends(ins, outs, send_sems, recv_sems) + self._forwards(outs, send_sems, recv_sems):
            cp.wait_send()


_FLIPS = [(dx, dy, dc) for dx in (0, 1) for dy in (0, 1) for dc in (0, 1) if (dx, dy, dc) != (0, 0, 0)]


class _ScatterSpec:
    def __init__(self, parts):
        self.inputs = list(parts)
        self.out_shapes = [jax.ShapeDtypeStruct((8, p.shape[1] // 2, p.shape[2]), p.dtype) for p in parts]
        self.n_sems = 7 * len(parts)

    def _sends(self, ins, outs, send_sems, recv_sems):
        x, y, c = _place()
        me = 4 * x + 2 * y + c
        copies = []
        for a in range(len(ins)):
            for k, (dx, dy, dc) in enumerate(_FLIPS):
                tx, ty, tc = x ^ dx, y ^ dy, c ^ dc
                copies.append(_remote(_rows_half(ins[a].at[2 * tx + ty], tc, 0), outs[a].at[me], send_sems, recv_sems,
                                      7 * a + k, (tx, ty, tc)))
        return copies

    def start(self, ins, outs, send_sems, recv_sems):
        for cp in self._sends(ins, outs, send_sems, recv_sems):
            cp.start()

    def relay(self, ins, outs, send_sems, recv_sems):
        pass

    def finish(self, ins, outs, send_sems, recv_sems):
        x, y, c = _place()
        for a in range(len(outs)):
            for k, (dx, dy, dc) in enumerate(_FLIPS):
                slot = outs[a].at[4 * (x ^ dx) + 2 * (y ^ dy) + (c ^ dc)]
                _remote(slot, slot, send_sems, recv_sems, 7 * a + k, (x, y, c)).wait_recv()
        for cp in self._sends(ins, outs, send_sems, recv_sems):
            cp.wait_send()


def _run_comm(comm, name):
    n_in, n_out = len(comm.inputs), len(comm.out_shapes)

    def body(*refs):
        ins, outs, sems = refs[:n_in], refs[n_in:n_in + n_out], refs[n_in + n_out:]
        comm.start(ins, outs, *sems)
        comm.relay(ins, outs, *sems)
        comm.finish(ins, outs, *sems)

    return list(pl.pallas_call(
        body, out_shape=tuple(comm.out_shapes), in_specs=[_ANY] * n_in, out_specs=(_ANY,) * n_out,
        scratch_shapes=[pltpu.SemaphoreType.DMA((comm.n_sems,)), pltpu.SemaphoreType.DMA((comm.n_sems,))], name=name,
    )(*comm.inputs))


def _join_halves(rs, name):
    n = len(rs)

    def body(*refs):
        in_refs, out_refs = refs[:n], refs[n:2 * n]
        send_sems, recv_sems = refs[2 * n:]
        x, y, c = _place()
        copies = [_remote(in_refs[a], out_refs[a], send_sems, recv_sems, a, (x, y, 1 - c)) for a in range(n)]
        for cp in copies:
            cp.start()
        for cp in copies:
            cp.wait()

    return pl.pallas_call(
        body,
        out_shape=tuple(jax.ShapeDtypeStruct(r.shape, r.dtype) for r in rs),
        in_specs=[_ANY] * n, out_specs=(_ANY,) * n,
        scratch_shapes=[pltpu.SemaphoreType.DMA((n,)), pltpu.SemaphoreType.DMA((n,))], name=name,
    )(*rs)


def _gather_all(small, name):
    r, w = small.shape
    flips = [(dx, dy, dc) for dx in (0, 1) for dy in (0, 1) for dc in (0, 1) if (dx, dy, dc) != (0, 0, 0)]

    def body(in_ref, out_ref, send_sems, recv_sems, local_sem):
        x, y, c = _place()
        me = 4 * x + 2 * y + c
        mine = pltpu.make_async_copy(in_ref, out_ref.at[me], local_sem)
        mine.start()
        sends = []
        for j, (dx, dy, dc) in enumerate(flips):
            to = (x ^ dx, y ^ dy, c ^ dc)
            cp = pltpu.make_async_remote_copy(src_ref=in_ref, dst_ref=out_ref.at[me], send_sem=send_sems.at[j],
                                              recv_sem=recv_sems.at[j], device_id=to, device_id_type=MESH)
            cp.start()
            sends.append(cp)
        for j, (dx, dy, dc) in enumerate(flips):
            slot = out_ref.at[4 * (x ^ dx) + 2 * (y ^ dy) + (c ^ dc)]
            pltpu.make_async_remote_copy(src_ref=slot, dst_ref=slot, send_sem=send_sems.at[j], recv_sem=recv_sems.at[j],
                                         device_id=(x, y, c), device_id_type=MESH).wait_recv()
        for cp in sends:
            cp.wait_send()
        mine.wait()

    return pl.pallas_call(
        body, out_shape=jax.ShapeDtypeStruct((8, r, w), small.dtype), in_specs=[_ANY], out_specs=_ANY,
        scratch_shapes=[pltpu.SemaphoreType.DMA((7,)), pltpu.SemaphoreType.DMA((7,)), pltpu.SemaphoreType.DMA], name=name,
    )(small)


_BIG = ("w_in_fox", "w_in_dil", "w_mem_kv", "w_out", "w_up", "w_down")
_CONVW_SHARD = (DEPTH, 3, 2 * D_FF // N_CHIPS)

_SMALL = (("norm_mix", (DEPTH, D_MODEL)), ("norm_mem", (DEPTH, D_MODEL)), ("norm_ffn", (DEPTH, D_MODEL)),
          ("conv_b", (DEPTH, 2 * D_FF)), ("norm_final", (D_MODEL,)), ("b_forget", (2, N_MIX)), ("conv_w", (DEPTH, 3, 2 * D_FF)))


def _pack_small(vals, spec):
    flat = jnp.concatenate([vals[n].reshape(-1).astype(F32) for n, _ in spec])
    rows = -(-flat.shape[0] // (8 * 128)) * 8
    return jnp.pad(flat, (0, rows * 128 - flat.shape[0])).reshape(rows, 128)


def _unpack_small(buf, spec):
    flat, out, off = buf.reshape(-1), {}, 0
    for n, shp in spec:
        k = int(np.prod(shp))
        out[n] = flat[off:off + k].reshape(shp)
        off += k
    return out


def _fox_cols_to_kernel(w):
    qkv, f, qm = w[:, :3 * D_MIX], w[:, 3 * D_MIX:3 * D_MIX + N_MIX], w[:, 3 * D_MIX + N_MIX:]
    return jnp.concatenate([qkv, qm, f, jnp.zeros((w.shape[0], 128 - N_MIX), w.dtype)], axis=1)


def _fox_cols_from_kernel(w):
    qkv, qm, f = w[:, :3 * D_MIX], w[:, 3 * D_MIX:3 * D_MIX + D_MEMQ], w[:, 3 * D_MIX + D_MEMQ:3 * D_MIX + D_MEMQ + N_MIX]
    return jnp.concatenate([qkv, f, qm], axis=1)


def _rope_pair_tables(s):
    inv = 1.0 / (ROPE_THETA ** (jnp.arange(0, HEAD_DIM, 2, dtype=F32) / HEAD_DIM))
    ang = jnp.arange(s, dtype=F32)[:, None] * inv[None, :]
    cos, sin = jnp.cos(ang), jnp.sin(ang)
    return jnp.concatenate([cos, cos, cos, cos], axis=1), jnp.concatenate([-sin, sin, -sin, sin], axis=1)


def _local_step(x, mem, target, small, ex):
    s = x.shape[0]
    cos2, sin2 = _rope_pair_tables(s)
    saved = []
    h = x
    keys = [("w_in", 0), ("w_mem_kv", 0)]
    spec = ex.gather_spec(keys)
    if spec is not None:
        ex.put_gathered(keys, _run_comm(spec, "gather_first"))
    for l in range(DEPTH):
        fox = l % 2 == 0
        slot = l // 2
        tag = f"L{l}"
        g_mix, g_mem, g_ffn = (small[n][l:l + 1] for n in ("norm_mix", "norm_mem", "norm_ffn"))
        xn = _rms_fwd(h, g_mix, f"rms_mix_{tag}")
        proj = _matmul(xn, ex.weight(("w_in", l)), "nn", F32, f"mm_in_{tag}")
        if fox:
            bfg = jnp.pad(small["b_forget"][slot:slot + 1], ((0, 0), (0, 128 - N_MIX)))
            qp, kp, vp, qmp = _prep_fox_fwd(proj, bfg, f"prep_fox_{tag}")
        else:
            bfg = None
            qp, kp, vp, qmp = _prep_dil_fwd(proj, cos2, sin2, f"prep_dil_{tag}")
        mn = _rms_fwd(mem, g_mem, f"rms_mem_{tag}")
        kvm = _matmul(mn, ex.weight(("w_mem_kv", l)), "nn", F32, f"mm_memkv_{tag}")
        kmp, vmp = _pad_heads([(kvm, 0, N_MEM), (kvm, D_MEMQ, N_MEM)], (1.0, 1.0), (N_MEM * HP,) * 2, f"pad_memkv_{tag}")
        mode = "causal" if fox else "dilated"
        keys = [("w_out", l), ("w_up", l), ("w_down", l)] + ([("w_in", l + 1), ("w_mem_kv", l + 1)] if l + 1 < DEPTH else [])
        spec = ex.gather_spec(keys)
        o_mix, lse_mix, got = _flash_fwd(qp, kp, vp, mode, f"flash_{mode}_fwd_{tag}", comm=spec)
        if spec is not None:
            ex.put_gathered(keys, got)
        o_mem, lse_mem, _ = _flash_fwd(qmp, kmp, vmp, "full", f"flash_mem_fwd_{tag}")
        heads = _unpad_heads([o_mix, o_mem], (1.0, 1.0), BF16, f"unpad_heads_{tag}")
        h_mid = _matmul(heads, ex.weight(("w_out", l)), "nn", F32, f"mm_out_{tag}", residual=h)
        xn2 = _rms_fwd(h_mid, g_ffn, f"rms_ffn_{tag}")
        w_val, w_gate = ex.weight(("w_up", l))
        u_val = _matmul(xn2, w_val, "nn", F32, f"mm_up_val_{tag}")
        u_gate = _matmul(xn2, w_gate, "nn", F32, f"mm_up_gate_{tag}")
        act = _conv_fwd(u_val, u_gate, *ex.conv(l), f"conv_fwd_{tag}")
        h_out = _matmul(act, ex.weight(("w_down", l)), "nn", F32, f"mm_down_{tag}", residual=h_mid)
        saved.append(dict(h=h, xn=xn, proj=proj, bfg=bfg, qp=qp, kp=kp, vp=vp, qmp=qmp, mn=mn, kmp=kmp, vmp=vmp, o_mix=o_mix,
                          lse_mix=lse_mix, o_mem=o_mem, lse_mem=lse_mem, heads=heads, h_mid=h_mid, xn2=xn2, u_val=u_val, u_gate=u_gate, act=act))
        h = h_out

    loss_blk, dh, dg_final = _loss_head(h, small["norm_final"].reshape(1, D_MODEL), target, "loss_head")

    grads = {k: [None] * DEPTH for k in ("conv_w", "conv_b", "norm_mix", "norm_mem", "norm_ffn")}
    grads["b_forget"] = [None, None]
    waiting = []
    for l in reversed(range(DEPTH)):
        fox = l % 2 == 0
        tag = f"L{l}"
        sv = saved[l]
        g_mix, g_mem, g_ffn = (small[n][l:l + 1] for n in ("norm_mix", "norm_mem", "norm_ffn"))
        da = _matmul(dh, ex.weight(("w_down", l)), "nt", F32, f"mm_da_{tag}")
        g_down = _matmul(sv["act"], dh, "tn", BF16, f"mm_dwdown_{tag}")
        du_val, du_gate, dcw_v, dcw_g, dcb_v, dcb_g = _conv_bwd(sv["u_val"], sv["u_gate"], da, *ex.conv(l), f"conv_bwd_{tag}")
        grads["conv_w"][l] = jnp.concatenate([dcw_v, dcw_g], axis=1)
        grads["conv_b"][l] = jnp.concatenate([dcb_v, dcb_g], axis=1)
        w_val, w_gate = ex.weight(("w_up", l))
        dxn2 = _matmul(du_val, w_val, "nt", F32, f"mm_dxn2_val_{tag}")
        dxn2 = _matmul(du_gate, w_gate, "nt", F32, f"mm_dxn2_gate_{tag}", residual=dxn2)
        g_up = (_matmul(sv["xn2"], du_val, "tn", BF16, f"mm_dwup_val_{tag}"), _matmul(sv["xn2"], du_gate, "tn", BF16, f"mm_dwup_gate_{tag}"))
        dh_mid, grads["norm_ffn"][l] = _rms_bwd(sv["h_mid"], g_ffn, dxn2, dh, f"rms_ffn_bwd_{tag}")
        dheads = _matmul(dh_mid, ex.weight(("w_out", l)), "nt", BF16, f"mm_dheads_{tag}")
        g_out = _matmul(sv["heads"], dh_mid, "tn", BF16, f"mm_dwout_{tag}")
        do_mix, do_mem = _pad_heads([(dheads, 0, N_MIX), (dheads, D_MIX, N_MEM)], (1.0, 1.0), (N_MIX * HP, N_MEM * HP),
                                    f"pad_dheads_{tag}")
        mode = "causal" if fox else "dilated"
        items = waiting + [(("w_down", l), g_down), (("w_up", l), g_up), (("w_out", l), g_out)]
        spec = ex.scatter_spec(items)
        dqp, dkp, dvp, got = _flash_bwd(sv["qp"], sv["kp"], sv["vp"], sv["o_mix"], do_mix, sv["lse_mix"], mode, f"flash_{mode}_bwd_{tag}",
                                        comm=spec)
        if spec is not None:
            ex.put_received(items, spec, got)
        dqmp, dkmp, dvmp, _ = _flash_bwd(sv["qmp"], sv["kmp"], sv["vmp"], sv["o_mem"], do_mem, sv["lse_mem"], "full",
                                         f"flash_mem_bwd_{tag}")
        if fox:
            dproj, dbf = _prep_fox_bwd(dqp, dkp, dvp, dqmp, sv["proj"], sv["bfg"], f"prep_fox_bwd_{tag}")
            grads["b_forget"][l // 2] = dbf[0, :N_MIX]
        else:
            dproj = _prep_dil_bwd(dqp, dkp, dvp, dqmp, cos2, sin2, f"prep_dil_bwd_{tag}")
        dkvm = _unpad_heads([dkmp, dvmp], (LN2, 1.0), BF16, f"unpad_dkvm_{tag}")
        g_memkv = _matmul(sv["mn"], dkvm, "tn", BF16, f"mm_dwmemkv_{tag}")
        dmn = _matmul(dkvm, ex.weight(("w_mem_kv", l)), "nt", F32, f"mm_dmn_{tag}")
        _, grads["norm_mem"][l] = _rms_bwd(mem, g_mem, dmn, None, f"rms_mem_bwd_{tag}")
        dxn = _matmul(dproj, ex.weight(("w_in", l)), "nt", F32, f"mm_dxn_{tag}")
        g_in = _matmul(sv["xn"], dproj, "tn", BF16, f"mm_dwin_{tag}")
        waiting = [(("w_mem_kv", l), g_memkv), (("w_in", l), g_in)]
        dh, grads["norm_mix"][l] = _rms_bwd(sv["h"], g_mix, dxn, dh_mid, f"rms_mix_bwd_{tag}")
    spec = ex.scatter_spec(waiting)
    if spec is not None:
        ex.put_received(waiting, spec, _run_comm(spec, "grad_scatter_last"))
    grads["norm_final"] = dg_final
    return loss_blk, dh, grads


class _Exchange:
    def __init__(self, own, conv_w_full, conv_b, chip, core):
        self.own, self.conv_w_full, self.conv_b, self.chip, self.core = own, conv_w_full, conv_b, chip, core
        self.full, self.recv = {}, {}

    def _shard(self, key):
        name, l = key
        if name == "w_in":
            return self.own["w_in_fox" if l % 2 == 0 else "w_in_dil"][l // 2]
        return self.own[name][l]

    def gather_spec(self, keys):
        return _GatherSpec([self._shard(k) for k in keys])

    def put_gathered(self, keys, outs):
        for key, o in zip(keys, outs):
            name, l = key
            g = lax.dynamic_update_slice_in_dim(o, self._shard(key)[None], self.chip, axis=0)
            if name == "w_in":
                w = jnp.concatenate([g[j] for j in range(N_CHIPS)], axis=1)
                self.full[key] = _fox_cols_to_kernel(w) if l % 2 == 0 else w
            elif name == "w_up":
                self.full[key] = (jnp.concatenate([g[0], g[1]], axis=1), jnp.concatenate([g[2], g[3]], axis=1))
            else:
                self.full[key] = g.reshape(N_CHIPS * g.shape[1], g.shape[2])

    def weight(self, key):
        return self.full[key]

    def conv(self, l):
        return self.conv_w_full[l], self.conv_b[l:l + 1]

    def scatter_spec(self, items):
        parts = []
        for (name, l), g in items:
            if name == "w_in":
                g = _fox_cols_from_kernel(g) if l % 2 == 0 else g
                parts.append(jnp.stack(jnp.split(g, N_CHIPS, axis=1)))
            elif name == "w_up":
                parts.append(jnp.stack(jnp.split(g[0], 2, axis=1) + jnp.split(g[1], 2, axis=1)))
            else:
                parts.append(g.reshape(N_CHIPS, g.shape[0] // N_CHIPS, g.shape[1]))
        return _ScatterSpec(parts)

    def put_received(self, items, spec, outs):
        for (key, _), part, o in zip(items, spec.inputs, outs):
            rh = o.shape[1]
            mine = lax.dynamic_slice_in_dim(lax.dynamic_index_in_dim(part, self.chip, 0, keepdims=False), self.core * rh, rh, axis=0)
            self.recv[key] = lax.dynamic_update_slice_in_dim(o, mine[None], 2 * self.chip + self.core, axis=0)


def kernel(x, mem, norm_mix, norm_mem, norm_ffn, w_in_fox, b_forget, w_in_dil, w_mem_kv, w_out, w_up, conv_w, conv_b, w_down, norm_final, loss_target, m_norm_mix, m_norm_mem, m_norm_ffn, m_w_in_fox, m_b_forget, m_w_in_dil, m_w_mem_kv, m_w_out, m_w_up, m_conv_w, m_conv_b, m_w_down, m_norm_final, v_norm_mix, v_norm_mem, v_norm_ffn, v_w_in_fox, v_b_forget, v_w_in_dil, v_w_mem_kv, v_w_out, v_w_up, v_conv_w, v_conv_b, v_w_down, v_norm_final):
    w_sh = dict(w_in_fox=w_in_fox, w_in_dil=w_in_dil, w_mem_kv=w_mem_kv, w_out=w_out, w_up=w_up, w_down=w_down, conv_w=conv_w)
    m_sh = dict(w_in_fox=m_w_in_fox, w_in_dil=m_w_in_dil, w_mem_kv=m_w_mem_kv, w_out=m_w_out, w_up=m_w_up, w_down=m_w_down, conv_w=m_conv_w)
    v_sh = dict(w_in_fox=v_w_in_fox, w_in_dil=v_w_in_dil, w_mem_kv=v_w_mem_kv, w_out=v_w_out, w_up=v_w_up, w_down=v_w_down, conv_w=v_conv_w)
    small = dict(norm_mix=norm_mix, norm_mem=norm_mem, norm_ffn=norm_ffn, conv_b=conv_b, norm_final=norm_final, b_forget=b_forget)
    m_small = dict(norm_mix=m_norm_mix, norm_mem=m_norm_mem, norm_ffn=m_norm_ffn, conv_b=m_conv_b, norm_final=m_norm_final, b_forget=m_b_forget)
    v_small = dict(norm_mix=v_norm_mix, norm_mem=v_norm_mem, norm_ffn=v_norm_ffn, conv_b=v_conv_b, norm_final=v_norm_final, b_forget=v_b_forget)
    chip = 2 * lax.axis_index("x") + lax.axis_index("y")
    core = lax.axis_index("c")

    conv_spec = (("conv_w", _CONVW_SHARD),)
    conv_all = _gather_all(_pack_small(dict(conv_w=conv_w), conv_spec), "gather_conv_w")
    conv_w_full = jnp.concatenate([_unpack_small(conv_all[2 * j], conv_spec)["conv_w"] for j in range(N_CHIPS)], axis=-1)
    ex = _Exchange({n: w_sh[n].astype(BF16) for n in _BIG}, conv_w_full, conv_b, chip, core)

    loss_blk, dx, grads = _local_step(x[0], mem[0], loss_target[0], small, ex)
    loss = lax.psum(loss_blk[0, 0], ("x", "y", "c"))

    layer_keys = {"w_in_fox": [("w_in", 0), ("w_in", 2)], "w_in_dil": [("w_in", 1), ("w_in", 3)]}
    keys = [k for n in _BIG for k in layer_keys.get(n, [(n, l) for l in range(DEPTH)])]
    halves = [_sum_rows(ex.recv[k], F32, f"grad_sum_{k[0]}_L{k[1]}") for k in keys]
    whole = {k: jnp.concatenate([jnp.where(core == 0, mine, other), jnp.where(core == 0, other, mine)], axis=0)
             for k, mine, other in zip(keys, halves, _join_halves(halves, "grad_join_halves"))}
    g_big = {n: jnp.stack([whole[k] for k in layer_keys.get(n, [(n, l) for l in range(DEPTH)])]) for n in _BIG}

    g_small_local = dict(
        norm_mix=jnp.concatenate(grads["norm_mix"]), norm_mem=jnp.concatenate(grads["norm_mem"]),
        norm_ffn=jnp.concatenate(grads["norm_ffn"]),
        conv_b=jnp.concatenate(grads["conv_b"]),
        norm_final=grads["norm_final"], b_forget=jnp.stack(grads["b_forget"]),
        conv_w=jnp.stack(grads["conv_w"]),
    )
    small_all = _gather_all(_pack_small(g_small_local, _SMALL), "small_gather_all")
    g_small = _unpack_small(_sum_rows(small_all, F32, "small_sum"), _SMALL)
    ncol = 2 * D_FF // N_CHIPS
    g_big["conv_w"] = lax.dynamic_slice_in_dim(g_small["conv_w"], chip * ncol, ncol, axis=2)

    out_g, out_d, out_m, out_v = {}, {}, {}, {}
    for n in _BIG + ("conv_w",):
        shp = w_sh[n].shape
        two_d = (-1, shp[-1])
        d, mo, vo = _adamw(w_sh[n].reshape(two_d), g_big[n].reshape(two_d), m_sh[n].reshape(two_d), v_sh[n].reshape(two_d), f"adamw_{n}")
        out_g[n], out_d[n], out_m[n], out_v[n] = g_big[n].reshape(shp), d.reshape(shp), mo.reshape(shp), vo.reshape(shp)
    spec = _SMALL[:-1]
    d, mo, vo = _adamw(_pack_small(small, spec), _pack_small(g_small, spec), _pack_small(m_small, spec), _pack_small(v_small, spec),
                       "adamw_small")
    d, mo, vo = _unpack_small(d, spec), _unpack_small(mo, spec), _unpack_small(vo, spec)
    for n, shp in spec:
        out_g[n], out_d[n], out_m[n], out_v[n] = g_small[n].reshape(shp), d[n], mo[n], vo[n]

    order = ("norm_mix", "norm_mem", "norm_ffn", "w_in_fox", "b_forget", "w_in_dil", "w_mem_kv", "w_out", "w_up", "conv_w", "conv_b",
             "w_down", "norm_final")
    return (loss, dx[None], *[out_g[n] for n in order], *[out_d[n] for n in order], *[out_m[n] for n in order],
            *[out_v[n] for n in order])
```

```python
import functools
import math

import numpy as np
import jax
import jax.numpy as jnp
from jax import lax
from jax.experimental import pallas as pl
from jax.experimental.pallas import tpu as pltpu

F32 = jnp.float32
BF16 = jnp.bfloat16

D_MODEL = 1024
DEPTH = 4
HEAD_DIM = 64
N_MIX = 12
N_MEM = 4
D_MIX = N_MIX * HEAD_DIM
D_MEMQ = N_MEM * HEAD_DIM
D_FF = 2816
FOX_IN = 3 * D_MIX + N_MIX + D_MEMQ
DIL_IN = 3 * D_MIX + D_MEMQ
HP = 128
SCALE = HEAD_DIM ** -0.5
NEG = -1e30
NORM_EPS = 1e-6
DIL_MAX = 2048
LOG2E = 1.0 / math.log(2.0)
LN2 = math.log(2.0)
QSCALE = SCALE * LOG2E
BWD_HEADS_PER_STEP = 3
FWD_HEADS_PER_STEP = 6
ROPE_THETA = 10000.0
N_CHIPS = 4
CONV_TF = 128
NORM_ROWS = 1024
HEAD_ROWS = 512
FLASH_BLK = 512
VMEM_LIMIT = 48 * 1024 * 1024

ADAM_LR = 0.001
ADAM_B1 = 0.9
ADAM_B2 = 0.999
ADAM_EPS = 1e-08
ADAM_WD = 0.01
ADAM_STEP = 10

MESH = pl.DeviceIdType.MESH


def _cp(*sem):
    return pltpu.CompilerParams(dimension_semantics=tuple(sem), vmem_limit_bytes=VMEM_LIMIT)


def _tile(n, cap, mult=128):
    if n <= cap:
        return n
    t = (cap // mult) * mult
    while t >= mult:
        if n % t == 0:
            return t
        t -= mult
    raise ValueError(f"no tile for {n} under {cap}")


_DIMS = {"nn": (((1,), (0,)), ((), ())), "nt": (((1,), (1,)), ((), ())), "tn": (((0,), (0,)), ((), ()))}


def _matmul(a, b, mode, out_dtype, name, residual=None):
    if mode == "nn":
        (m, k), n = a.shape, b.shape[1]
    elif mode == "nt":
        (m, k), n = a.shape, b.shape[0]
    else:
        (k, m), n = a.shape, b.shape[1]
    tn = 1408 if n % 1408 == 0 else _tile(n, 1024)
    tm = 1408 if (m % 1408 == 0 and tn <= 1024) else _tile(m, 1024)
    tk = _tile(k, 1408)
    nk = k // tk
    dims = _DIMS[mode]
    has_res = residual is not None

    def body(*refs):
        if has_res:
            a_ref, b_ref, r_ref, o_ref = refs[:4]
        else:
            a_ref, b_ref, o_ref = refs[:3]
        part = lax.dot_general(a_ref[...].astype(BF16), b_ref[...].astype(BF16), dims, preferred_element_type=F32)
        if nk == 1:
            if has_res:
                part = part + r_ref[...]
            o_ref[...] = part.astype(o_ref.dtype)
            return
        acc_ref = refs[-1]
        kk = pl.program_id(2)

        @pl.when(kk == 0)
        def _():
            acc_ref[...] = part

        @pl.when(kk > 0)
        def _():
            acc_ref[...] += part

        @pl.when(kk == nk - 1)
        def _():
            tot = acc_ref[...]
            if has_res:
                tot = tot + r_ref[...]
            o_ref[...] = tot.astype(o_ref.dtype)

    if mode == "nn":
        a_spec = pl.BlockSpec((tm, tk), lambda i, j, kk: (i, kk))
        b_spec = pl.BlockSpec((tk, tn), lambda i, j, kk: (kk, j))
    elif mode == "nt":
        a_spec = pl.BlockSpec((tm, tk), lambda i, j, kk: (i, kk))
        b_spec = pl.BlockSpec((tn, tk), lambda i, j, kk: (j, kk))
    else:
        a_spec = pl.BlockSpec((tk, tm), lambda i, j, kk: (kk, i))
        b_spec = pl.BlockSpec((tk, tn), lambda i, j, kk: (kk, j))
    in_specs = [a_spec, b_spec]
    args = [a, b]
    if has_res:
        in_specs.append(pl.BlockSpec((tm, tn), lambda i, j, kk: (i, j)))
        args.append(residual)
    return pl.pallas_call(
        body,
        out_shape=jax.ShapeDtypeStruct((m, n), out_dtype),
        grid=(m // tm, n // tn, nk),
        in_specs=in_specs,
        out_specs=pl.BlockSpec((tm, tn), lambda i, j, kk: (i, j)),
        scratch_shapes=[pltpu.VMEM((tm, tn), F32)] if nk > 1 else [],
        compiler_params=_cp("parallel", "parallel", "arbitrary"),
        name=name,
    )(*args)


def _rms_fwd(h, g, name):
    r, d = h.shape
    tr = _tile(r, NORM_ROWS, 8)

    def body(h_ref, g_ref, o_ref):
        x = h_ref[...]
        rstd = lax.rsqrt(jnp.mean(x * x, axis=-1, keepdims=True) + NORM_EPS)
        o_ref[...] = ((x * rstd) * g_ref[...]).astype(o_ref.dtype)

    return pl.pallas_call(
        body,
        out_shape=jax.ShapeDtypeStruct((r, d), BF16),
        grid=(r // tr,),
        in_specs=[pl.BlockSpec((tr, d), lambda i: (i, 0)), pl.BlockSpec((1, d), lambda i: (0, 0))],
        out_specs=pl.BlockSpec((tr, d), lambda i: (i, 0)),
        compiler_params=_cp("parallel"),
        name=name,
    )(h, g)


def _rms_bwd(h, g, dy, dres, name):
    r, d = h.shape
    tr = _tile(r, NORM_ROWS, 8)
    need_dh = dres is not None

    def body(*refs):
        if need_dh:
            h_ref, g_ref, dy_ref, dres_ref, dh_ref, dg_ref = refs
        else:
            h_ref, g_ref, dy_ref, dg_ref = refs
        i = pl.program_id(0)
        x = h_ref[...]
        rstd = lax.rsqrt(jnp.mean(x * x, axis=-1, keepdims=True) + NORM_EPS)
        nrm = x * rstd
        dyv = dy_ref[...].astype(F32)
        part = jnp.sum(dyv * nrm, axis=0, keepdims=True)

        @pl.when(i == 0)
        def _():
            dg_ref[...] = part

        @pl.when(i > 0)
        def _():
            dg_ref[...] += part

        if need_dh:
            gy = dyv * g_ref[...]
            dx = rstd * (gy - nrm * jnp.mean(gy * nrm, axis=-1, keepdims=True))
            dh_ref[...] = dres_ref[...] + dx

    row = pl.BlockSpec((tr, d), lambda i: (i, 0))
    vec = pl.BlockSpec((1, d), lambda i: (0, 0))
    if need_dh:
        return pl.pallas_call(
            body,
            out_shape=(jax.ShapeDtypeStruct((r, d), F32), jax.ShapeDtypeStruct((1, d), F32)),
            grid=(r // tr,),
            in_specs=[row, vec, row, row],
            out_specs=(row, vec),
            compiler_params=_cp("arbitrary"),
            name=name,
        )(h, g, dy, dres)
    return None, pl.pallas_call(
        body,
        out_shape=jax.ShapeDtypeStruct((1, d), F32),
        grid=(r // tr,),
        in_specs=[row, vec, row],
        out_specs=vec,
        compiler_params=_cp("arbitrary"),
        name=name,
    )(h, g, dy)


def _loss_head(h, g, target, name):
    r, d = h.shape
    tr = _tile(r, NORM_ROWS, 8)

    def body(h_ref, g_ref, t_ref, loss_ref, dh_ref, dg_ref):
        i = pl.program_id(0)
        x = h_ref[...]
        gv = g_ref[...]
        rstd = lax.rsqrt(jnp.mean(x * x, axis=-1, keepdims=True) + NORM_EPS)
        nrm = x * rstd
        err = nrm * gv - t_ref[...]
        lpart = 0.5 * jnp.sum(jnp.mean(err * err, axis=-1, keepdims=True), axis=0, keepdims=True)
        dyv = err * (1.0 / d)
        gpart = jnp.sum(dyv * nrm, axis=0, keepdims=True)

        @pl.when(i == 0)
        def _():
            loss_ref[...] = jnp.broadcast_to(lpart, loss_ref.shape)
            dg_ref[...] = gpart

        @pl.when(i > 0)
        def _():
            loss_ref[...] += jnp.broadcast_to(lpart, loss_ref.shape)
            dg_ref[...] += gpart

        gy = dyv * gv
        dh_ref[...] = rstd * (gy - nrm * jnp.mean(gy * nrm, axis=-1, keepdims=True))

    row = pl.BlockSpec((tr, d), lambda i: (i, 0))
    vec = pl.BlockSpec((1, d), lambda i: (0, 0))
    lsp = pl.BlockSpec((1, 128), lambda i: (0, 0))
    return pl.pallas_call(
        body,
        out_shape=(jax.ShapeDtypeStruct((1, 128), F32), jax.ShapeDtypeStruct((r, d), F32), jax.ShapeDtypeStruct((1, d), F32)),
        grid=(r // tr,),
        in_specs=[row, vec, row],
        out_specs=(lsp, row, vec),
        compiler_params=_cp("arbitrary"),
        name=name,
    )(h, g, target)


def _lane(shape):
    return lax.broadcasted_iota(jnp.int32, shape, 1)


def _head_tile(pair_tile, odd):
    return pltpu.roll(pair_tile, 64, 1) if odd else pair_tile


def _pair_tile(even_tile, odd_tile):
    lane = _lane(even_tile.shape)
    return jnp.where(lane < 64, even_tile, pltpu.roll(odd_tile, 64, 1))


def _pad_heads(xs, scales, out_widths, name):
    r = xs[0][0].shape[0]
    tr = _tile(r, HEAD_ROWS, 8)
    n_in = len(xs)

    def body(*refs):
        for idx in range(n_in):
            x_ref, o_ref = refs[idx], refs[n_in + idx]
            nh = xs[idx][2]
            for p in range(nh // 2):
                t = x_ref[:, p * 128:(p + 1) * 128].astype(F32) * scales[idx]
                lane = _lane(t.shape)
                o_ref[:, (2 * p) * HP:(2 * p + 1) * HP] = jnp.where(lane < 64, t, 0.0).astype(o_ref.dtype)
                o_ref[:, (2 * p + 1) * HP:(2 * p + 2) * HP] = jnp.where(lane < 64, pltpu.roll(t, 64, 1), 0.0).astype(o_ref.dtype)

    in_specs, args, out_specs, out_shape = [], [], [], []
    for (arr, c0, nh), w in zip(xs, out_widths):
        wcols = nh * 64
        assert c0 % wcols == 0 or c0 == 0
        blk = c0 // wcols if wcols else 0
        in_specs.append(pl.BlockSpec((tr, wcols), functools.partial(lambda i, b: (i, b), b=blk)))
        args.append(arr)
        out_specs.append(pl.BlockSpec((tr, w), lambda i: (i, 0)))
        out_shape.append(jax.ShapeDtypeStruct((r, w), BF16))
    return pl.pallas_call(
        body,
        out_shape=tuple(out_shape),
        grid=(r // tr,),
        in_specs=in_specs,
        out_specs=tuple(out_specs),
        compiler_params=_cp("parallel"),
        name=name,
    )(*args)


def _unpad_heads(xs, scales, out_dtype, name):
    r = xs[0].shape[0]
    tr = _tile(r, HEAD_ROWS, 8)
    nhs = [x.shape[1] // HP for x in xs]
    total = sum(nhs) * 64

    def body(*refs):
        o_ref = refs[-1]
        col = 0
        for x_ref, nh, sc in zip(refs[:-1], nhs, scales):
            for p in range(nh // 2):
                ev = x_ref[:, (2 * p) * HP:(2 * p + 1) * HP].astype(F32)
                od = x_ref[:, (2 * p + 1) * HP:(2 * p + 2) * HP].astype(F32)
                o_ref[:, col:col + 128] = (_pair_tile(ev, od) * sc).astype(o_ref.dtype)
                col += 128

    return pl.pallas_call(
        body,
        out_shape=jax.ShapeDtypeStruct((r, total), out_dtype),
        grid=(r // tr,),
        in_specs=[pl.BlockSpec((tr, x.shape[1]), lambda i: (i, 0)) for x in xs],
        out_specs=pl.BlockSpec((tr, total), lambda i: (i, 0)),
        compiler_params=_cp("parallel"),
        name=name,
    )(*xs)


def _bf16_split3(c):
    hi = c.astype(BF16).astype(F32)
    r1 = c - hi
    mid = r1.astype(BF16).astype(F32)
    lo = (r1 - mid).astype(BF16).astype(F32)
    return hi, mid, lo


def _log_sigmoid(z):
    return jnp.minimum(z, 0.0) - jnp.log(1.0 + jnp.exp(-jnp.abs(z)))


def _prep_fox_fwd(proj, bfg, name):
    s = proj.shape[0]
    ts = _tile(s, HEAD_ROWS, 8)
    fcol = 3 * D_MIX + D_MEMQ

    def body(p_ref, b_ref, q_ref, k_ref, v_ref, qm_ref, carry_ref):
        i = pl.program_id(0)

        @pl.when(i == 0)
        def _():
            carry_ref[...] = jnp.zeros_like(carry_ref)

        lane = _lane((ts, 128))
        z = p_ref[:, fcol:fcol + 128] + b_ref[...]
        logf = jnp.where(lane < N_MIX, _log_sigmoid(z), 0.0)
        rr = lax.broadcasted_iota(jnp.int32, (ts, ts), 0)
        cc = lax.broadcasted_iota(jnp.int32, (ts, ts), 1)
        tri = jnp.where(cc <= rr, 1.0, 0.0).astype(F32)
        c = jnp.dot(tri, logf, preferred_element_type=F32, precision=lax.Precision.HIGHEST) + carry_ref[0:1, :]
        carry_ref[...] = jnp.broadcast_to(c[ts - 1:ts, :], carry_ref.shape)
        for hh in range(N_MIX):
            p, odd = hh // 2, hh % 2
            ch = jnp.sum(jnp.where(lane == hh, c, 0.0), axis=-1, keepdims=True) * LOG2E
            hi, mid, lo = _bf16_split3(ch)
            qt = _head_tile(p_ref[:, p * 128:(p + 1) * 128], odd) * QSCALE
            kt = _head_tile(p_ref[:, D_MIX + p * 128:D_MIX + (p + 1) * 128], odd)
            vt = _head_tile(p_ref[:, 2 * D_MIX + p * 128:2 * D_MIX + (p + 1) * 128], odd)
            qa = jnp.where(lane == 64, hi, jnp.where(lane == 65, mid, jnp.where(lane == 66, lo, jnp.where(lane < 70, 1.0, 0.0))))
            ka = jnp.where(lane < 67, 1.0, jnp.where(lane == 67, -hi, jnp.where(lane == 68, -mid, jnp.where(lane == 69, -lo, 0.0))))
            q_ref[:, hh * HP:(hh + 1) * HP] = jnp.where(lane < 64, qt, qa).astype(BF16)
            k_ref[:, hh * HP:(hh + 1) * HP] = jnp.where(lane < 64, kt, ka).astype(BF16)
            v_ref[:, hh * HP:(hh + 1) * HP] = jnp.where(lane < 64, vt, 0.0).astype(BF16)
        for hh in range(N_MEM):
            p, odd = hh // 2, hh % 2
            t = _head_tile(p_ref[:, 3 * D_MIX + p * 128:3 * D_MIX + (p + 1) * 128], odd) * QSCALE
            qm_ref[:, hh * HP:(hh + 1) * HP] = jnp.where(lane < 64, t, 0.0).astype(BF16)

    wmix, wmem = N_MIX * HP, N_MEM * HP
    return pl.pallas_call(
        body,
        out_shape=(jax.ShapeDtypeStruct((s, wmix), BF16),) * 3 + (jax.ShapeDtypeStruct((s, wmem), BF16),),
        grid=(s // ts,),
        in_specs=[pl.BlockSpec((ts, proj.shape[1]), lambda i: (i, 0)), pl.BlockSpec((1, 128), lambda i: (0, 0))],
        out_specs=(pl.BlockSpec((ts, wmix), lambda i: (i, 0)),) * 3 + (pl.BlockSpec((ts, wmem), lambda i: (i, 0)),),
        scratch_shapes=[pltpu.VMEM((8, 128), F32)],
        compiler_params=_cp("arbitrary"),
        name=name,
    )(proj, bfg)


def _prep_fox_bwd(dq, dk, dv, dqm, proj, bfg, name):
    s = proj.shape[0]
    ts = _tile(s, HEAD_ROWS, 8)
    nb = s // ts
    fcol = 3 * D_MIX + D_MEMQ

    def body(dq_ref, dk_ref, dv_ref, dqm_ref, p_ref, b_ref, o_ref, db_ref, carry_ref):
        i = pl.program_id(0)

        @pl.when(i == 0)
        def _():
            carry_ref[...] = jnp.zeros_like(carry_ref)
            db_ref[...] = jnp.zeros_like(db_ref)

        lane = _lane((ts, 128))
        dc = jnp.zeros((ts, 128), F32)
        for p in range(N_MIX // 2):
            tq, tk, tv = [], [], []
            for odd in (0, 1):
                hh = 2 * p + odd
                dqt = dq_ref[:, hh * HP:(hh + 1) * HP]
                dkt = dk_ref[:, hh * HP:(hh + 1) * HP]
                col = jnp.sum(jnp.where(lane == 64, dqt, 0.0) - jnp.where(lane == 67, dkt, 0.0), axis=-1, keepdims=True)
                dc = dc + jnp.where(lane == hh, col, 0.0)
                tq.append(dqt)
                tk.append(dkt)
                tv.append(dv_ref[:, hh * HP:(hh + 1) * HP])
            o_ref[:, p * 128:(p + 1) * 128] = (_pair_tile(tq[0], tq[1]) * SCALE).astype(o_ref.dtype)
            o_ref[:, D_MIX + p * 128:D_MIX + (p + 1) * 128] = (_pair_tile(tk[0], tk[1]) * LN2).astype(o_ref.dtype)
            o_ref[:, 2 * D_MIX + p * 128:2 * D_MIX + (p + 1) * 128] = _pair_tile(tv[0], tv[1]).astype(o_ref.dtype)
        for p in range(N_MEM // 2):
            ev = dqm_ref[:, (2 * p) * HP:(2 * p + 1) * HP]
            od = dqm_ref[:, (2 * p + 1) * HP:(2 * p + 2) * HP]
            o_ref[:, 3 * D_MIX + p * 128:3 * D_MIX + (p + 1) * 128] = (_pair_tile(ev, od) * SCALE).astype(o_ref.dtype)
        rr = lax.broadcasted_iota(jnp.int32, (ts, ts), 0)
        cc = lax.broadcasted_iota(jnp.int32, (ts, ts), 1)
        triu = jnp.where(cc >= rr, 1.0, 0.0).astype(F32)
        dlogf = jnp.dot(triu, dc, preferred_element_type=F32, precision=lax.Precision.HIGHEST) + carry_ref[0:1, :]
        carry_ref[...] = jnp.broadcast_to(dlogf[0:1, :], carry_ref.shape)
        z = p_ref[...] + b_ref[...]
        dz = jnp.where(lane < N_MIX, dlogf / (1.0 + jnp.exp(z)), 0.0)
        o_ref[:, fcol:fcol + 128] = dz.astype(o_ref.dtype)
        db_ref[...] += jnp.sum(dz, axis=0, keepdims=True)

    wmix, wmem = N_MIX * HP, N_MEM * HP
    rev = lambda i: (nb - 1 - i, 0)
    return pl.pallas_call(
        body,
        out_shape=(jax.ShapeDtypeStruct(proj.shape, BF16), jax.ShapeDtypeStruct((1, 128), F32)),
        grid=(nb,),
        in_specs=[pl.BlockSpec((ts, wmix), rev)] * 3 + [pl.BlockSpec((ts, wmem), rev),
                                                         pl.BlockSpec((ts, 128), lambda i: (nb - 1 - i, fcol // 128)),
                                                         pl.BlockSpec((1, 128), lambda i: (0, 0))],
        out_specs=(pl.BlockSpec((ts, proj.shape[1]), rev), pl.BlockSpec((1, 128), lambda i: (0, 0))),
        scratch_shapes=[pltpu.VMEM((8, 128), F32)],
        compiler_params=_cp("arbitrary"),
        name=name,
    )(dq, dk, dv, dqm, proj, bfg)


def _rope_partner(x):
    lane = _lane(x.shape)
    return jnp.where((lane % 64) < 32, pltpu.roll(x, 96, 1), pltpu.roll(x, 32, 1))


def _prep_dil_fwd(proj, cos2, sin2, name):
    s = proj.shape[0]
    ts = _tile(s, HEAD_ROWS, 8)

    def body(p_ref, c_ref, s_ref, q_ref, k_ref, v_ref, qm_ref):
        lane = _lane((ts, 128))
        cosv, sinv = c_ref[...], s_ref[...]
        for p in range(N_MIX // 2):
            xq = p_ref[:, p * 128:(p + 1) * 128]
            xk = p_ref[:, D_MIX + p * 128:D_MIX + (p + 1) * 128]
            xv = p_ref[:, 2 * D_MIX + p * 128:2 * D_MIX + (p + 1) * 128]
            yq = (xq * cosv + _rope_partner(xq) * sinv) * QSCALE
            yk = xk * cosv + _rope_partner(xk) * sinv
            for odd in (0, 1):
                hh = 2 * p + odd
                q_ref[:, hh * HP:(hh + 1) * HP] = jnp.where(lane < 64, _head_tile(yq, odd), 0.0).astype(BF16)
                k_ref[:, hh * HP:(hh + 1) * HP] = jnp.where(lane < 64, _head_tile(yk, odd), 0.0).astype(BF16)
                v_ref[:, hh * HP:(hh + 1) * HP] = jnp.where(lane < 64, _head_tile(xv, odd), 0.0).astype(BF16)
        for p in range(N_MEM // 2):
            t = p_ref[:, 3 * D_MIX + p * 128:3 * D_MIX + (p + 1) * 128] * QSCALE
            for odd in (0, 1):
                hh = 2 * p + odd
                qm_ref[:, hh * HP:(hh + 1) * HP] = jnp.where(lane < 64, _head_tile(t, odd), 0.0).astype(BF16)

    wmix, wmem = N_MIX * HP, N_MEM * HP
    return pl.pallas_call(
        body,
        out_shape=(jax.ShapeDtypeStruct((s, wmix), BF16),) * 3 + (jax.ShapeDtypeStruct((s, wmem), BF16),),
        grid=(s // ts,),
        in_specs=[pl.BlockSpec((ts, proj.shape[1]), lambda i: (i, 0)), pl.BlockSpec((ts, 128), lambda i: (i, 0)),
                  pl.BlockSpec((ts, 128), lambda i: (i, 0))],
        out_specs=(pl.BlockSpec((ts, wmix), lambda i: (i, 0)),) * 3 + (pl.BlockSpec((ts, wmem), lambda i: (i, 0)),),
        compiler_params=_cp("parallel"),
        name=name,
    )(proj, cos2, sin2)


def _prep_dil_bwd(dq, dk, dv, dqm, cos2, sin2, name):
    s = dq.shape[0]
    ts = _tile(s, HEAD_ROWS, 8)

    def body(dq_ref, dk_ref, dv_ref, dqm_ref, c_ref, s_ref, o_ref):
        cosv, sinv = c_ref[...], s_ref[...]
        for p in range(N_MIX // 2):
            e, o = 2 * p, 2 * p + 1
            dyq = _pair_tile(dq_ref[:, e * HP:(e + 1) * HP], dq_ref[:, o * HP:(o + 1) * HP]) * SCALE
            dyk = _pair_tile(dk_ref[:, e * HP:(e + 1) * HP], dk_ref[:, o * HP:(o + 1) * HP]) * LN2
            dxv = _pair_tile(dv_ref[:, e * HP:(e + 1) * HP], dv_ref[:, o * HP:(o + 1) * HP])
            o_ref[:, p * 128:(p + 1) * 128] = (dyq * cosv - _rope_partner(dyq) * sinv).astype(o_ref.dtype)
            o_ref[:, D_MIX + p * 128:D_MIX + (p + 1) * 128] = (dyk * cosv - _rope_partner(dyk) * sinv).astype(o_ref.dtype)
            o_ref[:, 2 * D_MIX + p * 128:2 * D_MIX + (p + 1) * 128] = dxv.astype(o_ref.dtype)
        for p in range(N_MEM // 2):
            e, o = 2 * p, 2 * p + 1
            t = _pair_tile(dqm_ref[:, e * HP:(e + 1) * HP], dqm_ref[:, o * HP:(o + 1) * HP]) * SCALE
            o_ref[:, 3 * D_MIX + p * 128:3 * D_MIX + (p + 1) * 128] = t.astype(o_ref.dtype)

    wmix, wmem = N_MIX * HP, N_MEM * HP
    row = lambda w: pl.BlockSpec((ts, w), lambda i: (i, 0))
    return pl.pallas_call(
        body,
        out_shape=jax.ShapeDtypeStruct((s, DIL_IN), BF16),
        grid=(s // ts,),
        in_specs=[row(wmix)] * 3 + [row(wmem), row(128), row(128)],
        out_specs=row(DIL_IN),
        compiler_params=_cp("parallel"),
        name=name,
    )(dq, dk, dv, dqm, cos2, sin2)


def _mask_bias(mode, blk):
    n = 1 if mode == "causal" else DIL_MAX // blk + 1
    idx = jnp.arange(blk, dtype=jnp.int32)
    dist = jnp.arange(n, dtype=jnp.int32)[:, None, None] * blk + idx[None, :, None] - idx[None, None, :]
    if mode == "causal":
        return jnp.where(dist >= 0, 0.0, NEG).astype(F32)
    cnt = ((dist <= 128).astype(jnp.int32) + ((dist <= 512) & (dist % 4 == 0)).astype(jnp.int32)
           + ((dist <= DIL_MAX) & (dist % 16 == 0)).astype(jnp.int32))
    bias = jnp.where(cnt == 3, math.log2(3.0), jnp.where(cnt == 2, 1.0, 0.0))
    return jnp.where((dist >= 0) & (cnt > 0), bias, NEG).astype(F32)


def _flash_dims(q, k, mode):
    sq, w = q.shape
    sk = k.shape[0]
    tq = _tile(sq, FLASH_BLK, 8)
    tk = sk if mode == "full" else tq
    band = DIL_MAX // tk if mode == "dilated" else None
    return sq, sk, w, tq, tk, sq // tq, sk // tk, band


def _heads_per_step(n_heads, most):
    return max(h for h in range(1, most + 1) if n_heads % h == 0)


def _pair_starts(mode, nq, band, key_major):
    if mode == "full":
        counts = [nq] if key_major else [1] * nq
    elif key_major:
        counts = [(min(kj + band, nq - 1) if mode == "dilated" else nq - 1) - kj + 1 for kj in range(nq)]
    else:
        counts = [qi - (max(qi - band, 0) if mode == "dilated" else 0) + 1 for qi in range(nq)]
    starts = [int(v) for v in np.cumsum([0] + counts)]
    return starts[:-1], starts[-1]


def _pair_at(t, starts, mode, band, key_major):
    if mode == "full":
        return t, jnp.zeros((), jnp.int32)
    major, begin = 0, 0
    for prev, s in zip(starts[:-1], starts[1:]):
        reached = t >= s
        major = major + reached.astype(jnp.int32)
        begin = begin + jnp.where(reached, s - prev, 0)
    if key_major:
        return major + (t - begin), major
    return major, _first_key(mode, major, band) + (t - begin)


def _first_key(mode, qi, band):
    return jnp.maximum(qi - band, 0) if mode == "dilated" else 0


def _carried(comm, grid):
    if comm is None:
        return 0, 0, [], (lambda *a: None), (lambda *a: None)
    total = int(np.prod(grid))
    relay_at = (3 * total) // 4

    def step_index():
        idx = 0
        for d, n in enumerate(grid):
            idx = idx * n + pl.program_id(d)
        return idx

    def start(cins, couts, sems):
        pl.when(step_index() == 0)(lambda: comm.start(cins, couts, *sems))
        if 0 < relay_at < total - 1:
            pl.when(step_index() == relay_at)(lambda: comm.relay(cins, couts, *sems))

    def finish(cins, couts, sems):
        def last():
            if not 0 < relay_at < total - 1:
                comm.relay(cins, couts, *sems)
            comm.finish(cins, couts, *sems)

        pl.when(step_index() == total - 1)(last)

    sems = [pltpu.SemaphoreType.DMA((comm.n_sems,)), pltpu.SemaphoreType.DMA((comm.n_sems,))]
    return len(comm.inputs), len(comm.out_shapes), sems, start, finish


def _split_refs(refs, n_in, n_cin, n_out, n_cout, n_scratch):
    bounds = np.cumsum([0, n_in, n_cin, n_out, n_cout, n_scratch])
    return [refs[a:b] for a, b in zip(bounds[:-1], bounds[1:])] + [refs[bounds[-1]:]]


def _flash_fwd(q, k, v, mode, name, comm=None):
    sq, sk, w, tq, tk, nq, nk, band = _flash_dims(q, k, mode)
    hb = _heads_per_step(w // HP, FWD_HEADS_PER_STEP)
    wb = hb * HP
    nch = tk // 128
    has_bias = mode != "full"
    starts, n_pairs = _pair_starts(mode, nq, band, key_major=False)
    pair = functools.partial(_pair_at, starts=starts, mode=mode, band=band, key_major=False)
    grid = (w // wb, n_pairs)
    n_cin, n_cout, comm_sems, comm_start, comm_finish = _carried(comm, grid)

    def body(*refs):
        ins, cins, (o_ref, lse_ref), couts, (m_ref, l_ref, acc_ref), sems = _split_refs(refs, 3 + has_bias, n_cin, 2, n_cout, 3)
        q_ref, k_ref, v_ref = ins[:3]
        b_ref = ins[3] if has_bias else None
        qi, kj = pair(pl.program_id(1))
        comm_start(cins, couts, sems)

        @pl.when(kj == _first_key(mode, qi, band))
        def _():
            m_ref[...] = jnp.full_like(m_ref, -jnp.inf)
            l_ref[...] = jnp.zeros_like(l_ref)
            acc_ref[...] = jnp.zeros_like(acc_ref)

        def step(bias_tile):
            for h in range(hb):
                cols = slice(h * HP, (h + 1) * HP)
                sc = lax.dot_general(q_ref[:, cols], k_ref[:, cols], _DIMS["nt"], preferred_element_type=F32)
                if bias_tile is not None:
                    sc = sc + bias_tile()
                m_prev = m_ref[h]
                m_new = jnp.maximum(m_prev, jnp.max(sc, axis=-1, keepdims=True))
                alpha = jnp.exp2(m_prev - m_new)
                psum, chunks = None, []
                for c in range(nch):
                    pc = jnp.exp2(sc[:, c * 128:(c + 1) * 128] - m_new)
                    psum = pc if psum is None else psum + pc
                    chunks.append(pc.astype(BF16))
                p = chunks[0] if nch == 1 else jnp.concatenate(chunks, axis=1)
                l_ref[h] = alpha * l_ref[h] + psum
                acc_ref[h] = alpha * acc_ref[h] + jnp.dot(p, v_ref[:, cols], preferred_element_type=F32)
                m_ref[h] = m_new

        if mode == "full":
            step(None)
        elif mode == "causal":
            pl.when(kj == qi)(lambda: step(lambda: b_ref[0]))
            pl.when(kj < qi)(lambda: step(None))
        else:
            step(lambda: b_ref[qi - kj])

        @pl.when(kj == (0 if mode == "full" else qi))
        def _():
            for h in range(hb):
                cols = slice(h * HP, (h + 1) * HP)
                l = jnp.sum(l_ref[h], axis=-1, keepdims=True)
                o_ref[:, cols] = (acc_ref[h] / l).astype(o_ref.dtype)
                lse_ref[:, cols] = m_ref[h] + jnp.log2(l)

        comm_finish(cins, couts, sems)

    qspec = pl.BlockSpec((tq, wb), lambda hp, t: (pair(t)[0], hp))
    kspec = pl.BlockSpec((tk, wb), lambda hp, t: (pair(t)[1], hp))
    in_specs = [qspec, kspec, kspec]
    args = [q, k, v]
    if has_bias:
        bias = _mask_bias(mode, tq)
        in_specs.append(pl.BlockSpec(bias.shape, lambda hp, t: (0, 0, 0), pipeline_mode=pl.Buffered(1)))
        args.append(bias)
    out_shape = [jax.ShapeDtypeStruct((sq, w), BF16), jax.ShapeDtypeStruct((sq, w), F32)]
    out_specs = [qspec, qspec]
    if comm is not None:
        in_specs += [_ANY] * n_cin
        args += list(comm.inputs)
        out_shape += list(comm.out_shapes)
        out_specs += [_ANY] * n_cout
    res = pl.pallas_call(
        body,
        out_shape=tuple(out_shape),
        grid=grid,
        in_specs=in_specs,
        out_specs=tuple(out_specs),
        scratch_shapes=[pltpu.VMEM((hb, tq, HP), F32), pltpu.VMEM((hb, tq, HP), F32), pltpu.VMEM((hb, tq, HP), F32)] + comm_sems,
        compiler_params=_cp("parallel", "arbitrary") if comm is None else _cp("arbitrary", "arbitrary"),
        name=name,
    )(*args)
    return res[0], res[1], list(res[2:])


def _flash_bwd(q, k, v, o, do, lse, mode, name, comm=None):
    sq, sk, w, tq, tk, nq, nk, band = _flash_dims(q, k, mode)
    hb = _heads_per_step(w // HP, BWD_HEADS_PER_STEP)
    wb = hb * HP
    nch = tk // 128
    has_bias = mode != "full"
    starts, n_pairs = _pair_starts(mode, nq, band, key_major=True)
    pair = functools.partial(_pair_at, starts=starts, mode=mode, band=band, key_major=True)
    grid = (w // wb, n_pairs)
    n_cin, n_cout, comm_sems, comm_start, comm_finish = _carried(comm, grid)

    def body(*refs):
        ins, cins, (dq_ref, dk_ref, dv_ref), couts, (delta_ref,), sems = _split_refs(refs, 6 + has_bias, n_cin, 3, n_cout, 1)
        q_ref, k_ref, v_ref, o_ref, do_ref, lse_ref = ins[:6]
        b_ref = ins[6] if has_bias else None
        qi, kj = pair(pl.program_id(1))
        rows = pl.ds(pl.multiple_of(qi * tq, tq), tq)
        comm_start(cins, couts, sems)

        @pl.when(pl.program_id(1) == 0)
        def _():
            dq_ref[...] = jnp.zeros_like(dq_ref)

        @pl.when(qi == (0 if mode == "full" else kj))
        def _():
            dk_ref[...] = jnp.zeros_like(dk_ref)
            dv_ref[...] = jnp.zeros_like(dv_ref)

        @pl.when(kj == _first_key(mode, qi, band))
        def _():
            for h in range(hb):
                cols = slice(h * HP, (h + 1) * HP)
                dl = jnp.sum(do_ref[:, cols].astype(F32) * o_ref[:, cols].astype(F32), axis=-1, keepdims=True)
                delta_ref[h, rows, :] = jnp.broadcast_to(dl, (tq, HP))

        def step(bias_tile):
            for h in range(hb):
                cols = slice(h * HP, (h + 1) * HP)
                qv, kv, dov = q_ref[:, cols], k_ref[:, cols], do_ref[:, cols]
                sc = lax.dot_general(qv, kv, _DIMS["nt"], preferred_element_type=F32)
                if bias_tile is not None:
                    sc = sc + bias_tile()
                dp = lax.dot_general(dov, v_ref[:, cols], _DIMS["nt"], preferred_element_type=F32)
                lse_b = lse_ref[:, cols]
                dlt = delta_ref[h, rows, :]
                pch, dsch = [], []
                for c in range(nch):
                    lanes = slice(c * 128, (c + 1) * 128)
                    pc = jnp.exp2(sc[:, lanes] - lse_b)
                    pch.append(pc.astype(BF16))
                    dsch.append((pc * (dp[:, lanes] - dlt)).astype(BF16))
                p = pch[0] if nch == 1 else jnp.concatenate(pch, axis=1)
                ds = dsch[0] if nch == 1 else jnp.concatenate(dsch, axis=1)
                dv_ref[:, cols] += lax.dot_general(p, dov, _DIMS["tn"], preferred_element_type=F32)
                dk_ref[:, cols] += lax.dot_general(ds, qv, _DIMS["tn"], preferred_element_type=F32)
                dq_ref[rows, cols] += jnp.dot(ds, kv, preferred_element_type=F32)

        if mode == "full":
            step(None)
        elif mode == "causal":
            pl.when(qi == kj)(lambda: step(lambda: b_ref[0]))
            pl.when(qi > kj)(lambda: step(None))
        else:
            step(lambda: b_ref[qi - kj])

        comm_finish(cins, couts, sems)

    qspec = pl.BlockSpec((tq, wb), lambda hp, t: (pair(t)[0], hp))
    kspec = pl.BlockSpec((tk, wb), lambda hp, t: (pair(t)[1], hp))
    in_specs = [qspec, kspec, kspec, qspec, qspec, qspec]
    args = [q, k, v, o, do, lse]
    if has_bias:
        bias = _mask_bias(mode, tq)
        in_specs.append(pl.BlockSpec(bias.shape, lambda hp, t: (0, 0, 0), pipeline_mode=pl.Buffered(1)))
        args.append(bias)
    out_shape = [jax.ShapeDtypeStruct((sq, w), F32), jax.ShapeDtypeStruct((sk, w), F32), jax.ShapeDtypeStruct((sk, w), F32)]
    out_specs = [pl.BlockSpec((sq, wb), lambda hp, t: (0, hp)), kspec, kspec]
    if comm is not None:
        in_specs += [_ANY] * n_cin
        args += list(comm.inputs)
        out_shape += list(comm.out_shapes)
        out_specs += [_ANY] * n_cout
    res = pl.pallas_call(
        body,
        out_shape=tuple(out_shape),
        grid=grid,
        in_specs=in_specs,
        out_specs=tuple(out_specs),
        scratch_shapes=[pltpu.VMEM((hb, sq, HP), F32)] + comm_sems,
        compiler_params=_cp("parallel", "arbitrary") if comm is None else _cp("arbitrary", "arbitrary"),
        name=name,
    )(*args)
    return res[0], res[1], res[2], list(res[3:])


def _conv_rc(s):
    return _tile(s, 256, 8)


def _shift_down(x, prev8, nrows):
    rows = lax.broadcasted_iota(jnp.int32, x.shape, 0)
    out = pltpu.roll(x, nrows, 0)
    for i in range(nrows):
        out = jnp.where(rows == i, prev8[8 - nrows + i:8 - nrows + i + 1, :], out)
    return out


def _shift_up(x, next8, nrows):
    n = x.shape[0]
    rows = lax.broadcasted_iota(jnp.int32, x.shape, 0)
    out = pltpu.roll(x, n - nrows, 0)
    for i in range(nrows):
        out = jnp.where(rows == n - nrows + i, next8[i:i + 1, :], out)
    return out


def _conv_taps(uv_ref, ug_ref, r, rc):
    r0 = pl.multiple_of(r * rc, rc)
    x = jnp.concatenate([uv_ref[pl.ds(r0, rc), :], ug_ref[pl.ds(r0, rc), :]], axis=1)
    p0 = pl.multiple_of(jnp.maximum(r0 - 8, 0), 8)
    prev8 = jnp.where(r > 0, jnp.concatenate([uv_ref[pl.ds(p0, 8), :], ug_ref[pl.ds(p0, 8), :]], axis=1), 0.0)
    return r0, x, _shift_down(x, prev8, 1), _shift_down(x, prev8, 2)


def _conv_specs(s, tf, nf):
    strip = pl.BlockSpec((s, tf), lambda j: (0, j))
    return strip, [pl.BlockSpec((3, tf), lambda j: (0, j)), pl.BlockSpec((3, tf), lambda j: (0, j + nf)),
                   pl.BlockSpec((1, tf), lambda j: (0, j)), pl.BlockSpec((1, tf), lambda j: (0, j + nf))]


def _conv_params(wv_ref, wg_ref, bv_ref, bg_ref):
    w = jnp.concatenate([wv_ref[...], wg_ref[...]], axis=1)
    return w[0:1, :], w[1:2, :], w[2:3, :], jnp.concatenate([bv_ref[...], bg_ref[...]], axis=1)


def _conv_fwd(u_val, u_gate, cw, cb, name):
    s, f = u_val.shape
    tf = CONV_TF
    nf = f // tf
    rc = _conv_rc(s)

    def body(uv_ref, ug_ref, wv_ref, wg_ref, bv_ref, bg_ref, a_ref):
        w0, w1, w2, b = _conv_params(wv_ref, wg_ref, bv_ref, bg_ref)

        def chunk(r, carry):
            r0, x, x1, x2 = _conv_taps(uv_ref, ug_ref, r, rc)
            c = b + w0 * x2 + w1 * x1 + w2 * x
            val, gate = c[:, :tf], c[:, tf:]
            a_ref[pl.ds(r0, rc), :] = (gate * jax.nn.sigmoid(gate) * val).astype(a_ref.dtype)
            return carry

        lax.fori_loop(0, s // rc, chunk, 0)

    strip, params = _conv_specs(s, tf, nf)
    return pl.pallas_call(
        body,
        out_shape=jax.ShapeDtypeStruct((s, f), BF16),
        grid=(nf,),
        in_specs=[strip, strip] + params,
        out_specs=strip,
        compiler_params=_cp("parallel"),
        name=name,
    )(u_val, u_gate, cw, cw, cb, cb)


def _conv_bwd(u_val, u_gate, da, cw, cb, name):
    s, f = u_val.shape
    tf = CONV_TF
    nf = f // tf
    rc = _conv_rc(s)
    nchunk = s // rc

    def body(uv_ref, ug_ref, da_ref, wv_ref, wg_ref, bv_ref, bg_ref, duv_ref, dug_ref, dwv_ref, dwg_ref, dbv_ref, dbg_ref, next_ref):
        w0, w1, w2, b = _conv_params(wv_ref, wg_ref, bv_ref, bg_ref)
        next_ref[...] = jnp.zeros_like(next_ref)

        def chunk(it, carry):
            g0, g1, g2, gb = carry
            r = nchunk - 1 - it
            r0, x, x1, x2 = _conv_taps(uv_ref, ug_ref, r, rc)
            c = b + w0 * x2 + w1 * x1 + w2 * x
            val, gate = c[:, :tf], c[:, tf:]
            sg = jax.nn.sigmoid(gate)
            dav = da_ref[pl.ds(r0, rc), :]
            dc = jnp.concatenate([dav * (gate * sg), dav * val * (sg * (1.0 + gate * (1.0 - sg)))], axis=1)
            nxt = next_ref[...]
            du = (w2 * dc + w1 * _shift_up(dc, nxt, 1) + w0 * _shift_up(dc, nxt, 2)).astype(duv_ref.dtype)
            duv_ref[pl.ds(r0, rc), :] = du[:, :tf]
            dug_ref[pl.ds(r0, rc), :] = du[:, tf:]
            next_ref[...] = dc[0:8, :]
            return (g0 + jnp.sum(dc * x2, axis=0, keepdims=True), g1 + jnp.sum(dc * x1, axis=0, keepdims=True),
                    g2 + jnp.sum(dc * x, axis=0, keepdims=True), gb + jnp.sum(dc, axis=0, keepdims=True))

        zero = jnp.zeros((1, 2 * tf), F32)
        g0, g1, g2, gb = lax.fori_loop(0, nchunk, chunk, (zero, zero, zero, zero))
        for i, gi in enumerate((g0, g1, g2)):
            dwv_ref[i:i + 1, :] = gi[:, :tf]
            dwg_ref[i:i + 1, :] = gi[:, tf:]
        dbv_ref[...] = gb[:, :tf]
        dbg_ref[...] = gb[:, tf:]

    strip, params = _conv_specs(s, tf, nf)
    taps = pl.BlockSpec((3, tf), lambda j: (0, j))
    bias = pl.BlockSpec((1, tf), lambda j: (0, j))
    act = jax.ShapeDtypeStruct((s, f), BF16)
    return pl.pallas_call(
        body,
        out_shape=(act, act, jax.ShapeDtypeStruct((3, f), F32), jax.ShapeDtypeStruct((3, f), F32),
                   jax.ShapeDtypeStruct((1, f), F32), jax.ShapeDtypeStruct((1, f), F32)),
        grid=(nf,),
        in_specs=[strip, strip, strip] + params,
        out_specs=(strip, strip, taps, taps, bias, bias),
        scratch_shapes=[pltpu.VMEM((8, 2 * tf), F32)],
        compiler_params=_cp("parallel"),
        name=name,
    )(u_val, u_gate, da, cw, cw, cb, cb)


def _adamw(w, g, m, v, name):
    r, c = w.shape
    tr = _tile(r, 256, 8) if r % 8 == 0 else r
    c1 = 1.0 - ADAM_B1 ** ADAM_STEP
    c2 = 1.0 - ADAM_B2 ** ADAM_STEP

    def body(w_ref, g_ref, m_ref, v_ref, d_ref, mo_ref, vo_ref):
        gv = g_ref[...]
        mn = ADAM_B1 * m_ref[...] + (1.0 - ADAM_B1) * gv
        vn = ADAM_B2 * v_ref[...] + (1.0 - ADAM_B2) * (gv * gv)
        d_ref[...] = -ADAM_LR * ((mn / c1) / (jnp.sqrt(vn / c2) + ADAM_EPS) + ADAM_WD * w_ref[...])
        mo_ref[...] = mn
        vo_ref[...] = vn

    blk = pl.BlockSpec((tr, c), lambda i: (i, 0))
    shp = jax.ShapeDtypeStruct((r, c), F32)
    return pl.pallas_call(
        body, out_shape=(shp, shp, shp), grid=(r // tr,), in_specs=[blk] * 4, out_specs=(blk,) * 3,
        compiler_params=_cp("parallel"), name=name,
    )(w, g, m, v)


def _sum_rows(parts, out_dtype, name):
    n, r, c = parts.shape
    tr = _tile(r, 256, 8)

    def body(p_ref, o_ref):
        tot = p_ref[0].astype(F32)
        for i in range(1, n):
            tot = tot + p_ref[i].astype(F32)
        o_ref[...] = tot.astype(o_ref.dtype)

    return pl.pallas_call(
        body, out_shape=jax.ShapeDtypeStruct((r, c), out_dtype), grid=(r // tr,),
        in_specs=[pl.BlockSpec((n, tr, c), lambda i: (0, i, 0))], out_specs=pl.BlockSpec((tr, c), lambda i: (i, 0)),
        compiler_params=_cp("parallel"), name=name,
    )(parts)


_ANY = pl.BlockSpec(memory_space=pl.ANY)


def _place():
    return lax.axis_index("x"), lax.axis_index("y"), lax.axis_index("c")


def _other_chips(x, y):
    return [(1 - x, y), (x, 1 - y), (1 - x, 1 - y)]


def _rows_half(ref, c, axis):
    rh = ref.shape[axis] // 2
    idx = [slice(None)] * len(ref.shape)
    idx[axis] = pl.ds(pl.multiple_of(c * rh, 16), rh)
    return ref.at[tuple(idx)]


def _remote(src, dst, send_sems, recv_sems, kk, to):
    return pltpu.make_async_remote_copy(src_ref=src, dst_ref=dst, send_sem=send_sems.at[kk], recv_sem=recv_sems.at[kk],
                                        device_id=to, device_id_type=MESH)


class _GatherSpec:
    def __init__(self, shards):
        self.inputs = list(shards)
        self.out_shapes = [jax.ShapeDtypeStruct((N_CHIPS,) + s.shape, s.dtype) for s in shards]
        self.n_sems = 6 * len(shards)

    def _sends(self, ins, outs, send_sems, recv_sems):
        x, y, c = _place()
        me = 2 * x + y
        return [_remote(_rows_half(ins[a], c, 0), _rows_half(outs[a].at[me], c, 0), send_sems, recv_sems, 6 * a + j, (cx, cy, c))
                for a in range(len(ins)) for j, (cx, cy) in enumerate(_other_chips(x, y))]

    def _forwards(self, outs, send_sems, recv_sems):
        x, y, c = _place()
        return [_remote(_rows_half(outs[a].at[2 * cx + cy], c, 0), _rows_half(outs[a].at[2 * cx + cy], c, 0), send_sems, recv_sems,
                        6 * a + 3 + j, (x, y, 1 - c))
                for a in range(len(outs)) for j, (cx, cy) in enumerate(_other_chips(x, y))]

    def start(self, ins, outs, send_sems, recv_sems):
        for cp in self._sends(ins, outs, send_sems, recv_sems):
            cp.start()

    def relay(self, ins, outs, send_sems, recv_sems):
        x, y, c = _place()
        forwards = self._forwards(outs, send_sems, recv_sems)
        for a in range(len(outs)):
            for j, (cx, cy) in enumerate(_other_chips(x, y)):
                slot = _rows_half(outs[a].at[2 * cx + cy], c, 0)
                _remote(slot, slot, send_sems, recv_sems, 6 * a + j, (x, y, c)).wait_recv()
                forwards[3 * a + j].start()

    def finish(self, ins, outs, send_sems, recv_sems):
        x, y, c = _place()
        for a in range(len(outs)):
            for j, (cx, cy) in enumerate(_other_chips(x, y)):
                slot = _rows_half(outs[a].at[2 * cx + cy], 1 - c, 0)
                _remote(slot, slot, send_sems, recv_sems, 6 * a + 3 + j, (x, y, c)).wait_recv()
        for cp in self._sends(ins, outs, send_sems, recv_sems) + self._forwards(outs, send_sems, recv_sems):
            cp.wait_send()


_FLIPS = [(dx, dy, dc) for dx in (0, 1) for dy in (0, 1) for dc in (0, 1) if (dx, dy, dc) != (0, 0, 0)]


class _ScatterSpec:
    def __init__(self, parts):
        self.inputs = list(parts)
        self.out_shapes = [jax.ShapeDtypeStruct((8, p.shape[1] // 2, p.shape[2]), p.dtype) for p in parts]
        self.n_sems = 7 * len(parts)

    def _sends(self, ins, outs, send_sems, recv_sems):
        x, y, c = _place()
        me = 4 * x + 2 * y + c
        copies = []
        for a in range(len(ins)):
            for k, (dx, dy, dc) in enumerate(_FLIPS):
                tx, ty, tc = x ^ dx, y ^ dy, c ^ dc
                copies.append(_remote(_rows_half(ins[a].at[2 * tx + ty], tc, 0), outs[a].at[me], send_sems, recv_sems,
                                      7 * a + k, (tx, ty, tc)))
        return copies

    def start(self, ins, outs, send_sems, recv_sems):
        for cp in self._sends(ins, outs, send_sems, recv_sems):
            cp.start()

    def relay(self, ins, outs, send_sems, recv_sems):
        pass

    def finish(self, ins, outs, send_sems, recv_sems):
        x, y, c = _place()
        for a in range(len(outs)):
            for k, (dx, dy, dc) in enumerate(_FLIPS):
                slot = outs[a].at[4 * (x ^ dx) + 2 * (y ^ dy) + (c ^ dc)]
                _remote(slot, slot, send_sems, recv_sems, 7 * a + k, (x, y, c)).wait_recv()
        for cp in self._sends(ins, outs, send_sems, recv_sems):
            cp.wait_send()


def _run_comm(comm, name):
    n_in, n_out = len(comm.inputs), len(comm.out_shapes)

    def body(*refs):
        ins, outs, sems = refs[:n_in], refs[n_in:n_in + n_out], refs[n_in + n_out:]
        comm.start(ins, outs, *sems)
        comm.relay(ins, outs, *sems)
        comm.finish(ins, outs, *sems)

    return list(pl.pallas_call(
        body, out_shape=tuple(comm.out_shapes), in_specs=[_ANY] * n_in, out_specs=(_ANY,) * n_out,
        scratch_shapes=[pltpu.SemaphoreType.DMA((comm.n_sems,)), pltpu.SemaphoreType.DMA((comm.n_sems,))], name=name,
    )(*comm.inputs))


def _join_halves(rs, small, name):
    n = len(rs)

    def body(*refs):
        in_refs, small_ref = refs[:n], refs[n]
        out_refs, all_ref = refs[n + 1:2 * n + 1], refs[2 * n + 1]
        send_sems, recv_sems, local_sem = refs[2 * n + 2:]
        x, y, c = _place()
        me = 4 * x + 2 * y + c
        mine = pltpu.make_async_copy(small_ref, all_ref.at[me], local_sem)
        mine.start()
        copies = [_remote(in_refs[a], out_refs[a], send_sems, recv_sems, a, (x, y, 1 - c)) for a in range(n)]
        sends = [_remote(small_ref, all_ref.at[me], send_sems, recv_sems, n + k, (x ^ dx, y ^ dy, c ^ dc))
                 for k, (dx, dy, dc) in enumerate(_FLIPS)]
        for cp in copies + sends:
            cp.start()
        for cp in copies:
            cp.wait()
        for k, (dx, dy, dc) in enumerate(_FLIPS):
            slot = all_ref.at[4 * (x ^ dx) + 2 * (y ^ dy) + (c ^ dc)]
            _remote(slot, slot, send_sems, recv_sems, n + k, (x, y, c)).wait_recv()
        for cp in sends:
            cp.wait_send()
        mine.wait()

    res = pl.pallas_call(
        body,
        out_shape=tuple(jax.ShapeDtypeStruct(r.shape, r.dtype) for r in rs) + (jax.ShapeDtypeStruct((8,) + small.shape, small.dtype),),
        in_specs=[_ANY] * (n + 1), out_specs=(_ANY,) * (n + 1),
        scratch_shapes=[pltpu.SemaphoreType.DMA((n + 7,)), pltpu.SemaphoreType.DMA((n + 7,)), pltpu.SemaphoreType.DMA], name=name,
    )(*rs, small)
    return list(res[:n]), res[n]


def _gather_all(small, name):
    r, w = small.shape
    flips = [(dx, dy, dc) for dx in (0, 1) for dy in (0, 1) for dc in (0, 1) if (dx, dy, dc) != (0, 0, 0)]

    def body(in_ref, out_ref, send_sems, recv_sems, local_sem):
        x, y, c = _place()
        me = 4 * x + 2 * y + c
        mine = pltpu.make_async_copy(in_ref, out_ref.at[me], local_sem)
        mine.start()
        sends = []
        for j, (dx, dy, dc) in enumerate(flips):
            to = (x ^ dx, y ^ dy, c ^ dc)
            cp = pltpu.make_async_remote_copy(src_ref=in_ref, dst_ref=out_ref.at[me], send_sem=send_sems.at[j],
                                              recv_sem=recv_sems.at[j], device_id=to, device_id_type=MESH)
            cp.start()
            sends.append(cp)
        for j, (dx, dy, dc) in enumerate(flips):
            slot = out_ref.at[4 * (x ^ dx) + 2 * (y ^ dy) + (c ^ dc)]
            pltpu.make_async_remote_copy(src_ref=slot, dst_ref=slot, send_sem=send_sems.at[j], recv_sem=recv_sems.at[j],
                                         device_id=(x, y, c), device_id_type=MESH).wait_recv()
        for cp in sends:
            cp.wait_send()
        mine.wait()

    return pl.pallas_call(
        body, out_shape=jax.ShapeDtypeStruct((8, r, w), small.dtype), in_specs=[_ANY], out_specs=_ANY,
        scratch_shapes=[pltpu.SemaphoreType.DMA((7,)), pltpu.SemaphoreType.DMA((7,)), pltpu.SemaphoreType.DMA], name=name,
    )(small)


_BIG = ("w_in_fox", "w_in_dil", "w_mem_kv", "w_out", "w_up", "w_down")
_CONVW_SHARD = (DEPTH, 3, 2 * D_FF // N_CHIPS)

_SMALL = (("norm_mix", (DEPTH, D_MODEL)), ("norm_mem", (DEPTH, D_MODEL)), ("norm_ffn", (DEPTH, D_MODEL)),
          ("conv_b", (DEPTH, 2 * D_FF)), ("norm_final", (D_MODEL,)), ("b_forget", (2, N_MIX)), ("conv_w", (DEPTH, 3, 2 * D_FF)))


def _pack_small(vals, spec):
    flat = jnp.concatenate([vals[n].reshape(-1).astype(F32) for n, _ in spec])
    rows = -(-flat.shape[0] // (8 * 128)) * 8
    return jnp.pad(flat, (0, rows * 128 - flat.shape[0])).reshape(rows, 128)


def _unpack_small(buf, spec):
    flat, out, off = buf.reshape(-1), {}, 0
    for n, shp in spec:
        k = int(np.prod(shp))
        out[n] = flat[off:off + k].reshape(shp)
        off += k
    return out


def _fox_cols_to_kernel(w):
    qkv, f, qm = w[:, :3 * D_MIX], w[:, 3 * D_MIX:3 * D_MIX + N_MIX], w[:, 3 * D_MIX + N_MIX:]
    return jnp.concatenate([qkv, qm, f, jnp.zeros((w.shape[0], 128 - N_MIX), w.dtype)], axis=1)


def _fox_cols_from_kernel(w):
    qkv, qm, f = w[:, :3 * D_MIX], w[:, 3 * D_MIX:3 * D_MIX + D_MEMQ], w[:, 3 * D_MIX + D_MEMQ:3 * D_MIX + D_MEMQ + N_MIX]
    return jnp.concatenate([qkv, f, qm], axis=1)


def _rope_pair_tables(s):
    inv = 1.0 / (ROPE_THETA ** (jnp.arange(0, HEAD_DIM, 2, dtype=F32) / HEAD_DIM))
    ang = jnp.arange(s, dtype=F32)[:, None] * inv[None, :]
    cos, sin = jnp.cos(ang), jnp.sin(ang)
    return jnp.concatenate([cos, cos, cos, cos], axis=1), jnp.concatenate([-sin, sin, -sin, sin], axis=1)


def _local_step(x, mem, target, small, ex):
    s = x.shape[0]
    cos2, sin2 = _rope_pair_tables(s)
    saved = []
    h = x
    keys = [("w_in", 0), ("w_mem_kv", 0)]
    spec = ex.gather_spec(keys)
    if spec is not None:
        ex.put_gathered(keys, _run_comm(spec, "gather_first"))
    for l in range(DEPTH):
        fox = l % 2 == 0
        slot = l // 2
        tag = f"L{l}"
        g_mix, g_mem, g_ffn = (small[n][l:l + 1] for n in ("norm_mix", "norm_mem", "norm_ffn"))
        xn = _rms_fwd(h, g_mix, f"rms_mix_{tag}")
        proj = _matmul(xn, ex.weight(("w_in", l)), "nn", F32, f"mm_in_{tag}")
        if fox:
            bfg = jnp.pad(small["b_forget"][slot:slot + 1], ((0, 0), (0, 128 - N_MIX)))
            qp, kp, vp, qmp = _prep_fox_fwd(proj, bfg, f"prep_fox_{tag}")
        else:
            bfg = None
            qp, kp, vp, qmp = _prep_dil_fwd(proj, cos2, sin2, f"prep_dil_{tag}")
        mn = _rms_fwd(mem, g_mem, f"rms_mem_{tag}")
        kvm = _matmul(mn, ex.weight(("w_mem_kv", l)), "nn", F32, f"mm_memkv_{tag}")
        kmp, vmp = _pad_heads([(kvm, 0, N_MEM), (kvm, D_MEMQ, N_MEM)], (1.0, 1.0), (N_MEM * HP,) * 2, f"pad_memkv_{tag}")
        mode = "causal" if fox else "dilated"
        keys = [("w_out", l), ("w_up", l), ("w_down", l)] + ([("w_in", l + 1), ("w_mem_kv", l + 1)] if l + 1 < DEPTH else [])
        spec = ex.gather_spec(keys)
        o_mix, lse_mix, got = _flash_fwd(qp, kp, vp, mode, f"flash_{mode}_fwd_{tag}", comm=spec)
        if spec is not None:
            ex.put_gathered(keys, got)
        o_mem, lse_mem, _ = _flash_fwd(qmp, kmp, vmp, "full", f"flash_mem_fwd_{tag}")
        heads = _unpad_heads([o_mix, o_mem], (1.0, 1.0), BF16, f"unpad_heads_{tag}")
        h_mid = _matmul(heads, ex.weight(("w_out", l)), "nn", F32, f"mm_out_{tag}", residual=h)
        xn2 = _rms_fwd(h_mid, g_ffn, f"rms_ffn_{tag}")
        w_val, w_gate = ex.weight(("w_up", l))
        u_val = _matmul(xn2, w_val, "nn", F32, f"mm_up_val_{tag}")
        u_gate = _matmul(xn2, w_gate, "nn", F32, f"mm_up_gate_{tag}")
        act = _conv_fwd(u_val, u_gate, *ex.conv(l), f"conv_fwd_{tag}")
        h_out = _matmul(act, ex.weight(("w_down", l)), "nn", F32, f"mm_down_{tag}", residual=h_mid)
        saved.append(dict(h=h, xn=xn, proj=proj, bfg=bfg, qp=qp, kp=kp, vp=vp, qmp=qmp, mn=mn, kmp=kmp, vmp=vmp, o_mix=o_mix,
                          lse_mix=lse_mix, o_mem=o_mem, lse_mem=lse_mem, heads=heads, h_mid=h_mid, xn2=xn2, u_val=u_val, u_gate=u_gate, act=act))
        h = h_out

    loss_blk, dh, dg_final = _loss_head(h, small["norm_final"].reshape(1, D_MODEL), target, "loss_head")

    grads = {k: [None] * DEPTH for k in ("conv_w", "conv_b", "norm_mix", "norm_mem", "norm_ffn")}
    grads["b_forget"] = [None, None]
    waiting = []
    for l in reversed(range(DEPTH)):
        fox = l % 2 == 0
        tag = f"L{l}"
        sv = saved[l]
        g_mix, g_mem, g_ffn = (small[n][l:l + 1] for n in ("norm_mix", "norm_mem", "norm_ffn"))
        da = _matmul(dh, ex.weight(("w_down", l)), "nt", F32, f"mm_da_{tag}")
        g_down = _matmul(sv["act"], dh, "tn", BF16, f"mm_dwdown_{tag}")
        du_val, du_gate, dcw_v, dcw_g, dcb_v, dcb_g = _conv_bwd(sv["u_val"], sv["u_gate"], da, *ex.conv(l), f"conv_bwd_{tag}")
        grads["conv_w"][l] = jnp.concatenate([dcw_v, dcw_g], axis=1)
        grads["conv_b"][l] = jnp.concatenate([dcb_v, dcb_g], axis=1)
        w_val, w_gate = ex.weight(("w_up", l))
        dxn2 = _matmul(du_val, w_val, "nt", F32, f"mm_dxn2_val_{tag}")
        dxn2 = _matmul(du_gate, w_gate, "nt", F32, f"mm_dxn2_gate_{tag}", residual=dxn2)
        g_up = (_matmul(sv["xn2"], du_val, "tn", BF16, f"mm_dwup_val_{tag}"), _matmul(sv["xn2"], du_gate, "tn", BF16, f"mm_dwup_gate_{tag}"))
        dh_mid, grads["norm_ffn"][l] = _rms_bwd(sv["h_mid"], g_ffn, dxn2, dh, f"rms_ffn_bwd_{tag}")
        dheads = _matmul(dh_mid, ex.weight(("w_out", l)), "nt", BF16, f"mm_dheads_{tag}")
        g_out = _matmul(sv["heads"], dh_mid, "tn", BF16, f"mm_dwout_{tag}")
        do_mix, do_mem = _pad_heads([(dheads, 0, N_MIX), (dheads, D_MIX, N_MEM)], (1.0, 1.0), (N_MIX * HP, N_MEM * HP),
                                    f"pad_dheads_{tag}")
        mode = "causal" if fox else "dilated"
        items = waiting + [(("w_down", l), g_down), (("w_up", l), g_up), (("w_out", l), g_out)]
        spec = ex.scatter_spec(items)
        dqp, dkp, dvp, got = _flash_bwd(sv["qp"], sv["kp"], sv["vp"], sv["o_mix"], do_mix, sv["lse_mix"], mode, f"flash_{mode}_bwd_{tag}",
                                        comm=spec)
        if spec is not None:
            ex.put_received(items, spec, got)
        dqmp, dkmp, dvmp, _ = _flash_bwd(sv["qmp"], sv["kmp"], sv["vmp"], sv["o_mem"], do_mem, sv["lse_mem"], "full",
                                         f"flash_mem_bwd_{tag}")
        if fox:
            dproj, dbf = _prep_fox_bwd(dqp, dkp, dvp, dqmp, sv["proj"], sv["bfg"], f"prep_fox_bwd_{tag}")
            grads["b_forget"][l // 2] = dbf[0, :N_MIX]
        else:
            dproj = _prep_dil_bwd(dqp, dkp, dvp, dqmp, cos2, sin2, f"prep_dil_bwd_{tag}")
        dkvm = _unpad_heads([dkmp, dvmp], (LN2, 1.0), BF16, f"unpad_dkvm_{tag}")
        g_memkv = _matmul(sv["mn"], dkvm, "tn", BF16, f"mm_dwmemkv_{tag}")
        dmn = _matmul(dkvm, ex.weight(("w_mem_kv", l)), "nt", F32, f"mm_dmn_{tag}")
        _, grads["norm_mem"][l] = _rms_bwd(mem, g_mem, dmn, None, f"rms_mem_bwd_{tag}")
        dxn = _matmul(dproj, ex.weight(("w_in", l)), "nt", F32, f"mm_dxn_{tag}")
        g_in = _matmul(sv["xn"], dproj, "tn", BF16, f"mm_dwin_{tag}")
        waiting = [(("w_mem_kv", l), g_memkv), (("w_in", l), g_in)]
        dh, grads["norm_mix"][l] = _rms_bwd(sv["h"], g_mix, dxn, dh_mid, f"rms_mix_bwd_{tag}")
    spec = ex.scatter_spec(waiting)
    if spec is not None:
        ex.put_received(waiting, spec, _run_comm(spec, "grad_scatter_last"))
    grads["norm_final"] = dg_final
    return loss_blk, dh, grads


class _Exchange:
    def __init__(self, own, conv_w_full, conv_b, chip, core):
        self.own, self.conv_w_full, self.conv_b, self.chip, self.core = own, conv_w_full, conv_b, chip, core
        self.full, self.recv = {}, {}

    def _shard(self, key):
        name, l = key
        if name == "w_in":
            return self.own["w_in_fox" if l % 2 == 0 else "w_in_dil"][l // 2]
        return self.own[name][l]

    def gather_spec(self, keys):
        return _GatherSpec([self._shard(k) for k in keys])

    def put_gathered(self, keys, outs):
        for key, o in zip(keys, outs):
            name, l = key
            g = lax.dynamic_update_slice_in_dim(o, self._shard(key)[None], self.chip, axis=0)
            if name == "w_in":
                w = jnp.concatenate([g[j] for j in range(N_CHIPS)], axis=1)
                self.full[key] = _fox_cols_to_kernel(w) if l % 2 == 0 else w
            elif name == "w_up":
                self.full[key] = (jnp.concatenate([g[0], g[1]], axis=1), jnp.concatenate([g[2], g[3]], axis=1))
            else:
                self.full[key] = g.reshape(N_CHIPS * g.shape[1], g.shape[2])

    def weight(self, key):
        return self.full[key]

    def conv(self, l):
        return self.conv_w_full[l], self.conv_b[l:l + 1]

    def scatter_spec(self, items):
        parts = []
        for (name, l), g in items:
            if name == "w_in":
                g = _fox_cols_from_kernel(g) if l % 2 == 0 else g
                parts.append(jnp.stack(jnp.split(g, N_CHIPS, axis=1)))
            elif name == "w_up":
                parts.append(jnp.stack(jnp.split(g[0], 2, axis=1) + jnp.split(g[1], 2, axis=1)))
            else:
                parts.append(g.reshape(N_CHIPS, g.shape[0] // N_CHIPS, g.shape[1]))
        return _ScatterSpec(parts)

    def put_received(self, items, spec, outs):
        for (key, _), part, o in zip(items, spec.inputs, outs):
            rh = o.shape[1]
            mine = lax.dynamic_slice_in_dim(lax.dynamic_index_in_dim(part, self.chip, 0, keepdims=False), self.core * rh, rh, axis=0)
            self.recv[key] = lax.dynamic_update_slice_in_dim(o, mine[None], 2 * self.chip + self.core, axis=0)


def kernel(x, mem, norm_mix, norm_mem, norm_ffn, w_in_fox, b_forget, w_in_dil, w_mem_kv, w_out, w_up, conv_w, conv_b, w_down, norm_final, loss_target, m_norm_mix, m_norm_mem, m_norm_ffn, m_w_in_fox, m_b_forget, m_w_in_dil, m_w_mem_kv, m_w_out, m_w_up, m_conv_w, m_conv_b, m_w_down, m_norm_final, v_norm_mix, v_norm_mem, v_norm_ffn, v_w_in_fox, v_b_forget, v_w_in_dil, v_w_mem_kv, v_w_out, v_w_up, v_conv_w, v_conv_b, v_w_down, v_norm_final):
    w_sh = dict(w_in_fox=w_in_fox, w_in_dil=w_in_dil, w_mem_kv=w_mem_kv, w_out=w_out, w_up=w_up, w_down=w_down, conv_w=conv_w)
    m_sh = dict(w_in_fox=m_w_in_fox, w_in_dil=m_w_in_dil, w_mem_kv=m_w_mem_kv, w_out=m_w_out, w_up=m_w_up, w_down=m_w_down, conv_w=m_conv_w)
    v_sh = dict(w_in_fox=v_w_in_fox, w_in_dil=v_w_in_dil, w_mem_kv=v_w_mem_kv, w_out=v_w_out, w_up=v_w_up, w_down=v_w_down, conv_w=v_conv_w)
    small = dict(norm_mix=norm_mix, norm_mem=norm_mem, norm_ffn=norm_ffn, conv_b=conv_b, norm_final=norm_final, b_forget=b_forget)
    m_small = dict(norm_mix=m_norm_mix, norm_mem=m_norm_mem, norm_ffn=m_norm_ffn, conv_b=m_conv_b, norm_final=m_norm_final, b_forget=m_b_forget)
    v_small = dict(norm_mix=v_norm_mix, norm_mem=v_norm_mem, norm_ffn=v_norm_ffn, conv_b=v_conv_b, norm_final=v_norm_final, b_forget=v_b_forget)
    chip = 2 * lax.axis_index("x") + lax.axis_index("y")
    core = lax.axis_index("c")

    conv_spec = (("conv_w", _CONVW_SHARD),)
    conv_all = _gather_all(_pack_small(dict(conv_w=conv_w), conv_spec), "gather_conv_w")
    conv_w_full = jnp.concatenate([_unpack_small(conv_all[2 * j], conv_spec)["conv_w"] for j in range(N_CHIPS)], axis=-1)
    ex = _Exchange({n: w_sh[n].astype(BF16) for n in _BIG}, conv_w_full, conv_b, chip, core)

    loss_blk, dx, grads = _local_step(x[0], mem[0], loss_target[0], small, ex)
    loss = lax.psum(loss_blk[0, 0], ("x", "y", "c"))

    layer_keys = {"w_in_fox": [("w_in", 0), ("w_in", 2)], "w_in_dil": [("w_in", 1), ("w_in", 3)]}
    keys = [k for n in _BIG for k in layer_keys.get(n, [(n, l) for l in range(DEPTH)])]
    halves = [_sum_rows(ex.recv[k], F32, f"grad_sum_{k[0]}_L{k[1]}") for k in keys]
    g_small_local = dict(
        norm_mix=jnp.concatenate(grads["norm_mix"]), norm_mem=jnp.concatenate(grads["norm_mem"]),
        norm_ffn=jnp.concatenate(grads["norm_ffn"]),
        conv_b=jnp.concatenate(grads["conv_b"]),
        norm_final=grads["norm_final"], b_forget=jnp.stack(grads["b_forget"]),
        conv_w=jnp.stack(grads["conv_w"]),
    )
    others, small_all = _join_halves(halves, _pack_small(g_small_local, _SMALL), "grad_join_halves")
    whole = {k: jnp.concatenate([jnp.where(core == 0, mine, other), jnp.where(core == 0, other, mine)], axis=0)
             for k, mine, other in zip(keys, halves, others)}
    g_big = {n: jnp.stack([whole[k] for k in layer_keys.get(n, [(n, l) for l in range(DEPTH)])]) for n in _BIG}
    g_small = _unpack_small(_sum_rows(small_all, F32, "small_sum"), _SMALL)
    ncol = 2 * D_FF // N_CHIPS
    g_big["conv_w"] = lax.dynamic_slice_in_dim(g_small["conv_w"], chip * ncol, ncol, axis=2)

    out_g, out_d, out_m, out_v = {}, {}, {}, {}
    for n in _BIG + ("conv_w",):
        shp = w_sh[n].shape
        two_d = (-1, shp[-1])
        d, mo, vo = _adamw(w_sh[n].reshape(two_d), g_big[n].reshape(two_d), m_sh[n].reshape(two_d), v_sh[n].reshape(two_d), f"adamw_{n}")
        out_g[n], out_d[n], out_m[n], out_v[n] = g_big[n].reshape(shp), d.reshape(shp), mo.reshape(shp), vo.reshape(shp)
    spec = _SMALL[:-1]
    d, mo, vo = _adamw(_pack_small(small, spec), _pack_small(g_small, spec), _pack_small(m_small, spec), _pack_small(v_small, spec),
                       "adamw_small")
    d, mo, vo = _unpack_small(d, spec), _unpack_small(mo, spec), _unpack_small(vo, spec)
    for n, shp in spec:
        out_g[n], out_d[n], out_m[n], out_v[n] = g_small[n].reshape(shp), d[n], mo[n], vo[n]

    order = ("norm_mix", "norm_mem", "norm_ffn", "w_in_fox", "b_forget", "w_in_dil", "w_mem_kv", "w_out", "w_up", "conv_w", "conv_b",
             "w_down", "norm_final")
    return (loss, dx[None], *[out_g[n] for n in order], *[out_d[n] for n in order], *[out_m[n] for n in order],
            *[out_v[n] for n in order])
```
